```python
import math
import jax, jax.numpy as jnp
from jax import lax
import numpy as np

D_MODEL = 1024
BATCH = 8
SEQ = 4096
DEPTH = 1

N_MEM = 256
SWA_HEADS = 8
SWA_KV_HEADS = 2
SWA_HEAD_DIM = 64
WINDOW = 128
FOX_HEADS = 8
FOX_HEAD_DIM = 64
MEM_HEADS = 4
MEM_HEAD_DIM = 128
N_BRANCHES = 3
D_FF = 4 * D_MODEL
REL_BUCKETS = 32
REL_MAX_DIST = 128
BLOCK = 128
EPS = 1e-6
NEG = -1e30

SWA_Q = SWA_HEADS * SWA_HEAD_DIM
SWA_KV = SWA_KV_HEADS * SWA_HEAD_DIM
FOX_W = FOX_HEADS * FOX_HEAD_DIM
MEM_W = MEM_HEADS * MEM_HEAD_DIM
GATE_W = N_BRANCHES * D_MODEL
SPLIT_POINTS = (
    SWA_Q,
    SWA_Q + SWA_KV,
    SWA_Q + 2 * SWA_KV,
    SWA_Q + 2 * SWA_KV + FOX_W,
    SWA_Q + 2 * SWA_KV + 2 * FOX_W,
    SWA_Q + 2 * SWA_KV + 3 * FOX_W,
    SWA_Q + 2 * SWA_KV + 3 * FOX_W + FOX_HEADS,
    SWA_Q + 2 * SWA_KV + 3 * FOX_W + FOX_HEADS + MEM_W,
)
IN_WIDTH = SWA_Q + 2 * SWA_KV + 3 * FOX_W + FOX_HEADS + MEM_W + GATE_W

kernel_name = "hybrid_swa_fox_memory_gated_block"


def rmsnorm(x, g):
    xf = x.astype(jnp.float32)
    y = xf * lax.rsqrt(jnp.mean(xf * xf, axis=-1, keepdims=True) + EPS)
    return (y * g.astype(jnp.float32)).astype(x.dtype)


def t5_causal_bucket(dist):
    max_exact = REL_BUCKETS // 2
    d = jnp.maximum(dist, 0)
    df = jnp.maximum(d, 1).astype(jnp.float32)
    large = max_exact + (jnp.log(df / max_exact) / math.log(REL_MAX_DIST / max_exact)
                         * (REL_BUCKETS - max_exact)).astype(jnp.int32)
    large = jnp.minimum(large, REL_BUCKETS - 1)
    return jnp.where(d < max_exact, d, large)


def swa_attention(q, k, v, sinks, rel_bias):
    B, S = q.shape[0], q.shape[1]
    nb = S // BLOCK
    G = SWA_HEADS // SWA_KV_HEADS
    qb = q.reshape(B, nb, BLOCK, SWA_KV_HEADS, G, SWA_HEAD_DIM)
    pad = ((0, 0), (BLOCK, 0), (0, 0), (0, 0))
    kp = jnp.pad(k, pad).reshape(B, nb + 1, BLOCK, SWA_KV_HEADS, SWA_HEAD_DIM)
    vp = jnp.pad(v, pad).reshape(B, nb + 1, BLOCK, SWA_KV_HEADS, SWA_HEAD_DIM)
    kb = jnp.concatenate([kp[:, :-1], kp[:, 1:]], axis=2)
    vb = jnp.concatenate([vp[:, :-1], vp[:, 1:]], axis=2)
    scale = SWA_HEAD_DIM ** -0.5
    s = jnp.einsum('bnqhgd,bnkhd->bnhgqk', qb, kb).astype(jnp.float32) * scale
    t_loc = jnp.arange(BLOCK)[:, None] + BLOCK
    s_loc = jnp.arange(2 * BLOCK)[None, :]
    dist = t_loc - s_loc
    bias = rel_bias[t5_causal_bucket(dist)]
    bias = bias.reshape(BLOCK, 2 * BLOCK, SWA_KV_HEADS, G).transpose(2, 3, 0, 1)
    s = s + bias.astype(jnp.float32)
    band = (dist >= 0) & (dist < WINDOW)
    key_pos = jnp.arange(nb)[:, None] * BLOCK + s_loc - BLOCK
    valid = band[None] & (key_pos >= 0)[:, None, :]
    s = jnp.where(valid[None, :, None, None], s, NEG)
    sink = sinks.astype(jnp.float32).reshape(1, 1, SWA_KV_HEADS, G, 1, 1)
    m = jnp.maximum(jnp.max(s, axis=-1, keepdims=True), sink)
    p = jnp.exp(s - m)
    denom = jnp.sum(p, axis=-1, keepdims=True) + jnp.exp(sink - m)
    p = (p / denom).astype(v.dtype)
    o = jnp.einsum('bnhgqk,bnkhd->bnqhgd', p, vb)
    return o.reshape(B, S, SWA_Q)


def forgetting_attention(q, k, v, log_f):
    B, S = q.shape[0], q.shape[1]
    nb = S // BLOCK
    scale = FOX_HEAD_DIM ** -0.5
    c = jnp.cumsum(log_f, axis=1)
    c_k = c.transpose(0, 2, 1)
    qb = q.reshape(B, nb, BLOCK, FOX_HEADS, FOX_HEAD_DIM).transpose(1, 0, 2, 3, 4)
    cqb = c.reshape(B, nb, BLOCK, FOX_HEADS).transpose(1, 0, 3, 2)
    key_pos = jnp.arange(S)

    def one_block(args):
        qi, cqi, i = args
        s = jnp.einsum('bqhd,bkhd->bhqk', qi, k).astype(jnp.float32) * scale
        s = s + cqi[..., None] - c_k[:, :, None, :]
        q_pos = i * BLOCK + jnp.arange(BLOCK)
        causal = key_pos[None, :] <= q_pos[:, None]
        s = jnp.where(causal, s, NEG)
        p = jax.nn.softmax(s, axis=-1).astype(v.dtype)
        return jnp.einsum('bhqk,bkhd->bqhd', p, v)

    o = lax.map(one_block, (qb, cqb, jnp.arange(nb)))
    return o.transpose(1, 0, 2, 3, 4).reshape(B, S, FOX_W)


def memory_attention(q, mk, mv):
    B, S = q.shape[0], q.shape[1]
    scale = MEM_HEAD_DIM ** -0.5
    s = jnp.einsum('bshd,bmhd->bhsm', q, mk).astype(jnp.float32) * scale
    p = jax.nn.softmax(s, axis=-1).astype(mv.dtype)
    o = jnp.einsum('bhsm,bmhd->bshd', p, mv)
    return o.reshape(B, S, MEM_W)


def setup_inputs(seed: int = 0) -> dict:
    key = jax.random.key(seed)
    ks = jax.random.split(key, 24)
    f32 = jnp.float32

    def w(k, shape, fan_in):
        return jax.random.normal(k, shape, f32) * fan_in ** -0.5

    def gain(k, shape):
        return 1.0 + 0.05 * jax.random.normal(k, shape, f32)

    return {
        "x": jax.random.normal(ks[0], (BATCH, SEQ, D_MODEL), f32),
        "mem": jax.random.normal(ks[1], (BATCH, N_MEM, D_MODEL), f32),
        "g_mix": gain(ks[2], (DEPTH, D_MODEL)),
        "w_in": w(ks[3], (DEPTH, D_MODEL, IN_WIDTH), D_MODEL),
        "b_gate": 0.02 * jax.random.normal(ks[4], (DEPTH, GATE_W), f32),
        "b_forget": 3.0 + 0.1 * jax.random.normal(ks[5], (DEPTH, FOX_HEADS), f32),
        "qn_swa": gain(ks[6], (DEPTH, SWA_HEAD_DIM)),
        "kn_swa": gain(ks[7], (DEPTH, SWA_HEAD_DIM)),
        "sink_swa": 0.5 * jax.random.normal(ks[8], (DEPTH, SWA_HEADS), f32),
        "rel_bias": 0.5 * jax.random.normal(ks[9], (REL_BUCKETS, SWA_HEADS), f32),
        "qn_fox": gain(ks[10], (DEPTH, FOX_HEAD_DIM)),
        "kn_fox": gain(ks[11], (DEPTH, FOX_HEAD_DIM)),
        "g_mem": gain(ks[12], (DEPTH, D_MODEL)),
        "w_mem_kv": w(ks[13], (DEPTH, D_MODEL, 2 * MEM_W), D_MODEL),
        "qn_mem": gain(ks[14], (DEPTH, MEM_HEAD_DIM)),
        "kn_mem": gain(ks[15], (DEPTH, MEM_HEAD_DIM)),
        "w_o_swa": w(ks[16], (DEPTH, SWA_Q, D_MODEL), SWA_Q),
        "w_o_fox": w(ks[17], (DEPTH, FOX_W, D_MODEL), FOX_W),
        "w_o_mem": w(ks[18], (DEPTH, MEM_W, D_MODEL), MEM_W),
        "w_out": w(ks[19], (DEPTH, D_MODEL, D_MODEL), D_MODEL),
        "g_mlp": gain(ks[20], (DEPTH, D_MODEL)),
        "w_mlp_up": w(ks[21], (DEPTH, D_MODEL, D_FF), D_MODEL),
        "w_mlp_down": w(ks[22], (DEPTH, D_FF, D_MODEL), D_FF),
    }


def reference(x, mem, g_mix, w_in, b_gate, b_forget, qn_swa, kn_swa, sink_swa, rel_bias,
              qn_fox, kn_fox, g_mem, w_mem_kv, qn_mem, kn_mem, w_o_swa, w_o_fox, w_o_mem,
              w_out, g_mlp, w_mlp_up, w_mlp_down):
    B, S = x.shape[0], x.shape[1]
    M = mem.shape[1]
    for layer in range(DEPTH):
        h = rmsnorm(x, g_mix[layer])
        proj = h @ w_in[layer]
        qa, ka, va, qf, kf, vf, fl, qm, gl = jnp.split(proj, SPLIT_POINTS, axis=-1)

        qa = rmsnorm(qa.reshape(B, S, SWA_HEADS, SWA_HEAD_DIM), qn_swa[layer])
        ka = rmsnorm(ka.reshape(B, S, SWA_KV_HEADS, SWA_HEAD_DIM), kn_swa[layer])
        va = va.reshape(B, S, SWA_KV_HEADS, SWA_HEAD_DIM)
        ya = swa_attention(qa, ka, va, sink_swa[layer], rel_bias) @ w_o_swa[layer]

        qf = rmsnorm(qf.reshape(B, S, FOX_HEADS, FOX_HEAD_DIM), qn_fox[layer])
        kf = rmsnorm(kf.reshape(B, S, FOX_HEADS, FOX_HEAD_DIM), kn_fox[layer])
        vf = vf.reshape(B, S, FOX_HEADS, FOX_HEAD_DIM)
        log_f = jax.nn.log_sigmoid(fl.astype(jnp.float32) + b_forget[layer].astype(jnp.float32))
        yf = forgetting_attention(qf, kf, vf, log_f) @ w_o_fox[layer]

        mem_n = rmsnorm(mem, g_mem[layer])
        mk, mv = jnp.split(mem_n @ w_mem_kv[layer], 2, axis=-1)
        mk = rmsnorm(mk.reshape(B, M, MEM_HEADS, MEM_HEAD_DIM), kn_mem[layer])
        mv = mv.reshape(B, M, MEM_HEADS, MEM_HEAD_DIM)
        qm = rmsnorm(qm.reshape(B, S, MEM_HEADS, MEM_HEAD_DIM), qn_mem[layer])
        ym = memory_attention(qm, mk, mv) @ w_o_mem[layer]

        gates = jax.nn.sigmoid((gl + b_gate[layer]).astype(jnp.float32)).astype(x.dtype)
        gates = gates.reshape(B, S, N_BRANCHES, D_MODEL)
        merged = gates[:, :, 0] * ya + gates[:, :, 1] * yf + gates[:, :, 2] * ym
        x = x + merged @ w_out[layer]

        hm = rmsnorm(x, g_mlp[layer])
        u = jnp.square(jax.nn.relu(hm @ w_mlp_up[layer]))
        x = x + u @ w_mlp_down[layer]
    return x
```

```python
import functools
import math

import jax
import jax.numpy as jnp
from jax import lax
from jax.experimental import pallas as pl
from jax.experimental.pallas import tpu as pltpu

F32 = jnp.float32
BF16 = jnp.bfloat16

D_MODEL = 1024
N_MEM = 256
SWA_HEADS = 8
SWA_KV_HEADS = 2
SWA_GROUP = SWA_HEADS // SWA_KV_HEADS
HEAD64 = 64
WINDOW = 128
FOX_HEADS = 8
MEM_HEADS = 4
MEM_HEAD_DIM = 128
D_FF = 4 * D_MODEL
REL_BUCKETS = 32
REL_MAX_DIST = 128
BLOCK = 128
EPS = 1e-6
NEG = -1e30

LANES = 128
VMEM_LIMIT = 56 * 1024 * 1024

SWA_Q = SWA_HEADS * HEAD64
SWA_KV = SWA_KV_HEADS * HEAD64
FOX_W = FOX_HEADS * HEAD64
MEM_W = MEM_HEADS * MEM_HEAD_DIM
GATE_W = 3 * D_MODEL

C_QA = 0
C_KA = C_QA + SWA_Q
C_VA = C_KA + 2 * SWA_KV
C_QF = C_VA + 2 * SWA_KV
C_KF = C_QF + FOX_W
C_VF = C_KF + FOX_W
C_QM = C_VF + FOX_W
C_G = C_QM + MEM_W
C_FL = C_G + GATE_W
C_END = C_FL + LANES

AUG_PER_HEAD = 6


def _dot(a, b):
    return jnp.dot(a, b, preferred_element_type=F32)


def _dot_nt(a, b):
    return lax.dot_general(a, b, (((1,), (1,)), ((), ())), preferred_element_type=F32)


def _rms_rows(x, gain):
    ms = jnp.mean(x * x, axis=-1, keepdims=True)
    return x * lax.rsqrt(ms + EPS) * gain


def _head_rms(y, bd, gain, head_dim):
    y2 = y * y
    hi = y2.astype(BF16)
    lo = (y2 - hi.astype(F32)).astype(BF16)
    ss = _dot(hi, bd) + _dot(lo, bd)
    return y * lax.rsqrt(ss * (1.0 / head_dim) + EPS) * gain


def _const_spec(shape):
    return pl.BlockSpec(shape, lambda *_: (0,) * len(shape))


def _params(n_axes):
    return pltpu.CompilerParams(dimension_semantics=("arbitrary",) * n_axes,
                                vmem_limit_bytes=VMEM_LIMIT)


def _memkv_kernel(mem_ref, g_ref, w_ref, bd_ref, kn_ref, mk_ref, mv_ref):
    h = _rms_rows(mem_ref[...], g_ref[...]).astype(BF16)
    kv = _dot(h, w_ref[...])
    mk = _head_rms(kv[:, :MEM_W], bd_ref[...], kn_ref[...], MEM_HEAD_DIM)
    mk_ref[...] = mk.astype(BF16)
    mv_ref[...] = kv[:, MEM_W:].astype(BF16)


def _memkv(mem2, g_mem, w_kv, bd128, kn_t):
    rows = mem2.shape[0]
    return pl.pallas_call(
        _memkv_kernel,
        grid=(rows // N_MEM,),
        in_specs=[pl.BlockSpec((N_MEM, D_MODEL), lambda i: (i, 0)),
                  _const_spec((1, D_MODEL)),
                  _const_spec((D_MODEL, 2 * MEM_W)),
                  _const_spec((MEM_W, MEM_W)),
                  _const_spec((1, MEM_W))],
        out_specs=[pl.BlockSpec((N_MEM, MEM_W), lambda i: (i, 0)),
                   pl.BlockSpec((N_MEM, MEM_W), lambda i: (i, 0))],
        out_shape=[jax.ShapeDtypeStruct((rows, MEM_W), BF16)] * 2,
        compiler_params=_params(1),
        name="memkv",
    )(mem2, g_mem, w_kv, bd128, kn_t)


def _proj_kernel(x_ref, g_ref, w_ref, bg_ref, bd64_ref, bd128_ref,
                 gqa_ref, gka_ref, gqf_ref, gkf_ref, gqm_ref,
                 qa_ref, ka_ref, va_ref, qf_ref, kf_ref, vf_ref, qm_ref, gate_ref, fl_ref):
    h = _rms_rows(x_ref[...], g_ref[...]).astype(BF16)

    def seg(lo, width):
        return _dot(h, w_ref[:, lo:lo + width])

    bd64 = bd64_ref[...]
    qa_ref[...] = _head_rms(seg(C_QA, SWA_Q), bd64, gqa_ref[...], HEAD64).astype(BF16)
    ka_ref[...] = _head_rms(seg(C_KA, 2 * SWA_KV), bd64[:2 * SWA_KV, :2 * SWA_KV],
                            gka_ref[...], HEAD64).astype(BF16)
    va_ref[...] = seg(C_VA, 2 * SWA_KV).astype(BF16)
    qf_ref[...] = _head_rms(seg(C_QF, FOX_W), bd64, gqf_ref[...], HEAD64).astype(BF16)
    kf_ref[...] = _head_rms(seg(C_KF, FOX_W), bd64, gkf_ref[...], HEAD64).astype(BF16)
    vf_ref[...] = seg(C_VF, FOX_W).astype(BF16)
    qm_ref[...] = _head_rms(seg(C_QM, MEM_W), bd128_ref[...], gqm_ref[...],
                            MEM_HEAD_DIM).astype(BF16)
    for j in range(3):
        z = seg(C_G + j * D_MODEL, D_MODEL) + bg_ref[:, j * D_MODEL:(j + 1) * D_MODEL]
        gate_ref[:, j * D_MODEL:(j + 1) * D_MODEL] = (1.0 / (1.0 + jnp.exp(-z))).astype(BF16)
    fl_ref[...] = seg(C_FL, LANES)


def _proj(x2, g_mix, w1, b_gate, bd64, bd128, gqa, gka, gqf, gkf, gqm, tm):
    t = x2.shape[0]
    widths = [SWA_Q, 2 * SWA_KV, 2 * SWA_KV, FOX_W, FOX_W, FOX_W, MEM_W, GATE_W]
    out_shape = [jax.ShapeDtypeStruct((t, w), BF16) for w in widths]
    out_shape.append(jax.ShapeDtypeStruct((t, LANES), F32))
    out_specs = [pl.BlockSpec((tm, w), lambda i: (i, 0)) for w in widths]
    out_specs.append(pl.BlockSpec((tm, LANES), lambda i: (i, 0)))
    return pl.pallas_call(
        _proj_kernel,
        grid=(t // tm,),
        in_specs=[pl.BlockSpec((tm, D_MODEL), lambda i: (i, 0)),
                  _const_spec((1, D_MODEL)),
                  _const_spec((D_MODEL, C_END)),
                  _const_spec((1, GATE_W)),
                  _const_spec((SWA_Q, SWA_Q)),
                  _const_spec((MEM_W, MEM_W)),
                  _const_spec((1, SWA_Q)),
                  _const_spec((1, 2 * SWA_KV)),
                  _const_spec((1, FOX_W)),
                  _const_spec((1, FOX_W)),
                  _const_spec((1, MEM_W))],
        out_specs=out_specs,
        out_shape=out_shape,
        compiler_params=_params(1),
        name="proj",
    )(x2, g_mix, w1, b_gate, bd64, bd128, gqa, gka, gqf, gkf, gqm)


def _split3(c):
    hi = c.astype(BF16)
    r1 = c - hi.astype(F32)
    mid = r1.astype(BF16)
    lo = (r1 - mid.astype(F32)).astype(BF16)
    return hi, mid, lo


def _cum_kernel(fl_ref, bf_ref, tri_ref, place_ref, ones_ref, aq_ref, ak_ref, carry_ref,
                *, tiles_per_seq):
    i = pl.program_id(0)

    @pl.when(i % tiles_per_seq == 0)
    def _():
        carry_ref[...] = jnp.zeros_like(carry_ref)

    z = fl_ref[...] + bf_ref[...]
    logf = jnp.minimum(z, 0.0) - jnp.log(1.0 + jnp.exp(-jnp.abs(z)))
    hi, mid, lo = _split3(logf)
    tri = tri_ref[...]
    c = (_dot(tri, hi) + _dot(tri, mid)) + _dot(tri, lo) + carry_ref[0:1, :]
    tm = c.shape[0]
    carry_ref[...] = jnp.broadcast_to(c[tm - 1:tm, :], carry_ref.shape)

    lane = lax.broadcasted_iota(jnp.int32, c.shape, 1)
    c = jnp.where(lane < FOX_HEADS, c, 0.0)
    chi, cmid, clo = _split3(c)
    packed = (chi.astype(F32) + pltpu.roll(cmid.astype(F32), FOX_HEADS, 1)
              + pltpu.roll(clo.astype(F32), 2 * FOX_HEADS, 1)).astype(BF16)
    placed = _dot(packed, place_ref[...])
    aq_ref[...] = (placed[:, :FOX_W] + ones_ref[0:1, :FOX_W]).astype(BF16)
    ak_ref[...] = (ones_ref[0:1, FOX_W:] - placed[:, FOX_W:]).astype(BF16)


def _cum(fl, bf_pad, tri, place, ones_row, tm, seq):
    t = fl.shape[0]
    return pl.pallas_call(
        functools.partial(_cum_kernel, tiles_per_seq=seq // tm),
        grid=(t // tm,),
        in_specs=[pl.BlockSpec((tm, LANES), lambda i: (i, 0)),
                  _const_spec((1, LANES)),
                  _const_spec((tm, tm)),
                  _const_spec((LANES, 2 * FOX_W)),
                  _const_spec((8, 2 * FOX_W))],
        out_specs=[pl.BlockSpec((tm, FOX_W), lambda i: (i, 0)),
                   pl.BlockSpec((tm, FOX_W), lambda i: (i, 0))],
        out_shape=[jax.ShapeDtypeStruct((t, FOX_W), BF16)] * 2,
        scratch_shapes=[pltpu.VMEM((8, LANES), F32)],
        compiler_params=_params(1),
        name="cum",
    )(fl, bf_pad, tri, place, ones_row)


def _local_kernel(bucket_ref, relb_ref, sink_ref,
                  qa_ref, kc_ref, kp_ref, vc_ref, vp_ref, qm_ref, mk_ref, mv_ref,
                  oa_ref, om_ref, tbl_ref, *, tq):
    b = pl.program_id(0)
    i = pl.program_id(1)

    @pl.when(jnp.logical_and(b == 0, i == 0))
    def _():
        bucket = bucket_ref[...]
        band = bucket >= 0
        for h in range(SWA_HEADS):
            t = jnp.zeros(bucket.shape, F32)
            for k in range(REL_BUCKETS):
                t = jnp.where(bucket == k, relb_ref[k, h], t)
            tbl_ref[h] = jnp.where(band, t, NEG)

    lane = lax.broadcasted_iota(jnp.int32, (BLOCK, LANES), 1)
    lower = lane < HEAD64
    col = lax.broadcasted_iota(jnp.int32, (BLOCK, 2 * BLOCK), 1)
    pad_cols = jnp.logical_and(col < BLOCK, i == 0)

    for j in range(tq // BLOCK):
        r0 = j * BLOCK
        if j == 0:
            kwin = jnp.concatenate([kp_ref[...], kc_ref[0:BLOCK, :]], axis=0)
            vwin = jnp.concatenate([vp_ref[...], vc_ref[0:BLOCK, :]], axis=0)
        else:
            kwin = kc_ref[r0 - BLOCK:r0 + BLOCK, :]
            vwin = vc_ref[r0 - BLOCK:r0 + BLOCK, :]
        for hb in range(SWA_HEADS // 2):
            qblk = qa_ref[r0:r0 + BLOCK, hb * LANES:(hb + 1) * LANES]
            outs = []
            for half in range(2):
                h = 2 * hb + half
                g = h // SWA_GROUP
                qh = jnp.where(lower if half == 0 else jnp.logical_not(lower), qblk, 0)
                s = _dot_nt(qh, kwin[:, g * LANES:(g + 1) * LANES]) + tbl_ref[h]
                if j == 0:
                    s = jnp.where(pad_cols, NEG, s)
                sink = sink_ref[h]
                m = jnp.maximum(jnp.max(s, axis=-1, keepdims=True), sink)
                p = jnp.exp(s - m)
                denom = jnp.sum(p, axis=-1, keepdims=True) + jnp.exp(sink - m)
                o = _dot(p.astype(BF16), vwin[:, g * LANES:(g + 1) * LANES])
                outs.append(o / denom)
            oa_ref[r0:r0 + BLOCK, hb * LANES:(hb + 1) * LANES] = (
                jnp.where(lower, outs[0], outs[1]).astype(BF16))

    for h in range(MEM_HEADS):
        sl = slice(h * MEM_HEAD_DIM, (h + 1) * MEM_HEAD_DIM)
        s = _dot_nt(qm_ref[:, sl], mk_ref[:, sl])
        m = jnp.max(s, axis=-1, keepdims=True)
        p = jnp.exp(s - m)
        denom = jnp.sum(p, axis=-1, keepdims=True)
        o = _dot(p.astype(BF16), mv_ref[:, sl])
        om_ref[:, sl] = (o / denom).astype(BF16)


def _local(bucket, rel_bias, sinks, qa, ka, va, qm, mk, mv, batch, seq, tq):
    t = qa.shape[0]
    nq = seq // tq
    sub = tq // BLOCK

    def cur(b, i):
        return (b * nq + i, 0)

    def prev(b, i):
        return (jnp.maximum((b * nq + i) * sub - 1, 0), 0)

    smem = pl.BlockSpec(memory_space=pltpu.SMEM)
    return pl.pallas_call(
        functools.partial(_local_kernel, tq=tq),
        grid=(batch, nq),
        in_specs=[_const_spec((BLOCK, 2 * BLOCK)), smem, smem,
                  pl.BlockSpec((tq, SWA_Q), cur),
                  pl.BlockSpec((tq, 2 * SWA_KV), cur),
                  pl.BlockSpec((BLOCK, 2 * SWA_KV), prev),
                  pl.BlockSpec((tq, 2 * SWA_KV), cur),
                  pl.BlockSpec((BLOCK, 2 * SWA_KV), prev),
                  pl.BlockSpec((tq, MEM_W), cur),
                  pl.BlockSpec((N_MEM, MEM_W), lambda b, i: (b, 0)),
                  pl.BlockSpec((N_MEM, MEM_W), lambda b, i: (b, 0))],
        out_specs=[pl.BlockSpec((tq, SWA_Q), cur),
                   pl.BlockSpec((tq, MEM_W), cur)],
        out_shape=[jax.ShapeDtypeStruct((t, SWA_Q), BF16),
                   jax.ShapeDtypeStruct((t, MEM_W), BF16)],
        scratch_shapes=[pltpu.VMEM((SWA_HEADS, BLOCK, 2 * BLOCK), F32)],
        compiler_params=_params(2),
        name="local",
    )(bucket, rel_bias, sinks, qa, ka, ka, va, va, qm, mk, mv)


def _fox_kernel(q_ref, aq_ref, k_ref, ak_ref, v_ref, o_ref, acc_ref, m_ref, *, tq):
    qi = pl.program_id(2)
    lane = lax.broadcasted_iota(jnp.int32, (tq, LANES), 1)
    q = q_ref[...]
    aq = aq_ref[...]
    zero = jnp.zeros_like(q)
    qcat = [
        jnp.concatenate([jnp.where(lane < HEAD64, q, zero),
                         jnp.where(lane < AUG_PER_HEAD, aq, zero)], axis=1),
        jnp.concatenate([jnp.where(lane >= HEAD64, q, zero),
                         jnp.where(jnp.logical_and(lane >= AUG_PER_HEAD,
                                                   lane < 2 * AUG_PER_HEAD), aq, zero)], axis=1),
    ]
    ones = jnp.ones((tq, LANES), BF16)
    acc_ref[...] = jnp.zeros_like(acc_ref)
    m_ref[...] = jnp.full(m_ref.shape, NEG, F32)
    row = lax.broadcasted_iota(jnp.int32, (tq, tq), 0)
    colk = lax.broadcasted_iota(jnp.int32, (tq, tq), 1)
    causal = colk <= row

    def step(kv, masked):
        r0 = pl.multiple_of(kv * tq, tq)
        kcat = jnp.concatenate([k_ref[pl.ds(r0, tq), :], ak_ref[pl.ds(r0, tq), :]], axis=1)
        vcat = jnp.concatenate([v_ref[pl.ds(r0, tq), :], ones], axis=1)
        for j in range(2):
            s = _dot_nt(qcat[j], kcat)
            if masked:
                s = jnp.where(causal, s, NEG)
            m_prev = m_ref[j]
            m_next = jnp.maximum(m_prev, jnp.max(s, axis=-1, keepdims=True))
            alpha = jnp.exp(m_prev - m_next)
            p = jnp.exp(s - jnp.tile(m_next, (1, tq // LANES)))
            acc_ref[j] = (jnp.tile(alpha, (1, 2)) * acc_ref[j]
                          + _dot(p.astype(BF16), vcat))
            m_ref[j] = m_next

    def body(kv, carry):
        step(kv, False)
        return carry

    lax.fori_loop(0, qi, body, 0)
    step(qi, True)

    o0 = acc_ref[0]
    o1 = acc_ref[1]
    out = jnp.where(lane < HEAD64, o0[:, :LANES] / o0[:, LANES:], o1[:, :LANES] / o1[:, LANES:])
    o_ref[...] = out.astype(BF16)


def _fox(qf, aq, kf, ak, vf, batch, seq, tq):
    t = qf.shape[0]
    nq = seq // tq
    pairs = FOX_HEADS // 2

    def qmap(b, p, i):
        return (b * nq + i, p)

    def kmap(b, p, i):
        return (b, p)

    return pl.pallas_call(
        functools.partial(_fox_kernel, tq=tq),
        grid=(batch, pairs, nq),
        in_specs=[pl.BlockSpec((tq, LANES), qmap),
                  pl.BlockSpec((tq, LANES), qmap),
                  pl.BlockSpec((seq, LANES), kmap),
                  pl.BlockSpec((seq, LANES), kmap),
                  pl.BlockSpec((seq, LANES), kmap)],
        out_specs=pl.BlockSpec((tq, LANES), qmap),
        out_shape=jax.ShapeDtypeStruct((t, FOX_W), BF16),
        scratch_shapes=[pltpu.VMEM((2, tq, 2 * LANES), F32),
                        pltpu.VMEM((2, tq, LANES), F32)],
        compiler_params=_params(3),
        name="fox",
    )(qf, aq, kf, ak, vf)


def _merge_kernel(x_ref, oa_ref, of_ref, om_ref, gate_ref, wa_ref, wf_ref, wm_ref, wo_ref,
                  y_ref):
    ya = _dot(oa_ref[...], wa_ref[...])
    yf = _dot(of_ref[...], wf_ref[...])
    ym = _dot(om_ref[...], wm_ref[...])
    merged = (gate_ref[:, 0:D_MODEL].astype(F32) * ya
              + gate_ref[:, D_MODEL:2 * D_MODEL].astype(F32) * yf
              + gate_ref[:, 2 * D_MODEL:].astype(F32) * ym)
    y_ref[...] = x_ref[...] + _dot(merged.astype(BF16), wo_ref[...])


def _merge(x2, oa, of, om, gates, wa, wf, wm, wo, tm):
    t = x2.shape[0]
    row = lambda w: pl.BlockSpec((tm, w), lambda i: (i, 0))
    return pl.pallas_call(
        _merge_kernel,
        grid=(t // tm,),
        in_specs=[row(D_MODEL), row(SWA_Q), row(FOX_W), row(MEM_W), row(GATE_W),
                  _const_spec((SWA_Q, D_MODEL)), _const_spec((FOX_W, D_MODEL)),
                  _const_spec((MEM_W, D_MODEL)), _const_spec((D_MODEL, D_MODEL))],
        out_specs=row(D_MODEL),
        out_shape=jax.ShapeDtypeStruct((t, D_MODEL), F32),
        compiler_params=_params(1),
        name="merge",
    )(x2, oa, of, om, gates, wa, wf, wm, wo)


def _mlp_kernel(x_ref, g_ref, wu_ref, wd_ref, y_ref, *, chunk):
    x = x_ref[...]
    h = _rms_rows(x, g_ref[...]).astype(BF16)
    acc = x
    for c in range(D_FF // chunk):
        u = jnp.maximum(_dot(h, wu_ref[:, c * chunk:(c + 1) * chunk]), 0.0)
        acc = acc + _dot((u * u).astype(BF16), wd_ref[c * chunk:(c + 1) * chunk, :])
    y_ref[...] = acc


def _mlp(x1, g_mlp, wu, wd, tm, chunk=1024):
    t = x1.shape[0]
    return pl.pallas_call(
        functools.partial(_mlp_kernel, chunk=chunk),
        grid=(t // tm,),
        in_specs=[pl.BlockSpec((tm, D_MODEL), lambda i: (i, 0)),
                  _const_spec((1, D_MODEL)),
                  _const_spec((D_MODEL, D_FF)),
                  _const_spec((D_FF, D_MODEL))],
        out_specs=pl.BlockSpec((tm, D_MODEL), lambda i: (i, 0)),
        out_shape=jax.ShapeDtypeStruct((t, D_MODEL), F32),
        compiler_params=_params(1),
        name="mlp",
    )(x1, g_mlp, wu, wd)


def _block_diag_ones(width, head):
    idx = jnp.arange(width) // head
    return (idx[:, None] == idx[None, :]).astype(BF16)


def _t5_bucket_table():
    max_exact = REL_BUCKETS // 2
    t_loc = jnp.arange(BLOCK)[:, None] + BLOCK
    s_loc = jnp.arange(2 * BLOCK)[None, :]
    dist = t_loc - s_loc
    d = jnp.maximum(dist, 0)
    df = jnp.maximum(d, 1).astype(F32)
    large = max_exact + (jnp.log(df / max_exact) / math.log(REL_MAX_DIST / max_exact)
                         * (REL_BUCKETS - max_exact)).astype(jnp.int32)
    large = jnp.minimum(large, REL_BUCKETS - 1)
    bucket = jnp.where(d < max_exact, d, large)
    band = (dist >= 0) & (dist < WINDOW)
    return jnp.where(band, bucket, -1).astype(jnp.int32)


def _aug_placement():
    place = [[0.0] * (2 * FOX_W) for _ in range(LANES)]
    ones = [0.0] * (2 * FOX_W)
    for h in range(FOX_HEADS):
        base = (h // 2) * LANES + (h % 2) * AUG_PER_HEAD
        for part in range(3):
            src = part * FOX_HEADS + h
            place[src][base + part] = 1.0
            place[src][FOX_W + base + 3 + part] = 1.0
            ones[base + 3 + part] = 1.0
            ones[FOX_W + base + part] = 1.0
    return (jnp.array(place, BF16),
            jnp.broadcast_to(jnp.array(ones, F32)[None, :], (8, 2 * FOX_W)))


def _tile_gain(g, reps, scale=1.0):
    return (jnp.tile(g.astype(F32), reps) * scale)[None, :]


def _pick_tile(n, target):
    t = min(n, target)
    while n % t:
        t //= 2
    return t


def kernel(x, mem, g_mix, w_in, b_gate, b_forget, qn_swa, kn_swa, sink_swa, rel_bias,
           qn_fox, kn_fox, g_mem, w_mem_kv, qn_mem, kn_mem, w_o_swa, w_o_fox, w_o_mem,
           w_out, g_mlp, w_mlp_up, w_mlp_down):
    batch, seq, _ = x.shape
    n_layers = w_in.shape[0]
    t = batch * seq
    tm = _pick_tile(seq, 512)
    tq = _pick_tile(seq, 512)

    bd64 = _block_diag_ones(SWA_Q, HEAD64)
    bd128 = _block_diag_ones(MEM_W, MEM_HEAD_DIM)
    bucket = _t5_bucket_table()
    place, ones_row = _aug_placement()
    tri = (jnp.arange(tm)[:, None] >= jnp.arange(tm)[None, :]).astype(BF16)

    x2 = x.reshape(t, D_MODEL)
    mem2 = mem.reshape(batch * N_MEM, D_MODEL)
    for layer in range(n_layers):
        w = w_in[layer]
        o = 0
        parts = {}
        for name, width in (("qa", SWA_Q), ("ka", SWA_KV), ("va", SWA_KV), ("qf", FOX_W),
                            ("kf", FOX_W), ("vf", FOX_W), ("fl", FOX_HEADS), ("qm", MEM_W),
                            ("g", GATE_W)):
            parts[name] = w[:, o:o + width]
            o += width

        def dup(m):
            return jnp.concatenate([m[:, :HEAD64], m[:, :HEAD64], m[:, HEAD64:], m[:, HEAD64:]],
                                   axis=1)

        w1 = jnp.concatenate(
            [parts["qa"], dup(parts["ka"]), dup(parts["va"]), parts["qf"], parts["kf"],
             parts["vf"], parts["qm"], parts["g"],
             jnp.pad(parts["fl"], ((0, 0), (0, LANES - FOX_HEADS)))], axis=1).astype(BF16)

        mk, mv = _memkv(mem2, g_mem[layer][None, :], w_mem_kv[layer].astype(BF16), bd128,
                        _tile_gain(kn_mem[layer], MEM_HEADS))

        qa, ka, va, qf, kf, vf, qm, gates, fl = _proj(
            x2, g_mix[layer][None, :], w1, b_gate[layer][None, :], bd64, bd128,
            _tile_gain(qn_swa[layer], SWA_HEADS, HEAD64 ** -0.5),
            _tile_gain(kn_swa[layer], 2 * SWA_KV_HEADS),
            _tile_gain(qn_fox[layer], FOX_HEADS, HEAD64 ** -0.5),
            _tile_gain(kn_fox[layer], FOX_HEADS),
            _tile_gain(qn_mem[layer], MEM_HEADS, MEM_HEAD_DIM ** -0.5), tm)

        bf_pad = jnp.pad(b_forget[layer].astype(F32), (0, LANES - FOX_HEADS))[None, :]
        aq, ak = _cum(fl, bf_pad, tri, place, ones_row, tm, seq)

        oa, om = _local(bucket, rel_bias.astype(F32), sink_swa[layer].astype(F32),
                        qa, ka, va, qm, mk, mv, batch, seq, tq)
        of = _fox(qf, aq, kf, ak, vf, batch, seq, tq)

        x1 = _merge(x2, oa, of, om, gates, w_o_swa[layer].astype(BF16),
                    w_o_fox[layer].astype(BF16), w_o_mem[layer].astype(BF16),
                    w_out[layer].astype(BF16), tm)
        x2 = _mlp(x1, g_mlp[layer][None, :], w_mlp_up[layer].astype(BF16),
                  w_mlp_down[layer].astype(BF16), tm)
    return x2.reshape(batch, seq, D_MODEL)
```

```python
import functools
import math

import jax
import jax.numpy as jnp
from jax import lax
from jax.experimental import pallas as pl
from jax.experimental.pallas import tpu as pltpu

F32 = jnp.float32
BF16 = jnp.bfloat16

D_MODEL = 1024
N_MEM = 256
SWA_HEADS = 8
SWA_KV_HEADS = 2
SWA_GROUP = SWA_HEADS // SWA_KV_HEADS
HEAD64 = 64
WINDOW = 128
FOX_HEADS = 8
MEM_HEADS = 4
MEM_HEAD_DIM = 128
D_FF = 4 * D_MODEL
REL_BUCKETS = 32
REL_MAX_DIST = 128
BLOCK = 128
EPS = 1e-6
NEG = -1e30

LANES = 128
VMEM_LIMIT = 56 * 1024 * 1024

SWA_Q = SWA_HEADS * HEAD64
SWA_KV = SWA_KV_HEADS * HEAD64
FOX_W = FOX_HEADS * HEAD64
MEM_W = MEM_HEADS * MEM_HEAD_DIM
GATE_W = 3 * D_MODEL

C_QA = 0
C_KA = C_QA + SWA_Q
C_VA = C_KA + 2 * SWA_KV
C_QF = C_VA + 2 * SWA_KV
C_KF = C_QF + FOX_W
C_VF = C_KF + FOX_W
C_QM = C_VF + FOX_W
C_G = C_QM + MEM_W
C_FL = C_G + GATE_W
C_END = C_FL + LANES

AUG_PER_HEAD = 6


def _dot(a, b):
    return jnp.dot(a, b, preferred_element_type=F32)


def _dot_nt(a, b):
    return lax.dot_general(a, b, (((1,), (1,)), ((), ())), preferred_element_type=F32)


def _rms_rows(x, gain):
    ms = jnp.mean(x * x, axis=-1, keepdims=True)
    return x * lax.rsqrt(ms + EPS) * gain


def _head_rms(y, bd, gain, head_dim):
    y2 = y * y
    hi = y2.astype(BF16)
    lo = (y2 - hi.astype(F32)).astype(BF16)
    ss = _dot(hi, bd) + _dot(lo, bd)
    return y * lax.rsqrt(ss * (1.0 / head_dim) + EPS) * gain


def _const_spec(shape):
    return pl.BlockSpec(shape, lambda *_: (0,) * len(shape))


def _params(n_axes, flags=None):
    return pltpu.CompilerParams(dimension_semantics=("arbitrary",) * n_axes,
                                vmem_limit_bytes=VMEM_LIMIT, flags=flags)


def _memkv_kernel(mem_ref, g_ref, w_ref, bd_ref, kn_ref, mk_ref, mv_ref):
    h = _rms_rows(mem_ref[...], g_ref[...]).astype(BF16)
    kv = _dot(h, w_ref[...])
    mk = _head_rms(kv[:, :MEM_W], bd_ref[...], kn_ref[...], MEM_HEAD_DIM)
    mk_ref[...] = mk.astype(BF16)
    mv_ref[...] = kv[:, MEM_W:].astype(BF16)


def _memkv(mem2, g_mem, w_kv, bd128, kn_t):
    rows = mem2.shape[0]
    return pl.pallas_call(
        _memkv_kernel,
        grid=(rows // N_MEM,),
        in_specs=[pl.BlockSpec((N_MEM, D_MODEL), lambda i: (i, 0)),
                  _const_spec((1, D_MODEL)),
                  _const_spec((D_MODEL, 2 * MEM_W)),
                  _const_spec((MEM_W, MEM_W)),
                  _const_spec((1, MEM_W))],
        out_specs=[pl.BlockSpec((N_MEM, MEM_W), lambda i: (i, 0)),
                   pl.BlockSpec((N_MEM, MEM_W), lambda i: (i, 0))],
        out_shape=[jax.ShapeDtypeStruct((rows, MEM_W), BF16)] * 2,
        compiler_params=_params(1),
        name="memkv",
    )(mem2, g_mem, w_kv, bd128, kn_t)


def _proj_kernel(x_ref, g_ref, w_ref, bg_ref, bd64_ref, bd128_ref,
                 gqa_ref, gka_ref, gqf_ref, gkf_ref, gqm_ref,
                 qa_ref, ka_ref, va_ref, qf_ref, kf_ref, vf_ref, qm_ref, gate_ref, fl_ref):
    h = _rms_rows(x_ref[...], g_ref[...]).astype(BF16)

    def seg(lo, width):
        return _dot(h, w_ref[:, lo:lo + width])

    bd64 = bd64_ref[...]
    qa_ref[...] = _head_rms(seg(C_QA, SWA_Q), bd64, gqa_ref[...], HEAD64).astype(BF16)
    ka_ref[...] = _head_rms(seg(C_KA, 2 * SWA_KV), bd64[:2 * SWA_KV, :2 * SWA_KV],
                            gka_ref[...], HEAD64).astype(BF16)
    va_ref[...] = seg(C_VA, 2 * SWA_KV).astype(BF16)
    qf_ref[...] = _head_rms(seg(C_QF, FOX_W), bd64, gqf_ref[...], HEAD64).astype(BF16)
    kf_ref[...] = _head_rms(seg(C_KF, FOX_W), bd64, gkf_ref[...], HEAD64).astype(BF16)
    vf_ref[...] = seg(C_VF, FOX_W).astype(BF16)
    qm_ref[...] = _head_rms(seg(C_QM, MEM_W), bd128_ref[...], gqm_ref[...],
                            MEM_HEAD_DIM).astype(BF16)
    for j in range(3):
        z = seg(C_G + j * D_MODEL, D_MODEL) + bg_ref[:, j * D_MODEL:(j + 1) * D_MODEL]
        gate_ref[:, j * D_MODEL:(j + 1) * D_MODEL] = (1.0 / (1.0 + jnp.exp(-z))).astype(BF16)
    fl_ref[...] = seg(C_FL, LANES)


def _proj(x2, g_mix, w1, b_gate, bd64, bd128, gqa, gka, gqf, gkf, gqm, tm):
    t = x2.shape[0]
    widths = [SWA_Q, 2 * SWA_KV, 2 * SWA_KV, FOX_W, FOX_W, FOX_W, MEM_W, GATE_W]
    out_shape = [jax.ShapeDtypeStruct((t, w), BF16) for w in widths]
    out_shape.append(jax.ShapeDtypeStruct((t, LANES), F32))
    out_specs = [pl.BlockSpec((tm, w), lambda i: (i, 0)) for w in widths]
    out_specs.append(pl.BlockSpec((tm, LANES), lambda i: (i, 0)))
    return pl.pallas_call(
        _proj_kernel,
        grid=(t // tm,),
        in_specs=[pl.BlockSpec((tm, D_MODEL), lambda i: (i, 0)),
                  _const_spec((1, D_MODEL)),
                  _const_spec((D_MODEL, C_END)),
                  _const_spec((1, GATE_W)),
                  _const_spec((SWA_Q, SWA_Q)),
                  _const_spec((MEM_W, MEM_W)),
                  _const_spec((1, SWA_Q)),
                  _const_spec((1, 2 * SWA_KV)),
                  _const_spec((1, FOX_W)),
                  _const_spec((1, FOX_W)),
                  _const_spec((1, MEM_W))],
        out_specs=out_specs,
        out_shape=out_shape,
        compiler_params=_params(1),
        name="proj",
    )(x2, g_mix, w1, b_gate, bd64, bd128, gqa, gka, gqf, gkf, gqm)


def _split3(c):
    hi = c.astype(BF16)
    r1 = c - hi.astype(F32)
    mid = r1.astype(BF16)
    lo = (r1 - mid.astype(F32)).astype(BF16)
    return hi, mid, lo


def _cum_kernel(fl_ref, bf_ref, tri_ref, place_ref, ones_ref, aq_ref, ak_ref, carry_ref,
                *, tiles_per_seq):
    i = pl.program_id(0)

    @pl.when(i % tiles_per_seq == 0)
    def _():
        carry_ref[...] = jnp.zeros_like(carry_ref)

    z = fl_ref[...] + bf_ref[...]
    logf = jnp.minimum(z, 0.0) - jnp.log(1.0 + jnp.exp(-jnp.abs(z)))
    hi, mid, lo = _split3(logf)
    tri = tri_ref[...]
    c = (_dot(tri, hi) + _dot(tri, mid)) + _dot(tri, lo) + carry_ref[0:1, :]
    tm = c.shape[0]
    carry_ref[...] = jnp.broadcast_to(c[tm - 1:tm, :], carry_ref.shape)

    lane = lax.broadcasted_iota(jnp.int32, c.shape, 1)
    c = jnp.where(lane < FOX_HEADS, c, 0.0)
    chi, cmid, clo = _split3(c)
    packed = (chi.astype(F32) + pltpu.roll(cmid.astype(F32), FOX_HEADS, 1)
              + pltpu.roll(clo.astype(F32), 2 * FOX_HEADS, 1)).astype(BF16)
    placed = _dot(packed, place_ref[...])
    aq_ref[...] = (placed[:, :FOX_W] + ones_ref[0:1, :FOX_W]).astype(BF16)
    ak_ref[...] = (ones_ref[0:1, FOX_W:] - placed[:, FOX_W:]).astype(BF16)


def _cum(fl, bf_pad, tri, place, ones_row, tm, seq):
    t = fl.shape[0]
    return pl.pallas_call(
        functools.partial(_cum_kernel, tiles_per_seq=seq // tm),
        grid=(t // tm,),
        in_specs=[pl.BlockSpec((tm, LANES), lambda i: (i, 0)),
                  _const_spec((1, LANES)),
                  _const_spec((tm, tm)),
                  _const_spec((LANES, 2 * FOX_W)),
                  _const_spec((8, 2 * FOX_W))],
        out_specs=[pl.BlockSpec((tm, FOX_W), lambda i: (i, 0)),
                   pl.BlockSpec((tm, FOX_W), lambda i: (i, 0))],
        out_shape=[jax.ShapeDtypeStruct((t, FOX_W), BF16)] * 2,
        scratch_shapes=[pltpu.VMEM((8, LANES), F32)],
        compiler_params=_params(1),
        name="cum",
    )(fl, bf_pad, tri, place, ones_row)


def _local_kernel(bucket_ref, relb_ref, sink_ref,
                  qa_ref, kc_ref, kp_ref, vc_ref, vp_ref, qm_ref, mk_ref, mv_ref,
                  oa_ref, om_ref, tbl_ref, *, tq):
    b = pl.program_id(0)
    i = pl.program_id(1)

    @pl.when(jnp.logical_and(b == 0, i == 0))
    def _():
        bucket = bucket_ref[...]
        band = bucket >= 0
        for h in range(SWA_HEADS):
            t = jnp.zeros(bucket.shape, F32)
            for k in range(REL_BUCKETS):
                t = jnp.where(bucket == k, relb_ref[k, h], t)
            tbl_ref[h] = jnp.where(band, t, NEG)

    lane = lax.broadcasted_iota(jnp.int32, (BLOCK, LANES), 1)
    lower = lane < HEAD64
    col = lax.broadcasted_iota(jnp.int32, (BLOCK, 2 * BLOCK), 1)
    pad_cols = jnp.logical_and(col < BLOCK, i == 0)

    for j in range(tq // BLOCK):
        r0 = j * BLOCK
        if j == 0:
            kwin = jnp.concatenate([kp_ref[...], kc_ref[0:BLOCK, :]], axis=0)
            vwin = jnp.concatenate([vp_ref[...], vc_ref[0:BLOCK, :]], axis=0)
        else:
            kwin = kc_ref[r0 - BLOCK:r0 + BLOCK, :]
            vwin = vc_ref[r0 - BLOCK:r0 + BLOCK, :]
        for hb in range(SWA_HEADS // 2):
            qblk = qa_ref[r0:r0 + BLOCK, hb * LANES:(hb + 1) * LANES]
            outs = []
            for half in range(2):
                h = 2 * hb + half
                g = h // SWA_GROUP
                qh = jnp.where(lower if half == 0 else jnp.logical_not(lower), qblk, 0)
                s = _dot_nt(qh, kwin[:, g * LANES:(g + 1) * LANES]) + tbl_ref[h]
                if j == 0:
                    s = jnp.where(pad_cols, NEG, s)
                sink = sink_ref[h]
                m = jnp.maximum(jnp.max(s, axis=-1, keepdims=True), sink)
                p = jnp.exp(s - m)
                denom = jnp.sum(p, axis=-1, keepdims=True) + jnp.exp(sink - m)
                o = _dot(p.astype(BF16), vwin[:, g * LANES:(g + 1) * LANES])
                outs.append(o / denom)
            oa_ref[r0:r0 + BLOCK, hb * LANES:(hb + 1) * LANES] = (
                jnp.where(lower, outs[0], outs[1]).astype(BF16))

    for h in range(MEM_HEADS):
        sl = slice(h * MEM_HEAD_DIM, (h + 1) * MEM_HEAD_DIM)
        s = _dot_nt(qm_ref[:, sl], mk_ref[:, sl])
        m = jnp.max(s, axis=-1, keepdims=True)
        p = jnp.exp(s - m)
        denom = jnp.sum(p, axis=-1, keepdims=True)
        o = _dot(p.astype(BF16), mv_ref[:, sl])
        om_ref[:, sl] = (o / denom).astype(BF16)


def _local(bucket, rel_bias, sinks, qa, ka, va, qm, mk, mv, batch, seq, tq):
    t = qa.shape[0]
    nq = seq // tq
    sub = tq // BLOCK

    def cur(b, i):
        return (b * nq + i, 0)

    def prev(b, i):
        return (jnp.maximum((b * nq + i) * sub - 1, 0), 0)

    smem = pl.BlockSpec(memory_space=pltpu.SMEM)
    return pl.pallas_call(
        functools.partial(_local_kernel, tq=tq),
        grid=(batch, nq),
        in_specs=[_const_spec((BLOCK, 2 * BLOCK)), smem, smem,
                  pl.BlockSpec((tq, SWA_Q), cur),
                  pl.BlockSpec((tq, 2 * SWA_KV), cur),
                  pl.BlockSpec((BLOCK, 2 * SWA_KV), prev),
                  pl.BlockSpec((tq, 2 * SWA_KV), cur),
                  pl.BlockSpec((BLOCK, 2 * SWA_KV), prev),
                  pl.BlockSpec((tq, MEM_W), cur),
                  pl.BlockSpec((N_MEM, MEM_W), lambda b, i: (b, 0)),
                  pl.BlockSpec((N_MEM, MEM_W), lambda b, i: (b, 0))],
        out_specs=[pl.BlockSpec((tq, SWA_Q), cur),
                   pl.BlockSpec((tq, MEM_W), cur)],
        out_shape=[jax.ShapeDtypeStruct((t, SWA_Q), BF16),
                   jax.ShapeDtypeStruct((t, MEM_W), BF16)],
        scratch_shapes=[pltpu.VMEM((SWA_HEADS, BLOCK, 2 * BLOCK), F32)],
        compiler_params=_params(2),
        name="local",
    )(bucket, rel_bias, sinks, qa, ka, ka, va, va, qm, mk, mv)


def _fox_kernel(q_ref, aq_ref, k_ref, ak_ref, v_ref, o_ref, acc_ref, m_ref, *, tq):
    qi = pl.program_id(2)
    lane = lax.broadcasted_iota(jnp.int32, (tq, LANES), 1)
    q = q_ref[...]
    aq = aq_ref[...]
    zero = jnp.zeros_like(q)
    qcat = [
        jnp.concatenate([jnp.where(lane < HEAD64, q, zero),
                         jnp.where(lane < AUG_PER_HEAD, aq, zero)], axis=1),
        jnp.concatenate([jnp.where(lane >= HEAD64, q, zero),
                         jnp.where(jnp.logical_and(lane >= AUG_PER_HEAD,
                                                   lane < 2 * AUG_PER_HEAD), aq, zero)], axis=1),
    ]
    ones = jnp.ones((tq, LANES), BF16)
    acc_ref[...] = jnp.zeros_like(acc_ref)
    m_ref[...] = jnp.full(m_ref.shape, NEG, F32)
    row = lax.broadcasted_iota(jnp.int32, (tq, tq), 0)
    colk = lax.broadcasted_iota(jnp.int32, (tq, tq), 1)
    causal = colk <= row

    def step(kv, masked):
        r0 = pl.multiple_of(kv * tq, tq)
        kcat = jnp.concatenate([k_ref[pl.ds(r0, tq), :], ak_ref[pl.ds(r0, tq), :]], axis=1)
        vcat = jnp.concatenate([v_ref[pl.ds(r0, tq), :], ones], axis=1)
        for j in range(2):
            s = _dot_nt(qcat[j], kcat)
            if masked:
                s = jnp.where(causal, s, NEG)
            m_prev = m_ref[j]
            m_next = jnp.maximum(m_prev, jnp.max(s, axis=-1, keepdims=True))
            alpha = jnp.exp(m_prev - m_next)
            p = jnp.exp(s - jnp.tile(m_next, (1, tq // LANES)))
            acc_ref[j] = (jnp.tile(alpha, (1, 2)) * acc_ref[j]
                          + _dot(p.astype(BF16), vcat))
            m_ref[j] = m_next

    def body(kv2, carry):
        step(2 * kv2, False)
        step(2 * kv2 + 1, False)
        return carry

    lax.fori_loop(0, qi // 2, body, 0)

    @pl.when(qi % 2 == 1)
    def _():
        step(qi - 1, False)

    step(qi, True)

    o0 = acc_ref[0]
    o1 = acc_ref[1]
    out = jnp.where(lane < HEAD64, o0[:, :LANES] / o0[:, LANES:], o1[:, :LANES] / o1[:, LANES:])
    o_ref[...] = out.astype(BF16)


def _fox(qf, aq, kf, ak, vf, batch, seq, tq):
    t = qf.shape[0]
    nq = seq // tq
    pairs = FOX_HEADS // 2

    def qmap(b, p, i):
        return (b * nq + i, p)

    def kmap(b, p, i):
        return (b, p)

    return pl.pallas_call(
        functools.partial(_fox_kernel, tq=tq),
        grid=(batch, pairs, nq),
        in_specs=[pl.BlockSpec((tq, LANES), qmap),
                  pl.BlockSpec((tq, LANES), qmap),
                  pl.BlockSpec((seq, LANES), kmap),
                  pl.BlockSpec((seq, LANES), kmap),
                  pl.BlockSpec((seq, LANES), kmap)],
        out_specs=pl.BlockSpec((tq, LANES), qmap),
        out_shape=jax.ShapeDtypeStruct((t, FOX_W), BF16),
        scratch_shapes=[pltpu.VMEM((2, tq, 2 * LANES), F32),
                        pltpu.VMEM((2, tq, LANES), F32)],
        compiler_params=_params(3),
        name="fox",
    )(qf, aq, kf, ak, vf)


def _merge_kernel(x_ref, oa_ref, of_ref, om_ref, gate_ref, wa_ref, wf_ref, wm_ref, wo_ref,
                  y_ref):
    ya = _dot(oa_ref[...], wa_ref[...])
    yf = _dot(of_ref[...], wf_ref[...])
    ym = _dot(om_ref[...], wm_ref[...])
    merged = (gate_ref[:, 0:D_MODEL].astype(F32) * ya
              + gate_ref[:, D_MODEL:2 * D_MODEL].astype(F32) * yf
              + gate_ref[:, 2 * D_MODEL:].astype(F32) * ym)
    y_ref[...] = x_ref[...] + _dot(merged.astype(BF16), wo_ref[...])


def _merge(x2, oa, of, om, gates, wa, wf, wm, wo, tm):
    t = x2.shape[0]
    row = lambda w: pl.BlockSpec((tm, w), lambda i: (i, 0))
    return pl.pallas_call(
        _merge_kernel,
        grid=(t // tm,),
        in_specs=[row(D_MODEL), row(SWA_Q), row(FOX_W), row(MEM_W), row(GATE_W),
                  _const_spec((SWA_Q, D_MODEL)), _const_spec((FOX_W, D_MODEL)),
                  _const_spec((MEM_W, D_MODEL)), _const_spec((D_MODEL, D_MODEL))],
        out_specs=row(D_MODEL),
        out_shape=jax.ShapeDtypeStruct((t, D_MODEL), F32),
        compiler_params=_params(1),
        name="merge",
    )(x2, oa, of, om, gates, wa, wf, wm, wo)


def _mlp_kernel(x_ref, g_ref, wu_ref, wd_ref, y_ref, *, chunk):
    x = x_ref[...]
    h = _rms_rows(x, g_ref[...]).astype(BF16)
    acc = x
    for c in range(D_FF // chunk):
        u = jnp.maximum(_dot(h, wu_ref[:, c * chunk:(c + 1) * chunk]), 0.0)
        acc = acc + _dot((u * u).astype(BF16), wd_ref[c * chunk:(c + 1) * chunk, :])
    y_ref[...] = acc


def _mlp(x1, g_mlp, wu, wd, tm, chunk=1024):
    t = x1.shape[0]
    return pl.pallas_call(
        functools.partial(_mlp_kernel, chunk=chunk),
        grid=(t // tm,),
        in_specs=[pl.BlockSpec((tm, D_MODEL), lambda i: (i, 0)),
                  _const_spec((1, D_MODEL)),
                  _const_spec((D_MODEL, D_FF)),
                  _const_spec((D_FF, D_MODEL))],
        out_specs=pl.BlockSpec((tm, D_MODEL), lambda i: (i, 0)),
        out_shape=jax.ShapeDtypeStruct((t, D_MODEL), F32),
        compiler_params=_params(1),
        name="mlp",
    )(x1, g_mlp, wu, wd)


def _block_diag_ones(width, head):
    idx = jnp.arange(width) // head
    return (idx[:, None] == idx[None, :]).astype(BF16)


def _t5_bucket_table():
    max_exact = REL_BUCKETS // 2
    t_loc = jnp.arange(BLOCK)[:, None] + BLOCK
    s_loc = jnp.arange(2 * BLOCK)[None, :]
    dist = t_loc - s_loc
    d = jnp.maximum(dist, 0)
    df = jnp.maximum(d, 1).astype(F32)
    large = max_exact + (jnp.log(df / max_exact) / math.log(REL_MAX_DIST / max_exact)
                         * (REL_BUCKETS - max_exact)).astype(jnp.int32)
    large = jnp.minimum(large, REL_BUCKETS - 1)
    bucket = jnp.where(d < max_exact, d, large)
    band = (dist >= 0) & (dist < WINDOW)
    return jnp.where(band, bucket, -1).astype(jnp.int32)


def _aug_placement():
    place = [[0.0] * (2 * FOX_W) for _ in range(LANES)]
    ones = [0.0] * (2 * FOX_W)
    for h in range(FOX_HEADS):
        base = (h // 2) * LANES + (h % 2) * AUG_PER_HEAD
        for part in range(3):
            src = part * FOX_HEADS + h
            place[src][base + part] = 1.0
            place[src][FOX_W + base + 3 + part] = 1.0
            ones[base + 3 + part] = 1.0
            ones[FOX_W + base + part] = 1.0
    return (jnp.array(place, BF16),
            jnp.broadcast_to(jnp.array(ones, F32)[None, :], (8, 2 * FOX_W)))


def _tile_gain(g, reps, scale=1.0):
    return (jnp.tile(g.astype(F32), reps) * scale)[None, :]


def _pick_tile(n, target):
    t = min(n, target)
    while n % t:
        t //= 2
    return t


def kernel(x, mem, g_mix, w_in, b_gate, b_forget, qn_swa, kn_swa, sink_swa, rel_bias,
           qn_fox, kn_fox, g_mem, w_mem_kv, qn_mem, kn_mem, w_o_swa, w_o_fox, w_o_mem,
           w_out, g_mlp, w_mlp_up, w_mlp_down):
    batch, seq, _ = x.shape
    n_layers = w_in.shape[0]
    t = batch * seq
    tm = _pick_tile(seq, 512)
    tq = _pick_tile(seq, 512)

    bd64 = _block_diag_ones(SWA_Q, HEAD64)
    bd128 = _block_diag_ones(MEM_W, MEM_HEAD_DIM)
    bucket = _t5_bucket_table()
    place, ones_row = _aug_placement()
    tri = (jnp.arange(tm)[:, None] >= jnp.arange(tm)[None, :]).astype(BF16)

    x2 = x.reshape(t, D_MODEL)
    mem2 = mem.reshape(batch * N_MEM, D_MODEL)
    for layer in range(n_layers):
        w = w_in[layer]
        o = 0
        parts = {}
        for name, width in (("qa", SWA_Q), ("ka", SWA_KV), ("va", SWA_KV), ("qf", FOX_W),
                            ("kf", FOX_W), ("vf", FOX_W), ("fl", FOX_HEADS), ("qm", MEM_W),
                            ("g", GATE_W)):
            parts[name] = w[:, o:o + width]
            o += width

        def dup(m):
            return jnp.concatenate([m[:, :HEAD64], m[:, :HEAD64], m[:, HEAD64:], m[:, HEAD64:]],
                                   axis=1)

        w1 = jnp.concatenate(
            [parts["qa"], dup(parts["ka"]), dup(parts["va"]), parts["qf"], parts["kf"],
             parts["vf"], parts["qm"], parts["g"],
             jnp.pad(parts["fl"], ((0, 0), (0, LANES - FOX_HEADS)))], axis=1).astype(BF16)

        mk, mv = _memkv(mem2, g_mem[layer][None, :], w_mem_kv[layer].astype(BF16), bd128,
                        _tile_gain(kn_mem[layer], MEM_HEADS))

        qa, ka, va, qf, kf, vf, qm, gates, fl = _proj(
            x2, g_mix[layer][None, :], w1, b_gate[layer][None, :], bd64, bd128,
            _tile_gain(qn_swa[layer], SWA_HEADS, HEAD64 ** -0.5),
            _tile_gain(kn_swa[layer], 2 * SWA_KV_HEADS),
            _tile_gain(qn_fox[layer], FOX_HEADS, HEAD64 ** -0.5),
            _tile_gain(kn_fox[layer], FOX_HEADS),
            _tile_gain(qn_mem[layer], MEM_HEADS, MEM_HEAD_DIM ** -0.5), tm)

        bf_pad = jnp.pad(b_forget[layer].astype(F32), (0, LANES - FOX_HEADS))[None, :]
        aq, ak = _cum(fl, bf_pad, tri, place, ones_row, tm, seq)

        oa, om = _local(bucket, rel_bias.astype(F32), sink_swa[layer].astype(F32),
                        qa, ka, va, qm, mk, mv, batch, seq, tq)
        of = _fox(qf, aq, kf, ak, vf, batch, seq, tq)

        x1 = _merge(x2, oa, of, om, gates, w_o_swa[layer].astype(BF16),
                    w_o_fox[layer].astype(BF16), w_o_mem[layer].astype(BF16),
                    w_out[layer].astype(BF16), tm)
        x2 = _mlp(x1, g_mlp[layer][None, :], w_mlp_up[layer].astype(BF16),
                  w_mlp_down[layer].astype(BF16), tm)
    return x2.reshape(batch, seq, D_MODEL)
```

```python
import functools
import math

import jax
import jax.numpy as jnp
from jax import lax
from jax.experimental import pallas as pl
from jax.experimental.pallas import tpu as pltpu

F32 = jnp.float32
BF16 = jnp.bfloat16

D_MODEL = 1024
N_MEM = 256
SWA_HEADS = 8
SWA_KV_HEADS = 2
SWA_GROUP = SWA_HEADS // SWA_KV_HEADS
HEAD64 = 64
WINDOW = 128
FOX_HEADS = 8
MEM_HEADS = 4
MEM_HEAD_DIM = 128
D_FF = 4 * D_MODEL
REL_BUCKETS = 32
REL_MAX_DIST = 128
BLOCK = 128
EPS = 1e-6
NEG = -1e30
LOG2E = math.log2(math.e)

LANES = 128
VMEM_LIMIT = 56 * 1024 * 1024

SWA_Q = SWA_HEADS * HEAD64
SWA_KV = SWA_KV_HEADS * HEAD64
FOX_W = FOX_HEADS * HEAD64
MEM_W = MEM_HEADS * MEM_HEAD_DIM
GATE_W = 3 * D_MODEL

C_QA = 0
C_KA = C_QA + SWA_Q
C_VA = C_KA + 2 * SWA_KV
C_QF = C_VA + 2 * SWA_KV
C_KF = C_QF + FOX_W
C_QM = C_KF + FOX_W
C_G = C_QM + MEM_W
C_FL = C_G + GATE_W
C_END = C_FL + LANES

AUG_PER_HEAD = 6


def _dot(a, b):
    return jnp.dot(a, b, preferred_element_type=F32)


def _dot_nt(a, b):
    return lax.dot_general(a, b, (((1,), (1,)), ((), ())), preferred_element_type=F32)


def _rms_rows(x, gain):
    ms = jnp.mean(x * x, axis=-1, keepdims=True)
    return x * lax.rsqrt(ms + EPS) * gain


def _head_rms(y, bd, gain, head_dim):
    y2 = y * y
    hi = y2.astype(BF16)
    lo = (y2 - hi.astype(F32)).astype(BF16)
    ss = _dot(hi, bd) + _dot(lo, bd)
    return y * lax.rsqrt(ss * (1.0 / head_dim) + EPS) * gain


def _const_spec(shape):
    return pl.BlockSpec(shape, lambda *_: (0,) * len(shape))


def _params(n_axes, flags=None):
    return pltpu.CompilerParams(dimension_semantics=("arbitrary",) * n_axes,
                                vmem_limit_bytes=VMEM_LIMIT, flags=flags)


def _memkv_kernel(mem_ref, g_ref, w_ref, bd_ref, kn_ref, mk_ref, mv_ref):
    h = _rms_rows(mem_ref[...], g_ref[...]).astype(BF16)
    kv = _dot(h, w_ref[...])
    mk = _head_rms(kv[:, :MEM_W], bd_ref[...], kn_ref[...], MEM_HEAD_DIM)
    mk_ref[...] = mk.astype(BF16)
    mv_ref[...] = kv[:, MEM_W:].astype(BF16)


def _memkv(mem2, g_mem, w_kv, bd128, kn_t):
    rows = mem2.shape[0]
    return pl.pallas_call(
        _memkv_kernel,
        grid=(rows // N_MEM,),
        in_specs=[pl.BlockSpec((N_MEM, D_MODEL), lambda i: (i, 0)),
                  _const_spec((1, D_MODEL)),
                  _const_spec((D_MODEL, 2 * MEM_W)),
                  _const_spec((MEM_W, MEM_W)),
                  _const_spec((1, MEM_W))],
        out_specs=[pl.BlockSpec((N_MEM, MEM_W), lambda i: (i, 0)),
                   pl.BlockSpec((N_MEM, MEM_W), lambda i: (i, 0))],
        out_shape=[jax.ShapeDtypeStruct((rows, MEM_W), BF16)] * 2,
        compiler_params=_params(1),
        name="memkv",
    )(mem2, g_mem, w_kv, bd128, kn_t)


def _proj_kernel(x_ref, g_ref, w_ref, wvt_ref, bg_ref, bd64_ref, bd128_ref,
                 gqa_ref, gka_ref, gqf_ref, gkf_ref, gqm_ref,
                 qa_ref, ka_ref, va_ref, qf_ref, kf_ref, vt_ref, qm_ref, gate_ref, fl_ref):
    h = _rms_rows(x_ref[...], g_ref[...]).astype(BF16)

    def seg(lo, width):
        return _dot(h, w_ref[:, lo:lo + width])

    bd64 = bd64_ref[...]
    qa_ref[...] = _head_rms(seg(C_QA, SWA_Q), bd64, gqa_ref[...], HEAD64).astype(BF16)
    ka_ref[...] = _head_rms(seg(C_KA, 2 * SWA_KV), bd64[:2 * SWA_KV, :2 * SWA_KV],
                            gka_ref[...], HEAD64).astype(BF16)
    va_ref[...] = seg(C_VA, 2 * SWA_KV).astype(BF16)
    qf_ref[...] = _head_rms(seg(C_QF, FOX_W), bd64, gqf_ref[...], HEAD64).astype(BF16)
    kf_ref[...] = _head_rms(seg(C_KF, FOX_W), bd64, gkf_ref[...], HEAD64).astype(BF16)
    vt = _dot_nt(wvt_ref[...], h).astype(BF16)
    for p in range(FOX_HEADS // 2):
        vt_ref[p] = vt[p * LANES:(p + 1) * LANES, :]
    qm_ref[...] = _head_rms(seg(C_QM, MEM_W), bd128_ref[...], gqm_ref[...],
                            MEM_HEAD_DIM).astype(BF16)
    for j in range(3):
        z = seg(C_G + j * D_MODEL, D_MODEL) + bg_ref[:, j * D_MODEL:(j + 1) * D_MODEL]
        gate_ref[:, j * D_MODEL:(j + 1) * D_MODEL] = (1.0 / (1.0 + jnp.exp(-z))).astype(BF16)
    fl_ref[...] = seg(C_FL, LANES)


def _proj(x2, g_mix, w1, wvt, b_gate, bd64, bd128, gqa, gka, gqf, gkf, gqm, tm, seq):
    t = x2.shape[0]
    nt = seq // tm
    pairs = FOX_HEADS // 2

    def rows(w, dtype=BF16):
        return jax.ShapeDtypeStruct((t, w), dtype), pl.BlockSpec((tm, w), lambda i: (i, 0))

    outs = [rows(SWA_Q), rows(2 * SWA_KV), rows(2 * SWA_KV), rows(FOX_W), rows(FOX_W),
            (jax.ShapeDtypeStruct((t // seq, pairs, nt, LANES, tm), BF16),
             pl.BlockSpec((None, pairs, None, LANES, tm), lambda i: (i // nt, 0, i % nt, 0, 0))),
            rows(MEM_W), rows(GATE_W), rows(LANES, F32)]
    out_shape = [o[0] for o in outs]
    out_specs = [o[1] for o in outs]
    return pl.pallas_call(
        _proj_kernel,
        grid=(t // tm,),
        in_specs=[pl.BlockSpec((tm, D_MODEL), lambda i: (i, 0)),
                  _const_spec((1, D_MODEL)),
                  _const_spec((D_MODEL, C_END)),
                  _const_spec((FOX_W, D_MODEL)),
                  _const_spec((1, GATE_W)),
                  _const_spec((SWA_Q, SWA_Q)),
                  _const_spec((MEM_W, MEM_W)),
                  _const_spec((1, SWA_Q)),
                  _const_spec((1, 2 * SWA_KV)),
                  _const_spec((1, FOX_W)),
                  _const_spec((1, FOX_W)),
                  _const_spec((1, MEM_W))],
        out_specs=out_specs,
        out_shape=out_shape,
        compiler_params=_params(1),
        name="proj",
    )(x2, g_mix, w1, wvt, b_gate, bd64, bd128, gqa, gka, gqf, gkf, gqm)


def _split3(c):
    hi = c.astype(BF16)
    r1 = c - hi.astype(F32)
    mid = r1.astype(BF16)
    lo = (r1 - mid.astype(F32)).astype(BF16)
    return hi, mid, lo


def _cum_kernel(fl_ref, bf_ref, tri_ref, place_ref, ones_ref, aq_ref, ak_ref, carry_ref,
                *, tiles_per_seq):
    i = pl.program_id(0)

    @pl.when(i % tiles_per_seq == 0)
    def _():
        carry_ref[...] = jnp.zeros_like(carry_ref)

    z = fl_ref[...] + bf_ref[...]
    logf = jnp.minimum(z, 0.0) - jnp.log(1.0 + jnp.exp(-jnp.abs(z)))
    hi, mid, lo = _split3(logf)
    tri = tri_ref[...]
    c = (_dot(tri, hi) + _dot(tri, mid)) + _dot(tri, lo) + carry_ref[0:1, :]
    tm = c.shape[0]
    carry_ref[...] = jnp.broadcast_to(c[tm - 1:tm, :], carry_ref.shape)

    lane = lax.broadcasted_iota(jnp.int32, c.shape, 1)
    c = jnp.where(lane < FOX_HEADS, c * LOG2E, 0.0)
    chi, cmid, clo = _split3(c)
    packed = (chi.astype(F32) + pltpu.roll(cmid.astype(F32), FOX_HEADS, 1)
              + pltpu.roll(clo.astype(F32), 2 * FOX_HEADS, 1)).astype(BF16)
    placed = _dot(packed, place_ref[...])
    aq_ref[...] = (placed[:, :FOX_W] + ones_ref[0:1, :FOX_W]).astype(BF16)
    ak_ref[...] = (ones_ref[0:1, FOX_W:] - placed[:, FOX_W:]).astype(BF16)


def _cum(fl, bf_pad, tri, place, ones_row, tm, seq):
    t = fl.shape[0]
    return pl.pallas_call(
        functools.partial(_cum_kernel, tiles_per_seq=seq // tm),
        grid=(t // tm,),
        in_specs=[pl.BlockSpec((tm, LANES), lambda i: (i, 0)),
                  _const_spec((1, LANES)),
                  _const_spec((tm, tm)),
                  _const_spec((LANES, 2 * FOX_W)),
                  _const_spec((8, 2 * FOX_W))],
        out_specs=[pl.BlockSpec((tm, FOX_W), lambda i: (i, 0)),
                   pl.BlockSpec((tm, FOX_W), lambda i: (i, 0))],
        out_shape=[jax.ShapeDtypeStruct((t, FOX_W), BF16)] * 2,
        scratch_shapes=[pltpu.VMEM((8, LANES), F32)],
        compiler_params=_params(1),
        name="cum",
    )(fl, bf_pad, tri, place, ones_row)


def _local_kernel(bucket_ref, relb_ref, sink_ref,
                  qa_ref, kc_ref, kp_ref, vc_ref, vp_ref, qm_ref, mk_ref, mv_ref,
                  oa_ref, om_ref, tbl_ref, *, tq):
    b = pl.program_id(0)
    i = pl.program_id(1)

    @pl.when(jnp.logical_and(b == 0, i == 0))
    def _():
        bucket = bucket_ref[...]
        band = bucket >= 0
        for h in range(SWA_HEADS):
            t = jnp.zeros(bucket.shape, F32)
            for k in range(REL_BUCKETS):
                t = jnp.where(bucket == k, relb_ref[k, h], t)
            tbl_ref[h] = jnp.where(band, t, NEG)

    lane = lax.broadcasted_iota(jnp.int32, (BLOCK, LANES), 1)
    lower = lane < HEAD64
    col = lax.broadcasted_iota(jnp.int32, (BLOCK, 2 * BLOCK), 1)
    pad_cols = jnp.logical_and(col < BLOCK, i == 0)

    for j in range(tq // BLOCK):
        r0 = j * BLOCK
        if j == 0:
            kwin = jnp.concatenate([kp_ref[...], kc_ref[0:BLOCK, :]], axis=0)
            vwin = jnp.concatenate([vp_ref[...], vc_ref[0:BLOCK, :]], axis=0)
        else:
            kwin = kc_ref[r0 - BLOCK:r0 + BLOCK, :]
            vwin = vc_ref[r0 - BLOCK:r0 + BLOCK, :]
        for hb in range(SWA_HEADS // 2):
            qblk = qa_ref[r0:r0 + BLOCK, hb * LANES:(hb + 1) * LANES]
            outs = []
            for half in range(2):
                h = 2 * hb + half
                g = h // SWA_GROUP
                qh = jnp.where(lower if half == 0 else jnp.logical_not(lower), qblk, 0)
                s = _dot_nt(qh, kwin[:, g * LANES:(g + 1) * LANES]) + tbl_ref[h]
                if j == 0:
                    s = jnp.where(pad_cols, NEG, s)
                sink = sink_ref[h]
                m = jnp.maximum(jnp.max(s, axis=-1, keepdims=True), sink)
                p = jnp.exp(s - m)
                denom = jnp.sum(p, axis=-1, keepdims=True) + jnp.exp(sink - m)
                o = _dot(p.astype(BF16), vwin[:, g * LANES:(g + 1) * LANES])
                outs.append(o / denom)
            oa_ref[r0:r0 + BLOCK, hb * LANES:(hb + 1) * LANES] = (
                jnp.where(lower, outs[0], outs[1]).astype(BF16))

    for h in range(MEM_HEADS):
        sl = slice(h * MEM_HEAD_DIM, (h + 1) * MEM_HEAD_DIM)
        s = _dot_nt(qm_ref[:, sl], mk_ref[:, sl])
        m = jnp.max(s, axis=-1, keepdims=True)
        p = jnp.exp(s - m)
        denom = jnp.sum(p, axis=-1, keepdims=True)
        o = _dot(p.astype(BF16), mv_ref[:, sl])
        om_ref[:, sl] = (o / denom).astype(BF16)


def _local(bucket, rel_bias, sinks, qa, ka, va, qm, mk, mv, batch, seq, tq):
    t = qa.shape[0]
    nq = seq // tq
    sub = tq // BLOCK

    def cur(b, i):
        return (b * nq + i, 0)

    def prev(b, i):
        return (jnp.maximum((b * nq + i) * sub - 1, 0), 0)

    smem = pl.BlockSpec(memory_space=pltpu.SMEM)
    return pl.pallas_call(
        functools.partial(_local_kernel, tq=tq),
        grid=(batch, nq),
        in_specs=[_const_spec((BLOCK, 2 * BLOCK)), smem, smem,
                  pl.BlockSpec((tq, SWA_Q), cur),
                  pl.BlockSpec((tq, 2 * SWA_KV), cur),
                  pl.BlockSpec((BLOCK, 2 * SWA_KV), prev),
                  pl.BlockSpec((tq, 2 * SWA_KV), cur),
                  pl.BlockSpec((BLOCK, 2 * SWA_KV), prev),
                  pl.BlockSpec((tq, MEM_W), cur),
                  pl.BlockSpec((N_MEM, MEM_W), lambda b, i: (b, 0)),
                  pl.BlockSpec((N_MEM, MEM_W), lambda b, i: (b, 0))],
        out_specs=[pl.BlockSpec((tq, SWA_Q), cur),
                   pl.BlockSpec((tq, MEM_W), cur)],
        out_shape=[jax.ShapeDtypeStruct((t, SWA_Q), BF16),
                   jax.ShapeDtypeStruct((t, MEM_W), BF16)],
        scratch_shapes=[pltpu.VMEM((SWA_HEADS, BLOCK, 2 * BLOCK), F32)],
        compiler_params=_params(2),
        name="local",
    )(bucket, rel_bias, sinks, qa, ka, ka, va, va, qm, mk, mv)


def _fox_kernel(q_ref, aq_ref, k_ref, ak_ref, vt_ref, o_ref, sa_ref, sb_ref, acc_ref, m_ref,
                *, tq):
    qi = pl.program_id(2)
    lane = lax.broadcasted_iota(jnp.int32, (tq, LANES), 1)
    q = q_ref[...]
    aq = aq_ref[...]
    zero = jnp.zeros_like(q)
    qcat = [
        jnp.concatenate([jnp.where(lane < HEAD64, q, zero),
                         jnp.where(lane < AUG_PER_HEAD, aq, zero)], axis=1),
        jnp.concatenate([jnp.where(lane >= HEAD64, q, zero),
                         jnp.where(jnp.logical_and(lane >= AUG_PER_HEAD,
                                                   lane < 2 * AUG_PER_HEAD), aq, zero)], axis=1),
    ]
    ones = jnp.ones((16, tq), BF16)
    acc_ref[...] = jnp.zeros_like(acc_ref)
    m_ref[...] = jnp.full(m_ref.shape, NEG, F32)
    key = lax.broadcasted_iota(jnp.int32, (tq, tq), 0)
    qry = lax.broadcasted_iota(jnp.int32, (tq, tq), 1)
    causal = key <= qry

    def qk(j, s_ref):
        r0 = pl.multiple_of(j * tq, tq)
        kcat = jnp.concatenate([k_ref[pl.ds(r0, tq), :], ak_ref[pl.ds(r0, tq), :]], axis=1)
        for h in range(2):
            s_ref[h] = _dot_nt(kcat, qcat[h])

    def softmax_pv(j, s_ref, masked):
        for h in range(2):
            s = s_ref[h]
            if masked:
                s = jnp.where(causal, s, NEG)
            m_prev = m_ref[h]
            m_next = jnp.maximum(m_prev, jnp.max(s, axis=0, keepdims=True))
            alpha = jnp.exp2(m_prev - m_next)
            p = jnp.exp2(s - m_next).astype(BF16)
            lhs = jnp.concatenate([vt_ref[j, h * HEAD64:(h + 1) * HEAD64, :], ones], axis=0)
            acc_ref[h] = alpha * acc_ref[h] + _dot(lhs, p)
            m_ref[h] = m_next

    qk(0, sa_ref)

    def body(i, carry):
        j = 2 * i
        qk(j + 1, sb_ref)
        softmax_pv(j, sa_ref, False)
        qk(j + 2, sa_ref)
        softmax_pv(j + 1, sb_ref, False)
        return carry

    lax.fori_loop(0, qi // 2, body, 0)

    @pl.when(qi % 2 == 1)
    def _():
        qk(qi, sb_ref)
        softmax_pv(qi - 1, sa_ref, False)
        softmax_pv(qi, sb_ref, True)

    @pl.when(qi % 2 == 0)
    def _():
        softmax_pv(qi, sa_ref, True)

    halves = []
    for h in range(2):
        acc = acc_ref[h]
        halves.append(acc[:HEAD64, :] / acc[HEAD64:HEAD64 + 1, :])
    o_ref[...] = jnp.concatenate(halves, axis=0).T.astype(BF16)


def _fox(qf, aq, kf, ak, vt, batch, seq, tq):
    t = qf.shape[0]
    nq = seq // tq
    pairs = FOX_HEADS // 2

    def qmap(b, p, i):
        return (b * nq + i, p)

    def kmap(b, p, i):
        return (b, p)

    return pl.pallas_call(
        functools.partial(_fox_kernel, tq=tq),
        grid=(batch, pairs, nq),
        in_specs=[pl.BlockSpec((tq, LANES), qmap),
                  pl.BlockSpec((tq, LANES), qmap),
                  pl.BlockSpec((seq, LANES), kmap),
                  pl.BlockSpec((seq, LANES), kmap),
                  pl.BlockSpec((None, None, nq, LANES, tq), lambda b, p, i: (b, p, 0, 0, 0))],
        out_specs=pl.BlockSpec((tq, LANES), qmap),
        out_shape=jax.ShapeDtypeStruct((t, FOX_W), BF16),
        scratch_shapes=[pltpu.VMEM((2, tq, tq), F32),
                        pltpu.VMEM((2, tq, tq), F32),
                        pltpu.VMEM((2, HEAD64 + 16, tq), F32),
                        pltpu.VMEM((2, 1, tq), F32)],
        compiler_params=_params(3),
        name="fox",
    )(qf, aq, kf, ak, vt)


def _merge_kernel(x_ref, oa_ref, of_ref, om_ref, gate_ref, wa_ref, wf_ref, wm_ref, wo_ref,
                  y_ref):
    ya = _dot(oa_ref[...], wa_ref[...])
    yf = _dot(of_ref[...], wf_ref[...])
    ym = _dot(om_ref[...], wm_ref[...])
    merged = (gate_ref[:, 0:D_MODEL].astype(F32) * ya
              + gate_ref[:, D_MODEL:2 * D_MODEL].astype(F32) * yf
              + gate_ref[:, 2 * D_MODEL:].astype(F32) * ym)
    y_ref[...] = x_ref[...] + _dot(merged.astype(BF16), wo_ref[...])


def _merge(x2, oa, of, om, gates, wa, wf, wm, wo, tm):
    t = x2.shape[0]
    row = lambda w: pl.BlockSpec((tm, w), lambda i: (i, 0))
    return pl.pallas_call(
        _merge_kernel,
        grid=(t // tm,),
        in_specs=[row(D_MODEL), row(SWA_Q), row(FOX_W), row(MEM_W), row(GATE_W),
                  _const_spec((SWA_Q, D_MODEL)), _const_spec((FOX_W, D_MODEL)),
                  _const_spec((MEM_W, D_MODEL)), _const_spec((D_MODEL, D_MODEL))],
        out_specs=row(D_MODEL),
        out_shape=jax.ShapeDtypeStruct((t, D_MODEL), F32),
        compiler_params=_params(1),
        name="merge",
    )(x2, oa, of, om, gates, wa, wf, wm, wo)


def _mlp_kernel(x_ref, g_ref, wu_ref, wd_ref, y_ref, *, chunk):
    x = x_ref[...]
    h = _rms_rows(x, g_ref[...]).astype(BF16)
    acc = x
    for c in range(D_FF // chunk):
        u = jnp.maximum(_dot(h, wu_ref[:, c * chunk:(c + 1) * chunk]), 0.0)
        acc = acc + _dot((u * u).astype(BF16), wd_ref[c * chunk:(c + 1) * chunk, :])
    y_ref[...] = acc


def _mlp(x1, g_mlp, wu, wd, tm, chunk=1024):
    t = x1.shape[0]
    return pl.pallas_call(
        functools.partial(_mlp_kernel, chunk=chunk),
        grid=(t // tm,),
        in_specs=[pl.BlockSpec((tm, D_MODEL), lambda i: (i, 0)),
                  _const_spec((1, D_MODEL)),
                  _const_spec((D_MODEL, D_FF)),
                  _const_spec((D_FF, D_MODEL))],
        out_specs=pl.BlockSpec((tm, D_MODEL), lambda i: (i, 0)),
        out_shape=jax.ShapeDtypeStruct((t, D_MODEL), F32),
        compiler_params=_params(1),
        name="mlp",
    )(x1, g_mlp, wu, wd)


def _block_diag_ones(width, head):
    idx = jnp.arange(width) // head
    return (idx[:, None] == idx[None, :]).astype(BF16)


def _t5_bucket_table():
    max_exact = REL_BUCKETS // 2
    t_loc = jnp.arange(BLOCK)[:, None] + BLOCK
    s_loc = jnp.arange(2 * BLOCK)[None, :]
    dist = t_loc - s_loc
    d = jnp.maximum(dist, 0)
    df = jnp.maximum(d, 1).astype(F32)
    large = max_exact + (jnp.log(df / max_exact) / math.log(REL_MAX_DIST / max_exact)
                         * (REL_BUCKETS - max_exact)).astype(jnp.int32)
    large = jnp.minimum(large, REL_BUCKETS - 1)
    bucket = jnp.where(d < max_exact, d, large)
    band = (dist >= 0) & (dist < WINDOW)
    return jnp.where(band, bucket, -1).astype(jnp.int32)


def _aug_placement():
    place = [[0.0] * (2 * FOX_W) for _ in range(LANES)]
    ones = [0.0] * (2 * FOX_W)
    for h in range(FOX_HEADS):
        base = (h // 2) * LANES + (h % 2) * AUG_PER_HEAD
        for part in range(3):
            src = part * FOX_HEADS + h
            place[src][base + part] = 1.0
            place[src][FOX_W + base + 3 + part] = 1.0
            ones[base + 3 + part] = 1.0
            ones[FOX_W + base + part] = 1.0
    return (jnp.array(place, BF16),
            jnp.broadcast_to(jnp.array(ones, F32)[None, :], (8, 2 * FOX_W)))


def _tile_gain(g, reps, scale=1.0):
    return (jnp.tile(g.astype(F32), reps) * scale)[None, :]


def _pick_tile(n, target):
    t = min(n, target)
    while n % t:
        t //= 2
    return t


def kernel(x, mem, g_mix, w_in, b_gate, b_forget, qn_swa, kn_swa, sink_swa, rel_bias,
           qn_fox, kn_fox, g_mem, w_mem_kv, qn_mem, kn_mem, w_o_swa, w_o_fox, w_o_mem,
           w_out, g_mlp, w_mlp_up, w_mlp_down):
    batch, seq, _ = x.shape
    n_layers = w_in.shape[0]
    t = batch * seq
    tm = _pick_tile(seq, 512)
    tq = _pick_tile(seq, 512)

    bd64 = _block_diag_ones(SWA_Q, HEAD64)
    bd128 = _block_diag_ones(MEM_W, MEM_HEAD_DIM)
    bucket = _t5_bucket_table()
    place, ones_row = _aug_placement()
    tri = (jnp.arange(tm)[:, None] >= jnp.arange(tm)[None, :]).astype(BF16)

    x2 = x.reshape(t, D_MODEL)
    mem2 = mem.reshape(batch * N_MEM, D_MODEL)
    for layer in range(n_layers):
        w = w_in[layer]
        o = 0
        parts = {}
        for name, width in (("qa", SWA_Q), ("ka", SWA_KV), ("va", SWA_KV), ("qf", FOX_W),
                            ("kf", FOX_W), ("vf", FOX_W), ("fl", FOX_HEADS), ("qm", MEM_W),
                            ("g", GATE_W)):
            parts[name] = w[:, o:o + width]
            o += width

        def dup(m):
            return jnp.concatenate([m[:, :HEAD64], m[:, :HEAD64], m[:, HEAD64:], m[:, HEAD64:]],
                                   axis=1)

        w1 = jnp.concatenate(
            [parts["qa"], dup(parts["ka"]), dup(parts["va"]), parts["qf"], parts["kf"],
             parts["qm"], parts["g"],
             jnp.pad(parts["fl"], ((0, 0), (0, LANES - FOX_HEADS)))], axis=1).astype(BF16)

        mk, mv = _memkv(mem2, g_mem[layer][None, :], w_mem_kv[layer].astype(BF16), bd128,
                        _tile_gain(kn_mem[layer], MEM_HEADS))

        qa, ka, va, qf, kf, vt, qm, gates, fl = _proj(
            x2, g_mix[layer][None, :], w1, parts["vf"].T.astype(BF16), b_gate[layer][None, :],
            bd64, bd128,
            _tile_gain(qn_swa[layer], SWA_HEADS, HEAD64 ** -0.5),
            _tile_gain(kn_swa[layer], 2 * SWA_KV_HEADS),
            _tile_gain(qn_fox[layer], FOX_HEADS, HEAD64 ** -0.5 * LOG2E),
            _tile_gain(kn_fox[layer], FOX_HEADS),
            _tile_gain(qn_mem[layer], MEM_HEADS, MEM_HEAD_DIM ** -0.5), tm, seq)

        bf_pad = jnp.pad(b_forget[layer].astype(F32), (0, LANES - FOX_HEADS))[None, :]
        aq, ak = _cum(fl, bf_pad, tri, place, ones_row, tm, seq)

        oa, om = _local(bucket, rel_bias.astype(F32), sink_swa[layer].astype(F32),
                        qa, ka, va, qm, mk, mv, batch, seq, tq)
        of = _fox(qf, aq, kf, ak, vt, batch, seq, tq)

        x1 = _merge(x2, oa, of, om, gates, w_o_swa[layer].astype(BF16),
                    w_o_fox[layer].astype(BF16), w_o_mem[layer].astype(BF16),
                    w_out[layer].astype(BF16), tm)
        x2 = _mlp(x1, g_mlp[layer][None, :], w_mlp_up[layer].astype(BF16),
                  w_mlp_down[layer].astype(BF16), tm)
    return x2.reshape(batch, seq, D_MODEL)
```

```python
import functools
import math

import jax
import jax.numpy as jnp
from jax import lax
from jax.experimental import pallas as pl
from jax.experimental.pallas import tpu as pltpu

F32 = jnp.float32
BF16 = jnp.bfloat16

D_MODEL = 1024
N_MEM = 256
SWA_HEADS = 8
SWA_KV_HEADS = 2
SWA_GROUP = SWA_HEADS // SWA_KV_HEADS
HEAD64 = 64
WINDOW = 128
FOX_HEADS = 8
MEM_HEADS = 4
MEM_HEAD_DIM = 128
D_FF = 4 * D_MODEL
REL_BUCKETS = 32
REL_MAX_DIST = 128
BLOCK = 128
EPS = 1e-6
NEG = -1e30
LOG2E = math.log2(math.e)

LANES = 128
MXU_DIM = 256
VMEM_LIMIT = 56 * 1024 * 1024

SWA_Q = SWA_HEADS * HEAD64
SWA_KV = SWA_KV_HEADS * HEAD64
FOX_W = FOX_HEADS * HEAD64
MEM_W = MEM_HEADS * MEM_HEAD_DIM
GATE_W = 3 * D_MODEL

C_QA = 0
C_KA = C_QA + SWA_Q
C_VA = C_KA + 2 * SWA_KV
C_QF = C_VA + 2 * SWA_KV
C_KF = C_QF + FOX_W
C_QM = C_KF + FOX_W
C_G = C_QM + MEM_W
C_FL = C_G + GATE_W
C_END = C_FL + LANES

AUG_PER_HEAD = 6
FOX_SUB = 2


def _dot(a, b):
    return jnp.dot(a, b, preferred_element_type=F32)


def _dot_nt(a, b):
    return lax.dot_general(a, b, (((1,), (1,)), ((), ())), preferred_element_type=F32)


def _rms_rows(x, gain):
    ms = jnp.mean(x * x, axis=-1, keepdims=True)
    return x * lax.rsqrt(ms + EPS) * gain


def _head_rms(y, bd, gain, head_dim):
    y2 = (y * y).astype(BF16)
    width = y.shape[1]
    ss = jnp.concatenate([_dot(y2[:, c:c + MXU_DIM], bd) for c in range(0, width, MXU_DIM)],
                         axis=1)
    return y * lax.rsqrt(ss * (1.0 / head_dim) + EPS) * gain


def _const_spec(shape):
    return pl.BlockSpec(shape, lambda *_: (0,) * len(shape))


def _params(n_axes, flags=None):
    return pltpu.CompilerParams(dimension_semantics=("arbitrary",) * n_axes,
                                vmem_limit_bytes=VMEM_LIMIT, flags=flags)


def _memkv_kernel(mem_ref, g_ref, w_ref, bd_ref, kn_ref, mk_ref, mv_ref):
    h = _rms_rows(mem_ref[...], g_ref[...]).astype(BF16)
    kv = _dot(h, w_ref[...])
    mk = _head_rms(kv[:, :MEM_W], bd_ref[...], kn_ref[...], MEM_HEAD_DIM)
    mk_ref[...] = mk.astype(BF16)
    mv_ref[...] = kv[:, MEM_W:].astype(BF16)


def _memkv(mem2, g_mem, w_kv, bd128, kn_t):
    rows = mem2.shape[0]
    return pl.pallas_call(
        _memkv_kernel,
        grid=(rows // N_MEM,),
        in_specs=[pl.BlockSpec((N_MEM, D_MODEL), lambda i: (i, 0)),
                  _const_spec((1, D_MODEL)),
                  _const_spec((D_MODEL, 2 * MEM_W)),
                  _const_spec((MXU_DIM, MXU_DIM)),
                  _const_spec((1, MEM_W))],
        out_specs=[pl.BlockSpec((N_MEM, MEM_W), lambda i: (i, 0)),
                   pl.BlockSpec((N_MEM, MEM_W), lambda i: (i, 0))],
        out_shape=[jax.ShapeDtypeStruct((rows, MEM_W), BF16)] * 2,
        compiler_params=_params(1),
        name="memkv",
    )(mem2, g_mem, w_kv, bd128, kn_t)


def _proj_kernel(x_ref, g_ref, w_ref, wvt_ref, bg_ref, bd64_ref, bd128_ref,
                 gqa_ref, gka_ref, gqf_ref, gkf_ref, gqm_ref,
                 qa_ref, ka_ref, va_ref, qf_ref, kf_ref, vt_ref, qm_ref, gate_ref, fl_ref):
    h = _rms_rows(x_ref[...], g_ref[...]).astype(BF16)

    def seg(lo, width):
        return _dot(h, w_ref[:, lo:lo + width])

    bd64 = bd64_ref[...]
    qa_ref[...] = _head_rms(seg(C_QA, SWA_Q), bd64, gqa_ref[...], HEAD64).astype(BF16)
    ka_ref[...] = _head_rms(seg(C_KA, 2 * SWA_KV), bd64, gka_ref[...], HEAD64).astype(BF16)
    va_ref[...] = seg(C_VA, 2 * SWA_KV).astype(BF16)
    qf_ref[...] = _head_rms(seg(C_QF, FOX_W), bd64, gqf_ref[...], HEAD64).astype(BF16)
    kf_ref[...] = _head_rms(seg(C_KF, FOX_W), bd64, gkf_ref[...], HEAD64).astype(BF16)
    vt = _dot_nt(wvt_ref[...], h).astype(BF16)
    for p in range(FOX_HEADS // 2):
        vt_ref[p] = vt[p * LANES:(p + 1) * LANES, :]
    qm_ref[...] = _head_rms(seg(C_QM, MEM_W), bd128_ref[...], gqm_ref[...],
                            MEM_HEAD_DIM).astype(BF16)
    for j in range(3):
        z = seg(C_G + j * D_MODEL, D_MODEL) + bg_ref[:, j * D_MODEL:(j + 1) * D_MODEL]
        gate_ref[:, j * D_MODEL:(j + 1) * D_MODEL] = (1.0 / (1.0 + jnp.exp(-z))).astype(BF16)
    fl_ref[...] = seg(C_FL, LANES)


def _proj(x2, g_mix, w1, wvt, b_gate, bd64, bd128, gqa, gka, gqf, gkf, gqm, tm, seq):
    t = x2.shape[0]
    nt = seq // tm
    pairs = FOX_HEADS // 2

    def rows(w, dtype=BF16):
        return jax.ShapeDtypeStruct((t, w), dtype), pl.BlockSpec((tm, w), lambda i: (i, 0))

    outs = [rows(SWA_Q), rows(2 * SWA_KV), rows(2 * SWA_KV), rows(FOX_W), rows(FOX_W),
            (jax.ShapeDtypeStruct((t // seq, pairs, nt, LANES, tm), BF16),
             pl.BlockSpec((None, pairs, None, LANES, tm), lambda i: (i // nt, 0, i % nt, 0, 0))),
            rows(MEM_W), rows(GATE_W), rows(LANES, F32)]
    out_shape = [o[0] for o in outs]
    out_specs = [o[1] for o in outs]
    return pl.pallas_call(
        _proj_kernel,
        grid=(t // tm,),
        in_specs=[pl.BlockSpec((tm, D_MODEL), lambda i: (i, 0)),
                  _const_spec((1, D_MODEL)),
                  _const_spec((D_MODEL, C_END)),
                  _const_spec((FOX_W, D_MODEL)),
                  _const_spec((1, GATE_W)),
                  _const_spec((MXU_DIM, MXU_DIM)),
                  _const_spec((MXU_DIM, MXU_DIM)),
                  _const_spec((1, SWA_Q)),
                  _const_spec((1, 2 * SWA_KV)),
                  _const_spec((1, FOX_W)),
                  _const_spec((1, FOX_W)),
                  _const_spec((1, MEM_W))],
        out_specs=out_specs,
        out_shape=out_shape,
        compiler_params=_params(1),
        name="proj",
    )(x2, g_mix, w1, wvt, b_gate, bd64, bd128, gqa, gka, gqf, gkf, gqm)


def _split3(c):
    hi = c.astype(BF16)
    r1 = c - hi.astype(F32)
    mid = r1.astype(BF16)
    lo = (r1 - mid.astype(F32)).astype(BF16)
    return hi, mid, lo


def _pack3(x):
    hi, mid, lo = _split3(x)
    return (hi.astype(F32) + pltpu.roll(mid.astype(F32), FOX_HEADS, 1)
            + pltpu.roll(lo.astype(F32), 2 * FOX_HEADS, 1)).astype(BF16)


def _cum_kernel(fl_ref, bf_ref, tri_ref, place_ref, ones_ref, aq_ref, ak_ref, carry_ref,
                *, tiles_per_seq):
    i = pl.program_id(0)

    @pl.when(i % tiles_per_seq == 0)
    def _():
        carry_ref[...] = jnp.zeros_like(carry_ref)

    lane = lax.broadcasted_iota(jnp.int32, fl_ref.shape, 1)
    heads = lane < FOX_HEADS
    z = fl_ref[...] + bf_ref[...]
    logf = jnp.where(heads, jnp.minimum(z, 0.0) - jnp.log(1.0 + jnp.exp(-jnp.abs(z))), 0.0)
    part = _dot(tri_ref[...], _pack3(logf))
    c = ((part + pltpu.roll(part, LANES - FOX_HEADS, 1))
         + pltpu.roll(part, LANES - 2 * FOX_HEADS, 1)) + carry_ref[0:1, :]
    tm = c.shape[0]
    carry_ref[...] = jnp.broadcast_to(c[tm - 1:tm, :], carry_ref.shape)

    c = jnp.where(heads, c * LOG2E, 0.0)
    placed = _dot(_pack3(c), place_ref[...])
    aq_ref[...] = (placed[:, :LANES] + ones_ref[0:1, :LANES]).astype(BF16)
    ak_ref[...] = (ones_ref[0:1, LANES:] - placed[:, LANES:]).astype(BF16)


def _cum(fl, bf_pad, tri, place, ones_row, tm, seq):
    t = fl.shape[0]
    return pl.pallas_call(
        functools.partial(_cum_kernel, tiles_per_seq=seq // tm),
        grid=(t // tm,),
        in_specs=[pl.BlockSpec((tm, LANES), lambda i: (i, 0)),
                  _const_spec((1, LANES)),
                  _const_spec((tm, tm)),
                  _const_spec((LANES, 2 * LANES)),
                  _const_spec((8, 2 * LANES))],
        out_specs=[pl.BlockSpec((tm, LANES), lambda i: (i, 0)),
                   pl.BlockSpec((tm, LANES), lambda i: (i, 0))],
        out_shape=[jax.ShapeDtypeStruct((t, LANES), BF16)] * 2,
        scratch_shapes=[pltpu.VMEM((8, LANES), F32)],
        compiler_params=_params(1),
        name="cum",
    )(fl, bf_pad, tri, place, ones_row)


def _local_kernel(bucket_ref, relb_ref, sink_ref,
                  qa_ref, kc_ref, kp_ref, vc_ref, vp_ref, qm_ref, mk_ref, mv_ref,
                  oa_ref, om_ref, tbl_ref, *, tq):
    b = pl.program_id(0)
    i = pl.program_id(1)

    @pl.when(jnp.logical_and(b == 0, i == 0))
    def _():
        bucket = bucket_ref[...]
        band = bucket >= 0
        for h in range(SWA_HEADS):
            t = jnp.zeros(bucket.shape, F32)
            for k in range(REL_BUCKETS):
                t = jnp.where(bucket == k, relb_ref[k, h], t)
            tbl_ref[h] = jnp.where(band, t, NEG)

    lane = lax.broadcasted_iota(jnp.int32, (BLOCK, LANES), 1)
    lower = lane < HEAD64
    col = lax.broadcasted_iota(jnp.int32, (BLOCK, 2 * BLOCK), 1)
    pad_cols = jnp.logical_and(col < BLOCK, i == 0)

    for j in range(tq // BLOCK):
        r0 = j * BLOCK
        if j == 0:
            kwin = jnp.concatenate([kp_ref[...], kc_ref[0:BLOCK, :]], axis=0)
            vwin = jnp.concatenate([vp_ref[...], vc_ref[0:BLOCK, :]], axis=0)
        else:
            kwin = kc_ref[r0 - BLOCK:r0 + BLOCK, :]
            vwin = vc_ref[r0 - BLOCK:r0 + BLOCK, :]
        for hb in range(SWA_HEADS // 2):
            qblk = qa_ref[r0:r0 + BLOCK, hb * LANES:(hb + 1) * LANES]
            outs = []
            for half in range(2):
                h = 2 * hb + half
                g = h // SWA_GROUP
                qh = jnp.where(lower if half == 0 else jnp.logical_not(lower), qblk, 0)
                s = _dot_nt(qh, kwin[:, g * LANES:(g + 1) * LANES]) + tbl_ref[h]
                if j == 0:
                    s = jnp.where(pad_cols, NEG, s)
                sink = sink_ref[h]
                m = jnp.maximum(jnp.max(s, axis=-1, keepdims=True), sink)
                p = jnp.exp(s - m)
                denom = jnp.sum(p, axis=-1, keepdims=True) + jnp.exp(sink - m)
                o = _dot(p.astype(BF16), vwin[:, g * LANES:(g + 1) * LANES])
                outs.append(o / denom)
            oa_ref[r0:r0 + BLOCK, hb * LANES:(hb + 1) * LANES] = (
                jnp.where(lower, outs[0], outs[1]).astype(BF16))

    for h in range(MEM_HEADS):
        sl = slice(h * MEM_HEAD_DIM, (h + 1) * MEM_HEAD_DIM)
        s = _dot_nt(qm_ref[:, sl], mk_ref[:, sl])
        m = jnp.max(s, axis=-1, keepdims=True)
        p = jnp.exp(s - m)
        denom = jnp.sum(p, axis=-1, keepdims=True)
        o = _dot(p.astype(BF16), mv_ref[:, sl])
        om_ref[:, sl] = (o / denom).astype(BF16)


def _local(bucket, rel_bias, sinks, qa, ka, va, qm, mk, mv, batch, seq, tq):
    t = qa.shape[0]
    nq = seq // tq
    sub = tq // BLOCK

    def cur(b, i):
        return (b * nq + i, 0)

    def prev(b, i):
        return (jnp.maximum((b * nq + i) * sub - 1, 0), 0)

    smem = pl.BlockSpec(memory_space=pltpu.SMEM)
    return pl.pallas_call(
        functools.partial(_local_kernel, tq=tq),
        grid=(batch, nq),
        in_specs=[_const_spec((BLOCK, 2 * BLOCK)), smem, smem,
                  pl.BlockSpec((tq, SWA_Q), cur),
                  pl.BlockSpec((tq, 2 * SWA_KV), cur),
                  pl.BlockSpec((BLOCK, 2 * SWA_KV), prev),
                  pl.BlockSpec((tq, 2 * SWA_KV), cur),
                  pl.BlockSpec((BLOCK, 2 * SWA_KV), prev),
                  pl.BlockSpec((tq, MEM_W), cur),
                  pl.BlockSpec((N_MEM, MEM_W), lambda b, i: (b, 0)),
                  pl.BlockSpec((N_MEM, MEM_W), lambda b, i: (b, 0))],
        out_specs=[pl.BlockSpec((tq, SWA_Q), cur),
                   pl.BlockSpec((tq, MEM_W), cur)],
        out_shape=[jax.ShapeDtypeStruct((t, SWA_Q), BF16),
                   jax.ShapeDtypeStruct((t, MEM_W), BF16)],
        scratch_shapes=[pltpu.VMEM((SWA_HEADS, BLOCK, 2 * BLOCK), F32)],
        compiler_params=_params(2),
        name="local",
    )(bucket, rel_bias, sinks, qa, ka, ka, va, va, qm, mk, mv)


def _fox_kernel(q_ref, aq_ref, k_ref, ak_ref, vt_ref, o_ref, sa_ref, sb_ref, acc_ref, m_ref,
                *, tk):
    pair = pl.program_id(1)
    g = pl.program_id(2)
    lane = lax.broadcasted_iota(jnp.int32, (tk, LANES), 1)
    half = [lane < HEAD64, lane >= HEAD64]
    qcat = []
    for sub in range(FOX_SUB):
        q = q_ref[sub * tk:(sub + 1) * tk, :]
        aq = aq_ref[sub * tk:(sub + 1) * tk, :]
        zero = jnp.zeros_like(q)
        per_head = []
        for h in range(2):
            lo = (2 * pair + h) * AUG_PER_HEAD
            mine = jnp.logical_and(lane >= lo, lane < lo + AUG_PER_HEAD)
            per_head.append(jnp.concatenate([jnp.where(half[h], q, zero),
                                             jnp.where(mine, aq, zero)], axis=1))
        qcat.append(per_head)
    ones = jnp.ones((16, tk), BF16)
    acc_ref[...] = jnp.zeros_like(acc_ref)
    m_ref[...] = jnp.full(m_ref.shape, NEG, F32)
    key = lax.broadcasted_iota(jnp.int32, (tk, tk), 0)
    qry = lax.broadcasted_iota(jnp.int32, (tk, tk), 1)
    causal = key <= qry

    def qk(sub, j, s_ref):
        r0 = pl.multiple_of(j * tk, tk)
        kcat = jnp.concatenate([k_ref[pl.ds(r0, tk), :], ak_ref[pl.ds(r0, tk), :]], axis=1)
        for h in range(2):
            s_ref[h] = _dot_nt(kcat, qcat[sub][h])

    def softmax_pv(sub, j, s_ref, masked):
        for h in range(2):
            u = 2 * sub + h
            s = s_ref[h]
            if masked:
                s = jnp.where(causal, s, NEG)
            m_prev = m_ref[u]
            m_next = jnp.maximum(m_prev, jnp.max(s, axis=0, keepdims=True))
            alpha = jnp.exp2(m_prev - m_next)
            p = jnp.exp2(s - m_next).astype(BF16)
            lhs = jnp.concatenate([vt_ref[j, h * HEAD64:(h + 1) * HEAD64, :], ones], axis=0)
            acc_ref[u] = alpha * acc_ref[u] + _dot(lhs, p)
            m_ref[u] = m_next

    qk(0, 0, sa_ref)

    def body(t, carry):
        j = 2 * t
        qk(1, j, sb_ref)
        softmax_pv(0, j, sa_ref, False)
        qk(0, j + 1, sa_ref)
        softmax_pv(1, j, sb_ref, False)
        qk(1, j + 1, sb_ref)
        softmax_pv(0, j + 1, sa_ref, False)
        qk(0, j + 2, sa_ref)
        softmax_pv(1, j + 1, sb_ref, False)
        return carry

    lax.fori_loop(0, g, body, 0)
    j = 2 * g
    qk(1, j, sb_ref)
    softmax_pv(0, j, sa_ref, True)
    qk(1, j + 1, sa_ref)
    softmax_pv(1, j, sb_ref, False)
    softmax_pv(1, j + 1, sa_ref, True)

    for sub in range(FOX_SUB):
        halves = []
        for h in range(2):
            acc = acc_ref[2 * sub + h]
            halves.append(acc[:HEAD64, :] / acc[HEAD64:HEAD64 + 1, :])
        o_ref[sub * tk:(sub + 1) * tk, :] = jnp.concatenate(halves, axis=0).T.astype(BF16)


def _fox(qf, aq, kf, ak, vt, batch, seq, tk):
    t = qf.shape[0]
    tq = FOX_SUB * tk
    assert seq % tq == 0, (seq, tq)
    nq = seq // tq
    pairs = FOX_HEADS // 2
    return pl.pallas_call(
        functools.partial(_fox_kernel, tk=tk),
        grid=(batch, pairs, nq),
        in_specs=[pl.BlockSpec((tq, LANES), lambda b, p, i: (b * nq + i, p)),
                  pl.BlockSpec((tq, LANES), lambda b, p, i: (b * nq + i, 0)),
                  pl.BlockSpec((seq, LANES), lambda b, p, i: (b, p)),
                  pl.BlockSpec((seq, LANES), lambda b, p, i: (b, 0)),
                  pl.BlockSpec((None, None, seq // tk, LANES, tk),
                               lambda b, p, i: (b, p, 0, 0, 0))],
        out_specs=pl.BlockSpec((tq, LANES), lambda b, p, i: (b * nq + i, p)),
        out_shape=jax.ShapeDtypeStruct((t, FOX_W), BF16),
        scratch_shapes=[pltpu.VMEM((2, tk, tk), F32),
                        pltpu.VMEM((2, tk, tk), F32),
                        pltpu.VMEM((2 * FOX_SUB, HEAD64 + 16, tk), F32),
                        pltpu.VMEM((2 * FOX_SUB, 1, tk), F32)],
        compiler_params=_params(3),
        name="fox",
    )(qf, aq, kf, ak, vt)


def _merge_kernel(x_ref, oa_ref, of_ref, om_ref, gate_ref, wa_ref, wf_ref, wm_ref, wo_ref,
                  y_ref):
    ya = _dot(oa_ref[...], wa_ref[...])
    yf = _dot(of_ref[...], wf_ref[...])
    ym = _dot(om_ref[...], wm_ref[...])
    merged = (gate_ref[:, 0:D_MODEL].astype(F32) * ya
              + gate_ref[:, D_MODEL:2 * D_MODEL].astype(F32) * yf
              + gate_ref[:, 2 * D_MODEL:].astype(F32) * ym)
    y_ref[...] = x_ref[...] + _dot(merged.astype(BF16), wo_ref[...])


def _merge(x2, oa, of, om, gates, wa, wf, wm, wo, tm):
    t = x2.shape[0]
    row = lambda w: pl.BlockSpec((tm, w), lambda i: (i, 0))
    return pl.pallas_call(
        _merge_kernel,
        grid=(t // tm,),
        in_specs=[row(D_MODEL), row(SWA_Q), row(FOX_W), row(MEM_W), row(GATE_W),
                  _const_spec((SWA_Q, D_MODEL)), _const_spec((FOX_W, D_MODEL)),
                  _const_spec((MEM_W, D_MODEL)), _const_spec((D_MODEL, D_MODEL))],
        out_specs=row(D_MODEL),
        out_shape=jax.ShapeDtypeStruct((t, D_MODEL), F32),
        compiler_params=_params(1),
        name="merge",
    )(x2, oa, of, om, gates, wa, wf, wm, wo)


def _mlp_kernel(x_ref, g_ref, wu_ref, wd_ref, y_ref, *, chunk):
    x = x_ref[...]
    h = _rms_rows(x, g_ref[...]).astype(BF16)
    acc = x
    for c in range(D_FF // chunk):
        u = jnp.maximum(_dot(h, wu_ref[:, c * chunk:(c + 1) * chunk]), 0.0)
        acc = acc + _dot((u * u).astype(BF16), wd_ref[c * chunk:(c + 1) * chunk, :])
    y_ref[...] = acc


def _mlp(x1, g_mlp, wu, wd, tm, chunk=1024):
    t = x1.shape[0]
    return pl.pallas_call(
        functools.partial(_mlp_kernel, chunk=chunk),
        grid=(t // tm,),
        in_specs=[pl.BlockSpec((tm, D_MODEL), lambda i: (i, 0)),
                  _const_spec((1, D_MODEL)),
                  _const_spec((D_MODEL, D_FF)),
                  _const_spec((D_FF, D_MODEL))],
        out_specs=pl.BlockSpec((tm, D_MODEL), lambda i: (i, 0)),
        out_shape=jax.ShapeDtypeStruct((t, D_MODEL), F32),
        compiler_params=_params(1),
        name="mlp",
    )(x1, g_mlp, wu, wd)


def _block_diag_ones(width, head):
    idx = jnp.arange(width) // head
    return (idx[:, None] == idx[None, :]).astype(BF16)


def _t5_bucket_table():
    max_exact = REL_BUCKETS // 2
    t_loc = jnp.arange(BLOCK)[:, None] + BLOCK
    s_loc = jnp.arange(2 * BLOCK)[None, :]
    dist = t_loc - s_loc
    d = jnp.maximum(dist, 0)
    df = jnp.maximum(d, 1).astype(F32)
    large = max_exact + (jnp.log(df / max_exact) / math.log(REL_MAX_DIST / max_exact)
                         * (REL_BUCKETS - max_exact)).astype(jnp.int32)
    large = jnp.minimum(large, REL_BUCKETS - 1)
    bucket = jnp.where(d < max_exact, d, large)
    band = (dist >= 0) & (dist < WINDOW)
    return jnp.where(band, bucket, -1).astype(jnp.int32)


def _aug_placement():
    place = [[0.0] * (2 * LANES) for _ in range(LANES)]
    ones = [0.0] * (2 * LANES)
    for h in range(FOX_HEADS):
        base = h * AUG_PER_HEAD
        for part in range(3):
            src = part * FOX_HEADS + h
            place[src][base + part] = 1.0
            place[src][LANES + base + 3 + part] = 1.0
            ones[base + 3 + part] = 1.0
            ones[LANES + base + part] = 1.0
    return (jnp.array(place, BF16),
            jnp.broadcast_to(jnp.array(ones, F32)[None, :], (8, 2 * LANES)))


def _tile_gain(g, reps, scale=1.0):
    return (jnp.tile(g.astype(F32), reps) * scale)[None, :]


def _pick_tile(n, target):
    t = min(n, target)
    while n % t:
        t //= 2
    return t


def kernel(x, mem, g_mix, w_in, b_gate, b_forget, qn_swa, kn_swa, sink_swa, rel_bias,
           qn_fox, kn_fox, g_mem, w_mem_kv, qn_mem, kn_mem, w_o_swa, w_o_fox, w_o_mem,
           w_out, g_mlp, w_mlp_up, w_mlp_down):
    batch, seq, _ = x.shape
    n_layers = w_in.shape[0]
    t = batch * seq
    tm = _pick_tile(seq, 512)
    tq = _pick_tile(seq, 512)

    bd64 = _block_diag_ones(MXU_DIM, HEAD64)
    bd128 = _block_diag_ones(MXU_DIM, MEM_HEAD_DIM)
    bucket = _t5_bucket_table()
    place, ones_row = _aug_placement()
    tri = (jnp.arange(tm)[:, None] >= jnp.arange(tm)[None, :]).astype(BF16)

    x2 = x.reshape(t, D_MODEL)
    mem2 = mem.reshape(batch * N_MEM, D_MODEL)
    for layer in range(n_layers):
        w = w_in[layer]
        o = 0
        parts = {}
        for name, width in (("qa", SWA_Q), ("ka", SWA_KV), ("va", SWA_KV), ("qf", FOX_W),
                            ("kf", FOX_W), ("vf", FOX_W), ("fl", FOX_HEADS), ("qm", MEM_W),
                            ("g", GATE_W)):
            parts[name] = w[:, o:o + width]
            o += width

        def dup(m):
            return jnp.concatenate([m[:, :HEAD64], m[:, :HEAD64], m[:, HEAD64:], m[:, HEAD64:]],
                                   axis=1)

        w1 = jnp.concatenate(
            [parts["qa"], dup(parts["ka"]), dup(parts["va"]), parts["qf"], parts["kf"],
             parts["qm"], parts["g"],
             jnp.pad(parts["fl"], ((0, 0), (0, LANES - FOX_HEADS)))], axis=1).astype(BF16)

        mk, mv = _memkv(mem2, g_mem[layer][None, :], w_mem_kv[layer].astype(BF16), bd128,
                        _tile_gain(kn_mem[layer], MEM_HEADS))

        qa, ka, va, qf, kf, vt, qm, gates, fl = _proj(
            x2, g_mix[layer][None, :], w1, parts["vf"].T.astype(BF16), b_gate[layer][None, :],
            bd64, bd128,
            _tile_gain(qn_swa[layer], SWA_HEADS, HEAD64 ** -0.5),
            _tile_gain(kn_swa[layer], 2 * SWA_KV_HEADS),
            _tile_gain(qn_fox[layer], FOX_HEADS, HEAD64 ** -0.5 * LOG2E),
            _tile_gain(kn_fox[layer], FOX_HEADS),
            _tile_gain(qn_mem[layer], MEM_HEADS, MEM_HEAD_DIM ** -0.5), tm, seq)

        bf_pad = jnp.pad(b_forget[layer].astype(F32), (0, LANES - FOX_HEADS))[None, :]
        aq, ak = _cum(fl, bf_pad, tri, place, ones_row, tm, seq)

        oa, om = _local(bucket, rel_bias.astype(F32), sink_swa[layer].astype(F32),
                        qa, ka, va, qm, mk, mv, batch, seq, tq)
        of = _fox(qf, aq, kf, ak, vt, batch, seq, tm)

        x1 = _merge(x2, oa, of, om, gates, w_o_swa[layer].astype(BF16),
                    w_o_fox[layer].astype(BF16), w_o_mem[layer].astype(BF16),
                    w_out[layer].astype(BF16), tm)
        x2 = _mlp(x1, g_mlp[layer][None, :], w_mlp_up[layer].astype(BF16),
                  w_mlp_down[layer].astype(BF16), tm)
    return x2.reshape(batch, seq, D_MODEL)
```

```python
import functools
import math

import jax
import jax.numpy as jnp
from jax import lax
from jax.experimental import pallas as pl
from jax.experimental.pallas import tpu as pltpu

F32 = jnp.float32
BF16 = jnp.bfloat16

D_MODEL = 1024
N_MEM = 256
SWA_HEADS = 8
SWA_KV_HEADS = 2
SWA_GROUP = SWA_HEADS // SWA_KV_HEADS
HEAD64 = 64
WINDOW = 128
FOX_HEADS = 8
MEM_HEADS = 4
MEM_HEAD_DIM = 128
D_FF = 4 * D_MODEL
REL_BUCKETS = 32
REL_MAX_DIST = 128
BLOCK = 128
EPS = 1e-6
NEG = -1e30
LOG2E = math.log2(math.e)

LANES = 128
MXU_DIM = 256
VMEM_LIMIT = 56 * 1024 * 1024

SWA_Q = SWA_HEADS * HEAD64
SWA_KV = SWA_KV_HEADS * HEAD64
FOX_W = FOX_HEADS * HEAD64
MEM_W = MEM_HEADS * MEM_HEAD_DIM
GATE_W = 3 * D_MODEL

C_QA = 0
C_KA = C_QA + SWA_Q
C_VA = C_KA + 2 * SWA_KV
C_QF = C_VA + 2 * SWA_KV
C_KF = C_QF + FOX_W
C_QM = C_KF + FOX_W
C_G = C_QM + MEM_W
C_FL = C_G + GATE_W
C_END = C_FL + LANES

AUG_PER_HEAD = 7
FOX_FAST_MAX_SHIFT = 60.0
FOX_SUB = 2


def _dot(a, b):
    return jnp.dot(a, b, preferred_element_type=F32)


def _dot_nt(a, b):
    return lax.dot_general(a, b, (((1,), (1,)), ((), ())), preferred_element_type=F32)


def _rms_rows(x, gain):
    ms = jnp.mean(x * x, axis=-1, keepdims=True)
    return x * lax.rsqrt(ms + EPS) * gain


def _head_rms(y, bd, gain, head_dim):
    y2 = (y * y).astype(BF16)
    width = y.shape[1]
    ss = jnp.concatenate([_dot(y2[:, c:c + MXU_DIM], bd) for c in range(0, width, MXU_DIM)],
                         axis=1)
    return y * lax.rsqrt(ss * (1.0 / head_dim) + EPS) * gain


def _const_spec(shape):
    return pl.BlockSpec(shape, lambda *_: (0,) * len(shape))


def _params(n_axes, flags=None):
    return pltpu.CompilerParams(dimension_semantics=("arbitrary",) * n_axes,
                                vmem_limit_bytes=VMEM_LIMIT, flags=flags)


def _memkv_kernel(mem_ref, g_ref, w_ref, bd_ref, kn_ref, mk_ref, mv_ref):
    h = _rms_rows(mem_ref[...], g_ref[...]).astype(BF16)
    kv = _dot(h, w_ref[...])
    mk = _head_rms(kv[:, :MEM_W], bd_ref[...], kn_ref[...], MEM_HEAD_DIM)
    mk_ref[...] = mk.astype(BF16)
    mv_ref[...] = kv[:, MEM_W:].astype(BF16)


def _memkv(mem2, g_mem, w_kv, bd128, kn_t):
    rows = mem2.shape[0]
    return pl.pallas_call(
        _memkv_kernel,
        grid=(rows // N_MEM,),
        in_specs=[pl.BlockSpec((N_MEM, D_MODEL), lambda i: (i, 0)),
                  _const_spec((1, D_MODEL)),
                  _const_spec((D_MODEL, 2 * MEM_W)),
                  _const_spec((MXU_DIM, MXU_DIM)),
                  _const_spec((1, MEM_W))],
        out_specs=[pl.BlockSpec((N_MEM, MEM_W), lambda i: (i, 0)),
                   pl.BlockSpec((N_MEM, MEM_W), lambda i: (i, 0))],
        out_shape=[jax.ShapeDtypeStruct((rows, MEM_W), BF16)] * 2,
        compiler_params=_params(1),
        name="memkv",
    )(mem2, g_mem, w_kv, bd128, kn_t)


def _proj_kernel(x_ref, g_ref, w_ref, wvt_ref, bg_ref, bd64_ref, bd128_ref,
                 gqa_ref, gka_ref, gqf_ref, gkf_ref, gqm_ref,
                 qa_ref, ka_ref, va_ref, qf_ref, kf_ref, vt_ref, qm_ref, gate_ref, fl_ref):
    h = _rms_rows(x_ref[...], g_ref[...]).astype(BF16)

    def seg(lo, width):
        return _dot(h, w_ref[:, lo:lo + width])

    bd64 = bd64_ref[...]
    qa_ref[...] = _head_rms(seg(C_QA, SWA_Q), bd64, gqa_ref[...], HEAD64).astype(BF16)
    ka_ref[...] = _head_rms(seg(C_KA, 2 * SWA_KV), bd64, gka_ref[...], HEAD64).astype(BF16)
    va_ref[...] = seg(C_VA, 2 * SWA_KV).astype(BF16)
    qf_ref[...] = _head_rms(seg(C_QF, FOX_W), bd64, gqf_ref[...], HEAD64).astype(BF16)
    kf_ref[...] = _head_rms(seg(C_KF, FOX_W), bd64, gkf_ref[...], HEAD64).astype(BF16)
    vt = _dot_nt(wvt_ref[...], h).astype(BF16)
    for p in range(FOX_HEADS // 2):
        vt_ref[p] = vt[p * LANES:(p + 1) * LANES, :]
    qm_ref[...] = _head_rms(seg(C_QM, MEM_W), bd128_ref[...], gqm_ref[...],
                            MEM_HEAD_DIM).astype(BF16)
    for j in range(3):
        z = seg(C_G + j * D_MODEL, D_MODEL) + bg_ref[:, j * D_MODEL:(j + 1) * D_MODEL]
        gate_ref[:, j * D_MODEL:(j + 1) * D_MODEL] = (1.0 / (1.0 + jnp.exp(-z))).astype(BF16)
    fl_ref[...] = seg(C_FL, LANES)


def _proj(x2, g_mix, w1, wvt, b_gate, bd64, bd128, gqa, gka, gqf, gkf, gqm, tm, seq):
    t = x2.shape[0]
    nt = seq // tm
    pairs = FOX_HEADS // 2

    def rows(w, dtype=BF16):
        return jax.ShapeDtypeStruct((t, w), dtype), pl.BlockSpec((tm, w), lambda i: (i, 0))

    outs = [rows(SWA_Q), rows(2 * SWA_KV), rows(2 * SWA_KV), rows(FOX_W), rows(FOX_W),
            (jax.ShapeDtypeStruct((t // seq, pairs, nt, LANES, tm), BF16),
             pl.BlockSpec((None, pairs, None, LANES, tm), lambda i: (i // nt, 0, i % nt, 0, 0))),
            rows(MEM_W), rows(GATE_W), rows(LANES, F32)]
    out_shape = [o[0] for o in outs]
    out_specs = [o[1] for o in outs]
    return pl.pallas_call(
        _proj_kernel,
        grid=(t // tm,),
        in_specs=[pl.BlockSpec((tm, D_MODEL), lambda i: (i, 0)),
                  _const_spec((1, D_MODEL)),
                  _const_spec((D_MODEL, C_END)),
                  _const_spec((FOX_W, D_MODEL)),
                  _const_spec((1, GATE_W)),
                  _const_spec((MXU_DIM, MXU_DIM)),
                  _const_spec((MXU_DIM, MXU_DIM)),
                  _const_spec((1, SWA_Q)),
                  _const_spec((1, 2 * SWA_KV)),
                  _const_spec((1, FOX_W)),
                  _const_spec((1, FOX_W)),
                  _const_spec((1, MEM_W))],
        out_specs=out_specs,
        out_shape=out_shape,
        compiler_params=_params(1),
        name="proj",
    )(x2, g_mix, w1, wvt, b_gate, bd64, bd128, gqa, gka, gqf, gkf, gqm)


def _split3(c):
    hi = c.astype(BF16)
    r1 = c - hi.astype(F32)
    mid = r1.astype(BF16)
    lo = (r1 - mid.astype(F32)).astype(BF16)
    return hi, mid, lo


def _pack3(x):
    hi, mid, lo = _split3(x)
    return (hi.astype(F32) + pltpu.roll(mid.astype(F32), FOX_HEADS, 1)
            + pltpu.roll(lo.astype(F32), 2 * FOX_HEADS, 1)).astype(BF16)


def _cum_kernel(fl_ref, bf_ref, tri_ref, place_ref, const_ref, aq_ref, ak_ref, carry_ref,
                *, tiles_per_seq):
    i = pl.program_id(0)

    @pl.when(i % tiles_per_seq == 0)
    def _():
        carry_ref[...] = jnp.zeros_like(carry_ref)

    lane = lax.broadcasted_iota(jnp.int32, fl_ref.shape, 1)
    heads = lane < FOX_HEADS
    z = fl_ref[...] + bf_ref[...]
    logf = jnp.where(heads, jnp.minimum(z, 0.0) - jnp.log(1.0 + jnp.exp(-jnp.abs(z))), 0.0)
    part = _dot(tri_ref[...], _pack3(logf))
    c = ((part + pltpu.roll(part, LANES - FOX_HEADS, 1))
         + pltpu.roll(part, LANES - 2 * FOX_HEADS, 1)) + carry_ref[0:1, :]
    tm = c.shape[0]
    carry_ref[...] = jnp.broadcast_to(c[tm - 1:tm, :], carry_ref.shape)

    c = jnp.where(heads, c * LOG2E, 0.0)
    placed = _dot(_pack3(c), place_ref[...])
    aq_ref[...] = (placed[:, :LANES] + const_ref[0:1, :LANES]).astype(BF16)
    ak_ref[...] = (const_ref[0:1, LANES:] - placed[:, LANES:]).astype(BF16)


def _cum(fl, bf_pad, tri, place, ones_row, tm, seq):
    t = fl.shape[0]
    return pl.pallas_call(
        functools.partial(_cum_kernel, tiles_per_seq=seq // tm),
        grid=(t // tm,),
        in_specs=[pl.BlockSpec((tm, LANES), lambda i: (i, 0)),
                  _const_spec((1, LANES)),
                  _const_spec((tm, tm)),
                  _const_spec((LANES, 2 * LANES)),
                  _const_spec((8, 2 * LANES))],
        out_specs=[pl.BlockSpec((tm, LANES), lambda i: (i, 0)),
                   pl.BlockSpec((tm, LANES), lambda i: (i, 0))],
        out_shape=[jax.ShapeDtypeStruct((t, LANES), BF16)] * 2,
        scratch_shapes=[pltpu.VMEM((8, LANES), F32)],
        compiler_params=_params(1),
        name="cum",
    )(fl, bf_pad, tri, place, ones_row)


def _local_kernel(bucket_ref, relb_ref, sink_ref,
                  qa_ref, kc_ref, kp_ref, vc_ref, vp_ref, qm_ref, mk_ref, mv_ref,
                  oa_ref, om_ref, tbl_ref, *, tq):
    b = pl.program_id(0)
    i = pl.program_id(1)

    @pl.when(jnp.logical_and(b == 0, i == 0))
    def _():
        bucket = bucket_ref[...]
        band = bucket >= 0
        for h in range(SWA_HEADS):
            t = jnp.zeros(bucket.shape, F32)
            for k in range(REL_BUCKETS):
                t = jnp.where(bucket == k, relb_ref[k, h], t)
            tbl_ref[h] = jnp.where(band, t, NEG)

    lane = lax.broadcasted_iota(jnp.int32, (BLOCK, LANES), 1)
    lower = lane < HEAD64
    col = lax.broadcasted_iota(jnp.int32, (BLOCK, 2 * BLOCK), 1)
    pad_cols = jnp.logical_and(col < BLOCK, i == 0)

    for j in range(tq // BLOCK):
        r0 = j * BLOCK
        if j == 0:
            kwin = jnp.concatenate([kp_ref[...], kc_ref[0:BLOCK, :]], axis=0)
            vwin = jnp.concatenate([vp_ref[...], vc_ref[0:BLOCK, :]], axis=0)
        else:
            kwin = kc_ref[r0 - BLOCK:r0 + BLOCK, :]
            vwin = vc_ref[r0 - BLOCK:r0 + BLOCK, :]
        for hb in range(SWA_HEADS // 2):
            qblk = qa_ref[r0:r0 + BLOCK, hb * LANES:(hb + 1) * LANES]
            outs = []
            for half in range(2):
                h = 2 * hb + half
                g = h // SWA_GROUP
                qh = jnp.where(lower if half == 0 else jnp.logical_not(lower), qblk, 0)
                s = _dot_nt(qh, kwin[:, g * LANES:(g + 1) * LANES]) + tbl_ref[h]
                if j == 0:
                    s = jnp.where(pad_cols, NEG, s)
                sink = sink_ref[h]
                m = jnp.maximum(jnp.max(s, axis=-1, keepdims=True), sink)
                p = jnp.exp(s - m)
                denom = jnp.sum(p, axis=-1, keepdims=True) + jnp.exp(sink - m)
                o = _dot(p.astype(BF16), vwin[:, g * LANES:(g + 1) * LANES])
                outs.append(o / denom)
            oa_ref[r0:r0 + BLOCK, hb * LANES:(hb + 1) * LANES] = (
                jnp.where(lower, outs[0], outs[1]).astype(BF16))

    for h in range(MEM_HEADS):
        sl = slice(h * MEM_HEAD_DIM, (h + 1) * MEM_HEAD_DIM)
        s = _dot_nt(qm_ref[:, sl], mk_ref[:, sl])
        m = jnp.max(s, axis=-1, keepdims=True)
        p = jnp.exp(s - m)
        denom = jnp.sum(p, axis=-1, keepdims=True)
        o = _dot(p.astype(BF16), mv_ref[:, sl])
        om_ref[:, sl] = (o / denom).astype(BF16)


def _local(bucket, rel_bias, sinks, qa, ka, va, qm, mk, mv, batch, seq, tq):
    t = qa.shape[0]
    nq = seq // tq
    sub = tq // BLOCK

    def cur(b, i):
        return (b * nq + i, 0)

    def prev(b, i):
        return (jnp.maximum((b * nq + i) * sub - 1, 0), 0)

    smem = pl.BlockSpec(memory_space=pltpu.SMEM)
    return pl.pallas_call(
        functools.partial(_local_kernel, tq=tq),
        grid=(batch, nq),
        in_specs=[_const_spec((BLOCK, 2 * BLOCK)), smem, smem,
                  pl.BlockSpec((tq, SWA_Q), cur),
                  pl.BlockSpec((tq, 2 * SWA_KV), cur),
                  pl.BlockSpec((BLOCK, 2 * SWA_KV), prev),
                  pl.BlockSpec((tq, 2 * SWA_KV), cur),
                  pl.BlockSpec((BLOCK, 2 * SWA_KV), prev),
                  pl.BlockSpec((tq, MEM_W), cur),
                  pl.BlockSpec((N_MEM, MEM_W), lambda b, i: (b, 0)),
                  pl.BlockSpec((N_MEM, MEM_W), lambda b, i: (b, 0))],
        out_specs=[pl.BlockSpec((tq, SWA_Q), cur),
                   pl.BlockSpec((tq, MEM_W), cur)],
        out_shape=[jax.ShapeDtypeStruct((t, SWA_Q), BF16),
                   jax.ShapeDtypeStruct((t, MEM_W), BF16)],
        scratch_shapes=[pltpu.VMEM((SWA_HEADS, BLOCK, 2 * BLOCK), F32)],
        compiler_params=_params(2),
        name="local",
    )(bucket, rel_bias, sinks, qa, ka, ka, va, va, qm, mk, mv)


def _fox_kernel(fast_ref, q_ref, aq_ref, k_ref, ak_ref, vt_ref, o_ref,
                sa_ref, sb_ref, acc_ref, m_ref, *, tk):
    pair = pl.program_id(1)
    g = pl.program_id(2)
    lane = lax.broadcasted_iota(jnp.int32, (tk, LANES), 1)
    half = [lane < HEAD64, lane >= HEAD64]
    qcat = []
    for sub in range(FOX_SUB):
        q = q_ref[sub * tk:(sub + 1) * tk, :]
        aq = aq_ref[sub * tk:(sub + 1) * tk, :]
        zero = jnp.zeros_like(q)
        per_head = []
        for h in range(2):
            lo = (2 * pair + h) * AUG_PER_HEAD
            mine = jnp.logical_and(lane >= lo, lane < lo + AUG_PER_HEAD)
            per_head.append(jnp.concatenate([jnp.where(half[h], q, zero),
                                             jnp.where(mine, aq, zero)], axis=1))
        qcat.append(per_head)
    ones = jnp.ones((16, tk), BF16)
    acc_ref[...] = jnp.zeros_like(acc_ref)
    key = lax.broadcasted_iota(jnp.int32, (tk, tk), 0)
    qry = lax.broadcasted_iota(jnp.int32, (tk, tk), 1)
    causal = key <= qry

    def kcat(j):
        r0 = pl.multiple_of(j * tk, tk)
        return jnp.concatenate([k_ref[pl.ds(r0, tk), :], ak_ref[pl.ds(r0, tk), :]], axis=1)

    def vt_ones(j, h):
        return jnp.concatenate([vt_ref[j, h * HEAD64:(h + 1) * HEAD64, :], ones], axis=0)

    def qk(sub, j, s_ref):
        kc = kcat(j)
        for h in range(2):
            for c in range(0, tk, MXU_DIM):
                s_ref[h, :, c:c + MXU_DIM] = _dot_nt(kc, qcat[sub][h][c:c + MXU_DIM, :])

    def scores(s_ref, h, c, masked):
        s = s_ref[h, :, c:c + MXU_DIM]
        return jnp.where(causal[:, c:c + MXU_DIM], s, NEG) if masked else s

    def pipeline(softmax_pv):
        qk(0, 0, sa_ref)

        def body(t, carry):
            j = 2 * t
            qk(1, j, sb_ref)
            softmax_pv(0, j, sa_ref, False)
            qk(0, j + 1, sa_ref)
            softmax_pv(1, j, sb_ref, False)
            qk(1, j + 1, sb_ref)
            softmax_pv(0, j + 1, sa_ref, False)
            qk(0, j + 2, sa_ref)
            softmax_pv(1, j + 1, sb_ref, False)
            return carry

        lax.fori_loop(0, g, body, 0)
        j = 2 * g
        qk(1, j, sb_ref)
        softmax_pv(0, j, sa_ref, True)
        qk(1, j + 1, sa_ref)
        softmax_pv(1, j, sb_ref, False)
        softmax_pv(1, j + 1, sa_ref, True)

    @pl.when(fast_ref[0] == 1)
    def _():
        def softmax_pv(sub, j, s_ref, masked):
            for h in range(2):
                for c in range(0, tk, MXU_DIM):
                    p = jnp.exp2(scores(s_ref, h, c, masked)).astype(BF16)
                    acc_ref[2 * sub + h, :, c:c + MXU_DIM] += _dot(vt_ones(j, h), p)

        pipeline(softmax_pv)

    @pl.when(fast_ref[0] == 0)
    def _():
        m_ref[...] = jnp.full(m_ref.shape, NEG, F32)

        def softmax_pv(sub, j, s_ref, masked):
            for h in range(2):
                u = 2 * sub + h
                for c in range(0, tk, MXU_DIM):
                    s = scores(s_ref, h, c, masked)
                    m_prev = m_ref[u, :, c:c + MXU_DIM]
                    m_next = jnp.maximum(m_prev, jnp.max(s, axis=0, keepdims=True))
                    alpha = jnp.exp2(m_prev - m_next)
                    p = jnp.exp2(s - m_next).astype(BF16)
                    acc_ref[u, :, c:c + MXU_DIM] = (alpha * acc_ref[u, :, c:c + MXU_DIM]
                                                    + _dot(vt_ones(j, h), p))
                    m_ref[u, :, c:c + MXU_DIM] = m_next

        pipeline(softmax_pv)

    for sub in range(FOX_SUB):
        halves = []
        for h in range(2):
            acc = acc_ref[2 * sub + h]
            halves.append(acc[:HEAD64, :] / acc[HEAD64:HEAD64 + 1, :])
        o_ref[sub * tk:(sub + 1) * tk, :] = jnp.concatenate(halves, axis=0).T.astype(BF16)


def _fox(fast, qf, aq, kf, ak, vt, batch, seq, tk):
    t = qf.shape[0]
    tq = FOX_SUB * tk
    assert seq % tq == 0, (seq, tq)
    nq = seq // tq
    pairs = FOX_HEADS // 2
    return pl.pallas_call(
        functools.partial(_fox_kernel, tk=tk),
        grid=(batch, pairs, nq),
        in_specs=[pl.BlockSpec(memory_space=pltpu.SMEM),
                  pl.BlockSpec((tq, LANES), lambda b, p, i: (b * nq + i, p)),
                  pl.BlockSpec((tq, LANES), lambda b, p, i: (b * nq + i, 0)),
                  pl.BlockSpec((seq, LANES), lambda b, p, i: (b, p)),
                  pl.BlockSpec((seq, LANES), lambda b, p, i: (b, 0)),
                  pl.BlockSpec((None, None, seq // tk, LANES, tk),
                               lambda b, p, i: (b, p, 0, 0, 0))],
        out_specs=pl.BlockSpec((tq, LANES), lambda b, p, i: (b * nq + i, p)),
        out_shape=jax.ShapeDtypeStruct((t, FOX_W), BF16),
        scratch_shapes=[pltpu.VMEM((2, tk, tk), F32),
                        pltpu.VMEM((2, tk, tk), F32),
                        pltpu.VMEM((2 * FOX_SUB, HEAD64 + 16, tk), F32),
                        pltpu.VMEM((2 * FOX_SUB, 1, tk), F32)],
        compiler_params=_params(3),
        name="fox",
    )(fast, qf, aq, kf, ak, vt)


def _merge_kernel(x_ref, oa_ref, of_ref, om_ref, gate_ref, wa_ref, wf_ref, wm_ref, wo_ref,
                  y_ref):
    ya = _dot(oa_ref[...], wa_ref[...])
    yf = _dot(of_ref[...], wf_ref[...])
    ym = _dot(om_ref[...], wm_ref[...])
    merged = (gate_ref[:, 0:D_MODEL].astype(F32) * ya
              + gate_ref[:, D_MODEL:2 * D_MODEL].astype(F32) * yf
              + gate_ref[:, 2 * D_MODEL:].astype(F32) * ym)
    y_ref[...] = x_ref[...] + _dot(merged.astype(BF16), wo_ref[...])


def _merge(x2, oa, of, om, gates, wa, wf, wm, wo, tm):
    t = x2.shape[0]
    row = lambda w: pl.BlockSpec((tm, w), lambda i: (i, 0))
    return pl.pallas_call(
        _merge_kernel,
        grid=(t // tm,),
        in_specs=[row(D_MODEL), row(SWA_Q), row(FOX_W), row(MEM_W), row(GATE_W),
                  _const_spec((SWA_Q, D_MODEL)), _const_spec((FOX_W, D_MODEL)),
                  _const_spec((MEM_W, D_MODEL)), _const_spec((D_MODEL, D_MODEL))],
        out_specs=row(D_MODEL),
        out_shape=jax.ShapeDtypeStruct((t, D_MODEL), F32),
        compiler_params=_params(1),
        name="merge",
    )(x2, oa, of, om, gates, wa, wf, wm, wo)


def _mlp_kernel(x_ref, g_ref, wu_ref, wd_ref, y_ref, *, chunk):
    x = x_ref[...]
    h = _rms_rows(x, g_ref[...]).astype(BF16)
    acc = x
    for c in range(D_FF // chunk):
        u = jnp.maximum(_dot(h, wu_ref[:, c * chunk:(c + 1) * chunk]), 0.0)
        acc = acc + _dot((u * u).astype(BF16), wd_ref[c * chunk:(c + 1) * chunk, :])
    y_ref[...] = acc


def _mlp(x1, g_mlp, wu, wd, tm, chunk=1024):
    t = x1.shape[0]
    return pl.pallas_call(
        functools.partial(_mlp_kernel, chunk=chunk),
        grid=(t // tm,),
        in_specs=[pl.BlockSpec((tm, D_MODEL), lambda i: (i, 0)),
                  _const_spec((1, D_MODEL)),
                  _const_spec((D_MODEL, D_FF)),
                  _const_spec((D_FF, D_MODEL))],
        out_specs=pl.BlockSpec((tm, D_MODEL), lambda i: (i, 0)),
        out_shape=jax.ShapeDtypeStruct((t, D_MODEL), F32),
        compiler_params=_params(1),
        name="mlp",
    )(x1, g_mlp, wu, wd)


def _block_diag_ones(width, head):
    idx = jnp.arange(width) // head
    return (idx[:, None] == idx[None, :]).astype(BF16)


def _t5_bucket_table():
    max_exact = REL_BUCKETS // 2
    t_loc = jnp.arange(BLOCK)[:, None] + BLOCK
    s_loc = jnp.arange(2 * BLOCK)[None, :]
    dist = t_loc - s_loc
    d = jnp.maximum(dist, 0)
    df = jnp.maximum(d, 1).astype(F32)
    large = max_exact + (jnp.log(df / max_exact) / math.log(REL_MAX_DIST / max_exact)
                         * (REL_BUCKETS - max_exact)).astype(jnp.int32)
    large = jnp.minimum(large, REL_BUCKETS - 1)
    bucket = jnp.where(d < max_exact, d, large)
    band = (dist >= 0) & (dist < WINDOW)
    return jnp.where(band, bucket, -1).astype(jnp.int32)


def _aug_placement(shift):
    place = [[0.0] * (2 * LANES) for _ in range(LANES)]
    ones = [0.0] * (2 * LANES)
    shift_lanes = []
    for h in range(FOX_HEADS):
        base = h * AUG_PER_HEAD
        for part in range(3):
            src = part * FOX_HEADS + h
            place[src][base + part] = 1.0
            place[src][LANES + base + 3 + part] = 1.0
            ones[base + 3 + part] = 1.0
            ones[LANES + base + part] = 1.0
        ones[base + 6] = 1.0
        shift_lanes.append(LANES + base + 6)
    consts = jnp.array(ones, F32).at[jnp.array(shift_lanes)].set(-shift)
    return jnp.array(place, BF16), jnp.broadcast_to(consts[None, :], (8, 2 * LANES))


def _tile_gain(g, reps, scale=1.0):
    return (jnp.tile(g.astype(F32), reps) * scale)[None, :]


def _pick_tile(n, target):
    t = min(n, target)
    while n % t:
        t //= 2
    return t


def kernel(x, mem, g_mix, w_in, b_gate, b_forget, qn_swa, kn_swa, sink_swa, rel_bias,
           qn_fox, kn_fox, g_mem, w_mem_kv, qn_mem, kn_mem, w_o_swa, w_o_fox, w_o_mem,
           w_out, g_mlp, w_mlp_up, w_mlp_down):
    batch, seq, _ = x.shape
    n_layers = w_in.shape[0]
    t = batch * seq
    tm = _pick_tile(seq, 512)
    tq = _pick_tile(seq, 512)

    bd64 = _block_diag_ones(MXU_DIM, HEAD64)
    bd128 = _block_diag_ones(MXU_DIM, MEM_HEAD_DIM)
    bucket = _t5_bucket_table()
    tri = (jnp.arange(tm)[:, None] >= jnp.arange(tm)[None, :]).astype(BF16)

    x2 = x.reshape(t, D_MODEL)
    mem2 = mem.reshape(batch * N_MEM, D_MODEL)
    for layer in range(n_layers):
        w = w_in[layer]
        o = 0
        parts = {}
        for name, width in (("qa", SWA_Q), ("ka", SWA_KV), ("va", SWA_KV), ("qf", FOX_W),
                            ("kf", FOX_W), ("vf", FOX_W), ("fl", FOX_HEADS), ("qm", MEM_W),
                            ("g", GATE_W)):
            parts[name] = w[:, o:o + width]
            o += width

        def dup(m):
            return jnp.concatenate([m[:, :HEAD64], m[:, :HEAD64], m[:, HEAD64:], m[:, HEAD64:]],
                                   axis=1)

        w1 = jnp.concatenate(
            [parts["qa"], dup(parts["ka"]), dup(parts["va"]), parts["qf"], parts["kf"],
             parts["qm"], parts["g"],
             jnp.pad(parts["fl"], ((0, 0), (0, LANES - FOX_HEADS)))], axis=1).astype(BF16)

        mk, mv = _memkv(mem2, g_mem[layer][None, :], w_mem_kv[layer].astype(BF16), bd128,
                        _tile_gain(kn_mem[layer], MEM_HEADS))

        qa, ka, va, qf, kf, vt, qm, gates, fl = _proj(
            x2, g_mix[layer][None, :], w1, parts["vf"].T.astype(BF16), b_gate[layer][None, :],
            bd64, bd128,
            _tile_gain(qn_swa[layer], SWA_HEADS, HEAD64 ** -0.5),
            _tile_gain(kn_swa[layer], 2 * SWA_KV_HEADS),
            _tile_gain(qn_fox[layer], FOX_HEADS, HEAD64 ** -0.5 * LOG2E),
            _tile_gain(kn_fox[layer], FOX_HEADS),
            _tile_gain(qn_mem[layer], MEM_HEADS, MEM_HEAD_DIM ** -0.5), tm, seq)

        bound = (HEAD64 ** 0.5 * LOG2E * 1.02 * jnp.max(jnp.abs(qn_fox[layer]))
                 * jnp.max(jnp.abs(kn_fox[layer]))).astype(BF16).astype(F32)
        fast = bound <= FOX_FAST_MAX_SHIFT
        place, ones_row = _aug_placement(jnp.where(fast, bound, 0.0))
        bf_pad = jnp.pad(b_forget[layer].astype(F32), (0, LANES - FOX_HEADS))[None, :]
        aq, ak = _cum(fl, bf_pad, tri, place, ones_row, tm, seq)

        oa, om = _local(bucket, rel_bias.astype(F32), sink_swa[layer].astype(F32),
                        qa, ka, va, qm, mk, mv, batch, seq, tq)
        of = _fox(fast.astype(jnp.int32)[None], qf, aq, kf, ak, vt, batch, seq, tm)

        x1 = _merge(x2, oa, of, om, gates, w_o_swa[layer].astype(BF16),
                    w_o_fox[layer].astype(BF16), w_o_mem[layer].astype(BF16),
                    w_out[layer].astype(BF16), tm)
        x2 = _mlp(x1, g_mlp[layer][None, :], w_mlp_up[layer].astype(BF16),
                  w_mlp_down[layer].astype(BF16), tm)
    return x2.reshape(batch, seq, D_MODEL)
```

```python
import functools
import math

import jax
import jax.numpy as jnp
from jax import lax
from jax.experimental import pallas as pl
from jax.experimental.pallas import tpu as pltpu

F32 = jnp.float32
BF16 = jnp.bfloat16

D_MODEL = 1024
N_MEM = 256
SWA_HEADS = 8
SWA_KV_HEADS = 2
SWA_GROUP = SWA_HEADS // SWA_KV_HEADS
HEAD64 = 64
WINDOW = 128
FOX_HEADS = 8
MEM_HEADS = 4
MEM_HEAD_DIM = 128
D_FF = 4 * D_MODEL
REL_BUCKETS = 32
REL_MAX_DIST = 128
BLOCK = 128
EPS = 1e-6
NEG = -1e30
LOG2E = math.log2(math.e)

LANES = 128
MXU_DIM = 256
VMEM_LIMIT = 56 * 1024 * 1024

SWA_Q = SWA_HEADS * HEAD64
SWA_KV = SWA_KV_HEADS * HEAD64
FOX_W = FOX_HEADS * HEAD64
MEM_W = MEM_HEADS * MEM_HEAD_DIM
GATE_W = 3 * D_MODEL

C_QA = 0
C_KA = C_QA + SWA_Q
C_VA = C_KA + 2 * SWA_KV
C_QF = C_VA + 2 * SWA_KV
C_KF = C_QF + FOX_W
C_QM = C_KF + FOX_W
C_G = C_QM + MEM_W
C_FL = C_G + GATE_W
C_END = C_FL + LANES

AUG_PER_HEAD = 7
FOX_FAST_MAX_SHIFT = 60.0
FOX_SUB = 2


def _dot(a, b):
    return jnp.dot(a, b, preferred_element_type=F32)


def _dot_nt(a, b):
    return lax.dot_general(a, b, (((1,), (1,)), ((), ())), preferred_element_type=F32)


def _rms_rows(x, gain):
    ms = jnp.mean(x * x, axis=-1, keepdims=True)
    return x * lax.rsqrt(ms + EPS) * gain


def _head_rms(y, gain, head_dim):
    y2 = y * y
    lower = lax.broadcasted_iota(jnp.int32, (y.shape[0], LANES), 1) < HEAD64
    cols = []
    for c in range(0, y.shape[1], LANES):
        blk = y2[:, c:c + LANES]
        total = jnp.sum(blk, axis=-1, keepdims=True)
        if head_dim == LANES:
            cols.append(jnp.broadcast_to(total, blk.shape))
        else:
            low = jnp.sum(jnp.where(lower, blk, 0.0), axis=-1, keepdims=True)
            cols.append(jnp.where(lower, low, total - low))
    ss = jnp.concatenate(cols, axis=1)
    return y * lax.rsqrt(ss * (1.0 / head_dim) + EPS) * gain


def _const_spec(shape):
    return pl.BlockSpec(shape, lambda *_: (0,) * len(shape))


def _params(n_axes, flags=None):
    return pltpu.CompilerParams(dimension_semantics=("arbitrary",) * n_axes,
                                vmem_limit_bytes=VMEM_LIMIT, flags=flags)


def _memkv_kernel(mem_ref, g_ref, w_ref, kn_ref, mk_ref, mv_ref):
    h = _rms_rows(mem_ref[...], g_ref[...]).astype(BF16)
    kv = _dot(h, w_ref[...])
    mk = _head_rms(kv[:, :MEM_W], kn_ref[...], MEM_HEAD_DIM)
    mk_ref[...] = mk.astype(BF16)
    mv_ref[...] = kv[:, MEM_W:].astype(BF16)


def _memkv(mem2, g_mem, w_kv, kn_t):
    rows = mem2.shape[0]
    return pl.pallas_call(
        _memkv_kernel,
        grid=(rows // N_MEM,),
        in_specs=[pl.BlockSpec((N_MEM, D_MODEL), lambda i: (i, 0)),
                  _const_spec((1, D_MODEL)),
                  _const_spec((D_MODEL, 2 * MEM_W)),
                  _const_spec((1, MEM_W))],
        out_specs=[pl.BlockSpec((N_MEM, MEM_W), lambda i: (i, 0)),
                   pl.BlockSpec((N_MEM, MEM_W), lambda i: (i, 0))],
        out_shape=[jax.ShapeDtypeStruct((rows, MEM_W), BF16)] * 2,
        compiler_params=_params(1),
        name="memkv",
    )(mem2, g_mem, w_kv, kn_t)


def _proj_kernel(x_ref, g_ref, w_ref, wvt_ref, bg_ref,
                 gqa_ref, gka_ref, gqf_ref, gkf_ref, gqm_ref,
                 qa_ref, ka_ref, va_ref, qf_ref, kf_ref, vt_ref, qm_ref, gate_ref, fl_ref):
    h = _rms_rows(x_ref[...], g_ref[...]).astype(BF16)
    qkv = _dot(h, w_ref[:, :C_G])
    qa_ref[...] = _head_rms(qkv[:, C_QA:C_KA], gqa_ref[...], HEAD64).astype(BF16)
    ka_ref[...] = _head_rms(qkv[:, C_KA:C_VA], gka_ref[...], HEAD64).astype(BF16)
    va_ref[...] = qkv[:, C_VA:C_QF].astype(BF16)
    qf_ref[...] = _head_rms(qkv[:, C_QF:C_KF], gqf_ref[...], HEAD64).astype(BF16)
    kf_ref[...] = _head_rms(qkv[:, C_KF:C_QM], gkf_ref[...], HEAD64).astype(BF16)
    qm_ref[...] = _head_rms(qkv[:, C_QM:C_G], gqm_ref[...], MEM_HEAD_DIM).astype(BF16)
    vt = _dot_nt(wvt_ref[...], h).astype(BF16)
    for p in range(FOX_HEADS // 2):
        vt_ref[p] = vt[p * LANES:(p + 1) * LANES, :]
    zf = _dot(h, w_ref[:, C_G:C_END])
    z = zf[:, :GATE_W] + bg_ref[...]
    gate_ref[...] = (1.0 / (1.0 + jnp.exp(-z))).astype(BF16)
    fl_ref[...] = zf[:, GATE_W:]


def _proj(x2, g_mix, w1, wvt, b_gate, gqa, gka, gqf, gkf, gqm, tm, seq):
    t = x2.shape[0]
    nt = seq // tm
    pairs = FOX_HEADS // 2

    def rows(w, dtype=BF16):
        return jax.ShapeDtypeStruct((t, w), dtype), pl.BlockSpec((tm, w), lambda i: (i, 0))

    outs = [rows(SWA_Q), rows(2 * SWA_KV), rows(2 * SWA_KV), rows(FOX_W), rows(FOX_W),
            (jax.ShapeDtypeStruct((t // seq, pairs, nt, LANES, tm), BF16),
             pl.BlockSpec((None, pairs, None, LANES, tm), lambda i: (i // nt, 0, i % nt, 0, 0))),
            rows(MEM_W), rows(GATE_W), rows(LANES, F32)]
    out_shape = [o[0] for o in outs]
    out_specs = [o[1] for o in outs]
    return pl.pallas_call(
        _proj_kernel,
        grid=(t // tm,),
        in_specs=[pl.BlockSpec((tm, D_MODEL), lambda i: (i, 0)),
                  _const_spec((1, D_MODEL)),
                  _const_spec((D_MODEL, C_END)),
                  _const_spec((FOX_W, D_MODEL)),
                  _const_spec((1, GATE_W)),
                  _const_spec((1, SWA_Q)),
                  _const_spec((1, 2 * SWA_KV)),
                  _const_spec((1, FOX_W)),
                  _const_spec((1, FOX_W)),
                  _const_spec((1, MEM_W))],
        out_specs=out_specs,
        out_shape=out_shape,
        compiler_params=_params(1),
        name="proj",
    )(x2, g_mix, w1, wvt, b_gate, gqa, gka, gqf, gkf, gqm)


def _split3(c):
    hi = c.astype(BF16)
    r1 = c - hi.astype(F32)
    mid = r1.astype(BF16)
    lo = (r1 - mid.astype(F32)).astype(BF16)
    return hi, mid, lo


def _pack3(x):
    hi, mid, lo = _split3(x)
    return (hi.astype(F32) + pltpu.roll(mid.astype(F32), FOX_HEADS, 1)
            + pltpu.roll(lo.astype(F32), 2 * FOX_HEADS, 1)).astype(BF16)


def _cum_kernel(fl_ref, bf_ref, tri_ref, place_ref, const_ref, aq_ref, ak_ref, carry_ref,
                *, tiles_per_seq):
    i = pl.program_id(0)

    @pl.when(i % tiles_per_seq == 0)
    def _():
        carry_ref[...] = jnp.zeros_like(carry_ref)

    lane = lax.broadcasted_iota(jnp.int32, fl_ref.shape, 1)
    heads = lane < FOX_HEADS
    z = fl_ref[...] + bf_ref[...]
    logf = jnp.where(heads, jnp.minimum(z, 0.0) - jnp.log(1.0 + jnp.exp(-jnp.abs(z))), 0.0)
    part = _dot(tri_ref[...], _pack3(logf))
    c = ((part + pltpu.roll(part, LANES - FOX_HEADS, 1))
         + pltpu.roll(part, LANES - 2 * FOX_HEADS, 1)) + carry_ref[0:1, :]
    tm = c.shape[0]
    carry_ref[...] = jnp.broadcast_to(c[tm - 1:tm, :], carry_ref.shape)

    c = jnp.where(heads, c * LOG2E, 0.0)
    placed = _dot(_pack3(c), place_ref[...])
    aq_ref[...] = (placed[:, :LANES] + const_ref[0:1, :LANES]).astype(BF16)
    ak_ref[...] = (const_ref[0:1, LANES:] - placed[:, LANES:]).astype(BF16)


def _cum(fl, bf_pad, tri, place, ones_row, tm, seq):
    t = fl.shape[0]
    return pl.pallas_call(
        functools.partial(_cum_kernel, tiles_per_seq=seq // tm),
        grid=(t // tm,),
        in_specs=[pl.BlockSpec((tm, LANES), lambda i: (i, 0)),
                  _const_spec((1, LANES)),
                  _const_spec((tm, tm)),
                  _const_spec((LANES, 2 * LANES)),
                  _const_spec((8, 2 * LANES))],
        out_specs=[pl.BlockSpec((tm, LANES), lambda i: (i, 0)),
                   pl.BlockSpec((tm, LANES), lambda i: (i, 0))],
        out_shape=[jax.ShapeDtypeStruct((t, LANES), BF16)] * 2,
        scratch_shapes=[pltpu.VMEM((8, LANES), F32)],
        compiler_params=_params(1),
        name="cum",
    )(fl, bf_pad, tri, place, ones_row)


def _local_kernel(bucket_ref, relb_ref, sink_ref,
                  qa_ref, kc_ref, kp_ref, vc_ref, vp_ref, qm_ref, mk_ref, mv_ref,
                  oa_ref, om_ref, tbl_ref, *, tq):
    b = pl.program_id(0)
    i = pl.program_id(1)

    @pl.when(jnp.logical_and(b == 0, i == 0))
    def _():
        bucket = bucket_ref[...]
        band = bucket >= 0
        for h in range(SWA_HEADS):
            t = jnp.zeros(bucket.shape, F32)
            for k in range(REL_BUCKETS):
                t = jnp.where(bucket == k, relb_ref[k, h], t)
            tbl_ref[h] = jnp.where(band, t, NEG)

    lane = lax.broadcasted_iota(jnp.int32, (BLOCK, LANES), 1)
    lower = lane < HEAD64
    col = lax.broadcasted_iota(jnp.int32, (BLOCK, 2 * BLOCK), 1)
    pad_cols = jnp.logical_and(col < BLOCK, i == 0)

    for j in range(tq // BLOCK):
        r0 = j * BLOCK
        if j == 0:
            kwin = jnp.concatenate([kp_ref[...], kc_ref[0:BLOCK, :]], axis=0)
            vwin = jnp.concatenate([vp_ref[...], vc_ref[0:BLOCK, :]], axis=0)
        else:
            kwin = kc_ref[r0 - BLOCK:r0 + BLOCK, :]
            vwin = vc_ref[r0 - BLOCK:r0 + BLOCK, :]
        for hb in range(SWA_HEADS // 2):
            qblk = qa_ref[r0:r0 + BLOCK, hb * LANES:(hb + 1) * LANES]
            outs = []
            for half in range(2):
                h = 2 * hb + half
                g = h // SWA_GROUP
                qh = jnp.where(lower if half == 0 else jnp.logical_not(lower), qblk, 0)
                s = _dot_nt(qh, kwin[:, g * LANES:(g + 1) * LANES]) + tbl_ref[h]
                if j == 0:
                    s = jnp.where(pad_cols, NEG, s)
                sink = sink_ref[h]
                m = jnp.maximum(jnp.max(s, axis=-1, keepdims=True), sink)
                p = jnp.exp(s - m)
                denom = jnp.sum(p, axis=-1, keepdims=True) + jnp.exp(sink - m)
                o = _dot(p.astype(BF16), vwin[:, g * LANES:(g + 1) * LANES])
                outs.append(o / denom)
            oa_ref[r0:r0 + BLOCK, hb * LANES:(hb + 1) * LANES] = (
                jnp.where(lower, outs[0], outs[1]).astype(BF16))

    for h in range(MEM_HEADS):
        sl = slice(h * MEM_HEAD_DIM, (h + 1) * MEM_HEAD_DIM)
        s = _dot_nt(qm_ref[:, sl], mk_ref[:, sl])
        m = jnp.max(s, axis=-1, keepdims=True)
        p = jnp.exp(s - m)
        denom = jnp.sum(p, axis=-1, keepdims=True)
        o = _dot(p.astype(BF16), mv_ref[:, sl])
        om_ref[:, sl] = (o / denom).astype(BF16)


def _local(bucket, rel_bias, sinks, qa, ka, va, qm, mk, mv, batch, seq, tq):
    t = qa.shape[0]
    nq = seq // tq
    sub = tq // BLOCK

    def cur(b, i):
        return (b * nq + i, 0)

    def prev(b, i):
        return (jnp.maximum((b * nq + i) * sub - 1, 0), 0)

    smem = pl.BlockSpec(memory_space=pltpu.SMEM)
    return pl.pallas_call(
        functools.partial(_local_kernel, tq=tq),
        grid=(batch, nq),
        in_specs=[_const_spec((BLOCK, 2 * BLOCK)), smem, smem,
                  pl.BlockSpec((tq, SWA_Q), cur),
                  pl.BlockSpec((tq, 2 * SWA_KV), cur),
                  pl.BlockSpec((BLOCK, 2 * SWA_KV), prev),
                  pl.BlockSpec((tq, 2 * SWA_KV), cur),
                  pl.BlockSpec((BLOCK, 2 * SWA_KV), prev),
                  pl.BlockSpec((tq, MEM_W), cur),
                  pl.BlockSpec((N_MEM, MEM_W), lambda b, i: (b, 0)),
                  pl.BlockSpec((N_MEM, MEM_W), lambda b, i: (b, 0))],
        out_specs=[pl.BlockSpec((tq, SWA_Q), cur),
                   pl.BlockSpec((tq, MEM_W), cur)],
        out_shape=[jax.ShapeDtypeStruct((t, SWA_Q), BF16),
                   jax.ShapeDtypeStruct((t, MEM_W), BF16)],
        scratch_shapes=[pltpu.VMEM((SWA_HEADS, BLOCK, 2 * BLOCK), F32)],
        compiler_params=_params(2),
        name="local",
    )(bucket, rel_bias, sinks, qa, ka, ka, va, va, qm, mk, mv)


def _fox_kernel(fast_ref, q_ref, aq_ref, k_ref, ak_ref, vt_ref, o_ref,
                sa_ref, sb_ref, acc_ref, m_ref, *, tk):
    pair = pl.program_id(1)
    g = pl.program_id(2)
    lane = lax.broadcasted_iota(jnp.int32, (tk, LANES), 1)
    half = [lane < HEAD64, lane >= HEAD64]
    qcat = []
    for sub in range(FOX_SUB):
        q = q_ref[sub * tk:(sub + 1) * tk, :]
        aq = aq_ref[sub * tk:(sub + 1) * tk, :]
        zero = jnp.zeros_like(q)
        per_head = []
        for h in range(2):
            lo = (2 * pair + h) * AUG_PER_HEAD
            mine = jnp.logical_and(lane >= lo, lane < lo + AUG_PER_HEAD)
            per_head.append(jnp.concatenate([jnp.where(half[h], q, zero),
                                             jnp.where(mine, aq, zero)], axis=1))
        qcat.append(per_head)
    ones = jnp.ones((16, tk), BF16)
    acc_ref[...] = jnp.zeros_like(acc_ref)
    key = lax.broadcasted_iota(jnp.int32, (tk, tk), 0)
    qry = lax.broadcasted_iota(jnp.int32, (tk, tk), 1)
    causal = key <= qry

    def kcat(j):
        r0 = pl.multiple_of(j * tk, tk)
        return jnp.concatenate([k_ref[pl.ds(r0, tk), :], ak_ref[pl.ds(r0, tk), :]], axis=1)

    def vt_ones(j, h):
        return jnp.concatenate([vt_ref[j, h * HEAD64:(h + 1) * HEAD64, :], ones], axis=0)

    def qk(sub, j, s_ref):
        kc = kcat(j)
        for h in range(2):
            for c in range(0, tk, MXU_DIM):
                s_ref[h, :, c:c + MXU_DIM] = _dot_nt(kc, qcat[sub][h][c:c + MXU_DIM, :])

    def scores(s_ref, h, c, masked):
        s = s_ref[h, :, c:c + MXU_DIM]
        return jnp.where(causal[:, c:c + MXU_DIM], s, NEG) if masked else s

    def pipeline(softmax_pv):
        qk(0, 0, sa_ref)

        def body(t, carry):
            j = 2 * t
            qk(1, j, sb_ref)
            softmax_pv(0, j, sa_ref, False)
            qk(0, j + 1, sa_ref)
            softmax_pv(1, j, sb_ref, False)
            qk(1, j + 1, sb_ref)
            softmax_pv(0, j + 1, sa_ref, False)
            qk(0, j + 2, sa_ref)
            softmax_pv(1, j + 1, sb_ref, False)
            return carry

        lax.fori_loop(0, g, body, 0)
        j = 2 * g
        qk(1, j, sb_ref)
        softmax_pv(0, j, sa_ref, True)
        qk(1, j + 1, sa_ref)
        softmax_pv(1, j, sb_ref, False)
        softmax_pv(1, j + 1, sa_ref, True)

    @pl.when(fast_ref[0] == 1)
    def _():
        def softmax_pv(sub, j, s_ref, masked):
            for h in range(2):
                for c in range(0, tk, MXU_DIM):
                    p = jnp.exp2(scores(s_ref, h, c, masked)).astype(BF16)
                    acc_ref[2 * sub + h, :, c:c + MXU_DIM] += _dot(vt_ones(j, h), p)

        pipeline(softmax_pv)

    @pl.when(fast_ref[0] == 0)
    def _():
        m_ref[...] = jnp.full(m_ref.shape, NEG, F32)

        def softmax_pv(sub, j, s_ref, masked):
            for h in range(2):
                u = 2 * sub + h
                for c in range(0, tk, MXU_DIM):
                    s = scores(s_ref, h, c, masked)
                    m_prev = m_ref[u, :, c:c + MXU_DIM]
                    m_next = jnp.maximum(m_prev, jnp.max(s, axis=0, keepdims=True))
                    alpha = jnp.exp2(m_prev - m_next)
                    p = jnp.exp2(s - m_next).astype(BF16)
                    acc_ref[u, :, c:c + MXU_DIM] = (alpha * acc_ref[u, :, c:c + MXU_DIM]
                                                    + _dot(vt_ones(j, h), p))
                    m_ref[u, :, c:c + MXU_DIM] = m_next

        pipeline(softmax_pv)

    for sub in range(FOX_SUB):
        halves = []
        for h in range(2):
            acc = acc_ref[2 * sub + h]
            halves.append(acc[:HEAD64, :] / acc[HEAD64:HEAD64 + 1, :])
        o_ref[sub * tk:(sub + 1) * tk, :] = jnp.concatenate(halves, axis=0).T.astype(BF16)


def _fox(fast, qf, aq, kf, ak, vt, batch, seq, tk):
    t = qf.shape[0]
    tq = FOX_SUB * tk
    assert seq % tq == 0, (seq, tq)
    nq = seq // tq
    pairs = FOX_HEADS // 2
    return pl.pallas_call(
        functools.partial(_fox_kernel, tk=tk),
        grid=(batch, pairs, nq),
        in_specs=[pl.BlockSpec(memory_space=pltpu.SMEM),
                  pl.BlockSpec((tq, LANES), lambda b, p, i: (b * nq + i, p)),
                  pl.BlockSpec((tq, LANES), lambda b, p, i: (b * nq + i, 0)),
                  pl.BlockSpec((seq, LANES), lambda b, p, i: (b, p)),
                  pl.BlockSpec((seq, LANES), lambda b, p, i: (b, 0)),
                  pl.BlockSpec((None, None, seq // tk, LANES, tk),
                               lambda b, p, i: (b, p, 0, 0, 0))],
        out_specs=pl.BlockSpec((tq, LANES), lambda b, p, i: (b * nq + i, p)),
        out_shape=jax.ShapeDtypeStruct((t, FOX_W), BF16),
        scratch_shapes=[pltpu.VMEM((2, tk, tk), F32),
                        pltpu.VMEM((2, tk, tk), F32),
                        pltpu.VMEM((2 * FOX_SUB, HEAD64 + 16, tk), F32),
                        pltpu.VMEM((2 * FOX_SUB, 1, tk), F32)],
        compiler_params=_params(3),
        name="fox",
    )(fast, qf, aq, kf, ak, vt)


def _post_kernel(x_ref, oa_ref, of_ref, om_ref, gate_ref, wa_ref, wf_ref, wm_ref, wo_ref,
                 g_ref, wu_ref, wd_ref, y_ref, *, chunk):
    ya = _dot(oa_ref[...], wa_ref[...])
    yf = _dot(of_ref[...], wf_ref[...])
    ym = _dot(om_ref[...], wm_ref[...])
    merged = (gate_ref[:, 0:D_MODEL].astype(F32) * ya
              + gate_ref[:, D_MODEL:2 * D_MODEL].astype(F32) * yf
              + gate_ref[:, 2 * D_MODEL:].astype(F32) * ym)
    x1 = x_ref[...] + _dot(merged.astype(BF16), wo_ref[...])
    h = _rms_rows(x1, g_ref[...]).astype(BF16)
    acc = x1
    for c in range(0, D_FF, chunk):
        u = jnp.maximum(_dot(h, wu_ref[:, c:c + chunk]), 0.0)
        acc = acc + _dot((u * u).astype(BF16), wd_ref[c:c + chunk, :])
    y_ref[...] = acc


def _post(x2, oa, of, om, gates, wa, wf, wm, wo, g_mlp, wu, wd, tm, chunk=1024):
    t = x2.shape[0]
    row = lambda w: pl.BlockSpec((tm, w), lambda i: (i, 0))
    resident = lambda shape: pl.BlockSpec(shape, lambda i: (0,) * len(shape),
                                          pipeline_mode=pl.Buffered(1))
    return pl.pallas_call(
        functools.partial(_post_kernel, chunk=chunk),
        grid=(t // tm,),
        in_specs=[row(D_MODEL), row(SWA_Q), row(FOX_W), row(MEM_W), row(GATE_W),
                  resident((SWA_Q, D_MODEL)), resident((FOX_W, D_MODEL)),
                  resident((MEM_W, D_MODEL)), resident((D_MODEL, D_MODEL)),
                  resident((1, D_MODEL)), resident((D_MODEL, D_FF)), resident((D_FF, D_MODEL))],
        out_specs=row(D_MODEL),
        out_shape=jax.ShapeDtypeStruct((t, D_MODEL), F32),
        compiler_params=_params(1),
        name="post",
    )(x2, oa, of, om, gates, wa, wf, wm, wo, g_mlp, wu, wd)


def _t5_bucket_table():
    max_exact = REL_BUCKETS // 2
    t_loc = jnp.arange(BLOCK)[:, None] + BLOCK
    s_loc = jnp.arange(2 * BLOCK)[None, :]
    dist = t_loc - s_loc
    d = jnp.maximum(dist, 0)
    df = jnp.maximum(d, 1).astype(F32)
    large = max_exact + (jnp.log(df / max_exact) / math.log(REL_MAX_DIST / max_exact)
                         * (REL_BUCKETS - max_exact)).astype(jnp.int32)
    large = jnp.minimum(large, REL_BUCKETS - 1)
    bucket = jnp.where(d < max_exact, d, large)
    band = (dist >= 0) & (dist < WINDOW)
    return jnp.where(band, bucket, -1).astype(jnp.int32)


def _aug_placement(shift):
    place = [[0.0] * (2 * LANES) for _ in range(LANES)]
    ones = [0.0] * (2 * LANES)
    shift_lanes = []
    for h in range(FOX_HEADS):
        base = h * AUG_PER_HEAD
        for part in range(3):
            src = part * FOX_HEADS + h
            place[src][base + part] = 1.0
            place[src][LANES + base + 3 + part] = 1.0
            ones[base + 3 + part] = 1.0
            ones[LANES + base + part] = 1.0
        ones[base + 6] = 1.0
        shift_lanes.append(LANES + base + 6)
    consts = jnp.array(ones, F32).at[jnp.array(shift_lanes)].set(-shift)
    return jnp.array(place, BF16), jnp.broadcast_to(consts[None, :], (8, 2 * LANES))


def _tile_gain(g, reps, scale=1.0):
    return (jnp.tile(g.astype(F32), reps) * scale)[None, :]


def _pick_tile(n, target):
    t = min(n, target)
    while n % t:
        t //= 2
    return t


def kernel(x, mem, g_mix, w_in, b_gate, b_forget, qn_swa, kn_swa, sink_swa, rel_bias,
           qn_fox, kn_fox, g_mem, w_mem_kv, qn_mem, kn_mem, w_o_swa, w_o_fox, w_o_mem,
           w_out, g_mlp, w_mlp_up, w_mlp_down):
    batch, seq, _ = x.shape
    n_layers = w_in.shape[0]
    t = batch * seq
    tm = _pick_tile(seq, 512)
    tq = _pick_tile(seq, 512)

    bucket = _t5_bucket_table()
    tri = (jnp.arange(tm)[:, None] >= jnp.arange(tm)[None, :]).astype(BF16)

    x2 = x.reshape(t, D_MODEL)
    mem2 = mem.reshape(batch * N_MEM, D_MODEL)
    for layer in range(n_layers):
        w = w_in[layer]
        o = 0
        parts = {}
        for name, width in (("qa", SWA_Q), ("ka", SWA_KV), ("va", SWA_KV), ("qf", FOX_W),
                            ("kf", FOX_W), ("vf", FOX_W), ("fl", FOX_HEADS), ("qm", MEM_W),
                            ("g", GATE_W)):
            parts[name] = w[:, o:o + width]
            o += width

        def dup(m):
            return jnp.concatenate([m[:, :HEAD64], m[:, :HEAD64], m[:, HEAD64:], m[:, HEAD64:]],
                                   axis=1)

        w1 = jnp.concatenate(
            [parts["qa"], dup(parts["ka"]), dup(parts["va"]), parts["qf"], parts["kf"],
             parts["qm"], parts["g"],
             jnp.pad(parts["fl"], ((0, 0), (0, LANES - FOX_HEADS)))], axis=1).astype(BF16)

        mk, mv = _memkv(mem2, g_mem[layer][None, :], w_mem_kv[layer].astype(BF16),
                        _tile_gain(kn_mem[layer], MEM_HEADS))

        qa, ka, va, qf, kf, vt, qm, gates, fl = _proj(
            x2, g_mix[layer][None, :], w1, parts["vf"].T.astype(BF16), b_gate[layer][None, :],
            _tile_gain(qn_swa[layer], SWA_HEADS, HEAD64 ** -0.5),
            _tile_gain(kn_swa[layer], 2 * SWA_KV_HEADS),
            _tile_gain(qn_fox[layer], FOX_HEADS, HEAD64 ** -0.5 * LOG2E),
            _tile_gain(kn_fox[layer], FOX_HEADS),
            _tile_gain(qn_mem[layer], MEM_HEADS, MEM_HEAD_DIM ** -0.5), tm, seq)

        bound = (HEAD64 ** 0.5 * LOG2E * 1.02 * jnp.max(jnp.abs(qn_fox[layer]))
                 * jnp.max(jnp.abs(kn_fox[layer]))).astype(BF16).astype(F32)
        fast = bound <= FOX_FAST_MAX_SHIFT
        place, ones_row = _aug_placement(jnp.where(fast, bound, 0.0))
        bf_pad = jnp.pad(b_forget[layer].astype(F32), (0, LANES - FOX_HEADS))[None, :]
        aq, ak = _cum(fl, bf_pad, tri, place, ones_row, tm, seq)

        oa, om = _local(bucket, rel_bias.astype(F32), sink_swa[layer].astype(F32),
                        qa, ka, va, qm, mk, mv, batch, seq, tq)
        of = _fox(fast.astype(jnp.int32)[None], qf, aq, kf, ak, vt, batch, seq, tm)

        x2 = _post(x2, oa, of, om, gates, w_o_swa[layer].astype(BF16),
                   w_o_fox[layer].astype(BF16), w_o_mem[layer].astype(BF16),
                   w_out[layer].astype(BF16), g_mlp[layer][None, :],
                   w_mlp_up[layer].astype(BF16), w_mlp_down[layer].astype(BF16), tm)
    return x2.reshape(batch, seq, D_MODEL)
```

```python
import functools
import math

import jax
import jax.numpy as jnp
from jax import lax
from jax.experimental import pallas as pl
from jax.experimental.pallas import tpu as pltpu

F32 = jnp.float32
BF16 = jnp.bfloat16

D_MODEL = 1024
N_MEM = 256
SWA_HEADS = 8
SWA_KV_HEADS = 2
SWA_GROUP = SWA_HEADS // SWA_KV_HEADS
HEAD64 = 64
WINDOW = 128
FOX_HEADS = 8
MEM_HEADS = 4
MEM_HEAD_DIM = 128
D_FF = 4 * D_MODEL
REL_BUCKETS = 32
REL_MAX_DIST = 128
BLOCK = 128
EPS = 1e-6
NEG = -1e30
LOG2E = math.log2(math.e)

LANES = 128
MXU_DIM = 256
VMEM_LIMIT = 56 * 1024 * 1024

SWA_Q = SWA_HEADS * HEAD64
SWA_KV = SWA_KV_HEADS * HEAD64
FOX_W = FOX_HEADS * HEAD64
MEM_W = MEM_HEADS * MEM_HEAD_DIM
GATE_W = 3 * D_MODEL

C_QA = 0
C_KA = C_QA + SWA_Q
C_QF = C_KA + 2 * SWA_KV
C_KF = C_QF + FOX_W
C_QM = C_KF + FOX_W
C_G = C_QM + MEM_W
C_FL = C_G + GATE_W
C_END = C_FL + LANES

AUG_PER_HEAD = 7
FOX_FAST_MAX_SHIFT = 60.0
FOX_SUB = 2
LOCAL_LOOKAHEAD = 4


def _dot(a, b):
    return jnp.dot(a, b, preferred_element_type=F32)


def _dot_nt(a, b):
    return lax.dot_general(a, b, (((1,), (1,)), ((), ())), preferred_element_type=F32)


def _rms_rows(x, gain):
    ms = jnp.mean(x * x, axis=-1, keepdims=True)
    return x * lax.rsqrt(ms + EPS) * gain


def _head_rms(y, gain, head_dim):
    y2 = y * y
    lower = lax.broadcasted_iota(jnp.int32, (y.shape[0], LANES), 1) < HEAD64
    cols = []
    for c in range(0, y.shape[1], LANES):
        blk = y2[:, c:c + LANES]
        total = jnp.sum(blk, axis=-1, keepdims=True)
        if head_dim == LANES:
            cols.append(jnp.broadcast_to(total, blk.shape))
        else:
            low = jnp.sum(jnp.where(lower, blk, 0.0), axis=-1, keepdims=True)
            cols.append(jnp.where(lower, low, total - low))
    ss = jnp.concatenate(cols, axis=1)
    return y * lax.rsqrt(ss * (1.0 / head_dim) + EPS) * gain


def _const_spec(shape):
    return pl.BlockSpec(shape, lambda *_: (0,) * len(shape))


def _params(n_axes, flags=None):
    return pltpu.CompilerParams(dimension_semantics=("arbitrary",) * n_axes,
                                vmem_limit_bytes=VMEM_LIMIT, flags=flags)


def _memkv_kernel(mem_ref, g_ref, wk_ref, wvt_ref, kn_ref, mk_ref, mvt_ref):
    h = _rms_rows(mem_ref[...], g_ref[...]).astype(BF16)
    mk_ref[...] = _head_rms(_dot(h, wk_ref[...]), kn_ref[...], MEM_HEAD_DIM).astype(BF16)
    mvt_ref[...] = _dot_nt(wvt_ref[...], h).astype(BF16)


def _memkv(mem2, g_mem, wk, wvt, kn_t):
    rows = mem2.shape[0]
    nb = rows // N_MEM
    return pl.pallas_call(
        _memkv_kernel,
        grid=(nb,),
        in_specs=[pl.BlockSpec((N_MEM, D_MODEL), lambda i: (i, 0)),
                  _const_spec((1, D_MODEL)),
                  _const_spec((D_MODEL, MEM_W)),
                  _const_spec((MEM_W, D_MODEL)),
                  _const_spec((1, MEM_W))],
        out_specs=[pl.BlockSpec((N_MEM, MEM_W), lambda i: (i, 0)),
                   pl.BlockSpec((None, MEM_W, N_MEM), lambda i: (i, 0, 0))],
        out_shape=[jax.ShapeDtypeStruct((rows, MEM_W), BF16),
                   jax.ShapeDtypeStruct((nb, MEM_W, N_MEM), BF16)],
        compiler_params=_params(1),
        name="memkv",
    )(mem2, g_mem, wk, wvt, kn_t)


def _proj_kernel(x_ref, g_ref, w_ref, wvt_ref, bg_ref,
                 gqa_ref, gka_ref, gqf_ref, gkf_ref, gqm_ref,
                 qa_ref, ka_ref, vat_ref, qf_ref, kf_ref, vt_ref, qm_ref, gate_ref, fl_ref):
    h = _rms_rows(x_ref[...], g_ref[...]).astype(BF16)
    qkv = _dot(h, w_ref[:, :C_G])
    qa_ref[...] = _head_rms(qkv[:, C_QA:C_KA], gqa_ref[...], HEAD64).astype(BF16)
    ka_ref[...] = _head_rms(qkv[:, C_KA:C_QF], gka_ref[...], HEAD64).astype(BF16)
    qf_ref[...] = _head_rms(qkv[:, C_QF:C_KF], gqf_ref[...], HEAD64).astype(BF16)
    kf_ref[...] = _head_rms(qkv[:, C_KF:C_QM], gkf_ref[...], HEAD64).astype(BF16)
    qm_ref[...] = _head_rms(qkv[:, C_QM:C_G], gqm_ref[...], MEM_HEAD_DIM).astype(BF16)
    vt = _dot_nt(wvt_ref[...], h).astype(BF16)
    for p in range(FOX_HEADS // 2):
        vt_ref[p] = vt[p * LANES:(p + 1) * LANES, :]
    vat_ref[...] = vt[FOX_W:, :]
    zf = _dot(h, w_ref[:, C_G:C_END])
    z = zf[:, :GATE_W] + bg_ref[...]
    gate_ref[...] = (1.0 / (1.0 + jnp.exp(-z))).astype(BF16)
    fl_ref[...] = zf[:, GATE_W:]


def _proj(x2, g_mix, w1, wvt, b_gate, gqa, gka, gqf, gkf, gqm, tm, seq):
    t = x2.shape[0]
    nt = seq // tm
    pairs = FOX_HEADS // 2

    def rows(w, dtype=BF16):
        return jax.ShapeDtypeStruct((t, w), dtype), pl.BlockSpec((tm, w), lambda i: (i, 0))

    outs = [rows(SWA_Q), rows(2 * SWA_KV),
            (jax.ShapeDtypeStruct((t // seq, SWA_KV, seq), BF16),
             pl.BlockSpec((None, SWA_KV, tm), lambda i: (i // nt, 0, i % nt))),
            rows(FOX_W), rows(FOX_W),
            (jax.ShapeDtypeStruct((t // seq, pairs, nt, LANES, tm), BF16),
             pl.BlockSpec((None, pairs, None, LANES, tm), lambda i: (i // nt, 0, i % nt, 0, 0))),
            rows(MEM_W), rows(GATE_W), rows(LANES, F32)]
    out_shape = [o[0] for o in outs]
    out_specs = [o[1] for o in outs]
    return pl.pallas_call(
        _proj_kernel,
        grid=(t // tm,),
        in_specs=[pl.BlockSpec((tm, D_MODEL), lambda i: (i, 0)),
                  _const_spec((1, D_MODEL)),
                  _const_spec((D_MODEL, C_END)),
                  _const_spec((FOX_W + SWA_KV, D_MODEL)),
                  _const_spec((1, GATE_W)),
                  _const_spec((1, SWA_Q)),
                  _const_spec((1, 2 * SWA_KV)),
                  _const_spec((1, FOX_W)),
                  _const_spec((1, FOX_W)),
                  _const_spec((1, MEM_W))],
        out_specs=out_specs,
        out_shape=out_shape,
        compiler_params=_params(1),
        name="proj",
    )(x2, g_mix, w1, wvt, b_gate, gqa, gka, gqf, gkf, gqm)


def _split3(c):
    hi = c.astype(BF16)
    r1 = c - hi.astype(F32)
    mid = r1.astype(BF16)
    lo = (r1 - mid.astype(F32)).astype(BF16)
    return hi, mid, lo


def _pack3(x):
    hi, mid, lo = _split3(x)
    return (hi.astype(F32) + pltpu.roll(mid.astype(F32), FOX_HEADS, 1)
            + pltpu.roll(lo.astype(F32), 2 * FOX_HEADS, 1)).astype(BF16)


def _cum_kernel(fl_ref, bf_ref, tri_ref, place_ref, const_ref, aq_ref, ak_ref, carry_ref,
                *, tiles_per_seq):
    i = pl.program_id(0)

    @pl.when(i % tiles_per_seq == 0)
    def _():
        carry_ref[...] = jnp.zeros_like(carry_ref)

    lane = lax.broadcasted_iota(jnp.int32, fl_ref.shape, 1)
    heads = lane < FOX_HEADS
    z = fl_ref[...] + bf_ref[...]
    logf = jnp.where(heads, jnp.minimum(z, 0.0) - jnp.log(1.0 + jnp.exp(-jnp.abs(z))), 0.0)
    part = _dot(tri_ref[...], _pack3(logf))
    c = ((part + pltpu.roll(part, LANES - FOX_HEADS, 1))
         + pltpu.roll(part, LANES - 2 * FOX_HEADS, 1)) + carry_ref[0:1, :]
    tm = c.shape[0]
    carry_ref[...] = jnp.broadcast_to(c[tm - 1:tm, :], carry_ref.shape)

    c = jnp.where(heads, c * LOG2E, 0.0)
    placed = _dot(_pack3(c), place_ref[...])
    aq_ref[...] = (placed[:, :LANES] + const_ref[0:1, :LANES]).astype(BF16)
    ak_ref[...] = (const_ref[0:1, LANES:] - placed[:, LANES:]).astype(BF16)


def _cum(fl, bf_pad, tri, place, ones_row, tm, seq):
    t = fl.shape[0]
    return pl.pallas_call(
        functools.partial(_cum_kernel, tiles_per_seq=seq // tm),
        grid=(t // tm,),
        in_specs=[pl.BlockSpec((tm, LANES), lambda i: (i, 0)),
                  _const_spec((1, LANES)),
                  _const_spec((tm, tm)),
                  _const_spec((LANES, 2 * LANES)),
                  _const_spec((8, 2 * LANES))],
        out_specs=[pl.BlockSpec((tm, LANES), lambda i: (i, 0)),
                   pl.BlockSpec((tm, LANES), lambda i: (i, 0))],
        out_shape=[jax.ShapeDtypeStruct((t, LANES), BF16)] * 2,
        scratch_shapes=[pltpu.VMEM((8, LANES), F32)],
        compiler_params=_params(1),
        name="cum",
    )(fl, bf_pad, tri, place, ones_row)


def _local_kernel(bucket_ref, relb_ref, sink_ref,
                  qa_ref, kc_ref, kp_ref, vc_ref, vp_ref, qm_ref, mk_ref, mvt_ref,
                  oa_ref, om_ref, tbl_ref, *, tq):
    b = pl.program_id(0)
    i = pl.program_id(1)
    pairs = SWA_HEADS // 2

    @pl.when(jnp.logical_and(b == 0, i == 0))
    def _():
        bucket = bucket_ref[...]
        band = bucket >= 0
        for h in range(SWA_HEADS):
            t = jnp.zeros(bucket.shape, F32)
            for k in range(REL_BUCKETS):
                t = jnp.where(bucket == k, relb_ref[k, h], t)
            tbl_ref[h // 2, :, (h % 2) * BLOCK:(h % 2 + 1) * BLOCK] = jnp.where(band, t, NEG)

    lane = lax.broadcasted_iota(jnp.int32, (BLOCK, LANES), 1)
    lower = lane < HEAD64
    first_head = lax.broadcasted_iota(jnp.int32, (1, 2 * BLOCK), 1) < BLOCK
    key_row = lax.broadcasted_iota(jnp.int32, (2 * BLOCK, 2 * BLOCK), 0)
    pad_rows = jnp.logical_and(key_row < BLOCK, i == 0)
    ones = jnp.ones((16, 2 * BLOCK), BF16)

    def windows(j):
        r0 = j * BLOCK
        if j == 0:
            return (jnp.concatenate([kp_ref[...], kc_ref[0:BLOCK, :]], axis=0),
                    jnp.concatenate([vp_ref[...], vc_ref[:, 0:BLOCK]], axis=1))
        return kc_ref[r0 - BLOCK:r0 + BLOCK, :], vc_ref[:, r0 - BLOCK:r0 + BLOCK]

    def swa_scores(j, hb):
        g = (2 * hb) // SWA_GROUP
        qblk = qa_ref[j * BLOCK:(j + 1) * BLOCK, hb * LANES:(hb + 1) * LANES]
        qstack = jnp.concatenate([jnp.where(lower, qblk, 0),
                                  jnp.where(lower, 0, qblk)], axis=0)
        s = _dot_nt(windows(j)[0][:, g * LANES:(g + 1) * LANES], qstack) + tbl_ref[hb]
        return jnp.where(pad_rows, NEG, s) if j == 0 else s

    def swa_finish(j, hb, s):
        g = (2 * hb) // SWA_GROUP
        sink = jnp.where(first_head, sink_ref[2 * hb], sink_ref[2 * hb + 1])
        m = jnp.maximum(jnp.max(s, axis=0, keepdims=True), sink)
        p = jnp.exp2(s - m).astype(BF16)
        vt_ones = jnp.concatenate([windows(j)[1][g * HEAD64:(g + 1) * HEAD64, :], ones], axis=0)
        acc = _dot(vt_ones, p)
        o = acc[:HEAD64, :] / (acc[HEAD64:HEAD64 + 1, :] + jnp.exp2(sink - m))
        o = jnp.concatenate([o[:, :BLOCK], o[:, BLOCK:]], axis=0)
        oa_ref[j * BLOCK:(j + 1) * BLOCK, hb * LANES:(hb + 1) * LANES] = o.T.astype(BF16)

    ones_m = jnp.ones((16, N_MEM), BF16)

    def mem_scores(h):
        sl = slice(h * MEM_HEAD_DIM, (h + 1) * MEM_HEAD_DIM)
        return _dot_nt(mk_ref[:, sl], qm_ref[:, sl])

    def mem_finish(h, s):
        sl = slice(h * MEM_HEAD_DIM, (h + 1) * MEM_HEAD_DIM)
        m = jnp.max(s, axis=0, keepdims=True)
        p = jnp.exp2(s - m).astype(BF16)
        acc = _dot(jnp.concatenate([mvt_ref[sl, :], ones_m], axis=0), p)
        o = acc[:MEM_HEAD_DIM, :] / acc[MEM_HEAD_DIM:MEM_HEAD_DIM + 1, :]
        om_ref[:, sl] = o.T.astype(BF16)

    units = [(swa_scores, swa_finish, (j, hb)) for j in range(tq // BLOCK) for hb in range(pairs)]
    units += [(mem_scores, mem_finish, (h,)) for h in range(MEM_HEADS)]
    pending = []
    for n in range(len(units) + LOCAL_LOOKAHEAD):
        if n < len(units):
            score_fn, _, args = units[n]
            pending.append(score_fn(*args))
        if n >= LOCAL_LOOKAHEAD:
            _, finish_fn, args = units[n - LOCAL_LOOKAHEAD]
            finish_fn(*args, pending[n - LOCAL_LOOKAHEAD])


def _local(bucket_t, rel_bias, sinks, qa, ka, vat, qm, mk, mvt, batch, seq, tq):
    t = qa.shape[0]
    nq = seq // tq
    sub = tq // BLOCK

    def cur(b, i):
        return (b * nq + i, 0)

    def prev(b, i):
        return (jnp.maximum((b * nq + i) * sub - 1, 0), 0)

    smem = pl.BlockSpec(memory_space=pltpu.SMEM)
    return pl.pallas_call(
        functools.partial(_local_kernel, tq=tq),
        grid=(batch, nq),
        in_specs=[_const_spec((2 * BLOCK, BLOCK)), smem, smem,
                  pl.BlockSpec((tq, SWA_Q), cur),
                  pl.BlockSpec((tq, 2 * SWA_KV), cur),
                  pl.BlockSpec((BLOCK, 2 * SWA_KV), prev),
                  pl.BlockSpec((None, SWA_KV, tq), lambda b, i: (b, 0, i)),
                  pl.BlockSpec((None, SWA_KV, BLOCK),
                               lambda b, i: (b, 0, jnp.maximum(i * sub - 1, 0))),
                  pl.BlockSpec((tq, MEM_W), cur),
                  pl.BlockSpec((N_MEM, MEM_W), lambda b, i: (b, 0)),
                  pl.BlockSpec((None, MEM_W, N_MEM), lambda b, i: (b, 0, 0))],
        out_specs=[pl.BlockSpec((tq, SWA_Q), cur),
                   pl.BlockSpec((tq, MEM_W), cur)],
        out_shape=[jax.ShapeDtypeStruct((t, SWA_Q), BF16),
                   jax.ShapeDtypeStruct((t, MEM_W), BF16)],
        scratch_shapes=[pltpu.VMEM((SWA_HEADS // 2, 2 * BLOCK, 2 * BLOCK), F32)],
        compiler_params=_params(2),
        name="local",
    )(bucket_t, rel_bias, sinks, qa, ka, ka, vat, vat, qm, mk, mvt)


def _fox_kernel(fast_ref, q_ref, aq_ref, k_ref, ak_ref, vt_ref, o_ref,
                sa_ref, sb_ref, acc_ref, m_ref, *, tk):
    pair = pl.program_id(1)
    g = pl.program_id(2)
    lane = lax.broadcasted_iota(jnp.int32, (tk, LANES), 1)
    half = [lane < HEAD64, lane >= HEAD64]
    qcat = []
    for sub in range(FOX_SUB):
        q = q_ref[sub * tk:(sub + 1) * tk, :]
        aq = aq_ref[sub * tk:(sub + 1) * tk, :]
        zero = jnp.zeros_like(q)
        per_head = []
        for h in range(2):
            lo = (2 * pair + h) * AUG_PER_HEAD
            mine = jnp.logical_and(lane >= lo, lane < lo + AUG_PER_HEAD)
            per_head.append(jnp.concatenate([jnp.where(half[h], q, zero),
                                             jnp.where(mine, aq, zero)], axis=1))
        qcat.append(per_head)
    ones = jnp.ones((16, tk), BF16)
    acc_ref[...] = jnp.zeros_like(acc_ref)
    key = lax.broadcasted_iota(jnp.int32, (tk, tk), 0)
    qry = lax.broadcasted_iota(jnp.int32, (tk, tk), 1)
    causal = key <= qry

    def kcat(j):
        r0 = pl.multiple_of(j * tk, tk)
        return jnp.concatenate([k_ref[pl.ds(r0, tk), :], ak_ref[pl.ds(r0, tk), :]], axis=1)

    def vt_ones(j, h):
        return jnp.concatenate([vt_ref[j, h * HEAD64:(h + 1) * HEAD64, :], ones], axis=0)

    def qk(sub, j, s_ref):
        kc = kcat(j)
        for h in range(2):
            for c in range(0, tk, MXU_DIM):
                s_ref[h, :, c:c + MXU_DIM] = _dot_nt(kc, qcat[sub][h][c:c + MXU_DIM, :])

    def scores(s_ref, h, c, masked):
        s = s_ref[h, :, c:c + MXU_DIM]
        return jnp.where(causal[:, c:c + MXU_DIM], s, NEG) if masked else s

    def pipeline(softmax_pv):
        qk(0, 0, sa_ref)

        def body(t, carry):
            j = 2 * t
            qk(1, j, sb_ref)
            softmax_pv(0, j, sa_ref, False)
            qk(0, j + 1, sa_ref)
            softmax_pv(1, j, sb_ref, False)
            qk(1, j + 1, sb_ref)
            softmax_pv(0, j + 1, sa_ref, False)
            qk(0, j + 2, sa_ref)
            softmax_pv(1, j + 1, sb_ref, False)
            return carry

        lax.fori_loop(0, g, body, 0)
        j = 2 * g
        qk(1, j, sb_ref)
        softmax_pv(0, j, sa_ref, True)
        qk(1, j + 1, sa_ref)
        softmax_pv(1, j, sb_ref, False)
        softmax_pv(1, j + 1, sa_ref, True)

    @pl.when(fast_ref[0] == 1)
    def _():
        def softmax_pv(sub, j, s_ref, masked):
            for h in range(2):
                for c in range(0, tk, MXU_DIM):
                    p = jnp.exp2(scores(s_ref, h, c, masked)).astype(BF16)
                    acc_ref[2 * sub + h, :, c:c + MXU_DIM] += _dot(vt_ones(j, h), p)

        pipeline(softmax_pv)

    @pl.when(fast_ref[0] == 0)
    def _():
        m_ref[...] = jnp.full(m_ref.shape, NEG, F32)

        def softmax_pv(sub, j, s_ref, masked):
            for h in range(2):
                u = 2 * sub + h
                for c in range(0, tk, MXU_DIM):
                    s = scores(s_ref, h, c, masked)
                    m_prev = m_ref[u, :, c:c + MXU_DIM]
                    m_next = jnp.maximum(m_prev, jnp.max(s, axis=0, keepdims=True))
                    alpha = jnp.exp2(m_prev - m_next)
                    p = jnp.exp2(s - m_next).astype(BF16)
                    acc_ref[u, :, c:c + MXU_DIM] = (alpha * acc_ref[u, :, c:c + MXU_DIM]
                                                    + _dot(vt_ones(j, h), p))
                    m_ref[u, :, c:c + MXU_DIM] = m_next

        pipeline(softmax_pv)

    for sub in range(FOX_SUB):
        halves = []
        for h in range(2):
            acc = acc_ref[2 * sub + h]
            halves.append(acc[:HEAD64, :] / acc[HEAD64:HEAD64 + 1, :])
        o_ref[sub * tk:(sub + 1) * tk, :] = jnp.concatenate(halves, axis=0).T.astype(BF16)


def _fox(fast, qf, aq, kf, ak, vt, batch, seq, tk):
    t = qf.shape[0]
    tq = FOX_SUB * tk
    assert seq % tq == 0, (seq, tq)
    nq = seq // tq
    pairs = FOX_HEADS // 2
    return pl.pallas_call(
        functools.partial(_fox_kernel, tk=tk),
        grid=(batch, pairs, nq),
        in_specs=[pl.BlockSpec(memory_space=pltpu.SMEM),
                  pl.BlockSpec((tq, LANES), lambda b, p, i: (b * nq + i, p)),
                  pl.BlockSpec((tq, LANES), lambda b, p, i: (b * nq + i, 0)),
                  pl.BlockSpec((seq, LANES), lambda b, p, i: (b, p)),
                  pl.BlockSpec((seq, LANES), lambda b, p, i: (b, 0)),
                  pl.BlockSpec((None, None, seq // tk, LANES, tk),
                               lambda b, p, i: (b, p, 0, 0, 0))],
        out_specs=pl.BlockSpec((tq, LANES), lambda b, p, i: (b * nq + i, p)),
        out_shape=jax.ShapeDtypeStruct((t, FOX_W), BF16),
        scratch_shapes=[pltpu.VMEM((2, tk, tk), F32),
                        pltpu.VMEM((2, tk, tk), F32),
                        pltpu.VMEM((2 * FOX_SUB, HEAD64 + 16, tk), F32),
                        pltpu.VMEM((2 * FOX_SUB, 1, tk), F32)],
        compiler_params=_params(3),
        name="fox",
    )(fast, qf, aq, kf, ak, vt)


def _post_kernel(x_ref, oa_ref, of_ref, om_ref, gate_ref, wa_ref, wf_ref, wm_ref, wo_ref,
                 g_ref, wu_ref, wd_ref, y_ref, *, chunk):
    ya = _dot(oa_ref[...], wa_ref[...])
    yf = _dot(of_ref[...], wf_ref[...])
    ym = _dot(om_ref[...], wm_ref[...])
    merged = (gate_ref[:, 0:D_MODEL].astype(F32) * ya
              + gate_ref[:, D_MODEL:2 * D_MODEL].astype(F32) * yf
              + gate_ref[:, 2 * D_MODEL:].astype(F32) * ym)
    x1 = x_ref[...] + _dot(merged.astype(BF16), wo_ref[...])
    h = _rms_rows(x1, g_ref[...]).astype(BF16)
    acc = x1
    for c in range(0, D_FF, chunk):
        u = jnp.maximum(_dot(h, wu_ref[:, c:c + chunk]), 0.0)
        acc = acc + _dot((u * u).astype(BF16), wd_ref[c:c + chunk, :])
    y_ref[...] = acc


def _post(x2, oa, of, om, gates, wa, wf, wm, wo, g_mlp, wu, wd, tm, chunk=1024):
    t = x2.shape[0]
    row = lambda w: pl.BlockSpec((tm, w), lambda i: (i, 0))
    resident = lambda shape: pl.BlockSpec(shape, lambda i: (0,) * len(shape),
                                          pipeline_mode=pl.Buffered(1))
    return pl.pallas_call(
        functools.partial(_post_kernel, chunk=chunk),
        grid=(t // tm,),
        in_specs=[row(D_MODEL), row(SWA_Q), row(FOX_W), row(MEM_W), row(GATE_W),
                  resident((SWA_Q, D_MODEL)), resident((FOX_W, D_MODEL)),
                  resident((MEM_W, D_MODEL)), resident((D_MODEL, D_MODEL)),
                  resident((1, D_MODEL)), resident((D_MODEL, D_FF)), resident((D_FF, D_MODEL))],
        out_specs=row(D_MODEL),
        out_shape=jax.ShapeDtypeStruct((t, D_MODEL), F32),
        compiler_params=_params(1),
        name="post",
    )(x2, oa, of, om, gates, wa, wf, wm, wo, g_mlp, wu, wd)


def _t5_bucket_table():
    max_exact = REL_BUCKETS // 2
    t_loc = jnp.arange(BLOCK)[:, None] + BLOCK
    s_loc = jnp.arange(2 * BLOCK)[None, :]
    dist = t_loc - s_loc
    d = jnp.maximum(dist, 0)
    df = jnp.maximum(d, 1).astype(F32)
    large = max_exact + (jnp.log(df / max_exact) / math.log(REL_MAX_DIST / max_exact)
                         * (REL_BUCKETS - max_exact)).astype(jnp.int32)
    large = jnp.minimum(large, REL_BUCKETS - 1)
    bucket = jnp.where(d < max_exact, d, large)
    band = (dist >= 0) & (dist < WINDOW)
    return jnp.where(band, bucket, -1).astype(jnp.int32).T


def _aug_placement(shift):
    place = [[0.0] * (2 * LANES) for _ in range(LANES)]
    ones = [0.0] * (2 * LANES)
    shift_lanes = []
    for h in range(FOX_HEADS):
        base = h * AUG_PER_HEAD
        for part in range(3):
            src = part * FOX_HEADS + h
            place[src][base + part] = 1.0
            place[src][LANES + base + 3 + part] = 1.0
            ones[base + 3 + part] = 1.0
            ones[LANES + base + part] = 1.0
        ones[base + 6] = 1.0
        shift_lanes.append(LANES + base + 6)
    consts = jnp.array(ones, F32).at[jnp.array(shift_lanes)].set(-shift)
    return jnp.array(place, BF16), jnp.broadcast_to(consts[None, :], (8, 2 * LANES))


def _tile_gain(g, reps, scale=1.0):
    return (jnp.tile(g.astype(F32), reps) * scale)[None, :]


def _pick_tile(n, target):
    t = min(n, target)
    while n % t:
        t //= 2
    return t


def kernel(x, mem, g_mix, w_in, b_gate, b_forget, qn_swa, kn_swa, sink_swa, rel_bias,
           qn_fox, kn_fox, g_mem, w_mem_kv, qn_mem, kn_mem, w_o_swa, w_o_fox, w_o_mem,
           w_out, g_mlp, w_mlp_up, w_mlp_down):
    batch, seq, _ = x.shape
    n_layers = w_in.shape[0]
    t = batch * seq
    tm = _pick_tile(seq, 512)
    tq = _pick_tile(seq, 512)

    bucket = _t5_bucket_table()
    tri = (jnp.arange(tm)[:, None] >= jnp.arange(tm)[None, :]).astype(BF16)

    x2 = x.reshape(t, D_MODEL)
    mem2 = mem.reshape(batch * N_MEM, D_MODEL)
    for layer in range(n_layers):
        w = w_in[layer]
        o = 0
        parts = {}
        for name, width in (("qa", SWA_Q), ("ka", SWA_KV), ("va", SWA_KV), ("qf", FOX_W),
                            ("kf", FOX_W), ("vf", FOX_W), ("fl", FOX_HEADS), ("qm", MEM_W),
                            ("g", GATE_W)):
            parts[name] = w[:, o:o + width]
            o += width

        def dup(m):
            return jnp.concatenate([m[:, :HEAD64], m[:, :HEAD64], m[:, HEAD64:], m[:, HEAD64:]],
                                   axis=1)

        w1 = jnp.concatenate(
            [parts["qa"], dup(parts["ka"]), parts["qf"], parts["kf"],
             parts["qm"], parts["g"],
             jnp.pad(parts["fl"], ((0, 0), (0, LANES - FOX_HEADS)))], axis=1).astype(BF16)

        mk, mvt = _memkv(mem2, g_mem[layer][None, :], w_mem_kv[layer][:, :MEM_W].astype(BF16),
                         w_mem_kv[layer][:, MEM_W:].T.astype(BF16),
                         _tile_gain(kn_mem[layer], MEM_HEADS))

        wvt = jnp.concatenate([parts["vf"], parts["va"]], axis=1).T.astype(BF16)
        qa, ka, vat, qf, kf, vt, qm, gates, fl = _proj(
            x2, g_mix[layer][None, :], w1, wvt, b_gate[layer][None, :],
            _tile_gain(qn_swa[layer], SWA_HEADS, HEAD64 ** -0.5 * LOG2E),
            _tile_gain(kn_swa[layer], 2 * SWA_KV_HEADS),
            _tile_gain(qn_fox[layer], FOX_HEADS, HEAD64 ** -0.5 * LOG2E),
            _tile_gain(kn_fox[layer], FOX_HEADS),
            _tile_gain(qn_mem[layer], MEM_HEADS, MEM_HEAD_DIM ** -0.5 * LOG2E), tm, seq)

        bound = (HEAD64 ** 0.5 * LOG2E * 1.02 * jnp.max(jnp.abs(qn_fox[layer]))
                 * jnp.max(jnp.abs(kn_fox[layer]))).astype(BF16).astype(F32)
        fast = bound <= FOX_FAST_MAX_SHIFT
        place, ones_row = _aug_placement(jnp.where(fast, bound, 0.0))
        bf_pad = jnp.pad(b_forget[layer].astype(F32), (0, LANES - FOX_HEADS))[None, :]
        aq, ak = _cum(fl, bf_pad, tri, place, ones_row, tm, seq)

        oa, om = _local(bucket, rel_bias.astype(F32) * LOG2E, sink_swa[layer].astype(F32) * LOG2E,
                        qa, ka, vat, qm, mk, mvt, batch, seq, tq)
        of = _fox(fast.astype(jnp.int32)[None], qf, aq, kf, ak, vt, batch, seq, tm)

        x2 = _post(x2, oa, of, om, gates, w_o_swa[layer].astype(BF16),
                   w_o_fox[layer].astype(BF16), w_o_mem[layer].astype(BF16),
                   w_out[layer].astype(BF16), g_mlp[layer][None, :],
                   w_mlp_up[layer].astype(BF16), w_mlp_down[layer].astype(BF16), tm)
    return x2.reshape(batch, seq, D_MODEL)
```

```python
import functools
import math

import jax
import jax.numpy as jnp
from jax import lax
from jax.experimental import pallas as pl
from jax.experimental.pallas import tpu as pltpu

F32 = jnp.float32
BF16 = jnp.bfloat16

D_MODEL = 1024
N_MEM = 256
SWA_HEADS = 8
SWA_KV_HEADS = 2
SWA_GROUP = SWA_HEADS // SWA_KV_HEADS
HEAD64 = 64
WINDOW = 128
FOX_HEADS = 8
MEM_HEADS = 4
MEM_HEAD_DIM = 128
D_FF = 4 * D_MODEL
REL_BUCKETS = 32
REL_MAX_DIST = 128
BLOCK = 128
EPS = 1e-6
NEG = -1e30
LOG2E = math.log2(math.e)

LANES = 128
MXU_DIM = 256
VMEM_LIMIT = 56 * 1024 * 1024

SWA_Q = SWA_HEADS * HEAD64
SWA_KV = SWA_KV_HEADS * HEAD64
FOX_W = FOX_HEADS * HEAD64
MEM_W = MEM_HEADS * MEM_HEAD_DIM
GATE_W = 3 * D_MODEL

C_QA = 0
C_KA = C_QA + SWA_Q
C_QF = C_KA + 2 * SWA_KV
C_KF = C_QF + FOX_W
C_QM = C_KF + FOX_W
C_G = C_QM + MEM_W
C_FL = C_G + GATE_W
C_END = C_FL + LANES

AUG_PER_HEAD = 7
FOX_FAST_MAX_SHIFT = 60.0
FOX_SUB = 2
LOCAL_LOOKAHEAD = 4


def _dot(a, b):
    return jnp.dot(a, b, preferred_element_type=F32)


def _dot_nt(a, b):
    return lax.dot_general(a, b, (((1,), (1,)), ((), ())), preferred_element_type=F32)


def _rms_rows(x, gain):
    ms = jnp.mean(x * x, axis=-1, keepdims=True)
    return x * lax.rsqrt(ms + EPS) * gain


def _head_rms(y, gain, head_dim):
    y2 = y * y
    lower = lax.broadcasted_iota(jnp.int32, (y.shape[0], LANES), 1) < HEAD64
    cols = []
    for c in range(0, y.shape[1], LANES):
        blk = y2[:, c:c + LANES]
        total = jnp.sum(blk, axis=-1, keepdims=True)
        if head_dim == LANES:
            cols.append(jnp.broadcast_to(total, blk.shape))
        else:
            low = jnp.sum(jnp.where(lower, blk, 0.0), axis=-1, keepdims=True)
            cols.append(jnp.where(lower, low, total - low))
    ss = jnp.concatenate(cols, axis=1)
    return y * lax.rsqrt(ss * (1.0 / head_dim) + EPS) * gain


def _const_spec(shape):
    return pl.BlockSpec(shape, lambda *_: (0,) * len(shape))


def _params(n_axes, flags=None):
    return pltpu.CompilerParams(dimension_semantics=("arbitrary",) * n_axes,
                                vmem_limit_bytes=VMEM_LIMIT, flags=flags)


def _memkv_kernel(mem_ref, g_ref, wkt_ref, wvt_ref, kn_ref, mk_ref, mvt_ref):
    h = _rms_rows(mem_ref[...], g_ref[...]).astype(BF16)
    mk_ref[...] = _head_rms(_dot_nt(h, wkt_ref[...]), kn_ref[...], MEM_HEAD_DIM).astype(BF16)
    mvt_ref[...] = _dot_nt(wvt_ref[...], h).astype(BF16)


def _memkv(mem2, g_mem, wkt, wvt, kn_t):
    rows = mem2.shape[0]
    nb = rows // N_MEM
    return pl.pallas_call(
        _memkv_kernel,
        grid=(nb,),
        in_specs=[pl.BlockSpec((N_MEM, D_MODEL), lambda i: (i, 0)),
                  _const_spec((1, D_MODEL)),
                  _const_spec((MEM_W, D_MODEL)),
                  _const_spec((MEM_W, D_MODEL)),
                  _const_spec((1, MEM_W))],
        out_specs=[pl.BlockSpec((N_MEM, MEM_W), lambda i: (i, 0)),
                   pl.BlockSpec((None, MEM_W, N_MEM), lambda i: (i, 0, 0))],
        out_shape=[jax.ShapeDtypeStruct((rows, MEM_W), BF16),
                   jax.ShapeDtypeStruct((nb, MEM_W, N_MEM), BF16)],
        compiler_params=_params(1),
        name="memkv",
    )(mem2, g_mem, wkt, wvt, kn_t)


def _proj_kernel(x_ref, g_ref, w_ref, wvt_ref, bg_ref,
                 gqa_ref, gka_ref, gqf_ref, gkf_ref, gqm_ref,
                 qa_ref, ka_ref, vat_ref, qf_ref, kf_ref, vt_ref, qm_ref, gate_ref, fl_ref):
    h = _rms_rows(x_ref[...], g_ref[...]).astype(BF16)
    qkv = _dot_nt(h, w_ref[:C_G, :])
    qa_ref[...] = _head_rms(qkv[:, C_QA:C_KA], gqa_ref[...], HEAD64).astype(BF16)
    ka_ref[...] = _head_rms(qkv[:, C_KA:C_QF], gka_ref[...], HEAD64).astype(BF16)
    qf_ref[...] = _head_rms(qkv[:, C_QF:C_KF], gqf_ref[...], HEAD64).astype(BF16)
    kf_ref[...] = _head_rms(qkv[:, C_KF:C_QM], gkf_ref[...], HEAD64).astype(BF16)
    qm_ref[...] = _head_rms(qkv[:, C_QM:C_G], gqm_ref[...], MEM_HEAD_DIM).astype(BF16)
    vt = _dot_nt(wvt_ref[...], h).astype(BF16)
    for p in range(FOX_HEADS // 2):
        vt_ref[p] = vt[p * LANES:(p + 1) * LANES, :]
    vat_ref[...] = vt[FOX_W:, :]
    zf = _dot_nt(h, w_ref[C_G:C_END, :])
    z = zf[:, :GATE_W] + bg_ref[...]
    gate_ref[...] = (1.0 / (1.0 + jnp.exp(-z))).astype(BF16)
    fl_ref[...] = zf[:, GATE_W:]


def _proj(x2, g_mix, w1, wvt, b_gate, gqa, gka, gqf, gkf, gqm, tm, seq):
    t = x2.shape[0]
    nt = seq // tm
    pairs = FOX_HEADS // 2

    def rows(w, dtype=BF16):
        return jax.ShapeDtypeStruct((t, w), dtype), pl.BlockSpec((tm, w), lambda i: (i, 0))

    outs = [rows(SWA_Q), rows(2 * SWA_KV),
            (jax.ShapeDtypeStruct((t // seq, SWA_KV, seq), BF16),
             pl.BlockSpec((None, SWA_KV, tm), lambda i: (i // nt, 0, i % nt))),
            rows(FOX_W), rows(FOX_W),
            (jax.ShapeDtypeStruct((t // seq, pairs, nt, LANES, tm), BF16),
             pl.BlockSpec((None, pairs, None, LANES, tm), lambda i: (i // nt, 0, i % nt, 0, 0))),
            rows(MEM_W), rows(GATE_W), rows(LANES, F32)]
    out_shape = [o[0] for o in outs]
    out_specs = [o[1] for o in outs]
    return pl.pallas_call(
        _proj_kernel,
        grid=(t // tm,),
        in_specs=[pl.BlockSpec((tm, D_MODEL), lambda i: (i, 0)),
                  _const_spec((1, D_MODEL)),
                  _const_spec((C_END, D_MODEL)),
                  _const_spec((FOX_W + SWA_KV, D_MODEL)),
                  _const_spec((1, GATE_W)),
                  _const_spec((1, SWA_Q)),
                  _const_spec((1, 2 * SWA_KV)),
                  _const_spec((1, FOX_W)),
                  _const_spec((1, FOX_W)),
                  _const_spec((1, MEM_W))],
        out_specs=out_specs,
        out_shape=out_shape,
        compiler_params=_params(1),
        name="proj",
    )(x2, g_mix, w1, wvt, b_gate, gqa, gka, gqf, gkf, gqm)


def _split3(c):
    hi = c.astype(BF16)
    r1 = c - hi.astype(F32)
    mid = r1.astype(BF16)
    lo = (r1 - mid.astype(F32)).astype(BF16)
    return hi, mid, lo


def _pack3(x):
    hi, mid, lo = _split3(x)
    return (hi.astype(F32) + pltpu.roll(mid.astype(F32), FOX_HEADS, 1)
            + pltpu.roll(lo.astype(F32), 2 * FOX_HEADS, 1)).astype(BF16)


def _cum_kernel(fl_ref, bf_ref, tri_ref, place_ref, const_ref, aq_ref, ak_ref, *, tm):
    n_tiles = fl_ref.shape[0] // tm
    lane = lax.broadcasted_iota(jnp.int32, (tm, LANES), 1)
    heads = lane < FOX_HEADS
    tri = tri_ref[...]
    parts = []
    for k in range(n_tiles):
        z = fl_ref[k * tm:(k + 1) * tm, :] + bf_ref[...]
        logf = jnp.where(heads, jnp.minimum(z, 0.0) - jnp.log(1.0 + jnp.exp(-jnp.abs(z))), 0.0)
        parts.append(_dot(tri, _pack3(logf)))
    carry = jnp.zeros((1, LANES), F32)
    packed = []
    for part in parts:
        c = ((part + pltpu.roll(part, LANES - FOX_HEADS, 1))
             + pltpu.roll(part, LANES - 2 * FOX_HEADS, 1)) + carry
        carry = c[tm - 1:tm, :]
        packed.append(_pack3(jnp.where(heads, c * LOG2E, 0.0)))
    for k in range(n_tiles):
        placed = _dot(packed[k], place_ref[...])
        rows = slice(k * tm, (k + 1) * tm)
        aq_ref[rows, :] = (placed[:, :LANES] + const_ref[0:1, :LANES]).astype(BF16)
        ak_ref[rows, :] = (const_ref[0:1, LANES:] - placed[:, LANES:]).astype(BF16)


def _cum(fl, bf_pad, tri, place, consts, tm, seq):
    t = fl.shape[0]
    return pl.pallas_call(
        functools.partial(_cum_kernel, tm=tm),
        grid=(t // seq,),
        in_specs=[pl.BlockSpec((seq, LANES), lambda i: (i, 0)),
                  _const_spec((1, LANES)),
                  _const_spec((tm, tm)),
                  _const_spec((LANES, 2 * LANES)),
                  _const_spec((8, 2 * LANES))],
        out_specs=[pl.BlockSpec((seq, LANES), lambda i: (i, 0)),
                   pl.BlockSpec((seq, LANES), lambda i: (i, 0))],
        out_shape=[jax.ShapeDtypeStruct((t, LANES), BF16)] * 2,
        compiler_params=_params(1),
        name="cum",
    )(fl, bf_pad, tri, place, consts)


def _local_kernel(bucket_ref, relb_ref, sink_ref,
                  qa_ref, kc_ref, kp_ref, vc_ref, vp_ref, qm_ref, mk_ref, mvt_ref,
                  oa_ref, om_ref, tbl_ref, *, tq):
    b = pl.program_id(0)
    i = pl.program_id(1)
    pairs = SWA_HEADS // 2

    @pl.when(jnp.logical_and(b == 0, i == 0))
    def _():
        bucket = bucket_ref[...]
        band = bucket >= 0
        for h in range(SWA_HEADS):
            t = jnp.zeros(bucket.shape, F32)
            for k in range(REL_BUCKETS):
                t = jnp.where(bucket == k, relb_ref[k, h], t)
            tbl_ref[h // 2, :, (h % 2) * BLOCK:(h % 2 + 1) * BLOCK] = jnp.where(band, t, NEG)

    lane = lax.broadcasted_iota(jnp.int32, (BLOCK, LANES), 1)
    lower = lane < HEAD64
    first_head = lax.broadcasted_iota(jnp.int32, (1, 2 * BLOCK), 1) < BLOCK
    key_row = lax.broadcasted_iota(jnp.int32, (2 * BLOCK, 2 * BLOCK), 0)
    pad_rows = jnp.logical_and(key_row < BLOCK, i == 0)
    ones = jnp.ones((16, 2 * BLOCK), BF16)

    def windows(j):
        r0 = j * BLOCK
        if j == 0:
            return (jnp.concatenate([kp_ref[...], kc_ref[0:BLOCK, :]], axis=0),
                    jnp.concatenate([vp_ref[...], vc_ref[:, 0:BLOCK]], axis=1))
        return kc_ref[r0 - BLOCK:r0 + BLOCK, :], vc_ref[:, r0 - BLOCK:r0 + BLOCK]

    def swa_scores(j, hb):
        g = (2 * hb) // SWA_GROUP
        qblk = qa_ref[j * BLOCK:(j + 1) * BLOCK, hb * LANES:(hb + 1) * LANES]
        qstack = jnp.concatenate([jnp.where(lower, qblk, 0),
                                  jnp.where(lower, 0, qblk)], axis=0)
        s = _dot_nt(windows(j)[0][:, g * LANES:(g + 1) * LANES], qstack) + tbl_ref[hb]
        return jnp.where(pad_rows, NEG, s) if j == 0 else s

    def swa_finish(j, hb, s):
        g = (2 * hb) // SWA_GROUP
        sink = jnp.where(first_head, sink_ref[2 * hb], sink_ref[2 * hb + 1])
        m = jnp.maximum(jnp.max(s, axis=0, keepdims=True), sink)
        p = jnp.exp2(s - m).astype(BF16)
        vt_ones = jnp.concatenate([windows(j)[1][g * HEAD64:(g + 1) * HEAD64, :], ones], axis=0)
        acc = _dot(vt_ones, p)
        o = acc[:HEAD64, :] / (acc[HEAD64:HEAD64 + 1, :] + jnp.exp2(sink - m))
        o = jnp.concatenate([o[:, :BLOCK], o[:, BLOCK:]], axis=0)
        oa_ref[j * BLOCK:(j + 1) * BLOCK, hb * LANES:(hb + 1) * LANES] = o.T.astype(BF16)

    ones_m = jnp.ones((16, N_MEM), BF16)

    def mem_scores(h):
        sl = slice(h * MEM_HEAD_DIM, (h + 1) * MEM_HEAD_DIM)
        return _dot_nt(mk_ref[:, sl], qm_ref[:, sl])

    def mem_finish(h, s):
        sl = slice(h * MEM_HEAD_DIM, (h + 1) * MEM_HEAD_DIM)
        m = jnp.max(s, axis=0, keepdims=True)
        p = jnp.exp2(s - m).astype(BF16)
        acc = _dot(jnp.concatenate([mvt_ref[sl, :], ones_m], axis=0), p)
        o = acc[:MEM_HEAD_DIM, :] / acc[MEM_HEAD_DIM:MEM_HEAD_DIM + 1, :]
        om_ref[:, sl] = o.T.astype(BF16)

    units = [(swa_scores, swa_finish, (j, hb)) for j in range(tq // BLOCK) for hb in range(pairs)]
    units += [(mem_scores, mem_finish, (h,)) for h in range(MEM_HEADS)]
    pending = []
    for n in range(len(units) + LOCAL_LOOKAHEAD):
        if n < len(units):
            score_fn, _, args = units[n]
            pending.append(score_fn(*args))
        if n >= LOCAL_LOOKAHEAD:
            _, finish_fn, args = units[n - LOCAL_LOOKAHEAD]
            finish_fn(*args, pending[n - LOCAL_LOOKAHEAD])


def _local(bucket_t, rel_bias, sinks, qa, ka, vat, qm, mk, mvt, batch, seq, tq):
    t = qa.shape[0]
    nq = seq // tq
    sub = tq // BLOCK

    def cur(b, i):
        return (b * nq + i, 0)

    def prev(b, i):
        return (jnp.maximum((b * nq + i) * sub - 1, 0), 0)

    smem = pl.BlockSpec(memory_space=pltpu.SMEM)
    return pl.pallas_call(
        functools.partial(_local_kernel, tq=tq),
        grid=(batch, nq),
        in_specs=[_const_spec((2 * BLOCK, BLOCK)), smem, smem,
                  pl.BlockSpec((tq, SWA_Q), cur),
                  pl.BlockSpec((tq, 2 * SWA_KV), cur),
                  pl.BlockSpec((BLOCK, 2 * SWA_KV), prev),
                  pl.BlockSpec((None, SWA_KV, tq), lambda b, i: (b, 0, i)),
                  pl.BlockSpec((None, SWA_KV, BLOCK),
                               lambda b, i: (b, 0, jnp.maximum(i * sub - 1, 0))),
                  pl.BlockSpec((tq, MEM_W), cur),
                  pl.BlockSpec((N_MEM, MEM_W), lambda b, i: (b, 0)),
                  pl.BlockSpec((None, MEM_W, N_MEM), lambda b, i: (b, 0, 0))],
        out_specs=[pl.BlockSpec((tq, SWA_Q), cur),
                   pl.BlockSpec((tq, MEM_W), cur)],
        out_shape=[jax.ShapeDtypeStruct((t, SWA_Q), BF16),
                   jax.ShapeDtypeStruct((t, MEM_W), BF16)],
        scratch_shapes=[pltpu.VMEM((SWA_HEADS // 2, 2 * BLOCK, 2 * BLOCK), F32)],
        compiler_params=_params(2),
        name="local",
    )(bucket_t, rel_bias, sinks, qa, ka, ka, vat, vat, qm, mk, mvt)


def _fox_kernel(fast_ref, q_ref, aq_ref, k_ref, ak_ref, vt_ref, o_ref,
                sa_ref, sb_ref, acc_ref, m_ref, *, tk):
    pair = pl.program_id(1)
    g = pl.program_id(2)
    lane = lax.broadcasted_iota(jnp.int32, (tk, LANES), 1)
    half = [lane < HEAD64, lane >= HEAD64]
    qcat = []
    for sub in range(FOX_SUB):
        q = q_ref[sub * tk:(sub + 1) * tk, :]
        aq = aq_ref[sub * tk:(sub + 1) * tk, :]
        zero = jnp.zeros_like(q)
        per_head = []
        for h in range(2):
            lo = (2 * pair + h) * AUG_PER_HEAD
            mine = jnp.logical_and(lane >= lo, lane < lo + AUG_PER_HEAD)
            per_head.append(jnp.concatenate([jnp.where(half[h], q, zero),
                                             jnp.where(mine, aq, zero)], axis=1))
        qcat.append(per_head)
    ones = jnp.ones((16, tk), BF16)
    acc_ref[...] = jnp.zeros_like(acc_ref)
    key = lax.broadcasted_iota(jnp.int32, (tk, tk), 0)
    qry = lax.broadcasted_iota(jnp.int32, (tk, tk), 1)
    causal = key <= qry

    def kcat(j):
        r0 = pl.multiple_of(j * tk, tk)
        return jnp.concatenate([k_ref[pl.ds(r0, tk), :], ak_ref[pl.ds(r0, tk), :]], axis=1)

    def vt_ones(j, h):
        return jnp.concatenate([vt_ref[j, h * HEAD64:(h + 1) * HEAD64, :], ones], axis=0)

    def qk(sub, j, s_ref):
        kc = kcat(j)
        for h in range(2):
            for c in range(0, tk, MXU_DIM):
                s_ref[h, :, c:c + MXU_DIM] = _dot_nt(kc, qcat[sub][h][c:c + MXU_DIM, :])

    def scores(s_ref, h, c, masked):
        s = s_ref[h, :, c:c + MXU_DIM]
        return jnp.where(causal[:, c:c + MXU_DIM], s, NEG) if masked else s

    def pipeline(softmax_pv):
        qk(0, 0, sa_ref)

        def body(t, carry):
            j = 2 * t
            qk(1, j, sb_ref)
            softmax_pv(0, j, sa_ref, False)
            qk(0, j + 1, sa_ref)
            softmax_pv(1, j, sb_ref, False)
            qk(1, j + 1, sb_ref)
            softmax_pv(0, j + 1, sa_ref, False)
            qk(0, j + 2, sa_ref)
            softmax_pv(1, j + 1, sb_ref, False)
            return carry

        lax.fori_loop(0, g, body, 0)
        j = 2 * g
        qk(1, j, sb_ref)
        softmax_pv(0, j, sa_ref, True)
        qk(1, j + 1, sa_ref)
        softmax_pv(1, j, sb_ref, False)
        softmax_pv(1, j + 1, sa_ref, True)

    @pl.when(fast_ref[0] == 1)
    def _():
        def softmax_pv(sub, j, s_ref, masked):
            for h in range(2):
                for c in range(0, tk, MXU_DIM):
                    p = jnp.exp2(scores(s_ref, h, c, masked)).astype(BF16)
                    acc_ref[2 * sub + h, :, c:c + MXU_DIM] += _dot(vt_ones(j, h), p)

        pipeline(softmax_pv)

    @pl.when(fast_ref[0] == 0)
    def _():
        m_ref[...] = jnp.full(m_ref.shape, NEG, F32)

        def softmax_pv(sub, j, s_ref, masked):
            for h in range(2):
                u = 2 * sub + h
                for c in range(0, tk, MXU_DIM):
                    s = scores(s_ref, h, c, masked)
                    m_prev = m_ref[u, :, c:c + MXU_DIM]
                    m_next = jnp.maximum(m_prev, jnp.max(s, axis=0, keepdims=True))
                    alpha = jnp.exp2(m_prev - m_next)
                    p = jnp.exp2(s - m_next).astype(BF16)
                    acc_ref[u, :, c:c + MXU_DIM] = (alpha * acc_ref[u, :, c:c + MXU_DIM]
                                                    + _dot(vt_ones(j, h), p))
                    m_ref[u, :, c:c + MXU_DIM] = m_next

        pipeline(softmax_pv)

    for sub in range(FOX_SUB):
        halves = []
        for h in range(2):
            acc = acc_ref[2 * sub + h]
            halves.append(acc[:HEAD64, :] / acc[HEAD64:HEAD64 + 1, :])
        o_ref[sub * tk:(sub + 1) * tk, :] = jnp.concatenate(halves, axis=0).T.astype(BF16)


def _fox(fast, qf, aq, kf, ak, vt, batch, seq, tk):
    t = qf.shape[0]
    tq = FOX_SUB * tk
    assert seq % tq == 0, (seq, tq)
    nq = seq // tq
    pairs = FOX_HEADS // 2
    return pl.pallas_call(
        functools.partial(_fox_kernel, tk=tk),
        grid=(batch, pairs, nq),
        in_specs=[pl.BlockSpec(memory_space=pltpu.SMEM),
                  pl.BlockSpec((tq, LANES), lambda b, p, i: (b * nq + i, p)),
                  pl.BlockSpec((tq, LANES), lambda b, p, i: (b * nq + i, 0)),
                  pl.BlockSpec((seq, LANES), lambda b, p, i: (b, p)),
                  pl.BlockSpec((seq, LANES), lambda b, p, i: (b, 0)),
                  pl.BlockSpec((None, None, seq // tk, LANES, tk),
                               lambda b, p, i: (b, p, 0, 0, 0))],
        out_specs=pl.BlockSpec((tq, LANES), lambda b, p, i: (b * nq + i, p)),
        out_shape=jax.ShapeDtypeStruct((t, FOX_W), BF16),
        scratch_shapes=[pltpu.VMEM((2, tk, tk), F32),
                        pltpu.VMEM((2, tk, tk), F32),
                        pltpu.VMEM((2 * FOX_SUB, HEAD64 + 16, tk), F32),
                        pltpu.VMEM((2 * FOX_SUB, 1, tk), F32)],
        compiler_params=_params(3),
        name="fox",
    )(fast, qf, aq, kf, ak, vt)


def _post_kernel(x_ref, oa_ref, of_ref, om_ref, gate_ref, wa_ref, wf_ref, wm_ref, wo_ref,
                 g_ref, wu_ref, wd_ref, y_ref, *, chunk):
    ya = _dot(oa_ref[...], wa_ref[...])
    yf = _dot(of_ref[...], wf_ref[...])
    ym = _dot(om_ref[...], wm_ref[...])
    merged = (gate_ref[:, 0:D_MODEL].astype(F32) * ya
              + gate_ref[:, D_MODEL:2 * D_MODEL].astype(F32) * yf
              + gate_ref[:, 2 * D_MODEL:].astype(F32) * ym)
    x1 = x_ref[...] + _dot(merged.astype(BF16), wo_ref[...])
    h = _rms_rows(x1, g_ref[...]).astype(BF16)
    acc = x1
    for c in range(0, D_FF, chunk):
        u = jnp.maximum(_dot(h, wu_ref[:, c:c + chunk]), 0.0)
        acc = acc + _dot((u * u).astype(BF16), wd_ref[c:c + chunk, :])
    y_ref[...] = acc


def _post(x2, oa, of, om, gates, wa, wf, wm, wo, g_mlp, wu, wd, tm, chunk=1024):
    t = x2.shape[0]
    row = lambda w: pl.BlockSpec((tm, w), lambda i: (i, 0))
    resident = lambda shape: pl.BlockSpec(shape, lambda i: (0,) * len(shape),
                                          pipeline_mode=pl.Buffered(1))
    return pl.pallas_call(
        functools.partial(_post_kernel, chunk=chunk),
        grid=(t // tm,),
        in_specs=[row(D_MODEL), row(SWA_Q), row(FOX_W), row(MEM_W), row(GATE_W),
                  resident((SWA_Q, D_MODEL)), resident((FOX_W, D_MODEL)),
                  resident((MEM_W, D_MODEL)), resident((D_MODEL, D_MODEL)),
                  resident((1, D_MODEL)), resident((D_MODEL, D_FF)), resident((D_FF, D_MODEL))],
        out_specs=row(D_MODEL),
        out_shape=jax.ShapeDtypeStruct((t, D_MODEL), F32),
        compiler_params=_params(1),
        name="post",
    )(x2, oa, of, om, gates, wa, wf, wm, wo, g_mlp, wu, wd)


def _t5_bucket_table():
    max_exact = REL_BUCKETS // 2
    t_loc = jnp.arange(BLOCK)[:, None] + BLOCK
    s_loc = jnp.arange(2 * BLOCK)[None, :]
    dist = t_loc - s_loc
    d = jnp.maximum(dist, 0)
    df = jnp.maximum(d, 1).astype(F32)
    large = max_exact + (jnp.log(df / max_exact) / math.log(REL_MAX_DIST / max_exact)
                         * (REL_BUCKETS - max_exact)).astype(jnp.int32)
    large = jnp.minimum(large, REL_BUCKETS - 1)
    bucket = jnp.where(d < max_exact, d, large)
    band = (dist >= 0) & (dist < WINDOW)
    return jnp.where(band, bucket, -1).astype(jnp.int32).T


def _aug_placement(shift):
    place = [[0.0] * (2 * LANES) for _ in range(LANES)]
    ones = [0.0] * (2 * LANES)
    shift_lanes = []
    for h in range(FOX_HEADS):
        base = h * AUG_PER_HEAD
        for part in range(3):
            src = part * FOX_HEADS + h
            place[src][base + part] = 1.0
            place[src][LANES + base + 3 + part] = 1.0
            ones[base + 3 + part] = 1.0
            ones[LANES + base + part] = 1.0
        ones[base + 6] = 1.0
        shift_lanes.append(LANES + base + 6)
    consts = jnp.array(ones, F32).at[jnp.array(shift_lanes)].set(-shift)
    return jnp.array(place, BF16), jnp.broadcast_to(consts[None, :], (8, 2 * LANES))


def _tile_gain(g, reps, scale=1.0):
    return (jnp.tile(g.astype(F32), reps) * scale)[None, :]


def _pick_tile(n, target):
    t = min(n, target)
    while n % t:
        t //= 2
    return t


def kernel(x, mem, g_mix, w_in, b_gate, b_forget, qn_swa, kn_swa, sink_swa, rel_bias,
           qn_fox, kn_fox, g_mem, w_mem_kv, qn_mem, kn_mem, w_o_swa, w_o_fox, w_o_mem,
           w_out, g_mlp, w_mlp_up, w_mlp_down):
    batch, seq, _ = x.shape
    n_layers = w_in.shape[0]
    t = batch * seq
    tm = _pick_tile(seq, 512)
    tq = _pick_tile(seq, 512)

    bucket = _t5_bucket_table()
    tri = (jnp.arange(tm)[:, None] >= jnp.arange(tm)[None, :]).astype(BF16)

    x2 = x.reshape(t, D_MODEL)
    mem2 = mem.reshape(batch * N_MEM, D_MODEL)
    for layer in range(n_layers):
        wt = w_in[layer].T
        o = 0
        parts = {}
        for name, width in (("qa", SWA_Q), ("ka", SWA_KV), ("va", SWA_KV), ("qf", FOX_W),
                            ("kf", FOX_W), ("vf", FOX_W), ("fl", FOX_HEADS), ("qm", MEM_W),
                            ("g", GATE_W)):
            parts[name] = wt[o:o + width]
            o += width

        def dup(m):
            return jnp.concatenate([m[:HEAD64], m[:HEAD64], m[HEAD64:], m[HEAD64:]], axis=0)

        w1 = jnp.concatenate(
            [parts["qa"], dup(parts["ka"]), parts["qf"], parts["kf"],
             parts["qm"], parts["g"],
             jnp.pad(parts["fl"], ((0, LANES - FOX_HEADS), (0, 0)))], axis=0).astype(BF16)

        wmt = w_mem_kv[layer].T.astype(BF16)
        mk, mvt = _memkv(mem2, g_mem[layer][None, :], wmt[:MEM_W], wmt[MEM_W:],
                         _tile_gain(kn_mem[layer], MEM_HEADS))

        bound = (HEAD64 ** 0.5 * LOG2E * 1.02 * jnp.max(jnp.abs(qn_fox[layer]))
                 * jnp.max(jnp.abs(kn_fox[layer]))).astype(BF16).astype(F32)
        fast = bound <= FOX_FAST_MAX_SHIFT
        place, consts = _aug_placement(jnp.where(fast, bound, 0.0))
        bf_pad = jnp.pad(b_forget[layer].astype(F32), (0, LANES - FOX_HEADS))[None, :]

        wvt = jnp.concatenate([parts["vf"], parts["va"]], axis=0).astype(BF16)
        qa, ka, vat, qf, kf, vt, qm, gates, fl = _proj(
            x2, g_mix[layer][None, :], w1, wvt, b_gate[layer][None, :],
            _tile_gain(qn_swa[layer], SWA_HEADS, HEAD64 ** -0.5 * LOG2E),
            _tile_gain(kn_swa[layer], 2 * SWA_KV_HEADS),
            _tile_gain(qn_fox[layer], FOX_HEADS, HEAD64 ** -0.5 * LOG2E),
            _tile_gain(kn_fox[layer], FOX_HEADS),
            _tile_gain(qn_mem[layer], MEM_HEADS, MEM_HEAD_DIM ** -0.5 * LOG2E), tm, seq)
        aq, ak = _cum(fl, bf_pad, tri, place, consts, tm, seq)

        oa, om = _local(bucket, rel_bias.astype(F32) * LOG2E, sink_swa[layer].astype(F32) * LOG2E,
                        qa, ka, vat, qm, mk, mvt, batch, seq, tq)
        of = _fox(fast.astype(jnp.int32)[None], qf, aq, kf, ak, vt, batch, seq, tm)

        x2 = _post(x2, oa, of, om, gates, w_o_swa[layer].astype(BF16),
                   w_o_fox[layer].astype(BF16), w_o_mem[layer].astype(BF16),
                   w_out[layer].astype(BF16), g_mlp[layer][None, :],
                   w_mlp_up[layer].astype(BF16), w_mlp_down[layer].astype(BF16), tm)
    return x2.reshape(batch, seq, D_MODEL)
```

```python
import functools
import math

import jax
import jax.numpy as jnp
from jax import lax
from jax.experimental import pallas as pl
from jax.experimental.pallas import tpu as pltpu

F32 = jnp.float32
BF16 = jnp.bfloat16

D_MODEL = 1024
N_MEM = 256
SWA_HEADS = 8
SWA_KV_HEADS = 2
SWA_GROUP = SWA_HEADS // SWA_KV_HEADS
HEAD64 = 64
WINDOW = 128
FOX_HEADS = 8
MEM_HEADS = 4
MEM_HEAD_DIM = 128
D_FF = 4 * D_MODEL
REL_BUCKETS = 32
REL_MAX_DIST = 128
BLOCK = 128
EPS = 1e-6
NEG = -1e30
LOG2E = math.log2(math.e)

LANES = 128
MXU_DIM = 256
VMEM_LIMIT = 56 * 1024 * 1024

SWA_Q = SWA_HEADS * HEAD64
SWA_KV = SWA_KV_HEADS * HEAD64
FOX_W = FOX_HEADS * HEAD64
MEM_W = MEM_HEADS * MEM_HEAD_DIM
GATE_W = 3 * D_MODEL

C_QA = 0
C_KA = C_QA + SWA_Q
C_QF = C_KA + 2 * SWA_KV
C_KF = C_QF + FOX_W
C_QM = C_KF + FOX_W
C_G = C_QM + MEM_W
C_FL = C_G + GATE_W
C_END = C_FL + LANES

AUG_PER_HEAD = 7
FOX_FAST_MAX_SHIFT = 60.0
FOX_SUB = 4
LOCAL_LOOKAHEAD = 4


def _dot(a, b):
    return jnp.dot(a, b, preferred_element_type=F32)


def _dot_nt(a, b):
    return lax.dot_general(a, b, (((1,), (1,)), ((), ())), preferred_element_type=F32)


def _rms_rows(x, gain):
    ms = jnp.mean(x * x, axis=-1, keepdims=True)
    return x * lax.rsqrt(ms + EPS) * gain


def _head_rms(y, gain, head_dim):
    y2 = y * y
    lower = lax.broadcasted_iota(jnp.int32, (y.shape[0], LANES), 1) < HEAD64
    cols = []
    for c in range(0, y.shape[1], LANES):
        blk = y2[:, c:c + LANES]
        total = jnp.sum(blk, axis=-1, keepdims=True)
        if head_dim == LANES:
            cols.append(jnp.broadcast_to(total, blk.shape))
        else:
            low = jnp.sum(jnp.where(lower, blk, 0.0), axis=-1, keepdims=True)
            cols.append(jnp.where(lower, low, total - low))
    ss = jnp.concatenate(cols, axis=1)
    return y * lax.rsqrt(ss * (1.0 / head_dim) + EPS) * gain


def _const_spec(shape):
    return pl.BlockSpec(shape, lambda *_: (0,) * len(shape))


def _params(n_axes, flags=None):
    return pltpu.CompilerParams(dimension_semantics=("arbitrary",) * n_axes,
                                vmem_limit_bytes=VMEM_LIMIT, flags=flags)


def _memkv_kernel(mem_ref, g_ref, wkt_ref, wvt_ref, kn_ref, mk_ref, mvt_ref):
    h = _rms_rows(mem_ref[...], g_ref[...]).astype(BF16)
    mk_ref[...] = _head_rms(_dot_nt(h, wkt_ref[...]), kn_ref[...], MEM_HEAD_DIM).astype(BF16)
    mvt_ref[...] = _dot_nt(wvt_ref[...], h).astype(BF16)


def _memkv(mem2, g_mem, wkt, wvt, kn_t):
    rows = mem2.shape[0]
    nb = rows // N_MEM
    return pl.pallas_call(
        _memkv_kernel,
        grid=(nb,),
        in_specs=[pl.BlockSpec((N_MEM, D_MODEL), lambda i: (i, 0)),
                  _const_spec((1, D_MODEL)),
                  _const_spec((MEM_W, D_MODEL)),
                  _const_spec((MEM_W, D_MODEL)),
                  _const_spec((1, MEM_W))],
        out_specs=[pl.BlockSpec((N_MEM, MEM_W), lambda i: (i, 0)),
                   pl.BlockSpec((None, MEM_W, N_MEM), lambda i: (i, 0, 0))],
        out_shape=[jax.ShapeDtypeStruct((rows, MEM_W), BF16),
                   jax.ShapeDtypeStruct((nb, MEM_W, N_MEM), BF16)],
        compiler_params=_params(1),
        name="memkv",
    )(mem2, g_mem, wkt, wvt, kn_t)


def _proj_kernel(x_ref, g_ref, w_ref, wvt_ref, bg_ref,
                 gqa_ref, gka_ref, gqf_ref, gkf_ref, gqm_ref,
                 qa_ref, ka_ref, vat_ref, qf_ref, kf_ref, vt_ref, qm_ref, gate_ref, fl_ref):
    h = _rms_rows(x_ref[...], g_ref[...]).astype(BF16)
    qkv = _dot_nt(h, w_ref[:C_G, :])
    qa_ref[...] = _head_rms(qkv[:, C_QA:C_KA], gqa_ref[...], HEAD64).astype(BF16)
    ka_ref[...] = _head_rms(qkv[:, C_KA:C_QF], gka_ref[...], HEAD64).astype(BF16)
    qf_ref[...] = _head_rms(qkv[:, C_QF:C_KF], gqf_ref[...], HEAD64).astype(BF16)
    kf_ref[...] = _head_rms(qkv[:, C_KF:C_QM], gkf_ref[...], HEAD64).astype(BF16)
    qm_ref[...] = _head_rms(qkv[:, C_QM:C_G], gqm_ref[...], MEM_HEAD_DIM).astype(BF16)
    vt = _dot_nt(wvt_ref[...], h).astype(BF16)
    for p in range(FOX_HEADS // 2):
        vt_ref[p] = vt[p * LANES:(p + 1) * LANES, :]
    vat_ref[...] = vt[FOX_W:, :]
    zf = _dot_nt(h, w_ref[C_G:C_END, :])
    z = zf[:, :GATE_W] + bg_ref[...]
    gate_ref[...] = (1.0 / (1.0 + jnp.exp(-z))).astype(BF16)
    fl_ref[...] = zf[:, GATE_W:]


def _proj(x2, g_mix, w1, wvt, b_gate, gqa, gka, gqf, gkf, gqm, tm, seq):
    t = x2.shape[0]
    nt = seq // tm
    pairs = FOX_HEADS // 2

    def rows(w, dtype=BF16):
        return jax.ShapeDtypeStruct((t, w), dtype), pl.BlockSpec((tm, w), lambda i: (i, 0))

    outs = [rows(SWA_Q), rows(2 * SWA_KV),
            (jax.ShapeDtypeStruct((t // seq, SWA_KV, seq), BF16),
             pl.BlockSpec((None, SWA_KV, tm), lambda i: (i // nt, 0, i % nt))),
            rows(FOX_W), rows(FOX_W),
            (jax.ShapeDtypeStruct((t // seq, pairs, nt, LANES, tm), BF16),
             pl.BlockSpec((None, pairs, None, LANES, tm), lambda i: (i // nt, 0, i % nt, 0, 0))),
            rows(MEM_W), rows(GATE_W), rows(LANES, F32)]
    out_shape = [o[0] for o in outs]
    out_specs = [o[1] for o in outs]
    return pl.pallas_call(
        _proj_kernel,
        grid=(t // tm,),
        in_specs=[pl.BlockSpec((tm, D_MODEL), lambda i: (i, 0)),
                  _const_spec((1, D_MODEL)),
                  _const_spec((C_END, D_MODEL)),
                  _const_spec((FOX_W + SWA_KV, D_MODEL)),
                  _const_spec((1, GATE_W)),
                  _const_spec((1, SWA_Q)),
                  _const_spec((1, 2 * SWA_KV)),
                  _const_spec((1, FOX_W)),
                  _const_spec((1, FOX_W)),
                  _const_spec((1, MEM_W))],
        out_specs=out_specs,
        out_shape=out_shape,
        compiler_params=_params(1),
        name="proj",
    )(x2, g_mix, w1, wvt, b_gate, gqa, gka, gqf, gkf, gqm)


def _split3(c):
    hi = c.astype(BF16)
    r1 = c - hi.astype(F32)
    mid = r1.astype(BF16)
    lo = (r1 - mid.astype(F32)).astype(BF16)
    return hi, mid, lo


def _pack3(x):
    hi, mid, lo = _split3(x)
    return (hi.astype(F32) + pltpu.roll(mid.astype(F32), FOX_HEADS, 1)
            + pltpu.roll(lo.astype(F32), 2 * FOX_HEADS, 1)).astype(BF16)


def _cum_kernel(fl_ref, bf_ref, tri_ref, place_ref, const_ref, aq_ref, ak_ref, *, tm):
    n_tiles = fl_ref.shape[0] // tm
    lane = lax.broadcasted_iota(jnp.int32, (tm, LANES), 1)
    heads = lane < FOX_HEADS
    tri = tri_ref[...]
    parts = []
    for k in range(n_tiles):
        z = fl_ref[k * tm:(k + 1) * tm, :] + bf_ref[...]
        logf = jnp.where(heads, jnp.minimum(z, 0.0) - jnp.log(1.0 + jnp.exp(-jnp.abs(z))), 0.0)
        parts.append(_dot(tri, _pack3(logf)))
    carry = jnp.zeros((1, LANES), F32)
    packed = []
    for part in parts:
        c = ((part + pltpu.roll(part, LANES - FOX_HEADS, 1))
             + pltpu.roll(part, LANES - 2 * FOX_HEADS, 1)) + carry
        carry = c[tm - 1:tm, :]
        packed.append(_pack3(jnp.where(heads, c * LOG2E, 0.0)))
    for k in range(n_tiles):
        placed = _dot(packed[k], place_ref[...])
        rows = slice(k * tm, (k + 1) * tm)
        aq_ref[rows, :] = (placed[:, :LANES] + const_ref[0:1, :LANES]).astype(BF16)
        ak_ref[rows, :] = (const_ref[0:1, LANES:] - placed[:, LANES:]).astype(BF16)


def _cum(fl, bf_pad, tri, place, consts, tm, seq):
    t = fl.shape[0]
    return pl.pallas_call(
        functools.partial(_cum_kernel, tm=tm),
        grid=(t // seq,),
        in_specs=[pl.BlockSpec((seq, LANES), lambda i: (i, 0)),
                  _const_spec((1, LANES)),
                  _const_spec((tm, tm)),
                  _const_spec((LANES, 2 * LANES)),
                  _const_spec((8, 2 * LANES))],
        out_specs=[pl.BlockSpec((seq, LANES), lambda i: (i, 0)),
                   pl.BlockSpec((seq, LANES), lambda i: (i, 0))],
        out_shape=[jax.ShapeDtypeStruct((t, LANES), BF16)] * 2,
        compiler_params=_params(1),
        name="cum",
    )(fl, bf_pad, tri, place, consts)


def _local_kernel(bucket_ref, relb_ref, sink_ref,
                  qa_ref, kc_ref, kp_ref, vc_ref, vp_ref, qm_ref, mk_ref, mvt_ref,
                  oa_ref, om_ref, tbl_ref, *, tq):
    b = pl.program_id(0)
    i = pl.program_id(1)
    pairs = SWA_HEADS // 2

    @pl.when(jnp.logical_and(b == 0, i == 0))
    def _():
        bucket = bucket_ref[...]
        band = bucket >= 0
        for h in range(SWA_HEADS):
            t = jnp.zeros(bucket.shape, F32)
            for k in range(REL_BUCKETS):
                t = jnp.where(bucket == k, relb_ref[k, h], t)
            tbl_ref[h // 2, :, (h % 2) * BLOCK:(h % 2 + 1) * BLOCK] = jnp.where(band, t, NEG)

    lane = lax.broadcasted_iota(jnp.int32, (BLOCK, LANES), 1)
    lower = lane < HEAD64
    first_head = lax.broadcasted_iota(jnp.int32, (1, 2 * BLOCK), 1) < BLOCK
    key_row = lax.broadcasted_iota(jnp.int32, (2 * BLOCK, 2 * BLOCK), 0)
    pad_rows = jnp.logical_and(key_row < BLOCK, i == 0)
    ones = jnp.ones((16, 2 * BLOCK), BF16)

    def windows(j):
        r0 = j * BLOCK
        if j == 0:
            return (jnp.concatenate([kp_ref[...], kc_ref[0:BLOCK, :]], axis=0),
                    jnp.concatenate([vp_ref[...], vc_ref[:, 0:BLOCK]], axis=1))
        return kc_ref[r0 - BLOCK:r0 + BLOCK, :], vc_ref[:, r0 - BLOCK:r0 + BLOCK]

    def swa_scores(j, hb):
        g = (2 * hb) // SWA_GROUP
        qblk = qa_ref[j * BLOCK:(j + 1) * BLOCK, hb * LANES:(hb + 1) * LANES]
        qstack = jnp.concatenate([jnp.where(lower, qblk, 0),
                                  jnp.where(lower, 0, qblk)], axis=0)
        s = _dot_nt(windows(j)[0][:, g * LANES:(g + 1) * LANES], qstack) + tbl_ref[hb]
        return jnp.where(pad_rows, NEG, s) if j == 0 else s

    def swa_finish(j, hb, s):
        g = (2 * hb) // SWA_GROUP
        sink = jnp.where(first_head, sink_ref[2 * hb], sink_ref[2 * hb + 1])
        m = jnp.maximum(jnp.max(s, axis=0, keepdims=True), sink)
        p = jnp.exp2(s - m).astype(BF16)
        vt_ones = jnp.concatenate([windows(j)[1][g * HEAD64:(g + 1) * HEAD64, :], ones], axis=0)
        acc = _dot(vt_ones, p)
        o = acc[:HEAD64, :] / (acc[HEAD64:HEAD64 + 1, :] + jnp.exp2(sink - m))
        o = jnp.concatenate([o[:, :BLOCK], o[:, BLOCK:]], axis=0)
        oa_ref[j * BLOCK:(j + 1) * BLOCK, hb * LANES:(hb + 1) * LANES] = o.T.astype(BF16)

    ones_m = jnp.ones((16, N_MEM), BF16)

    def mem_scores(h):
        sl = slice(h * MEM_HEAD_DIM, (h + 1) * MEM_HEAD_DIM)
        return _dot_nt(mk_ref[:, sl], qm_ref[:, sl])

    def mem_finish(h, s):
        sl = slice(h * MEM_HEAD_DIM, (h + 1) * MEM_HEAD_DIM)
        m = jnp.max(s, axis=0, keepdims=True)
        p = jnp.exp2(s - m).astype(BF16)
        acc = _dot(jnp.concatenate([mvt_ref[sl, :], ones_m], axis=0), p)
        o = acc[:MEM_HEAD_DIM, :] / acc[MEM_HEAD_DIM:MEM_HEAD_DIM + 1, :]
        om_ref[:, sl] = o.T.astype(BF16)

    units = [(swa_scores, swa_finish, (j, hb)) for j in range(tq // BLOCK) for hb in range(pairs)]
    units += [(mem_scores, mem_finish, (h,)) for h in range(MEM_HEADS)]
    pending = []
    for n in range(len(units) + LOCAL_LOOKAHEAD):
        if n < len(units):
            score_fn, _, args = units[n]
            pending.append(score_fn(*args))
        if n >= LOCAL_LOOKAHEAD:
            _, finish_fn, args = units[n - LOCAL_LOOKAHEAD]
            finish_fn(*args, pending[n - LOCAL_LOOKAHEAD])


def _local(bucket_t, rel_bias, sinks, qa, ka, vat, qm, mk, mvt, batch, seq, tq):
    t = qa.shape[0]
    nq = seq // tq
    sub = tq // BLOCK

    def cur(b, i):
        return (b * nq + i, 0)

    def prev(b, i):
        return (jnp.maximum((b * nq + i) * sub - 1, 0), 0)

    smem = pl.BlockSpec(memory_space=pltpu.SMEM)
    return pl.pallas_call(
        functools.partial(_local_kernel, tq=tq),
        grid=(batch, nq),
        in_specs=[_const_spec((2 * BLOCK, BLOCK)), smem, smem,
                  pl.BlockSpec((tq, SWA_Q), cur),
                  pl.BlockSpec((tq, 2 * SWA_KV), cur),
                  pl.BlockSpec((BLOCK, 2 * SWA_KV), prev),
                  pl.BlockSpec((None, SWA_KV, tq), lambda b, i: (b, 0, i)),
                  pl.BlockSpec((None, SWA_KV, BLOCK),
                               lambda b, i: (b, 0, jnp.maximum(i * sub - 1, 0))),
                  pl.BlockSpec((tq, MEM_W), cur),
                  pl.BlockSpec((N_MEM, MEM_W), lambda b, i: (b, 0)),
                  pl.BlockSpec((None, MEM_W, N_MEM), lambda b, i: (b, 0, 0))],
        out_specs=[pl.BlockSpec((tq, SWA_Q), cur),
                   pl.BlockSpec((tq, MEM_W), cur)],
        out_shape=[jax.ShapeDtypeStruct((t, SWA_Q), BF16),
                   jax.ShapeDtypeStruct((t, MEM_W), BF16)],
        scratch_shapes=[pltpu.VMEM((SWA_HEADS // 2, 2 * BLOCK, 2 * BLOCK), F32)],
        compiler_params=_params(2),
        name="local",
    )(bucket_t, rel_bias, sinks, qa, ka, ka, vat, vat, qm, mk, mvt)


def _fox_kernel(fast_ref, q_ref, aq_ref, k_ref, ak_ref, vt_ref, o_ref,
                sa_ref, sb_ref, acc_ref, m_ref, *, tk, n_sub, n_steps):
    pair = pl.program_id(1)
    g = pl.program_id(2)
    lane = lax.broadcasted_iota(jnp.int32, (tk, LANES), 1)
    half = [lane < HEAD64, lane >= HEAD64]
    qcat = []
    for sub in range(n_sub):
        q = q_ref[sub * tk:(sub + 1) * tk, :]
        aq = aq_ref[sub * tk:(sub + 1) * tk, :]
        zero = jnp.zeros_like(q)
        per_head = []
        for h in range(2):
            lo = (2 * pair + h) * AUG_PER_HEAD
            mine = jnp.logical_and(lane >= lo, lane < lo + AUG_PER_HEAD)
            per_head.append(jnp.concatenate([jnp.where(half[h], q, zero),
                                             jnp.where(mine, aq, zero)], axis=1))
        qcat.append(per_head)
    ones = jnp.ones((16, tk), BF16)
    acc_ref[...] = jnp.zeros_like(acc_ref)
    key = lax.broadcasted_iota(jnp.int32, (tk, tk), 0)
    qry = lax.broadcasted_iota(jnp.int32, (tk, tk), 1)
    causal = key <= qry

    def kcat(j):
        r0 = pl.multiple_of(j * tk, tk)
        return jnp.concatenate([k_ref[pl.ds(r0, tk), :], ak_ref[pl.ds(r0, tk), :]], axis=1)

    def vt_ones(j, h):
        return jnp.concatenate([vt_ref[j, h * HEAD64:(h + 1) * HEAD64, :], ones], axis=0)

    def qk(sub, j, s_ref):
        kc = kcat(j)
        for h in range(2):
            for c in range(0, tk, MXU_DIM):
                s_ref[h, :, c:c + MXU_DIM] = _dot_nt(kc, qcat[sub][h][c:c + MXU_DIM, :])

    @pl.when(fast_ref[0] == 1)
    def _():
        bufs = (sa_ref, sb_ref)

        def exp_pv(sub, j, s_ref, masked):
            for h in range(2):
                for c in range(0, tk, MXU_DIM):
                    s = s_ref[h, :, c:c + MXU_DIM]
                    if masked:
                        s = jnp.where(causal[:, c:c + MXU_DIM], s, NEG)
                    acc_ref[2 * sub + h, :, c:c + MXU_DIM] += _dot(vt_ones(j, h),
                                                                   jnp.exp2(s).astype(BF16))

        full = [(sub, jj) for jj in range(n_sub) for sub in range(n_sub)]
        diag = [(sub, jj) for jj in range(n_sub) for sub in range(jj, n_sub)]
        assert len(full) % 2 == 0
        qk(0, 0, bufs[0])

        def body(t, carry):
            for n, (sub, jj) in enumerate(full):
                nsub, njj = full[n + 1] if n + 1 < len(full) else (0, n_sub)
                qk(nsub, n_sub * t + njj, bufs[(n + 1) % 2])
                exp_pv(sub, n_sub * t + jj, bufs[n % 2], False)
            return carry

        if n_steps > 1:
            lax.fori_loop(0, g, body, 0)
        for n, (sub, jj) in enumerate(diag):
            if n + 1 < len(diag):
                qk(diag[n + 1][0], n_sub * g + diag[n + 1][1], bufs[(n + 1) % 2])
            exp_pv(sub, n_sub * g + jj, bufs[n % 2], sub == jj)

    @pl.when(fast_ref[0] == 0)
    def _():
        m_ref[...] = jnp.full(m_ref.shape, NEG, F32)
        for sub in range(n_sub):
            last = n_sub * g + sub

            def body(j, carry, sub=sub, last=last):
                qk(sub, j, sa_ref)
                for h in range(2):
                    u = 2 * sub + h
                    s = jnp.where(jnp.logical_or(causal, j < last), sa_ref[h], NEG)
                    m_prev = m_ref[u]
                    m_next = jnp.maximum(m_prev, jnp.max(s, axis=0, keepdims=True))
                    alpha = jnp.exp2(m_prev - m_next)
                    p = jnp.exp2(s - m_next).astype(BF16)
                    acc_ref[u] = alpha * acc_ref[u] + _dot(vt_ones(j, h), p)
                    m_ref[u] = m_next
                return carry

            lax.fori_loop(0, last + 1, body, 0)

    for sub in range(n_sub):
        halves = []
        for h in range(2):
            acc = acc_ref[2 * sub + h]
            halves.append(acc[:HEAD64, :] / acc[HEAD64:HEAD64 + 1, :])
        o_ref[sub * tk:(sub + 1) * tk, :] = jnp.concatenate(halves, axis=0).T.astype(BF16)


def _fox(fast, qf, aq, kf, ak, vt, batch, seq, tk):
    t = qf.shape[0]
    n_sub = min(FOX_SUB, seq // tk)
    tq = n_sub * tk
    assert seq % tq == 0, (seq, tq)
    nq = seq // tq
    pairs = FOX_HEADS // 2
    return pl.pallas_call(
        functools.partial(_fox_kernel, tk=tk, n_sub=n_sub, n_steps=nq),
        grid=(batch, pairs, nq),
        in_specs=[pl.BlockSpec(memory_space=pltpu.SMEM),
                  pl.BlockSpec((tq, LANES), lambda b, p, i: (b * nq + i, p)),
                  pl.BlockSpec((tq, LANES), lambda b, p, i: (b * nq + i, 0)),
                  pl.BlockSpec((seq, LANES), lambda b, p, i: (b, p)),
                  pl.BlockSpec((seq, LANES), lambda b, p, i: (b, 0)),
                  pl.BlockSpec((None, None, seq // tk, LANES, tk),
                               lambda b, p, i: (b, p, 0, 0, 0))],
        out_specs=pl.BlockSpec((tq, LANES), lambda b, p, i: (b * nq + i, p)),
        out_shape=jax.ShapeDtypeStruct((t, FOX_W), BF16),
        scratch_shapes=[pltpu.VMEM((2, tk, tk), F32),
                        pltpu.VMEM((2, tk, tk), F32),
                        pltpu.VMEM((2 * n_sub, HEAD64 + 16, tk), F32),
                        pltpu.VMEM((2 * n_sub, 1, tk), F32)],
        compiler_params=_params(3),
        name="fox",
    )(fast, qf, aq, kf, ak, vt)


def _post_kernel(x_ref, oa_ref, of_ref, om_ref, gate_ref, wa_ref, wf_ref, wm_ref, wo_ref,
                 g_ref, wu_ref, wd_ref, y_ref, *, chunk):
    ya = _dot(oa_ref[...], wa_ref[...])
    yf = _dot(of_ref[...], wf_ref[...])
    ym = _dot(om_ref[...], wm_ref[...])
    merged = (gate_ref[:, 0:D_MODEL].astype(F32) * ya
              + gate_ref[:, D_MODEL:2 * D_MODEL].astype(F32) * yf
              + gate_ref[:, 2 * D_MODEL:].astype(F32) * ym)
    x1 = x_ref[...] + _dot(merged.astype(BF16), wo_ref[...])
    h = _rms_rows(x1, g_ref[...]).astype(BF16)
    acc = x1
    for c in range(0, D_FF, chunk):
        u = jnp.maximum(_dot(h, wu_ref[:, c:c + chunk]), 0.0)
        acc = acc + _dot((u * u).astype(BF16), wd_ref[c:c + chunk, :])
    y_ref[...] = acc


def _post(x2, oa, of, om, gates, wa, wf, wm, wo, g_mlp, wu, wd, tm, chunk=1024):
    t = x2.shape[0]
    row = lambda w: pl.BlockSpec((tm, w), lambda i: (i, 0))
    resident = lambda shape: pl.BlockSpec(shape, lambda i: (0,) * len(shape),
                                          pipeline_mode=pl.Buffered(1))
    return pl.pallas_call(
        functools.partial(_post_kernel, chunk=chunk),
        grid=(t // tm,),
        in_specs=[row(D_MODEL), row(SWA_Q), row(FOX_W), row(MEM_W), row(GATE_W),
                  resident((SWA_Q, D_MODEL)), resident((FOX_W, D_MODEL)),
                  resident((MEM_W, D_MODEL)), resident((D_MODEL, D_MODEL)),
                  resident((1, D_MODEL)), resident((D_MODEL, D_FF)), resident((D_FF, D_MODEL))],
        out_specs=row(D_MODEL),
        out_shape=jax.ShapeDtypeStruct((t, D_MODEL), F32),
        compiler_params=_params(1),
        name="post",
    )(x2, oa, of, om, gates, wa, wf, wm, wo, g_mlp, wu, wd)


def _t5_bucket_table():
    max_exact = REL_BUCKETS // 2
    t_loc = jnp.arange(BLOCK)[:, None] + BLOCK
    s_loc = jnp.arange(2 * BLOCK)[None, :]
    dist = t_loc - s_loc
    d = jnp.maximum(dist, 0)
    df = jnp.maximum(d, 1).astype(F32)
    large = max_exact + (jnp.log(df / max_exact) / math.log(REL_MAX_DIST / max_exact)
                         * (REL_BUCKETS - max_exact)).astype(jnp.int32)
    large = jnp.minimum(large, REL_BUCKETS - 1)
    bucket = jnp.where(d < max_exact, d, large)
    band = (dist >= 0) & (dist < WINDOW)
    return jnp.where(band, bucket, -1).astype(jnp.int32).T


def _aug_placement(shift):
    place = [[0.0] * (2 * LANES) for _ in range(LANES)]
    ones = [0.0] * (2 * LANES)
    shift_lanes = []
    for h in range(FOX_HEADS):
        base = h * AUG_PER_HEAD
        for part in range(3):
            src = part * FOX_HEADS + h
            place[src][base + part] = 1.0
            place[src][LANES + base + 3 + part] = 1.0
            ones[base + 3 + part] = 1.0
            ones[LANES + base + part] = 1.0
        ones[base + 6] = 1.0
        shift_lanes.append(LANES + base + 6)
    consts = jnp.array(ones, F32).at[jnp.array(shift_lanes)].set(-shift)
    return jnp.array(place, BF16), jnp.broadcast_to(consts[None, :], (8, 2 * LANES))


def _tile_gain(g, reps, scale=1.0):
    return (jnp.tile(g.astype(F32), reps) * scale)[None, :]


def _pick_tile(n, target):
    t = min(n, target)
    while n % t:
        t //= 2
    return t


def kernel(x, mem, g_mix, w_in, b_gate, b_forget, qn_swa, kn_swa, sink_swa, rel_bias,
           qn_fox, kn_fox, g_mem, w_mem_kv, qn_mem, kn_mem, w_o_swa, w_o_fox, w_o_mem,
           w_out, g_mlp, w_mlp_up, w_mlp_down):
    batch, seq, _ = x.shape
    n_layers = w_in.shape[0]
    t = batch * seq
    tm = _pick_tile(seq, 512)
    tq = _pick_tile(seq, 512)

    bucket = _t5_bucket_table()
    tri = (jnp.arange(tm)[:, None] >= jnp.arange(tm)[None, :]).astype(BF16)

    x2 = x.reshape(t, D_MODEL)
    mem2 = mem.reshape(batch * N_MEM, D_MODEL)
    for layer in range(n_layers):
        wt = w_in[layer].T
        o = 0
        parts = {}
        for name, width in (("qa", SWA_Q), ("ka", SWA_KV), ("va", SWA_KV), ("qf", FOX_W),
                            ("kf", FOX_W), ("vf", FOX_W), ("fl", FOX_HEADS), ("qm", MEM_W),
                            ("g", GATE_W)):
            parts[name] = wt[o:o + width]
            o += width

        def dup(m):
            return jnp.concatenate([m[:HEAD64], m[:HEAD64], m[HEAD64:], m[HEAD64:]], axis=0)

        w1 = jnp.concatenate(
            [parts["qa"], dup(parts["ka"]), parts["qf"], parts["kf"],
             parts["qm"], parts["g"],
             jnp.pad(parts["fl"], ((0, LANES - FOX_HEADS), (0, 0)))], axis=0).astype(BF16)

        wmt = w_mem_kv[layer].T.astype(BF16)
        mk, mvt = _memkv(mem2, g_mem[layer][None, :], wmt[:MEM_W], wmt[MEM_W:],
                         _tile_gain(kn_mem[layer], MEM_HEADS))

        bound = (HEAD64 ** 0.5 * LOG2E * 1.02 * jnp.max(jnp.abs(qn_fox[layer]))
                 * jnp.max(jnp.abs(kn_fox[layer]))).astype(BF16).astype(F32)
        fast = bound <= FOX_FAST_MAX_SHIFT
        place, consts = _aug_placement(jnp.where(fast, bound, 0.0))
        bf_pad = jnp.pad(b_forget[layer].astype(F32), (0, LANES - FOX_HEADS))[None, :]

        wvt = jnp.concatenate([parts["vf"], parts["va"]], axis=0).astype(BF16)
        qa, ka, vat, qf, kf, vt, qm, gates, fl = _proj(
            x2, g_mix[layer][None, :], w1, wvt, b_gate[layer][None, :],
            _tile_gain(qn_swa[layer], SWA_HEADS, HEAD64 ** -0.5 * LOG2E),
            _tile_gain(kn_swa[layer], 2 * SWA_KV_HEADS),
            _tile_gain(qn_fox[layer], FOX_HEADS, HEAD64 ** -0.5 * LOG2E),
            _tile_gain(kn_fox[layer], FOX_HEADS),
            _tile_gain(qn_mem[layer], MEM_HEADS, MEM_HEAD_DIM ** -0.5 * LOG2E), tm, seq)
        aq, ak = _cum(fl, bf_pad, tri, place, consts, tm, seq)

        oa, om = _local(bucket, rel_bias.astype(F32) * LOG2E, sink_swa[layer].astype(F32) * LOG2E,
                        qa, ka, vat, qm, mk, mvt, batch, seq, tq)
        of = _fox(fast.astype(jnp.int32)[None], qf, aq, kf, ak, vt, batch, seq, tm)

        x2 = _post(x2, oa, of, om, gates, w_o_swa[layer].astype(BF16),
                   w_o_fox[layer].astype(BF16), w_o_mem[layer].astype(BF16),
                   w_out[layer].astype(BF16), g_mlp[layer][None, :],
                   w_mlp_up[layer].astype(BF16), w_mlp_down[layer].astype(BF16), tm)
    return x2.reshape(batch, seq, D_MODEL)
```

```python
import functools
import math

import jax
import jax.numpy as jnp
from jax import lax
from jax.experimental import pallas as pl
from jax.experimental.pallas import tpu as pltpu

F32 = jnp.float32
BF16 = jnp.bfloat16

D_MODEL = 1024
N_MEM = 256
SWA_HEADS = 8
SWA_KV_HEADS = 2
SWA_GROUP = SWA_HEADS // SWA_KV_HEADS
HEAD64 = 64
WINDOW = 128
FOX_HEADS = 8
MEM_HEADS = 4
MEM_HEAD_DIM = 128
D_FF = 4 * D_MODEL
REL_BUCKETS = 32
REL_MAX_DIST = 128
BLOCK = 128
EPS = 1e-6
NEG = -1e30
LOG2E = math.log2(math.e)

LANES = 128
MXU_DIM = 256
VMEM_LIMIT = 56 * 1024 * 1024

SWA_Q = SWA_HEADS * HEAD64
SWA_KV = SWA_KV_HEADS * HEAD64
FOX_W = FOX_HEADS * HEAD64
MEM_W = MEM_HEADS * MEM_HEAD_DIM
GATE_W = 3 * D_MODEL

C_QA = 0
C_KA = C_QA + SWA_Q
C_QF = C_KA + 2 * SWA_KV
C_KF = C_QF + FOX_W
C_QM = C_KF + FOX_W
C_G = C_QM + MEM_W
C_FL = C_G + GATE_W
C_END = C_FL + LANES

AUG_PER_HEAD = 7
FOX_FAST_MAX_SHIFT = 60.0
FOX_SUB = 8
LOCAL_LOOKAHEAD = 4


def _dot(a, b):
    return jnp.dot(a, b, preferred_element_type=F32)


def _dot_nt(a, b):
    return lax.dot_general(a, b, (((1,), (1,)), ((), ())), preferred_element_type=F32)


def _rms_rows(x, gain):
    ms = jnp.mean(x * x, axis=-1, keepdims=True)
    return x * lax.rsqrt(ms + EPS) * gain


def _head_rms(y, gain, head_dim):
    y2 = y * y
    lower = lax.broadcasted_iota(jnp.int32, (y.shape[0], LANES), 1) < HEAD64
    cols = []
    for c in range(0, y.shape[1], LANES):
        blk = y2[:, c:c + LANES]
        total = jnp.sum(blk, axis=-1, keepdims=True)
        if head_dim == LANES:
            cols.append(jnp.broadcast_to(total, blk.shape))
        else:
            low = jnp.sum(jnp.where(lower, blk, 0.0), axis=-1, keepdims=True)
            cols.append(jnp.where(lower, low, total - low))
    ss = jnp.concatenate(cols, axis=1)
    return y * lax.rsqrt(ss * (1.0 / head_dim) + EPS) * gain


def _const_spec(shape):
    return pl.BlockSpec(shape, lambda *_: (0,) * len(shape))


def _params(n_axes, flags=None):
    return pltpu.CompilerParams(dimension_semantics=("arbitrary",) * n_axes,
                                vmem_limit_bytes=VMEM_LIMIT, flags=flags)


def _memkv_kernel(mem_ref, g_ref, wkt_ref, wvt_ref, kn_ref, mk_ref, mvt_ref):
    h = _rms_rows(mem_ref[...], g_ref[...]).astype(BF16)
    mk_ref[...] = _head_rms(_dot_nt(h, wkt_ref[...]), kn_ref[...], MEM_HEAD_DIM).astype(BF16)
    mvt_ref[...] = _dot_nt(wvt_ref[...], h).astype(BF16)


def _memkv(mem2, g_mem, wkt, wvt, kn_t):
    rows = mem2.shape[0]
    nb = rows // N_MEM
    return pl.pallas_call(
        _memkv_kernel,
        grid=(nb,),
        in_specs=[pl.BlockSpec((N_MEM, D_MODEL), lambda i: (i, 0)),
                  _const_spec((1, D_MODEL)),
                  _const_spec((MEM_W, D_MODEL)),
                  _const_spec((MEM_W, D_MODEL)),
                  _const_spec((1, MEM_W))],
        out_specs=[pl.BlockSpec((N_MEM, MEM_W), lambda i: (i, 0)),
                   pl.BlockSpec((None, MEM_W, N_MEM), lambda i: (i, 0, 0))],
        out_shape=[jax.ShapeDtypeStruct((rows, MEM_W), BF16),
                   jax.ShapeDtypeStruct((nb, MEM_W, N_MEM), BF16)],
        compiler_params=_params(1),
        name="memkv",
    )(mem2, g_mem, wkt, wvt, kn_t)


def _proj_kernel(x_ref, g_ref, w_ref, wvt_ref, bg_ref,
                 gqa_ref, gka_ref, gqf_ref, gkf_ref, gqm_ref,
                 qa_ref, ka_ref, vat_ref, qf_ref, kf_ref, vt_ref, qm_ref, gate_ref, fl_ref):
    h = _rms_rows(x_ref[...], g_ref[...]).astype(BF16)
    qkv = _dot_nt(h, w_ref[:C_G, :])
    qa_ref[...] = _head_rms(qkv[:, C_QA:C_KA], gqa_ref[...], HEAD64).astype(BF16)
    ka_ref[...] = _head_rms(qkv[:, C_KA:C_QF], gka_ref[...], HEAD64).astype(BF16)
    qf_ref[...] = _head_rms(qkv[:, C_QF:C_KF], gqf_ref[...], HEAD64).astype(BF16)
    kf_ref[...] = _head_rms(qkv[:, C_KF:C_QM], gkf_ref[...], HEAD64).astype(BF16)
    qm_ref[...] = _head_rms(qkv[:, C_QM:C_G], gqm_ref[...], MEM_HEAD_DIM).astype(BF16)
    vt = _dot_nt(wvt_ref[...], h).astype(BF16)
    for p in range(FOX_HEADS // 2):
        vt_ref[p] = vt[p * LANES:(p + 1) * LANES, :]
    vat_ref[...] = vt[FOX_W:, :]
    zf = _dot_nt(h, w_ref[C_G:C_END, :])
    z = zf[:, :GATE_W] + bg_ref[...]
    gate_ref[...] = (1.0 / (1.0 + jnp.exp(-z))).astype(BF16)
    fl_ref[...] = zf[:, GATE_W:]


def _proj(x2, g_mix, w1, wvt, b_gate, gqa, gka, gqf, gkf, gqm, tm, seq):
    t = x2.shape[0]
    nt = seq // tm
    pairs = FOX_HEADS // 2

    def rows(w, dtype=BF16):
        return jax.ShapeDtypeStruct((t, w), dtype), pl.BlockSpec((tm, w), lambda i: (i, 0))

    outs = [rows(SWA_Q), rows(2 * SWA_KV),
            (jax.ShapeDtypeStruct((t // seq, SWA_KV, seq), BF16),
             pl.BlockSpec((None, SWA_KV, tm), lambda i: (i // nt, 0, i % nt))),
            rows(FOX_W), rows(FOX_W),
            (jax.ShapeDtypeStruct((t // seq, pairs, nt, LANES, tm), BF16),
             pl.BlockSpec((None, pairs, None, LANES, tm), lambda i: (i // nt, 0, i % nt, 0, 0))),
            rows(MEM_W), rows(GATE_W), rows(LANES, F32)]
    out_shape = [o[0] for o in outs]
    out_specs = [o[1] for o in outs]
    return pl.pallas_call(
        _proj_kernel,
        grid=(t // tm,),
        in_specs=[pl.BlockSpec((tm, D_MODEL), lambda i: (i, 0)),
                  _const_spec((1, D_MODEL)),
                  _const_spec((C_END, D_MODEL)),
                  _const_spec((FOX_W + SWA_KV, D_MODEL)),
                  _const_spec((1, GATE_W)),
                  _const_spec((1, SWA_Q)),
                  _const_spec((1, 2 * SWA_KV)),
                  _const_spec((1, FOX_W)),
                  _const_spec((1, FOX_W)),
                  _const_spec((1, MEM_W))],
        out_specs=out_specs,
        out_shape=out_shape,
        compiler_params=_params(1),
        name="proj",
    )(x2, g_mix, w1, wvt, b_gate, gqa, gka, gqf, gkf, gqm)


def _split3(c):
    hi = c.astype(BF16)
    r1 = c - hi.astype(F32)
    mid = r1.astype(BF16)
    lo = (r1 - mid.astype(F32)).astype(BF16)
    return hi, mid, lo


def _pack3(x):
    hi, mid, lo = _split3(x)
    return (hi.astype(F32) + pltpu.roll(mid.astype(F32), FOX_HEADS, 1)
            + pltpu.roll(lo.astype(F32), 2 * FOX_HEADS, 1)).astype(BF16)


def _cum_kernel(fl_ref, bf_ref, tri_ref, place_ref, const_ref, aq_ref, ak_ref, *, tm):
    n_tiles = fl_ref.shape[0] // tm
    lane = lax.broadcasted_iota(jnp.int32, (tm, LANES), 1)
    heads = lane < FOX_HEADS
    tri = tri_ref[...]
    parts = []
    for k in range(n_tiles):
        z = fl_ref[k * tm:(k + 1) * tm, :] + bf_ref[...]
        logf = jnp.where(heads, jnp.minimum(z, 0.0) - jnp.log(1.0 + jnp.exp(-jnp.abs(z))), 0.0)
        parts.append(_dot(tri, _pack3(logf)))
    carry = jnp.zeros((1, LANES), F32)
    packed = []
    for part in parts:
        c = ((part + pltpu.roll(part, LANES - FOX_HEADS, 1))
             + pltpu.roll(part, LANES - 2 * FOX_HEADS, 1)) + carry
        carry = c[tm - 1:tm, :]
        packed.append(_pack3(jnp.where(heads, c * LOG2E, 0.0)))
    for k in range(n_tiles):
        placed = _dot(packed[k], place_ref[...])
        rows = slice(k * tm, (k + 1) * tm)
        aq_ref[rows, :] = (placed[:, :LANES] + const_ref[0:1, :LANES]).astype(BF16)
        ak_ref[rows, :] = (const_ref[0:1, LANES:] - placed[:, LANES:]).astype(BF16)


def _cum(fl, bf_pad, tri, place, consts, tm, seq):
    t = fl.shape[0]
    return pl.pallas_call(
        functools.partial(_cum_kernel, tm=tm),
        grid=(t // seq,),
        in_specs=[pl.BlockSpec((seq, LANES), lambda i: (i, 0)),
                  _const_spec((1, LANES)),
                  _const_spec((tm, tm)),
                  _const_spec((LANES, 2 * LANES)),
                  _const_spec((8, 2 * LANES))],
        out_specs=[pl.BlockSpec((seq, LANES), lambda i: (i, 0)),
                   pl.BlockSpec((seq, LANES), lambda i: (i, 0))],
        out_shape=[jax.ShapeDtypeStruct((t, LANES), BF16)] * 2,
        compiler_params=_params(1),
        name="cum",
    )(fl, bf_pad, tri, place, consts)


def _local_kernel(bucket_ref, relb_ref, sink_ref,
                  qa_ref, kc_ref, kp_ref, vc_ref, vp_ref, qm_ref, mk_ref, mvt_ref,
                  oa_ref, om_ref, tbl_ref, *, tq):
    b = pl.program_id(0)
    i = pl.program_id(1)
    pairs = SWA_HEADS // 2

    @pl.when(jnp.logical_and(b == 0, i == 0))
    def _():
        bucket = bucket_ref[...]
        band = bucket >= 0
        for h in range(SWA_HEADS):
            t = jnp.zeros(bucket.shape, F32)
            for k in range(REL_BUCKETS):
                t = jnp.where(bucket == k, relb_ref[k, h], t)
            tbl_ref[h // 2, :, (h % 2) * BLOCK:(h % 2 + 1) * BLOCK] = jnp.where(band, t, NEG)

    lane = lax.broadcasted_iota(jnp.int32, (BLOCK, LANES), 1)
    lower = lane < HEAD64
    first_head = lax.broadcasted_iota(jnp.int32, (1, 2 * BLOCK), 1) < BLOCK
    key_row = lax.broadcasted_iota(jnp.int32, (2 * BLOCK, 2 * BLOCK), 0)
    pad_rows = jnp.logical_and(key_row < BLOCK, i == 0)
    ones = jnp.ones((16, 2 * BLOCK), BF16)

    def windows(j):
        r0 = j * BLOCK
        if j == 0:
            return (jnp.concatenate([kp_ref[...], kc_ref[0:BLOCK, :]], axis=0),
                    jnp.concatenate([vp_ref[...], vc_ref[:, 0:BLOCK]], axis=1))
        return kc_ref[r0 - BLOCK:r0 + BLOCK, :], vc_ref[:, r0 - BLOCK:r0 + BLOCK]

    def swa_scores(j, hb):
        g = (2 * hb) // SWA_GROUP
        qblk = qa_ref[j * BLOCK:(j + 1) * BLOCK, hb * LANES:(hb + 1) * LANES]
        qstack = jnp.concatenate([jnp.where(lower, qblk, 0),
                                  jnp.where(lower, 0, qblk)], axis=0)
        s = _dot_nt(windows(j)[0][:, g * LANES:(g + 1) * LANES], qstack) + tbl_ref[hb]
        return jnp.where(pad_rows, NEG, s) if j == 0 else s

    def swa_finish(j, hb, s):
        g = (2 * hb) // SWA_GROUP
        sink = jnp.where(first_head, sink_ref[2 * hb], sink_ref[2 * hb + 1])
        m = jnp.maximum(jnp.max(s, axis=0, keepdims=True), sink)
        p = jnp.exp2(s - m).astype(BF16)
        vt_ones = jnp.concatenate([windows(j)[1][g * HEAD64:(g + 1) * HEAD64, :], ones], axis=0)
        acc = _dot(vt_ones, p)
        o = acc[:HEAD64, :] / (acc[HEAD64:HEAD64 + 1, :] + jnp.exp2(sink - m))
        o = jnp.concatenate([o[:, :BLOCK], o[:, BLOCK:]], axis=0)
        oa_ref[j * BLOCK:(j + 1) * BLOCK, hb * LANES:(hb + 1) * LANES] = o.T.astype(BF16)

    ones_m = jnp.ones((16, N_MEM), BF16)

    def mem_scores(h):
        sl = slice(h * MEM_HEAD_DIM, (h + 1) * MEM_HEAD_DIM)
        return _dot_nt(mk_ref[:, sl], qm_ref[:, sl])

    def mem_finish(h, s):
        sl = slice(h * MEM_HEAD_DIM, (h + 1) * MEM_HEAD_DIM)
        m = jnp.max(s, axis=0, keepdims=True)
        p = jnp.exp2(s - m).astype(BF16)
        acc = _dot(jnp.concatenate([mvt_ref[sl, :], ones_m], axis=0), p)
        o = acc[:MEM_HEAD_DIM, :] / acc[MEM_HEAD_DIM:MEM_HEAD_DIM + 1, :]
        om_ref[:, sl] = o.T.astype(BF16)

    units = [(swa_scores, swa_finish, (j, hb)) for j in range(tq // BLOCK) for hb in range(pairs)]
    units += [(mem_scores, mem_finish, (h,)) for h in range(MEM_HEADS)]
    pending = []
    for n in range(len(units) + LOCAL_LOOKAHEAD):
        if n < len(units):
            score_fn, _, args = units[n]
            pending.append(score_fn(*args))
        if n >= LOCAL_LOOKAHEAD:
            _, finish_fn, args = units[n - LOCAL_LOOKAHEAD]
            finish_fn(*args, pending[n - LOCAL_LOOKAHEAD])


def _local(bucket_t, rel_bias, sinks, qa, ka, vat, qm, mk, mvt, batch, seq, tq):
    t = qa.shape[0]
    nq = seq // tq
    sub = tq // BLOCK

    def cur(b, i):
        return (b * nq + i, 0)

    def prev(b, i):
        return (jnp.maximum((b * nq + i) * sub - 1, 0), 0)

    smem = pl.BlockSpec(memory_space=pltpu.SMEM)
    return pl.pallas_call(
        functools.partial(_local_kernel, tq=tq),
        grid=(batch, nq),
        in_specs=[_const_spec((2 * BLOCK, BLOCK)), smem, smem,
                  pl.BlockSpec((tq, SWA_Q), cur),
                  pl.BlockSpec((tq, 2 * SWA_KV), cur),
                  pl.BlockSpec((BLOCK, 2 * SWA_KV), prev),
                  pl.BlockSpec((None, SWA_KV, tq), lambda b, i: (b, 0, i)),
                  pl.BlockSpec((None, SWA_KV, BLOCK),
                               lambda b, i: (b, 0, jnp.maximum(i * sub - 1, 0))),
                  pl.BlockSpec((tq, MEM_W), cur),
                  pl.BlockSpec((N_MEM, MEM_W), lambda b, i: (b, 0)),
                  pl.BlockSpec((None, MEM_W, N_MEM), lambda b, i: (b, 0, 0))],
        out_specs=[pl.BlockSpec((tq, SWA_Q), cur),
                   pl.BlockSpec((tq, MEM_W), cur)],
        out_shape=[jax.ShapeDtypeStruct((t, SWA_Q), BF16),
                   jax.ShapeDtypeStruct((t, MEM_W), BF16)],
        scratch_shapes=[pltpu.VMEM((SWA_HEADS // 2, 2 * BLOCK, 2 * BLOCK), F32)],
        compiler_params=_params(2),
        name="local",
    )(bucket_t, rel_bias, sinks, qa, ka, ka, vat, vat, qm, mk, mvt)


def _fox_kernel(fast_ref, q_ref, aq_ref, k_ref, ak_ref, vt_ref, o_ref,
                sa_ref, sb_ref, acc_ref, m_ref, *, tk, n_sub, n_steps):
    pair = pl.program_id(1)
    g = pl.program_id(2)
    lane = lax.broadcasted_iota(jnp.int32, (tk, LANES), 1)
    half = [lane < HEAD64, lane >= HEAD64]
    qcat = []
    for sub in range(n_sub):
        q = q_ref[sub * tk:(sub + 1) * tk, :]
        aq = aq_ref[sub * tk:(sub + 1) * tk, :]
        zero = jnp.zeros_like(q)
        per_head = []
        for h in range(2):
            lo = (2 * pair + h) * AUG_PER_HEAD
            mine = jnp.logical_and(lane >= lo, lane < lo + AUG_PER_HEAD)
            per_head.append(jnp.concatenate([jnp.where(half[h], q, zero),
                                             jnp.where(mine, aq, zero)], axis=1))
        qcat.append(per_head)
    ones = jnp.ones((16, tk), BF16)
    acc_ref[...] = jnp.zeros_like(acc_ref)
    key = lax.broadcasted_iota(jnp.int32, (tk, tk), 0)
    qry = lax.broadcasted_iota(jnp.int32, (tk, tk), 1)
    causal = key <= qry

    def kcat(j):
        r0 = pl.multiple_of(j * tk, tk)
        return jnp.concatenate([k_ref[pl.ds(r0, tk), :], ak_ref[pl.ds(r0, tk), :]], axis=1)

    def vt_ones(j, h):
        return jnp.concatenate([vt_ref[j, h * HEAD64:(h + 1) * HEAD64, :], ones], axis=0)

    def qk(sub, j, s_ref):
        kc = kcat(j)
        for h in range(2):
            for c in range(0, tk, MXU_DIM):
                s_ref[h, :, c:c + MXU_DIM] = _dot_nt(kc, qcat[sub][h][c:c + MXU_DIM, :])

    @pl.when(fast_ref[0] == 1)
    def _():
        bufs = (sa_ref, sb_ref)

        def exp_pv(sub, j, s_ref, masked):
            for h in range(2):
                for c in range(0, tk, MXU_DIM):
                    s = s_ref[h, :, c:c + MXU_DIM]
                    if masked:
                        s = jnp.where(causal[:, c:c + MXU_DIM], s, NEG)
                    acc_ref[2 * sub + h, :, c:c + MXU_DIM] += _dot(vt_ones(j, h),
                                                                   jnp.exp2(s).astype(BF16))

        full = [(sub, jj) for jj in range(n_sub) for sub in range(n_sub)]
        diag = [(sub, jj) for jj in range(n_sub) for sub in range(jj, n_sub)]
        assert len(full) % 2 == 0
        qk(0, 0, bufs[0])

        def body(t, carry):
            for n, (sub, jj) in enumerate(full):
                nsub, njj = full[n + 1] if n + 1 < len(full) else (0, n_sub)
                qk(nsub, n_sub * t + njj, bufs[(n + 1) % 2])
                exp_pv(sub, n_sub * t + jj, bufs[n % 2], False)
            return carry

        if n_steps > 1:
            lax.fori_loop(0, g, body, 0)
        for n, (sub, jj) in enumerate(diag):
            if n + 1 < len(diag):
                qk(diag[n + 1][0], n_sub * g + diag[n + 1][1], bufs[(n + 1) % 2])
            exp_pv(sub, n_sub * g + jj, bufs[n % 2], sub == jj)

    @pl.when(fast_ref[0] == 0)
    def _():
        m_ref[...] = jnp.full(m_ref.shape, NEG, F32)
        for sub in range(n_sub):
            last = n_sub * g + sub

            def body(j, carry, sub=sub, last=last):
                qk(sub, j, sa_ref)
                for h in range(2):
                    u = 2 * sub + h
                    s = jnp.where(jnp.logical_or(causal, j < last), sa_ref[h], NEG)
                    m_prev = m_ref[u]
                    m_next = jnp.maximum(m_prev, jnp.max(s, axis=0, keepdims=True))
                    alpha = jnp.exp2(m_prev - m_next)
                    p = jnp.exp2(s - m_next).astype(BF16)
                    acc_ref[u] = alpha * acc_ref[u] + _dot(vt_ones(j, h), p)
                    m_ref[u] = m_next
                return carry

            lax.fori_loop(0, last + 1, body, 0)

    for sub in range(n_sub):
        halves = []
        for h in range(2):
            acc = acc_ref[2 * sub + h]
            halves.append(acc[:HEAD64, :] / acc[HEAD64:HEAD64 + 1, :])
        o_ref[sub * tk:(sub + 1) * tk, :] = jnp.concatenate(halves, axis=0).T.astype(BF16)


def _fox(fast, qf, aq, kf, ak, vt, batch, seq, tk):
    t = qf.shape[0]
    n_sub = min(FOX_SUB, seq // tk)
    tq = n_sub * tk
    assert seq % tq == 0, (seq, tq)
    nq = seq // tq
    pairs = FOX_HEADS // 2
    return pl.pallas_call(
        functools.partial(_fox_kernel, tk=tk, n_sub=n_sub, n_steps=nq),
        grid=(batch, pairs, nq),
        in_specs=[pl.BlockSpec(memory_space=pltpu.SMEM),
                  pl.BlockSpec((tq, LANES), lambda b, p, i: (b * nq + i, p)),
                  pl.BlockSpec((tq, LANES), lambda b, p, i: (b * nq + i, 0)),
                  pl.BlockSpec((seq, LANES), lambda b, p, i: (b, p)),
                  pl.BlockSpec((seq, LANES), lambda b, p, i: (b, 0)),
                  pl.BlockSpec((None, None, seq // tk, LANES, tk),
                               lambda b, p, i: (b, p, 0, 0, 0))],
        out_specs=pl.BlockSpec((tq, LANES), lambda b, p, i: (b * nq + i, p)),
        out_shape=jax.ShapeDtypeStruct((t, FOX_W), BF16),
        scratch_shapes=[pltpu.VMEM((2, tk, tk), F32),
                        pltpu.VMEM((2, tk, tk), F32),
                        pltpu.VMEM((2 * n_sub, HEAD64 + 16, tk), F32),
                        pltpu.VMEM((2 * n_sub, 1, tk), F32)],
        compiler_params=_params(3),
        name="fox",
    )(fast, qf, aq, kf, ak, vt)


def _post_kernel(x_ref, oa_ref, of_ref, om_ref, gate_ref, wa_ref, wf_ref, wm_ref, wo_ref,
                 g_ref, wu_ref, wd_ref, y_ref, *, chunk):
    ya = _dot(oa_ref[...], wa_ref[...])
    yf = _dot(of_ref[...], wf_ref[...])
    ym = _dot(om_ref[...], wm_ref[...])
    merged = (gate_ref[:, 0:D_MODEL].astype(F32) * ya
              + gate_ref[:, D_MODEL:2 * D_MODEL].astype(F32) * yf
              + gate_ref[:, 2 * D_MODEL:].astype(F32) * ym)
    x1 = x_ref[...] + _dot(merged.astype(BF16), wo_ref[...])
    h = _rms_rows(x1, g_ref[...]).astype(BF16)
    acc = x1
    for c in range(0, D_FF, chunk):
        u = jnp.maximum(_dot(h, wu_ref[:, c:c + chunk]), 0.0)
        acc = acc + _dot((u * u).astype(BF16), wd_ref[c:c + chunk, :])
    y_ref[...] = acc


def _post(x2, oa, of, om, gates, wa, wf, wm, wo, g_mlp, wu, wd, tm, chunk=1024):
    t = x2.shape[0]
    row = lambda w: pl.BlockSpec((tm, w), lambda i: (i, 0))
    resident = lambda shape: pl.BlockSpec(shape, lambda i: (0,) * len(shape),
                                          pipeline_mode=pl.Buffered(1))
    return pl.pallas_call(
        functools.partial(_post_kernel, chunk=chunk),
        grid=(t // tm,),
        in_specs=[row(D_MODEL), row(SWA_Q), row(FOX_W), row(MEM_W), row(GATE_W),
                  resident((SWA_Q, D_MODEL)), resident((FOX_W, D_MODEL)),
                  resident((MEM_W, D_MODEL)), resident((D_MODEL, D_MODEL)),
                  resident((1, D_MODEL)), resident((D_MODEL, D_FF)), resident((D_FF, D_MODEL))],
        out_specs=row(D_MODEL),
        out_shape=jax.ShapeDtypeStruct((t, D_MODEL), F32),
        compiler_params=_params(1),
        name="post",
    )(x2, oa, of, om, gates, wa, wf, wm, wo, g_mlp, wu, wd)


def _t5_bucket_table():
    max_exact = REL_BUCKETS // 2
    t_loc = jnp.arange(BLOCK)[:, None] + BLOCK
    s_loc = jnp.arange(2 * BLOCK)[None, :]
    dist = t_loc - s_loc
    d = jnp.maximum(dist, 0)
    df = jnp.maximum(d, 1).astype(F32)
    large = max_exact + (jnp.log(df / max_exact) / math.log(REL_MAX_DIST / max_exact)
                         * (REL_BUCKETS - max_exact)).astype(jnp.int32)
    large = jnp.minimum(large, REL_BUCKETS - 1)
    bucket = jnp.where(d < max_exact, d, large)
    band = (dist >= 0) & (dist < WINDOW)
    return jnp.where(band, bucket, -1).astype(jnp.int32).T


def _aug_placement(shift):
    place = [[0.0] * (2 * LANES) for _ in range(LANES)]
    ones = [0.0] * (2 * LANES)
    shift_lanes = []
    for h in range(FOX_HEADS):
        base = h * AUG_PER_HEAD
        for part in range(3):
            src = part * FOX_HEADS + h
            place[src][base + part] = 1.0
            place[src][LANES + base + 3 + part] = 1.0
            ones[base + 3 + part] = 1.0
            ones[LANES + base + part] = 1.0
        ones[base + 6] = 1.0
        shift_lanes.append(LANES + base + 6)
    consts = jnp.array(ones, F32).at[jnp.array(shift_lanes)].set(-shift)
    return jnp.array(place, BF16), jnp.broadcast_to(consts[None, :], (8, 2 * LANES))


def _tile_gain(g, reps, scale=1.0):
    return (jnp.tile(g.astype(F32), reps) * scale)[None, :]


def _pick_tile(n, target):
    t = min(n, target)
    while n % t:
        t //= 2
    return t


def kernel(x, mem, g_mix, w_in, b_gate, b_forget, qn_swa, kn_swa, sink_swa, rel_bias,
           qn_fox, kn_fox, g_mem, w_mem_kv, qn_mem, kn_mem, w_o_swa, w_o_fox, w_o_mem,
           w_out, g_mlp, w_mlp_up, w_mlp_down):
    batch, seq, _ = x.shape
    n_layers = w_in.shape[0]
    t = batch * seq
    tm = _pick_tile(seq, 512)
    tq = _pick_tile(seq, 512)

    bucket = _t5_bucket_table()
    tri = (jnp.arange(tm)[:, None] >= jnp.arange(tm)[None, :]).astype(BF16)

    x2 = x.reshape(t, D_MODEL)
    mem2 = mem.reshape(batch * N_MEM, D_MODEL)
    for layer in range(n_layers):
        wt = w_in[layer].T
        o = 0
        parts = {}
        for name, width in (("qa", SWA_Q), ("ka", SWA_KV), ("va", SWA_KV), ("qf", FOX_W),
                            ("kf", FOX_W), ("vf", FOX_W), ("fl", FOX_HEADS), ("qm", MEM_W),
                            ("g", GATE_W)):
            parts[name] = wt[o:o + width]
            o += width

        def dup(m):
            return jnp.concatenate([m[:HEAD64], m[:HEAD64], m[HEAD64:], m[HEAD64:]], axis=0)

        w1 = jnp.concatenate(
            [parts["qa"], dup(parts["ka"]), parts["qf"], parts["kf"],
             parts["qm"], parts["g"],
             jnp.pad(parts["fl"], ((0, LANES - FOX_HEADS), (0, 0)))], axis=0).astype(BF16)

        wmt = w_mem_kv[layer].T.astype(BF16)
        mk, mvt = _memkv(mem2, g_mem[layer][None, :], wmt[:MEM_W], wmt[MEM_W:],
                         _tile_gain(kn_mem[layer], MEM_HEADS))

        bound = (HEAD64 ** 0.5 * LOG2E * 1.02 * jnp.max(jnp.abs(qn_fox[layer]))
                 * jnp.max(jnp.abs(kn_fox[layer]))).astype(BF16).astype(F32)
        fast = bound <= FOX_FAST_MAX_SHIFT
        place, consts = _aug_placement(jnp.where(fast, bound, 0.0))
        bf_pad = jnp.pad(b_forget[layer].astype(F32), (0, LANES - FOX_HEADS))[None, :]

        wvt = jnp.concatenate([parts["vf"], parts["va"]], axis=0).astype(BF16)
        qa, ka, vat, qf, kf, vt, qm, gates, fl = _proj(
            x2, g_mix[layer][None, :], w1, wvt, b_gate[layer][None, :],
            _tile_gain(qn_swa[layer], SWA_HEADS, HEAD64 ** -0.5 * LOG2E),
            _tile_gain(kn_swa[layer], 2 * SWA_KV_HEADS),
            _tile_gain(qn_fox[layer], FOX_HEADS, HEAD64 ** -0.5 * LOG2E),
            _tile_gain(kn_fox[layer], FOX_HEADS),
            _tile_gain(qn_mem[layer], MEM_HEADS, MEM_HEAD_DIM ** -0.5 * LOG2E), tm, seq)
        aq, ak = _cum(fl, bf_pad, tri, place, consts, tm, seq)

        oa, om = _local(bucket, rel_bias.astype(F32) * LOG2E, sink_swa[layer].astype(F32) * LOG2E,
                        qa, ka, vat, qm, mk, mvt, batch, seq, tq)
        of = _fox(fast.astype(jnp.int32)[None], qf, aq, kf, ak, vt, batch, seq, tm)

        x2 = _post(x2, oa, of, om, gates, w_o_swa[layer].astype(BF16),
                   w_o_fox[layer].astype(BF16), w_o_mem[layer].astype(BF16),
                   w_out[layer].astype(BF16), g_mlp[layer][None, :],
                   w_mlp_up[layer].astype(BF16), w_mlp_down[layer].astype(BF16), tm)
    return x2.reshape(batch, seq, D_MODEL)
```

```python
import functools
import math

import jax
import jax.numpy as jnp
from jax import lax
from jax.experimental import pallas as pl
from jax.experimental.pallas import tpu as pltpu

F32 = jnp.float32
BF16 = jnp.bfloat16

D_MODEL = 1024
N_MEM = 256
SWA_HEADS = 8
SWA_KV_HEADS = 2
SWA_GROUP = SWA_HEADS // SWA_KV_HEADS
HEAD64 = 64
WINDOW = 128
FOX_HEADS = 8
MEM_HEADS = 4
MEM_HEAD_DIM = 128
D_FF = 4 * D_MODEL
REL_BUCKETS = 32
REL_MAX_DIST = 128
BLOCK = 128
EPS = 1e-6
NEG = -1e30
LOG2E = math.log2(math.e)

LANES = 128
MXU_DIM = 256
BF16_ROWS = 16
VMEM_LIMIT = 56 * 1024 * 1024

SWA_Q = SWA_HEADS * HEAD64
SWA_KV = SWA_KV_HEADS * HEAD64
FOX_W = FOX_HEADS * HEAD64
MEM_W = MEM_HEADS * MEM_HEAD_DIM
GATE_W = 3 * D_MODEL

C_QA = 0
C_KA = C_QA + SWA_Q
C_QF = C_KA + 2 * SWA_KV
C_KF = C_QF + FOX_W
C_QM = C_KF + FOX_W
C_G = C_QM + MEM_W
C_FL = C_G + GATE_W
C_END = C_FL + LANES

AUG_PER_HEAD = 7
FOX_FAST_MAX_SHIFT = 60.0
FOX_SUB = 4
LOCAL_LOOKAHEAD = 4


def _dot(a, b):
    return jnp.dot(a, b, preferred_element_type=F32)


def _dot_nt(a, b):
    return lax.dot_general(a, b, (((1,), (1,)), ((), ())), preferred_element_type=F32)


def _rms_rows(x, gain):
    ms = jnp.mean(x * x, axis=-1, keepdims=True)
    return x * lax.rsqrt(ms + EPS) * gain


def _head_rms(y, gain, head_dim):
    y2 = y * y
    lower = lax.broadcasted_iota(jnp.int32, (y.shape[0], LANES), 1) < HEAD64
    cols = []
    for c in range(0, y.shape[1], LANES):
        blk = y2[:, c:c + LANES]
        total = jnp.sum(blk, axis=-1, keepdims=True)
        if head_dim == LANES:
            cols.append(jnp.broadcast_to(total, blk.shape))
        else:
            low = jnp.sum(jnp.where(lower, blk, 0.0), axis=-1, keepdims=True)
            cols.append(jnp.where(lower, low, total - low))
    ss = jnp.concatenate(cols, axis=1)
    return y * lax.rsqrt(ss * (1.0 / head_dim) + EPS) * gain


def _const_spec(shape):
    return pl.BlockSpec(shape, lambda *_: (0,) * len(shape))


def _params(n_axes, flags=None):
    return pltpu.CompilerParams(dimension_semantics=("arbitrary",) * n_axes,
                                vmem_limit_bytes=VMEM_LIMIT, flags=flags)


def _memkv_kernel(mem_ref, g_ref, wkt_ref, wvt_ref, kn_ref, mk_ref, mvt_ref):
    h = _rms_rows(mem_ref[...], g_ref[...]).astype(BF16)
    mk_ref[...] = _head_rms(_dot_nt(h, wkt_ref[...]), kn_ref[...], MEM_HEAD_DIM).astype(BF16)
    mvt_ref[...] = _dot_nt(wvt_ref[...], h).astype(BF16)


def _memkv(mem2, g_mem, wkt, wvt, kn_t):
    rows = mem2.shape[0]
    nb = rows // N_MEM
    return pl.pallas_call(
        _memkv_kernel,
        grid=(nb,),
        in_specs=[pl.BlockSpec((N_MEM, D_MODEL), lambda i: (i, 0)),
                  _const_spec((1, D_MODEL)),
                  _const_spec((MEM_W, D_MODEL)),
                  _const_spec((MEM_W, D_MODEL)),
                  _const_spec((1, MEM_W))],
        out_specs=[pl.BlockSpec((N_MEM, MEM_W), lambda i: (i, 0)),
                   pl.BlockSpec((None, MEM_W, N_MEM), lambda i: (i, 0, 0))],
        out_shape=[jax.ShapeDtypeStruct((rows, MEM_W), BF16),
                   jax.ShapeDtypeStruct((nb, MEM_W, N_MEM), BF16)],
        compiler_params=_params(1),
        name="memkv",
    )(mem2, g_mem, wkt, wvt, kn_t)


def _proj_kernel(x_ref, g_ref, w_ref, wvt_ref, bg_ref,
                 gqa_ref, gka_ref, gqf_ref, gkf_ref, gqm_ref,
                 qa_ref, ka_ref, vat_ref, qf_ref, kf_ref, vt_ref, qm_ref, gate_ref, fl_ref):
    h = _rms_rows(x_ref[...], g_ref[...]).astype(BF16)
    qkv = _dot_nt(h, w_ref[:C_G, :])
    qa_ref[...] = _head_rms(qkv[:, C_QA:C_KA], gqa_ref[...], HEAD64).astype(BF16)
    ka_ref[...] = _head_rms(qkv[:, C_KA:C_QF], gka_ref[...], HEAD64).astype(BF16)
    qf_ref[...] = _head_rms(qkv[:, C_QF:C_KF], gqf_ref[...], HEAD64).astype(BF16)
    kf_ref[...] = _head_rms(qkv[:, C_KF:C_QM], gkf_ref[...], HEAD64).astype(BF16)
    qm_ref[...] = _head_rms(qkv[:, C_QM:C_G], gqm_ref[...], MEM_HEAD_DIM).astype(BF16)
    vt = _dot_nt(wvt_ref[...], h).astype(BF16)
    for p in range(FOX_HEADS // 2):
        vt_ref[p] = vt[p * LANES:(p + 1) * LANES, :]
    vat_ref[...] = vt[FOX_W:, :]
    zf = _dot_nt(h, w_ref[C_G:C_END, :])
    z = zf[:, :GATE_W] + bg_ref[...]
    gate_ref[...] = (1.0 / (1.0 + jnp.exp(-z))).astype(BF16)
    fl_ref[...] = zf[:, GATE_W:]


def _proj(x2, g_mix, w1, wvt, b_gate, gqa, gka, gqf, gkf, gqm, tm, seq):
    t = x2.shape[0]
    nt = seq // tm
    pairs = FOX_HEADS // 2

    def rows(w, dtype=BF16):
        return jax.ShapeDtypeStruct((t, w), dtype), pl.BlockSpec((tm, w), lambda i: (i, 0))

    outs = [rows(SWA_Q), rows(2 * SWA_KV),
            (jax.ShapeDtypeStruct((t // seq, SWA_KV, seq), BF16),
             pl.BlockSpec((None, SWA_KV, tm), lambda i: (i // nt, 0, i % nt))),
            rows(FOX_W), rows(FOX_W),
            (jax.ShapeDtypeStruct((t // seq, pairs, nt, LANES, tm), BF16),
             pl.BlockSpec((None, pairs, None, LANES, tm), lambda i: (i // nt, 0, i % nt, 0, 0))),
            rows(MEM_W), rows(GATE_W), rows(LANES, F32)]
    out_shape = [o[0] for o in outs]
    out_specs = [o[1] for o in outs]
    return pl.pallas_call(
        _proj_kernel,
        grid=(t // tm,),
        in_specs=[pl.BlockSpec((tm, D_MODEL), lambda i: (i, 0)),
                  _const_spec((1, D_MODEL)),
                  _const_spec((C_END, D_MODEL)),
                  _const_spec((FOX_W + SWA_KV, D_MODEL)),
                  _const_spec((1, GATE_W)),
                  _const_spec((1, SWA_Q)),
                  _const_spec((1, 2 * SWA_KV)),
                  _const_spec((1, FOX_W)),
                  _const_spec((1, FOX_W)),
                  _const_spec((1, MEM_W))],
        out_specs=out_specs,
        out_shape=out_shape,
        compiler_params=_params(1),
        name="proj",
    )(x2, g_mix, w1, wvt, b_gate, gqa, gka, gqf, gkf, gqm)


def _split3(c):
    hi = c.astype(BF16)
    r1 = c - hi.astype(F32)
    mid = r1.astype(BF16)
    lo = (r1 - mid.astype(F32)).astype(BF16)
    return hi, mid, lo


def _pack3(x):
    hi, mid, lo = _split3(x)
    return (hi.astype(F32) + pltpu.roll(mid.astype(F32), FOX_HEADS, 1)
            + pltpu.roll(lo.astype(F32), 2 * FOX_HEADS, 1)).astype(BF16)


def _cum_kernel(fl_ref, bf_ref, tri_ref, place_ref, const_ref, aq_ref, ak_ref, *, tm):
    n_tiles = fl_ref.shape[0] // tm
    lane = lax.broadcasted_iota(jnp.int32, (tm, LANES), 1)
    heads = lane < FOX_HEADS
    tri = tri_ref[...]
    parts = []
    for k in range(n_tiles):
        z = fl_ref[k * tm:(k + 1) * tm, :] + bf_ref[...]
        logf = jnp.where(heads, jnp.minimum(z, 0.0) - jnp.log(1.0 + jnp.exp(-jnp.abs(z))), 0.0)
        parts.append(_dot(tri, _pack3(logf)))
    carry = jnp.zeros((1, LANES), F32)
    packed = []
    for part in parts:
        c = ((part + pltpu.roll(part, LANES - FOX_HEADS, 1))
             + pltpu.roll(part, LANES - 2 * FOX_HEADS, 1)) + carry
        carry = c[tm - 1:tm, :]
        packed.append(_pack3(jnp.where(heads, c * LOG2E, 0.0)))
    for k in range(n_tiles):
        placed = _dot(packed[k], place_ref[...])
        rows = slice(k * tm, (k + 1) * tm)
        aq_ref[rows, :] = (placed[:, :LANES] + const_ref[0:1, :LANES]).astype(BF16)
        ak_ref[rows, :] = (const_ref[0:1, LANES:] - placed[:, LANES:]).astype(BF16)


def _cum(fl, bf_pad, tri, place, consts, tm, seq):
    t = fl.shape[0]
    return pl.pallas_call(
        functools.partial(_cum_kernel, tm=tm),
        grid=(t // seq,),
        in_specs=[pl.BlockSpec((seq, LANES), lambda i: (i, 0)),
                  _const_spec((1, LANES)),
                  _const_spec((tm, tm)),
                  _const_spec((LANES, 2 * LANES)),
                  _const_spec((8, 2 * LANES))],
        out_specs=[pl.BlockSpec((seq, LANES), lambda i: (i, 0)),
                   pl.BlockSpec((seq, LANES), lambda i: (i, 0))],
        out_shape=[jax.ShapeDtypeStruct((t, LANES), BF16)] * 2,
        compiler_params=_params(1),
        name="cum",
    )(fl, bf_pad, tri, place, consts)


def _local_kernel(bucket_ref, relb_ref, sink_ref,
                  qa_ref, kc_ref, kp_ref, vc_ref, vp_ref, qm_ref, mk_ref, mvt_ref,
                  oa_ref, om_ref, tbl_ref, *, tq):
    b = pl.program_id(0)
    i = pl.program_id(1)
    pairs = SWA_HEADS // 2

    @pl.when(jnp.logical_and(b == 0, i == 0))
    def _():
        bucket = bucket_ref[...]
        band = bucket >= 0
        for h in range(SWA_HEADS):
            t = jnp.zeros(bucket.shape, F32)
            for k in range(REL_BUCKETS):
                t = jnp.where(bucket == k, relb_ref[k, h], t)
            tbl_ref[h // 2, :, (h % 2) * BLOCK:(h % 2 + 1) * BLOCK] = jnp.where(band, t, NEG)

    lane = lax.broadcasted_iota(jnp.int32, (BLOCK, LANES), 1)
    lower = lane < HEAD64
    first_head = lax.broadcasted_iota(jnp.int32, (1, 2 * BLOCK), 1) < BLOCK
    key_row = lax.broadcasted_iota(jnp.int32, (2 * BLOCK, 2 * BLOCK), 0)
    pad_rows = jnp.logical_and(key_row < BLOCK, i == 0)
    ones = jnp.ones((BF16_ROWS, 2 * BLOCK), BF16)

    def windows(j):
        r0 = j * BLOCK
        if j == 0:
            return (jnp.concatenate([kp_ref[...], kc_ref[0:BLOCK, :]], axis=0),
                    jnp.concatenate([vp_ref[...], vc_ref[:, 0:BLOCK]], axis=1))
        return kc_ref[r0 - BLOCK:r0 + BLOCK, :], vc_ref[:, r0 - BLOCK:r0 + BLOCK]

    def swa_scores(j, hb):
        g = (2 * hb) // SWA_GROUP
        qblk = qa_ref[j * BLOCK:(j + 1) * BLOCK, hb * LANES:(hb + 1) * LANES]
        qstack = jnp.concatenate([jnp.where(lower, qblk, 0),
                                  jnp.where(lower, 0, qblk)], axis=0)
        s = _dot_nt(windows(j)[0][:, g * LANES:(g + 1) * LANES], qstack) + tbl_ref[hb]
        return jnp.where(pad_rows, NEG, s) if j == 0 else s

    def swa_finish(j, hb, s):
        g = (2 * hb) // SWA_GROUP
        sink = jnp.where(first_head, sink_ref[2 * hb], sink_ref[2 * hb + 1])
        m = jnp.maximum(jnp.max(s, axis=0, keepdims=True), sink)
        p = jnp.exp2(s - m).astype(BF16)
        vt_ones = jnp.concatenate([windows(j)[1][g * HEAD64:(g + 1) * HEAD64, :], ones], axis=0)
        acc = _dot(vt_ones, p)
        o = acc[:HEAD64, :] / (acc[HEAD64:HEAD64 + 1, :] + jnp.exp2(sink - m))
        o = jnp.concatenate([o[:, :BLOCK], o[:, BLOCK:]], axis=0)
        oa_ref[j * BLOCK:(j + 1) * BLOCK, hb * LANES:(hb + 1) * LANES] = o.T.astype(BF16)

    ones_m = jnp.ones((BF16_ROWS, N_MEM), BF16)

    def mem_scores(h):
        sl = slice(h * MEM_HEAD_DIM, (h + 1) * MEM_HEAD_DIM)
        return _dot_nt(mk_ref[:, sl], qm_ref[:, sl])

    def mem_finish(h, s):
        sl = slice(h * MEM_HEAD_DIM, (h + 1) * MEM_HEAD_DIM)
        m = jnp.max(s, axis=0, keepdims=True)
        p = jnp.exp2(s - m).astype(BF16)
        acc = _dot(jnp.concatenate([mvt_ref[sl, :], ones_m], axis=0), p)
        o = acc[:MEM_HEAD_DIM, :] / acc[MEM_HEAD_DIM:MEM_HEAD_DIM + 1, :]
        om_ref[:, sl] = o.T.astype(BF16)

    units = [(swa_scores, swa_finish, (j, hb)) for j in range(tq // BLOCK) for hb in range(pairs)]
    units += [(mem_scores, mem_finish, (h,)) for h in range(MEM_HEADS)]
    pending = []
    for n in range(len(units) + LOCAL_LOOKAHEAD):
        if n < len(units):
            score_fn, _, args = units[n]
            pending.append(score_fn(*args))
        if n >= LOCAL_LOOKAHEAD:
            _, finish_fn, args = units[n - LOCAL_LOOKAHEAD]
            finish_fn(*args, pending[n - LOCAL_LOOKAHEAD])


def _local(bucket_t, rel_bias, sinks, qa, ka, vat, qm, mk, mvt, batch, seq, tq):
    t = qa.shape[0]
    nq = seq // tq
    sub = tq // BLOCK

    def cur(b, i):
        return (b * nq + i, 0)

    def prev(b, i):
        return (jnp.maximum((b * nq + i) * sub - 1, 0), 0)

    smem = pl.BlockSpec(memory_space=pltpu.SMEM)
    return pl.pallas_call(
        functools.partial(_local_kernel, tq=tq),
        grid=(batch, nq),
        in_specs=[_const_spec((2 * BLOCK, BLOCK)), smem, smem,
                  pl.BlockSpec((tq, SWA_Q), cur),
                  pl.BlockSpec((tq, 2 * SWA_KV), cur),
                  pl.BlockSpec((BLOCK, 2 * SWA_KV), prev),
                  pl.BlockSpec((None, SWA_KV, tq), lambda b, i: (b, 0, i)),
                  pl.BlockSpec((None, SWA_KV, BLOCK),
                               lambda b, i: (b, 0, jnp.maximum(i * sub - 1, 0))),
                  pl.BlockSpec((tq, MEM_W), cur),
                  pl.BlockSpec((N_MEM, MEM_W), lambda b, i: (b, 0)),
                  pl.BlockSpec((None, MEM_W, N_MEM), lambda b, i: (b, 0, 0))],
        out_specs=[pl.BlockSpec((tq, SWA_Q), cur),
                   pl.BlockSpec((tq, MEM_W), cur)],
        out_shape=[jax.ShapeDtypeStruct((t, SWA_Q), BF16),
                   jax.ShapeDtypeStruct((t, MEM_W), BF16)],
        scratch_shapes=[pltpu.VMEM((SWA_HEADS // 2, 2 * BLOCK, 2 * BLOCK), F32)],
        compiler_params=_params(2),
        name="local",
    )(bucket_t, rel_bias, sinks, qa, ka, ka, vat, vat, qm, mk, mvt)


def _fox_kernel(fast_ref, q_ref, aq_ref, k_ref, ak_ref, vt_ref, o_ref,
                sa_ref, sb_ref, acc_ref, m_ref, *, tk, n_sub, n_steps):
    pair = pl.program_id(1)
    g = pl.program_id(2)
    lane = lax.broadcasted_iota(jnp.int32, (tk, LANES), 1)
    half = [lane < HEAD64, lane >= HEAD64]
    qcat = []
    for sub in range(n_sub):
        q = q_ref[sub * tk:(sub + 1) * tk, :]
        aq = aq_ref[sub * tk:(sub + 1) * tk, :]
        zero = jnp.zeros_like(q)
        per_head = []
        for h in range(2):
            lo = (2 * pair + h) * AUG_PER_HEAD
            mine = jnp.logical_and(lane >= lo, lane < lo + AUG_PER_HEAD)
            per_head.append(jnp.concatenate([jnp.where(half[h], q, zero),
                                             jnp.where(mine, aq, zero)], axis=1))
        qcat.append(per_head)
    ones = jnp.ones((BF16_ROWS, tk), BF16)
    acc_ref[...] = jnp.zeros_like(acc_ref)
    key = lax.broadcasted_iota(jnp.int32, (tk, tk), 0)
    qry = lax.broadcasted_iota(jnp.int32, (tk, tk), 1)
    causal = key <= qry

    def kcat(j):
        r0 = pl.multiple_of(j * tk, tk)
        return jnp.concatenate([k_ref[pl.ds(r0, tk), :], ak_ref[pl.ds(r0, tk), :]], axis=1)

    def vt_ones(j, h):
        return jnp.concatenate([vt_ref[j, h * HEAD64:(h + 1) * HEAD64, :], ones], axis=0)

    def visible_keys(c, diagonal):
        return min(tk, c + MXU_DIM) if diagonal else tk

    def qk(sub, j, s_ref, diagonal=False):
        kc = kcat(j)
        for h in range(2):
            for c in range(0, tk, MXU_DIM):
                nk = visible_keys(c, diagonal)
                s_ref[h, :nk, c:c + MXU_DIM] = _dot_nt(kc[:nk], qcat[sub][h][c:c + MXU_DIM, :])

    @pl.when(fast_ref[0] == 1)
    def _():
        bufs = (sa_ref, sb_ref)

        def exp_pv(sub, j, s_ref, masked):
            for h in range(2):
                for c in range(0, tk, MXU_DIM):
                    nk = visible_keys(c, masked)
                    s = s_ref[h, :nk, c:c + MXU_DIM]
                    if masked:
                        s = jnp.where(causal[:nk, c:c + MXU_DIM], s, NEG)
                    acc_ref[2 * sub + h, :, c:c + MXU_DIM] += _dot(vt_ones(j, h)[:, :nk],
                                                                   jnp.exp2(s).astype(BF16))

        full = [(sub, jj) for jj in range(n_sub) for sub in range(n_sub)]
        diag = [(sub, jj) for jj in range(n_sub) for sub in range(jj, n_sub)]
        assert len(full) % 2 == 0
        qk(0, 0, bufs[0])

        def body(t, carry):
            for n, (sub, jj) in enumerate(full):
                nsub, njj = full[n + 1] if n + 1 < len(full) else (0, n_sub)
                qk(nsub, n_sub * t + njj, bufs[(n + 1) % 2])
                exp_pv(sub, n_sub * t + jj, bufs[n % 2], False)
            return carry

        if n_steps > 1:
            lax.fori_loop(0, g, body, 0)
        for n, (sub, jj) in enumerate(diag):
            if n + 1 < len(diag):
                nsub, njj = diag[n + 1]
                qk(nsub, n_sub * g + njj, bufs[(n + 1) % 2], diagonal=nsub == njj)
            exp_pv(sub, n_sub * g + jj, bufs[n % 2], sub == jj)

    @pl.when(fast_ref[0] == 0)
    def _():
        m_ref[...] = jnp.full(m_ref.shape, NEG, F32)
        for sub in range(n_sub):
            last = n_sub * g + sub

            def body(j, carry, sub=sub, last=last):
                qk(sub, j, sa_ref)
                for h in range(2):
                    u = 2 * sub + h
                    s = jnp.where(jnp.logical_or(causal, j < last), sa_ref[h], NEG)
                    m_prev = m_ref[u]
                    m_next = jnp.maximum(m_prev, jnp.max(s, axis=0, keepdims=True))
                    alpha = jnp.exp2(m_prev - m_next)
                    p = jnp.exp2(s - m_next).astype(BF16)
                    acc_ref[u] = alpha * acc_ref[u] + _dot(vt_ones(j, h), p)
                    m_ref[u] = m_next
                return carry

            lax.fori_loop(0, last + 1, body, 0)

    for sub in range(n_sub):
        halves = []
        for h in range(2):
            acc = acc_ref[2 * sub + h]
            halves.append(acc[:HEAD64, :] / acc[HEAD64:HEAD64 + 1, :])
        o_ref[sub * tk:(sub + 1) * tk, :] = jnp.concatenate(halves, axis=0).T.astype(BF16)


def _fox(fast, qf, aq, kf, ak, vt, batch, seq, tk):
    t = qf.shape[0]
    n_sub = min(FOX_SUB, seq // tk)
    tq = n_sub * tk
    assert seq % tq == 0, (seq, tq)
    nq = seq // tq
    pairs = FOX_HEADS // 2
    return pl.pallas_call(
        functools.partial(_fox_kernel, tk=tk, n_sub=n_sub, n_steps=nq),
        grid=(batch, pairs, nq),
        in_specs=[pl.BlockSpec(memory_space=pltpu.SMEM),
                  pl.BlockSpec((tq, LANES), lambda b, p, i: (b * nq + i, p)),
                  pl.BlockSpec((tq, LANES), lambda b, p, i: (b * nq + i, 0)),
                  pl.BlockSpec((seq, LANES), lambda b, p, i: (b, p)),
                  pl.BlockSpec((seq, LANES), lambda b, p, i: (b, 0)),
                  pl.BlockSpec((None, None, seq // tk, LANES, tk),
                               lambda b, p, i: (b, p, 0, 0, 0))],
        out_specs=pl.BlockSpec((tq, LANES), lambda b, p, i: (b * nq + i, p)),
        out_shape=jax.ShapeDtypeStruct((t, FOX_W), BF16),
        scratch_shapes=[pltpu.VMEM((2, tk, tk), F32),
                        pltpu.VMEM((2, tk, tk), F32),
                        pltpu.VMEM((2 * n_sub, HEAD64 + BF16_ROWS, tk), F32),
                        pltpu.VMEM((2 * n_sub, 1, tk), F32)],
        compiler_params=_params(3),
        name="fox",
    )(fast, qf, aq, kf, ak, vt)


def _post_kernel(x_ref, oa_ref, of_ref, om_ref, gate_ref, wa_ref, wf_ref, wm_ref, wo_ref,
                 g_ref, wu_ref, wd_ref, y_ref, *, chunk):
    ya = _dot(oa_ref[...], wa_ref[...])
    yf = _dot(of_ref[...], wf_ref[...])
    ym = _dot(om_ref[...], wm_ref[...])
    merged = (gate_ref[:, 0:D_MODEL].astype(F32) * ya
              + gate_ref[:, D_MODEL:2 * D_MODEL].astype(F32) * yf
              + gate_ref[:, 2 * D_MODEL:].astype(F32) * ym)
    x1 = x_ref[...] + _dot(merged.astype(BF16), wo_ref[...])
    h = _rms_rows(x1, g_ref[...]).astype(BF16)
    acc = x1
    for c in range(0, D_FF, chunk):
        u = jnp.maximum(_dot(h, wu_ref[:, c:c + chunk]), 0.0)
        acc = acc + _dot((u * u).astype(BF16), wd_ref[c:c + chunk, :])
    y_ref[...] = acc


def _post(x2, oa, of, om, gates, wa, wf, wm, wo, g_mlp, wu, wd, tm, chunk=1024):
    t = x2.shape[0]
    row = lambda w: pl.BlockSpec((tm, w), lambda i: (i, 0))
    resident = lambda shape: pl.BlockSpec(shape, lambda i: (0,) * len(shape),
                                          pipeline_mode=pl.Buffered(1))
    return pl.pallas_call(
        functools.partial(_post_kernel, chunk=chunk),
        grid=(t // tm,),
        in_specs=[row(D_MODEL), row(SWA_Q), row(FOX_W), row(MEM_W), row(GATE_W),
                  resident((SWA_Q, D_MODEL)), resident((FOX_W, D_MODEL)),
                  resident((MEM_W, D_MODEL)), resident((D_MODEL, D_MODEL)),
                  resident((1, D_MODEL)), resident((D_MODEL, D_FF)), resident((D_FF, D_MODEL))],
        out_specs=row(D_MODEL),
        out_shape=jax.ShapeDtypeStruct((t, D_MODEL), F32),
        compiler_params=_params(1),
        name="post",
    )(x2, oa, of, om, gates, wa, wf, wm, wo, g_mlp, wu, wd)


def _t5_bucket_table():
    max_exact = REL_BUCKETS // 2
    t_loc = jnp.arange(BLOCK)[:, None] + BLOCK
    s_loc = jnp.arange(2 * BLOCK)[None, :]
    dist = t_loc - s_loc
    d = jnp.maximum(dist, 0)
    df = jnp.maximum(d, 1).astype(F32)
    large = max_exact + (jnp.log(df / max_exact) / math.log(REL_MAX_DIST / max_exact)
                         * (REL_BUCKETS - max_exact)).astype(jnp.int32)
    large = jnp.minimum(large, REL_BUCKETS - 1)
    bucket = jnp.where(d < max_exact, d, large)
    band = (dist >= 0) & (dist < WINDOW)
    return jnp.where(band, bucket, -1).astype(jnp.int32).T


def _aug_placement(shift):
    place = [[0.0] * (2 * LANES) for _ in range(LANES)]
    ones = [0.0] * (2 * LANES)
    shift_lanes = []
    for h in range(FOX_HEADS):
        base = h * AUG_PER_HEAD
        for part in range(3):
            src = part * FOX_HEADS + h
            place[src][base + part] = 1.0
            place[src][LANES + base + 3 + part] = 1.0
            ones[base + 3 + part] = 1.0
            ones[LANES + base + part] = 1.0
        ones[base + 6] = 1.0
        shift_lanes.append(LANES + base + 6)
    consts = jnp.array(ones, F32).at[jnp.array(shift_lanes)].set(-shift)
    return jnp.array(place, BF16), jnp.broadcast_to(consts[None, :], (8, 2 * LANES))


def _tile_gain(g, reps, scale=1.0):
    return (jnp.tile(g.astype(F32), reps) * scale)[None, :]


def _pick_tile(n, target):
    t = min(n, target)
    while n % t:
        t //= 2
    return t


def kernel(x, mem, g_mix, w_in, b_gate, b_forget, qn_swa, kn_swa, sink_swa, rel_bias,
           qn_fox, kn_fox, g_mem, w_mem_kv, qn_mem, kn_mem, w_o_swa, w_o_fox, w_o_mem,
           w_out, g_mlp, w_mlp_up, w_mlp_down):
    batch, seq, _ = x.shape
    n_layers = w_in.shape[0]
    t = batch * seq
    tm = _pick_tile(seq, 512)
    tq = _pick_tile(seq, 512)

    bucket = _t5_bucket_table()
    tri = (jnp.arange(tm)[:, None] >= jnp.arange(tm)[None, :]).astype(BF16)

    x2 = x.reshape(t, D_MODEL)
    mem2 = mem.reshape(batch * N_MEM, D_MODEL)
    for layer in range(n_layers):
        wt = w_in[layer].T
        o = 0
        parts = {}
        for name, width in (("qa", SWA_Q), ("ka", SWA_KV), ("va", SWA_KV), ("qf", FOX_W),
                            ("kf", FOX_W), ("vf", FOX_W), ("fl", FOX_HEADS), ("qm", MEM_W),
                            ("g", GATE_W)):
            parts[name] = wt[o:o + width]
            o += width

        def dup(m):
            return jnp.concatenate([m[:HEAD64], m[:HEAD64], m[HEAD64:], m[HEAD64:]], axis=0)

        w1 = jnp.concatenate(
            [parts["qa"], dup(parts["ka"]), parts["qf"], parts["kf"],
             parts["qm"], parts["g"],
             jnp.pad(parts["fl"], ((0, LANES - FOX_HEADS), (0, 0)))], axis=0).astype(BF16)

        wmt = w_mem_kv[layer].T.astype(BF16)
        mk, mvt = _memkv(mem2, g_mem[layer][None, :], wmt[:MEM_W], wmt[MEM_W:],
                         _tile_gain(kn_mem[layer], MEM_HEADS))

        bound = (HEAD64 ** 0.5 * LOG2E * 1.02 * jnp.max(jnp.abs(qn_fox[layer]))
                 * jnp.max(jnp.abs(kn_fox[layer]))).astype(BF16).astype(F32)
        fast = bound <= FOX_FAST_MAX_SHIFT
        place, consts = _aug_placement(jnp.where(fast, bound, 0.0))
        bf_pad = jnp.pad(b_forget[layer].astype(F32), (0, LANES - FOX_HEADS))[None, :]

        wvt = jnp.concatenate([parts["vf"], parts["va"]], axis=0).astype(BF16)
        qa, ka, vat, qf, kf, vt, qm, gates, fl = _proj(
            x2, g_mix[layer][None, :], w1, wvt, b_gate[layer][None, :],
            _tile_gain(qn_swa[layer], SWA_HEADS, HEAD64 ** -0.5 * LOG2E),
            _tile_gain(kn_swa[layer], 2 * SWA_KV_HEADS),
            _tile_gain(qn_fox[layer], FOX_HEADS, HEAD64 ** -0.5 * LOG2E),
            _tile_gain(kn_fox[layer], FOX_HEADS),
            _tile_gain(qn_mem[layer], MEM_HEADS, MEM_HEAD_DIM ** -0.5 * LOG2E), tm, seq)
        aq, ak = _cum(fl, bf_pad, tri, place, consts, tm, seq)

        oa, om = _local(bucket, rel_bias.astype(F32) * LOG2E, sink_swa[layer].astype(F32) * LOG2E,
                        qa, ka, vat, qm, mk, mvt, batch, seq, tq)
        of = _fox(fast.astype(jnp.int32)[None], qf, aq, kf, ak, vt, batch, seq, tm)

        x2 = _post(x2, oa, of, om, gates, w_o_swa[layer].astype(BF16),
                   w_o_fox[layer].astype(BF16), w_o_mem[layer].astype(BF16),
                   w_out[layer].astype(BF16), g_mlp[layer][None, :],
                   w_mlp_up[layer].astype(BF16), w_mlp_down[layer].astype(BF16), tm)
    return x2.reshape(batch, seq, D_MODEL)
```

```python
import functools
import math

import jax
import jax.numpy as jnp
from jax import lax
from jax.experimental import pallas as pl
from jax.experimental.pallas import tpu as pltpu

F32 = jnp.float32
BF16 = jnp.bfloat16

D_MODEL = 1024
N_MEM = 256
SWA_HEADS = 8
SWA_KV_HEADS = 2
SWA_GROUP = SWA_HEADS // SWA_KV_HEADS
HEAD64 = 64
WINDOW = 128
FOX_HEADS = 8
MEM_HEADS = 4
MEM_HEAD_DIM = 128
D_FF = 4 * D_MODEL
REL_BUCKETS = 32
REL_MAX_DIST = 128
BLOCK = 128
EPS = 1e-6
NEG = -1e30
LOG2E = math.log2(math.e)

LANES = 128
MXU_DIM = 256
BF16_ROWS = 16
VMEM_LIMIT = 56 * 1024 * 1024

SWA_Q = SWA_HEADS * HEAD64
SWA_KV = SWA_KV_HEADS * HEAD64
FOX_W = FOX_HEADS * HEAD64
MEM_W = MEM_HEADS * MEM_HEAD_DIM
GATE_W = 3 * D_MODEL

C_QA = 0
C_KA = C_QA + SWA_Q
C_KF = C_KA + 2 * SWA_KV
C_QM = C_KF + FOX_W
C_G = C_QM + MEM_W
C_FL = C_G + GATE_W
C_END = C_FL + LANES

AUG_PER_HEAD = 7
FOX_FAST_MAX_SHIFT = 60.0
FOX_SUB = 4
LOCAL_LOOKAHEAD = 4


def _dot(a, b):
    return jnp.dot(a, b, preferred_element_type=F32)


def _dot_nt(a, b):
    return lax.dot_general(a, b, (((1,), (1,)), ((), ())), preferred_element_type=F32)


def _rms_rows(x, gain):
    ms = jnp.mean(x * x, axis=-1, keepdims=True)
    return x * lax.rsqrt(ms + EPS) * gain


def _head_rms(y, gain, head_dim):
    y2 = y * y
    lower = lax.broadcasted_iota(jnp.int32, (y.shape[0], LANES), 1) < HEAD64
    cols = []
    for c in range(0, y.shape[1], LANES):
        blk = y2[:, c:c + LANES]
        total = jnp.sum(blk, axis=-1, keepdims=True)
        if head_dim == LANES:
            cols.append(jnp.broadcast_to(total, blk.shape))
        else:
            low = jnp.sum(jnp.where(lower, blk, 0.0), axis=-1, keepdims=True)
            cols.append(jnp.where(lower, low, total - low))
    ss = jnp.concatenate(cols, axis=1)
    return y * lax.rsqrt(ss * (1.0 / head_dim) + EPS) * gain


def _const_spec(shape):
    return pl.BlockSpec(shape, lambda *_: (0,) * len(shape))


def _params(n_axes, flags=None):
    return pltpu.CompilerParams(dimension_semantics=("arbitrary",) * n_axes,
                                vmem_limit_bytes=VMEM_LIMIT, flags=flags)


def _memkv_kernel(mem_ref, g_ref, wkt_ref, wvt_ref, kn_ref, mk_ref, mvt_ref):
    h = _rms_rows(mem_ref[...], g_ref[...]).astype(BF16)
    mk_ref[...] = _head_rms(_dot_nt(h, wkt_ref[...]), kn_ref[...], MEM_HEAD_DIM).astype(BF16)
    mvt_ref[...] = _dot_nt(wvt_ref[...], h).astype(BF16)


def _memkv(mem2, g_mem, wkt, wvt, kn_t):
    rows = mem2.shape[0]
    nb = rows // N_MEM
    return pl.pallas_call(
        _memkv_kernel,
        grid=(nb,),
        in_specs=[pl.BlockSpec((N_MEM, D_MODEL), lambda i: (i, 0)),
                  _const_spec((1, D_MODEL)),
                  _const_spec((MEM_W, D_MODEL)),
                  _const_spec((MEM_W, D_MODEL)),
                  _const_spec((1, MEM_W))],
        out_specs=[pl.BlockSpec((N_MEM, MEM_W), lambda i: (i, 0)),
                   pl.BlockSpec((None, MEM_W, N_MEM), lambda i: (i, 0, 0))],
        out_shape=[jax.ShapeDtypeStruct((rows, MEM_W), BF16),
                   jax.ShapeDtypeStruct((nb, MEM_W, N_MEM), BF16)],
        compiler_params=_params(1),
        name="memkv",
    )(mem2, g_mem, wkt, wvt, kn_t)


def _proj_kernel(x_ref, g_ref, w_ref, wt_ref, bg_ref,
                 gqa_ref, gka_ref, gqft_ref, gkf_ref, gqm_ref,
                 qa_ref, ka_ref, vat_ref, qft_ref, kf_ref, vt_ref, qm_ref, gate_ref, fl_ref):
    h = _rms_rows(x_ref[...], g_ref[...]).astype(BF16)
    tm = h.shape[0]
    qkv = _dot_nt(h, w_ref[:C_G, :])
    qa_ref[...] = _head_rms(qkv[:, C_QA:C_KA], gqa_ref[...], HEAD64).astype(BF16)
    ka_ref[...] = _head_rms(qkv[:, C_KA:C_KF], gka_ref[...], HEAD64).astype(BF16)
    kf_ref[...] = _head_rms(qkv[:, C_KF:C_QM], gkf_ref[...], HEAD64).astype(BF16)
    qm_ref[...] = _head_rms(qkv[:, C_QM:C_G], gqm_ref[...], MEM_HEAD_DIM).astype(BF16)
    t = _dot_nt(wt_ref[...], h)
    for p in range(FOX_HEADS // 2):
        vt_ref[p] = t[p * LANES:(p + 1) * LANES, :].astype(BF16)
    vat_ref[...] = t[FOX_W:FOX_W + SWA_KV, :].astype(BF16)
    gain_t = jnp.tile(gqft_ref[...], (1, tm // LANES))
    for p in range(FOX_HEADS // 2):
        rows = []
        for e in range(2):
            r0 = FOX_W + SWA_KV + (2 * p + e) * HEAD64
            y = t[r0:r0 + HEAD64, :]
            ms = jnp.sum(y * y, axis=0, keepdims=True) * (1.0 / HEAD64)
            rows.append(y * lax.rsqrt(ms + EPS) * gain_t[r0 - FOX_W - SWA_KV:r0 - FOX_W - SWA_KV
                                                         + HEAD64, :])
        qft_ref[p] = jnp.concatenate(rows, axis=0).astype(BF16)
    zf = _dot_nt(h, w_ref[C_G:C_END, :])
    z = zf[:, :GATE_W] + bg_ref[...]
    gate_ref[...] = (1.0 / (1.0 + jnp.exp(-z))).astype(BF16)
    fl_ref[...] = zf[:, GATE_W:]


def _proj(x2, g_mix, w1, wt, b_gate, gqa, gka, gqft, gkf, gqm, tm, seq):
    t = x2.shape[0]
    nt = seq // tm
    pairs = FOX_HEADS // 2

    def rows(w, dtype=BF16):
        return jax.ShapeDtypeStruct((t, w), dtype), pl.BlockSpec((tm, w), lambda i: (i, 0))

    outs = [rows(SWA_Q), rows(2 * SWA_KV),
            (jax.ShapeDtypeStruct((t // seq, SWA_KV, seq), BF16),
             pl.BlockSpec((None, SWA_KV, tm), lambda i: (i // nt, 0, i % nt))),
            (jax.ShapeDtypeStruct((t // seq, pairs, LANES, seq), BF16),
             pl.BlockSpec((None, pairs, LANES, tm), lambda i: (i // nt, 0, 0, i % nt))),
            rows(FOX_W),
            (jax.ShapeDtypeStruct((t // seq, pairs, nt, LANES, tm), BF16),
             pl.BlockSpec((None, pairs, None, LANES, tm), lambda i: (i // nt, 0, i % nt, 0, 0))),
            rows(MEM_W), rows(GATE_W), rows(LANES, F32)]
    out_shape = [o[0] for o in outs]
    out_specs = [o[1] for o in outs]
    return pl.pallas_call(
        _proj_kernel,
        grid=(t // tm,),
        in_specs=[pl.BlockSpec((tm, D_MODEL), lambda i: (i, 0)),
                  _const_spec((1, D_MODEL)),
                  _const_spec((C_END, D_MODEL)),
                  _const_spec((2 * FOX_W + SWA_KV, D_MODEL)),
                  _const_spec((1, GATE_W)),
                  _const_spec((1, SWA_Q)),
                  _const_spec((1, 2 * SWA_KV)),
                  _const_spec((FOX_W, LANES)),
                  _const_spec((1, FOX_W)),
                  _const_spec((1, MEM_W))],
        out_specs=out_specs,
        out_shape=out_shape,
        compiler_params=_params(1),
        name="proj",
    )(x2, g_mix, w1, wt, b_gate, gqa, gka, gqft, gkf, gqm)


def _split3(c):
    hi = c.astype(BF16)
    r1 = c - hi.astype(F32)
    mid = r1.astype(BF16)
    lo = (r1 - mid.astype(F32)).astype(BF16)
    return hi, mid, lo


def _pack3(x):
    hi, mid, lo = _split3(x)
    return (hi.astype(F32) + pltpu.roll(mid.astype(F32), FOX_HEADS, 1)
            + pltpu.roll(lo.astype(F32), 2 * FOX_HEADS, 1)).astype(BF16)


def _cum_kernel(fl_ref, bf_ref, tri_ref, place_ref, const_ref, aqt_ref, ak_ref, *, tm):
    n_tiles = fl_ref.shape[0] // tm
    lane = lax.broadcasted_iota(jnp.int32, (tm, LANES), 1)
    heads = lane < FOX_HEADS
    tri = tri_ref[...]
    parts = []
    for k in range(n_tiles):
        z = fl_ref[k * tm:(k + 1) * tm, :] + bf_ref[...]
        logf = jnp.where(heads, jnp.minimum(z, 0.0) - jnp.log(1.0 + jnp.exp(-jnp.abs(z))), 0.0)
        parts.append(_dot(tri, _pack3(logf)))
    carry = jnp.zeros((1, LANES), F32)
    packed = []
    for part in parts:
        c = ((part + pltpu.roll(part, LANES - FOX_HEADS, 1))
             + pltpu.roll(part, LANES - 2 * FOX_HEADS, 1)) + carry
        carry = c[tm - 1:tm, :]
        packed.append(_pack3(jnp.where(heads, c * LOG2E, 0.0)))
    for k in range(n_tiles):
        placed = _dot(packed[k], place_ref[...])
        rows = slice(k * tm, (k + 1) * tm)
        aqt_ref[:, rows] = (placed[:, :LANES] + const_ref[0:1, :LANES]).T.astype(BF16)
        ak_ref[rows, :] = (const_ref[0:1, LANES:] - placed[:, LANES:]).astype(BF16)


def _cum(fl, bf_pad, tri, place, consts, tm, seq):
    t = fl.shape[0]
    return pl.pallas_call(
        functools.partial(_cum_kernel, tm=tm),
        grid=(t // seq,),
        in_specs=[pl.BlockSpec((seq, LANES), lambda i: (i, 0)),
                  _const_spec((1, LANES)),
                  _const_spec((tm, tm)),
                  _const_spec((LANES, 2 * LANES)),
                  _const_spec((8, 2 * LANES))],
        out_specs=[pl.BlockSpec((None, LANES, seq), lambda i: (i, 0, 0)),
                   pl.BlockSpec((seq, LANES), lambda i: (i, 0))],
        out_shape=[jax.ShapeDtypeStruct((t // seq, LANES, seq), BF16),
                   jax.ShapeDtypeStruct((t, LANES), BF16)],
        compiler_params=_params(1),
        name="cum",
    )(fl, bf_pad, tri, place, consts)


def _local_kernel(bucket_ref, relb_ref, sink_ref,
                  qa_ref, kc_ref, kp_ref, vc_ref, vp_ref, qm_ref, mk_ref, mvt_ref,
                  oa_ref, om_ref, tbl_ref, *, tq):
    b = pl.program_id(0)
    i = pl.program_id(1)
    pairs = SWA_HEADS // 2

    @pl.when(jnp.logical_and(b == 0, i == 0))
    def _():
        bucket = bucket_ref[...]
        band = bucket >= 0
        for h in range(SWA_HEADS):
            t = jnp.zeros(bucket.shape, F32)
            for k in range(REL_BUCKETS):
                t = jnp.where(bucket == k, relb_ref[k, h], t)
            tbl_ref[h // 2, :, (h % 2) * BLOCK:(h % 2 + 1) * BLOCK] = jnp.where(band, t, NEG)

    lane = lax.broadcasted_iota(jnp.int32, (BLOCK, LANES), 1)
    lower = lane < HEAD64
    first_head = lax.broadcasted_iota(jnp.int32, (1, 2 * BLOCK), 1) < BLOCK
    key_row = lax.broadcasted_iota(jnp.int32, (2 * BLOCK, 2 * BLOCK), 0)
    pad_rows = jnp.logical_and(key_row < BLOCK, i == 0)
    ones = jnp.ones((BF16_ROWS, 2 * BLOCK), BF16)

    def windows(j):
        r0 = j * BLOCK
        if j == 0:
            return (jnp.concatenate([kp_ref[...], kc_ref[0:BLOCK, :]], axis=0),
                    jnp.concatenate([vp_ref[...], vc_ref[:, 0:BLOCK]], axis=1))
        return kc_ref[r0 - BLOCK:r0 + BLOCK, :], vc_ref[:, r0 - BLOCK:r0 + BLOCK]

    def swa_scores(j, hb):
        g = (2 * hb) // SWA_GROUP
        qblk = qa_ref[j * BLOCK:(j + 1) * BLOCK, hb * LANES:(hb + 1) * LANES]
        qstack = jnp.concatenate([jnp.where(lower, qblk, 0),
                                  jnp.where(lower, 0, qblk)], axis=0)
        s = _dot_nt(windows(j)[0][:, g * LANES:(g + 1) * LANES], qstack) + tbl_ref[hb]
        return jnp.where(pad_rows, NEG, s) if j == 0 else s

    def swa_finish(j, hb, s):
        g = (2 * hb) // SWA_GROUP
        sink = jnp.where(first_head, sink_ref[2 * hb], sink_ref[2 * hb + 1])
        m = jnp.maximum(jnp.max(s, axis=0, keepdims=True), sink)
        p = jnp.exp2(s - m).astype(BF16)
        vt_ones = jnp.concatenate([windows(j)[1][g * HEAD64:(g + 1) * HEAD64, :], ones], axis=0)
        acc = _dot(vt_ones, p)
        o = acc[:HEAD64, :] / (acc[HEAD64:HEAD64 + 1, :] + jnp.exp2(sink - m))
        o = jnp.concatenate([o[:, :BLOCK], o[:, BLOCK:]], axis=0)
        oa_ref[j * BLOCK:(j + 1) * BLOCK, hb * LANES:(hb + 1) * LANES] = o.T.astype(BF16)

    ones_m = jnp.ones((BF16_ROWS, N_MEM), BF16)

    def mem_scores(h):
        sl = slice(h * MEM_HEAD_DIM, (h + 1) * MEM_HEAD_DIM)
        return _dot_nt(mk_ref[:, sl], qm_ref[:, sl])

    def mem_finish(h, s):
        sl = slice(h * MEM_HEAD_DIM, (h + 1) * MEM_HEAD_DIM)
        m = jnp.max(s, axis=0, keepdims=True)
        p = jnp.exp2(s - m).astype(BF16)
        acc = _dot(jnp.concatenate([mvt_ref[sl, :], ones_m], axis=0), p)
        o = acc[:MEM_HEAD_DIM, :] / acc[MEM_HEAD_DIM:MEM_HEAD_DIM + 1, :]
        om_ref[:, sl] = o.T.astype(BF16)

    units = [(swa_scores, swa_finish, (j, hb)) for j in range(tq // BLOCK) for hb in range(pairs)]
    units += [(mem_scores, mem_finish, (h,)) for h in range(MEM_HEADS)]
    pending = []
    for n in range(len(units) + LOCAL_LOOKAHEAD):
        if n < len(units):
            score_fn, _, args = units[n]
            pending.append(score_fn(*args))
        if n >= LOCAL_LOOKAHEAD:
            _, finish_fn, args = units[n - LOCAL_LOOKAHEAD]
            finish_fn(*args, pending[n - LOCAL_LOOKAHEAD])


def _local(bucket_t, rel_bias, sinks, qa, ka, vat, qm, mk, mvt, batch, seq, tq):
    t = qa.shape[0]
    nq = seq // tq
    sub = tq // BLOCK

    def cur(b, i):
        return (b * nq + i, 0)

    def prev(b, i):
        return (jnp.maximum((b * nq + i) * sub - 1, 0), 0)

    smem = pl.BlockSpec(memory_space=pltpu.SMEM)
    return pl.pallas_call(
        functools.partial(_local_kernel, tq=tq),
        grid=(batch, nq),
        in_specs=[_const_spec((2 * BLOCK, BLOCK)), smem, smem,
                  pl.BlockSpec((tq, SWA_Q), cur),
                  pl.BlockSpec((tq, 2 * SWA_KV), cur),
                  pl.BlockSpec((BLOCK, 2 * SWA_KV), prev),
                  pl.BlockSpec((None, SWA_KV, tq), lambda b, i: (b, 0, i)),
                  pl.BlockSpec((None, SWA_KV, BLOCK),
                               lambda b, i: (b, 0, jnp.maximum(i * sub - 1, 0))),
                  pl.BlockSpec((tq, MEM_W), cur),
                  pl.BlockSpec((N_MEM, MEM_W), lambda b, i: (b, 0)),
                  pl.BlockSpec((None, MEM_W, N_MEM), lambda b, i: (b, 0, 0))],
        out_specs=[pl.BlockSpec((tq, SWA_Q), cur),
                   pl.BlockSpec((tq, MEM_W), cur)],
        out_shape=[jax.ShapeDtypeStruct((t, SWA_Q), BF16),
                   jax.ShapeDtypeStruct((t, MEM_W), BF16)],
        scratch_shapes=[pltpu.VMEM((SWA_HEADS // 2, 2 * BLOCK, 2 * BLOCK), F32)],
        compiler_params=_params(2),
        name="local",
    )(bucket_t, rel_bias, sinks, qa, ka, ka, vat, vat, qm, mk, mvt)


def _fox_kernel(fast_ref, qt_ref, aqt_ref, k_ref, ak_ref, vt_ref, o_ref,
                sa_ref, sb_ref, acc_ref, m_ref, *, tk, n_sub, n_steps):
    pair = pl.program_id(1)
    g = pl.program_id(2)
    row = lax.broadcasted_iota(jnp.int32, (LANES, tk), 0)
    half = [row < HEAD64, row >= HEAD64]
    qcat = []
    for sub in range(n_sub):
        q = qt_ref[:, sub * tk:(sub + 1) * tk]
        aq = aqt_ref[:, sub * tk:(sub + 1) * tk]
        zero = jnp.zeros_like(q)
        per_head = []
        for h in range(2):
            lo = (2 * pair + h) * AUG_PER_HEAD
            mine = jnp.logical_and(row >= lo, row < lo + AUG_PER_HEAD)
            per_head.append(jnp.concatenate([jnp.where(half[h], q, zero),
                                             jnp.where(mine, aq, zero)], axis=0))
        qcat.append(per_head)
    ones = jnp.ones((BF16_ROWS, tk), BF16)
    acc_ref[...] = jnp.zeros_like(acc_ref)
    key = lax.broadcasted_iota(jnp.int32, (tk, tk), 0)
    qry = lax.broadcasted_iota(jnp.int32, (tk, tk), 1)
    causal = key <= qry

    def kcat(j):
        r0 = pl.multiple_of(j * tk, tk)
        return jnp.concatenate([k_ref[pl.ds(r0, tk), :], ak_ref[pl.ds(r0, tk), :]], axis=1)

    def vt_ones(j, h):
        return jnp.concatenate([vt_ref[j, h * HEAD64:(h + 1) * HEAD64, :], ones], axis=0)

    def visible_keys(c, diagonal):
        return min(tk, c + MXU_DIM) if diagonal else tk

    def qk(sub, j, s_ref, diagonal=False):
        kc = kcat(j)
        for h in range(2):
            for c in range(0, tk, MXU_DIM):
                nk = visible_keys(c, diagonal)
                s_ref[h, :nk, c:c + MXU_DIM] = _dot(kc[:nk], qcat[sub][h][:, c:c + MXU_DIM])

    @pl.when(fast_ref[0] == 1)
    def _():
        bufs = (sa_ref, sb_ref)

        def exp_pv(sub, j, s_ref, masked):
            for h in range(2):
                for c in range(0, tk, MXU_DIM):
                    nk = visible_keys(c, masked)
                    s = s_ref[h, :nk, c:c + MXU_DIM]
                    if masked:
                        s = jnp.where(causal[:nk, c:c + MXU_DIM], s, NEG)
                    acc_ref[2 * sub + h, :, c:c + MXU_DIM] += _dot(vt_ones(j, h)[:, :nk],
                                                                   jnp.exp2(s).astype(BF16))

        full = [(sub, jj) for jj in range(n_sub) for sub in range(n_sub)]
        diag = [(sub, jj) for jj in range(n_sub) for sub in range(jj, n_sub)]
        assert len(full) % 2 == 0
        qk(0, 0, bufs[0])

        def body(t, carry):
            for n, (sub, jj) in enumerate(full):
                nsub, njj = full[n + 1] if n + 1 < len(full) else (0, n_sub)
                qk(nsub, n_sub * t + njj, bufs[(n + 1) % 2])
                exp_pv(sub, n_sub * t + jj, bufs[n % 2], False)
            return carry

        if n_steps > 1:
            lax.fori_loop(0, g, body, 0)
        for n, (sub, jj) in enumerate(diag):
            if n + 1 < len(diag):
                nsub, njj = diag[n + 1]
                qk(nsub, n_sub * g + njj, bufs[(n + 1) % 2], diagonal=nsub == njj)
            exp_pv(sub, n_sub * g + jj, bufs[n % 2], sub == jj)

    @pl.when(fast_ref[0] == 0)
    def _():
        m_ref[...] = jnp.full(m_ref.shape, NEG, F32)
        for sub in range(n_sub):
            last = n_sub * g + sub

            def body(j, carry, sub=sub, last=last):
                qk(sub, j, sa_ref)
                for h in range(2):
                    u = 2 * sub + h
                    s = jnp.where(jnp.logical_or(causal, j < last), sa_ref[h], NEG)
                    m_prev = m_ref[u]
                    m_next = jnp.maximum(m_prev, jnp.max(s, axis=0, keepdims=True))
                    alpha = jnp.exp2(m_prev - m_next)
                    p = jnp.exp2(s - m_next).astype(BF16)
                    acc_ref[u] = alpha * acc_ref[u] + _dot(vt_ones(j, h), p)
                    m_ref[u] = m_next
                return carry

            lax.fori_loop(0, last + 1, body, 0)

    for sub in range(n_sub):
        halves = []
        for h in range(2):
            acc = acc_ref[2 * sub + h]
            halves.append(acc[:HEAD64, :] / acc[HEAD64:HEAD64 + 1, :])
        o_ref[sub * tk:(sub + 1) * tk, :] = jnp.concatenate(halves, axis=0).T.astype(BF16)


def _fox(fast, qft, aqt, kf, ak, vt, batch, seq, tk):
    t = kf.shape[0]
    n_sub = min(FOX_SUB, seq // tk)
    tq = n_sub * tk
    assert seq % tq == 0, (seq, tq)
    nq = seq // tq
    pairs = FOX_HEADS // 2
    return pl.pallas_call(
        functools.partial(_fox_kernel, tk=tk, n_sub=n_sub, n_steps=nq),
        grid=(batch, pairs, nq),
        in_specs=[pl.BlockSpec(memory_space=pltpu.SMEM),
                  pl.BlockSpec((None, None, LANES, tq), lambda b, p, i: (b, p, 0, i)),
                  pl.BlockSpec((None, LANES, tq), lambda b, p, i: (b, 0, i)),
                  pl.BlockSpec((seq, LANES), lambda b, p, i: (b, p)),
                  pl.BlockSpec((seq, LANES), lambda b, p, i: (b, 0)),
                  pl.BlockSpec((None, None, seq // tk, LANES, tk),
                               lambda b, p, i: (b, p, 0, 0, 0))],
        out_specs=pl.BlockSpec((tq, LANES), lambda b, p, i: (b * nq + i, p)),
        out_shape=jax.ShapeDtypeStruct((t, FOX_W), BF16),
        scratch_shapes=[pltpu.VMEM((2, tk, tk), F32),
                        pltpu.VMEM((2, tk, tk), F32),
                        pltpu.VMEM((2 * n_sub, HEAD64 + BF16_ROWS, tk), F32),
                        pltpu.VMEM((2 * n_sub, 1, tk), F32)],
        compiler_params=_params(3),
        name="fox",
    )(fast, qft, aqt, kf, ak, vt)


def _post_kernel(x_ref, oa_ref, of_ref, om_ref, gate_ref, wa_ref, wf_ref, wm_ref, wo_ref,
                 g_ref, wu_ref, wd_ref, y_ref, *, chunk):
    ya = _dot(oa_ref[...], wa_ref[...])
    yf = _dot(of_ref[...], wf_ref[...])
    ym = _dot(om_ref[...], wm_ref[...])
    merged = (gate_ref[:, 0:D_MODEL].astype(F32) * ya
              + gate_ref[:, D_MODEL:2 * D_MODEL].astype(F32) * yf
              + gate_ref[:, 2 * D_MODEL:].astype(F32) * ym)
    x1 = x_ref[...] + _dot(merged.astype(BF16), wo_ref[...])
    h = _rms_rows(x1, g_ref[...]).astype(BF16)
    acc = x1
    for c in range(0, D_FF, chunk):
        u = jnp.maximum(_dot(h, wu_ref[:, c:c + chunk]), 0.0)
        acc = acc + _dot((u * u).astype(BF16), wd_ref[c:c + chunk, :])
    y_ref[...] = acc


def _post(x2, oa, of, om, gates, wa, wf, wm, wo, g_mlp, wu, wd, tm, chunk=1024):
    t = x2.shape[0]
    row = lambda w: pl.BlockSpec((tm, w), lambda i: (i, 0))
    resident = lambda shape: pl.BlockSpec(shape, lambda i: (0,) * len(shape),
                                          pipeline_mode=pl.Buffered(1))
    return pl.pallas_call(
        functools.partial(_post_kernel, chunk=chunk),
        grid=(t // tm,),
        in_specs=[row(D_MODEL), row(SWA_Q), row(FOX_W), row(MEM_W), row(GATE_W),
                  resident((SWA_Q, D_MODEL)), resident((FOX_W, D_MODEL)),
                  resident((MEM_W, D_MODEL)), resident((D_MODEL, D_MODEL)),
                  resident((1, D_MODEL)), resident((D_MODEL, D_FF)), resident((D_FF, D_MODEL))],
        out_specs=row(D_MODEL),
        out_shape=jax.ShapeDtypeStruct((t, D_MODEL), F32),
        compiler_params=_params(1),
        name="post",
    )(x2, oa, of, om, gates, wa, wf, wm, wo, g_mlp, wu, wd)


def _t5_bucket_table():
    max_exact = REL_BUCKETS // 2
    t_loc = jnp.arange(BLOCK)[:, None] + BLOCK
    s_loc = jnp.arange(2 * BLOCK)[None, :]
    dist = t_loc - s_loc
    d = jnp.maximum(dist, 0)
    df = jnp.maximum(d, 1).astype(F32)
    large = max_exact + (jnp.log(df / max_exact) / math.log(REL_MAX_DIST / max_exact)
                         * (REL_BUCKETS - max_exact)).astype(jnp.int32)
    large = jnp.minimum(large, REL_BUCKETS - 1)
    bucket = jnp.where(d < max_exact, d, large)
    band = (dist >= 0) & (dist < WINDOW)
    return jnp.where(band, bucket, -1).astype(jnp.int32).T


def _aug_placement(shift):
    place = [[0.0] * (2 * LANES) for _ in range(LANES)]
    ones = [0.0] * (2 * LANES)
    shift_lanes = []
    for h in range(FOX_HEADS):
        base = h * AUG_PER_HEAD
        for part in range(3):
            src = part * FOX_HEADS + h
            place[src][base + part] = 1.0
            place[src][LANES + base + 3 + part] = 1.0
            ones[base + 3 + part] = 1.0
            ones[LANES + base + part] = 1.0
        ones[base + 6] = 1.0
        shift_lanes.append(LANES + base + 6)
    consts = jnp.array(ones, F32).at[jnp.array(shift_lanes)].set(-shift)
    return jnp.array(place, BF16), jnp.broadcast_to(consts[None, :], (8, 2 * LANES))


def _tile_gain(g, reps, scale=1.0):
    return (jnp.tile(g.astype(F32), reps) * scale)[None, :]


def _pick_tile(n, target):
    t = min(n, target)
    while n % t:
        t //= 2
    return t


def kernel(x, mem, g_mix, w_in, b_gate, b_forget, qn_swa, kn_swa, sink_swa, rel_bias,
           qn_fox, kn_fox, g_mem, w_mem_kv, qn_mem, kn_mem, w_o_swa, w_o_fox, w_o_mem,
           w_out, g_mlp, w_mlp_up, w_mlp_down):
    batch, seq, _ = x.shape
    n_layers = w_in.shape[0]
    t = batch * seq
    tm = _pick_tile(seq, 512)
    tq = _pick_tile(seq, 512)

    bucket = _t5_bucket_table()
    tri = (jnp.arange(tm)[:, None] >= jnp.arange(tm)[None, :]).astype(BF16)

    x2 = x.reshape(t, D_MODEL)
    mem2 = mem.reshape(batch * N_MEM, D_MODEL)
    for layer in range(n_layers):
        wt = w_in[layer].T
        o = 0
        parts = {}
        for name, width in (("qa", SWA_Q), ("ka", SWA_KV), ("va", SWA_KV), ("qf", FOX_W),
                            ("kf", FOX_W), ("vf", FOX_W), ("fl", FOX_HEADS), ("qm", MEM_W),
                            ("g", GATE_W)):
            parts[name] = wt[o:o + width]
            o += width

        def dup(m):
            return jnp.concatenate([m[:HEAD64], m[:HEAD64], m[HEAD64:], m[HEAD64:]], axis=0)

        w1 = jnp.concatenate(
            [parts["qa"], dup(parts["ka"]), parts["kf"], parts["qm"], parts["g"],
             jnp.pad(parts["fl"], ((0, LANES - FOX_HEADS), (0, 0)))], axis=0).astype(BF16)

        wmt = w_mem_kv[layer].T.astype(BF16)
        mk, mvt = _memkv(mem2, g_mem[layer][None, :], wmt[:MEM_W], wmt[MEM_W:],
                         _tile_gain(kn_mem[layer], MEM_HEADS))

        bound = (HEAD64 ** 0.5 * LOG2E * 1.02 * jnp.max(jnp.abs(qn_fox[layer]))
                 * jnp.max(jnp.abs(kn_fox[layer]))).astype(BF16).astype(F32)
        fast = bound <= FOX_FAST_MAX_SHIFT
        place, consts = _aug_placement(jnp.where(fast, bound, 0.0))
        bf_pad = jnp.pad(b_forget[layer].astype(F32), (0, LANES - FOX_HEADS))[None, :]

        wt = jnp.concatenate([parts["vf"], parts["va"], parts["qf"]], axis=0).astype(BF16)
        gqft = jnp.broadcast_to(
            _tile_gain(qn_fox[layer], FOX_HEADS, HEAD64 ** -0.5 * LOG2E).T, (FOX_W, LANES))
        qa, ka, vat, qft, kf, vt, qm, gates, fl = _proj(
            x2, g_mix[layer][None, :], w1, wt, b_gate[layer][None, :],
            _tile_gain(qn_swa[layer], SWA_HEADS, HEAD64 ** -0.5 * LOG2E),
            _tile_gain(kn_swa[layer], 2 * SWA_KV_HEADS),
            gqft,
            _tile_gain(kn_fox[layer], FOX_HEADS),
            _tile_gain(qn_mem[layer], MEM_HEADS, MEM_HEAD_DIM ** -0.5 * LOG2E), tm, seq)
        aqt, ak = _cum(fl, bf_pad, tri, place, consts, tm, seq)

        oa, om = _local(bucket, rel_bias.astype(F32) * LOG2E, sink_swa[layer].astype(F32) * LOG2E,
                        qa, ka, vat, qm, mk, mvt, batch, seq, tq)
        of = _fox(fast.astype(jnp.int32)[None], qft, aqt, kf, ak, vt, batch, seq, tm)

        x2 = _post(x2, oa, of, om, gates, w_o_swa[layer].astype(BF16),
                   w_o_fox[layer].astype(BF16), w_o_mem[layer].astype(BF16),
                   w_out[layer].astype(BF16), g_mlp[layer][None, :],
                   w_mlp_up[layer].astype(BF16), w_mlp_down[layer].astype(BF16), tm)
    return x2.reshape(batch, seq, D_MODEL)
```

```python
import functools
import math

import jax
import jax.numpy as jnp
import numpy as np
from jax import lax
from jax.experimental import pallas as pl
from jax.experimental.pallas import tpu as pltpu

F32 = jnp.float32
BF16 = jnp.bfloat16

D_MODEL = 1024
N_MEM = 256
SWA_HEADS = 8
SWA_KV_HEADS = 2
SWA_GROUP = SWA_HEADS // SWA_KV_HEADS
HEAD64 = 64
WINDOW = 128
FOX_HEADS = 8
MEM_HEADS = 4
MEM_HEAD_DIM = 128
D_FF = 4 * D_MODEL
REL_BUCKETS = 32
REL_MAX_DIST = 128
BLOCK = 128
EPS = 1e-6
NEG = -1e30
LOG2E = math.log2(math.e)

LANES = 128
MXU_DIM = 256
BF16_ROWS = 16
VMEM_LIMIT = 56 * 1024 * 1024

SWA_Q = SWA_HEADS * HEAD64
SWA_KV = SWA_KV_HEADS * HEAD64
FOX_W = FOX_HEADS * HEAD64
MEM_W = MEM_HEADS * MEM_HEAD_DIM
GATE_W = 3 * D_MODEL

C_QA = 0
C_KA = C_QA + SWA_Q
C_KF = C_KA + 2 * SWA_KV
C_QM = C_KF + FOX_W
C_G = C_QM + MEM_W
C_FL = C_G + GATE_W
C_END = C_FL + LANES

AUG_PER_HEAD = 7
FOX_FAST_MAX_SHIFT = 60.0
FOX_SUB = 4
LOCAL_LOOKAHEAD = 4


def _dot(a, b):
    return jnp.dot(a, b, preferred_element_type=F32)


def _dot_nt(a, b):
    return lax.dot_general(a, b, (((1,), (1,)), ((), ())), preferred_element_type=F32)


def _rms_rows(x, gain):
    ms = jnp.mean(x * x, axis=-1, keepdims=True)
    return x * lax.rsqrt(ms + EPS) * gain


def _head_rms(y, gain, head_dim):
    y2 = y * y
    lower = lax.broadcasted_iota(jnp.int32, (y.shape[0], LANES), 1) < HEAD64
    cols = []
    for c in range(0, y.shape[1], LANES):
        blk = y2[:, c:c + LANES]
        total = jnp.sum(blk, axis=-1, keepdims=True)
        if head_dim == LANES:
            cols.append(jnp.broadcast_to(total, blk.shape))
        else:
            low = jnp.sum(jnp.where(lower, blk, 0.0), axis=-1, keepdims=True)
            cols.append(jnp.where(lower, low, total - low))
    ss = jnp.concatenate(cols, axis=1)
    return y * lax.rsqrt(ss * (1.0 / head_dim) + EPS) * gain


def _const_spec(shape):
    return pl.BlockSpec(shape, lambda *_: (0,) * len(shape))


def _params(n_axes, flags=None):
    return pltpu.CompilerParams(dimension_semantics=("arbitrary",) * n_axes,
                                vmem_limit_bytes=VMEM_LIMIT, flags=flags)


def _memkv_kernel(mem_ref, g_ref, wkt_ref, wvt_ref, kn_ref, mk_ref, mvt_ref):
    h = _rms_rows(mem_ref[...], g_ref[...]).astype(BF16)
    mk_ref[...] = _head_rms(_dot_nt(h, wkt_ref[...]), kn_ref[...], MEM_HEAD_DIM).astype(BF16)
    mvt_ref[...] = _dot_nt(wvt_ref[...], h).astype(BF16)


def _memkv(mem2, g_mem, wkt, wvt, kn_t):
    rows = mem2.shape[0]
    nb = rows // N_MEM
    return pl.pallas_call(
        _memkv_kernel,
        grid=(nb,),
        in_specs=[pl.BlockSpec((N_MEM, D_MODEL), lambda i: (i, 0)),
                  _const_spec((1, D_MODEL)),
                  _const_spec((MEM_W, D_MODEL)),
                  _const_spec((MEM_W, D_MODEL)),
                  _const_spec((1, MEM_W))],
        out_specs=[pl.BlockSpec((N_MEM, MEM_W), lambda i: (i, 0)),
                   pl.BlockSpec((None, MEM_W, N_MEM), lambda i: (i, 0, 0))],
        out_shape=[jax.ShapeDtypeStruct((rows, MEM_W), BF16),
                   jax.ShapeDtypeStruct((nb, MEM_W, N_MEM), BF16)],
        compiler_params=_params(1),
        name="memkv",
    )(mem2, g_mem, wkt, wvt, kn_t)


def _proj_kernel(x_ref, g_ref, w_ref, wt_ref, bg_ref,
                 gqa_ref, gka_ref, gqft_ref, gkf_ref, gqm_ref,
                 qa_ref, ka_ref, vat_ref, qft_ref, kf_ref, vt_ref, qm_ref, gate_ref, fl_ref):
    h = _rms_rows(x_ref[...], g_ref[...]).astype(BF16)
    tm = h.shape[0]
    qkv = _dot_nt(h, w_ref[:C_G, :])
    qa_ref[...] = _head_rms(qkv[:, C_QA:C_KA], gqa_ref[...], HEAD64).astype(BF16)
    ka_ref[...] = _head_rms(qkv[:, C_KA:C_KF], gka_ref[...], HEAD64).astype(BF16)
    kf_ref[...] = _head_rms(qkv[:, C_KF:C_QM], gkf_ref[...], HEAD64).astype(BF16)
    qm_ref[...] = _head_rms(qkv[:, C_QM:C_G], gqm_ref[...], MEM_HEAD_DIM).astype(BF16)
    t = _dot_nt(wt_ref[...], h)
    for p in range(FOX_HEADS // 2):
        vt_ref[p] = t[p * LANES:(p + 1) * LANES, :].astype(BF16)
    vat_ref[...] = t[FOX_W:FOX_W + SWA_KV, :].astype(BF16)
    gain_t = jnp.tile(gqft_ref[...], (1, tm // LANES))
    for p in range(FOX_HEADS // 2):
        rows = []
        for e in range(2):
            r0 = FOX_W + SWA_KV + (2 * p + e) * HEAD64
            y = t[r0:r0 + HEAD64, :]
            ms = jnp.sum(y * y, axis=0, keepdims=True) * (1.0 / HEAD64)
            rows.append(y * lax.rsqrt(ms + EPS) * gain_t[r0 - FOX_W - SWA_KV:r0 - FOX_W - SWA_KV
                                                         + HEAD64, :])
        qft_ref[p] = jnp.concatenate(rows, axis=0).astype(BF16)
    zf = _dot_nt(h, w_ref[C_G:C_END, :])
    z = zf[:, :GATE_W] + bg_ref[...]
    gate_ref[...] = (1.0 / (1.0 + jnp.exp(-z))).astype(BF16)
    fl_ref[...] = zf[:, GATE_W:]


def _proj(x2, g_mix, w1, wt, b_gate, gqa, gka, gqft, gkf, gqm, tm, seq):
    t = x2.shape[0]
    nt = seq // tm
    pairs = FOX_HEADS // 2

    def rows(w, dtype=BF16):
        return jax.ShapeDtypeStruct((t, w), dtype), pl.BlockSpec((tm, w), lambda i: (i, 0))

    outs = [rows(SWA_Q), rows(2 * SWA_KV),
            (jax.ShapeDtypeStruct((t // seq, SWA_KV, seq), BF16),
             pl.BlockSpec((None, SWA_KV, tm), lambda i: (i // nt, 0, i % nt))),
            (jax.ShapeDtypeStruct((t // seq, pairs, LANES, seq), BF16),
             pl.BlockSpec((None, pairs, LANES, tm), lambda i: (i // nt, 0, 0, i % nt))),
            rows(FOX_W),
            (jax.ShapeDtypeStruct((t // seq, pairs, nt, LANES, tm), BF16),
             pl.BlockSpec((None, pairs, None, LANES, tm), lambda i: (i // nt, 0, i % nt, 0, 0))),
            rows(MEM_W), rows(GATE_W), rows(LANES, F32)]
    out_shape = [o[0] for o in outs]
    out_specs = [o[1] for o in outs]
    return pl.pallas_call(
        _proj_kernel,
        grid=(t // tm,),
        in_specs=[pl.BlockSpec((tm, D_MODEL), lambda i: (i, 0)),
                  _const_spec((1, D_MODEL)),
                  _const_spec((C_END, D_MODEL)),
                  _const_spec((2 * FOX_W + SWA_KV, D_MODEL)),
                  _const_spec((1, GATE_W)),
                  _const_spec((1, SWA_Q)),
                  _const_spec((1, 2 * SWA_KV)),
                  _const_spec((FOX_W, LANES)),
                  _const_spec((1, FOX_W)),
                  _const_spec((1, MEM_W))],
        out_specs=out_specs,
        out_shape=out_shape,
        compiler_params=_params(1),
        name="proj",
    )(x2, g_mix, w1, wt, b_gate, gqa, gka, gqft, gkf, gqm)


def _split3(c):
    hi = c.astype(BF16)
    r1 = c - hi.astype(F32)
    mid = r1.astype(BF16)
    lo = (r1 - mid.astype(F32)).astype(BF16)
    return hi, mid, lo


def _pack3(x):
    hi, mid, lo = _split3(x)
    return (hi.astype(F32) + pltpu.roll(mid.astype(F32), FOX_HEADS, 1)
            + pltpu.roll(lo.astype(F32), 2 * FOX_HEADS, 1)).astype(BF16)


def _cum_kernel(fl_ref, bf_ref, tri_ref, place_ref, const_ref, aqt_ref, ak_ref, *, tm):
    n_tiles = fl_ref.shape[0] // tm
    lane = lax.broadcasted_iota(jnp.int32, (tm, LANES), 1)
    heads = lane < FOX_HEADS
    tri = tri_ref[...]
    parts = []
    for k in range(n_tiles):
        z = fl_ref[k * tm:(k + 1) * tm, :] + bf_ref[...]
        logf = jnp.where(heads, jnp.minimum(z, 0.0) - jnp.log(1.0 + jnp.exp(-jnp.abs(z))), 0.0)
        parts.append(_dot(tri, _pack3(logf)))
    carry = jnp.zeros((1, LANES), F32)
    packed = []
    for part in parts:
        c = ((part + pltpu.roll(part, LANES - FOX_HEADS, 1))
             + pltpu.roll(part, LANES - 2 * FOX_HEADS, 1)) + carry
        carry = c[tm - 1:tm, :]
        packed.append(_pack3(jnp.where(heads, c * LOG2E, 0.0)))
    for k in range(n_tiles):
        placed = _dot(packed[k], place_ref[...])
        rows = slice(k * tm, (k + 1) * tm)
        aqt_ref[:, rows] = (placed[:, :LANES] + const_ref[0:1, :LANES]).T.astype(BF16)
        ak_ref[rows, :] = (const_ref[0:1, LANES:] - placed[:, LANES:]).astype(BF16)


def _cum(fl, bf_pad, tri, place, consts, tm, seq):
    t = fl.shape[0]
    return pl.pallas_call(
        functools.partial(_cum_kernel, tm=tm),
        grid=(t // seq,),
        in_specs=[pl.BlockSpec((seq, LANES), lambda i: (i, 0)),
                  _const_spec((1, LANES)),
                  _const_spec((tm, tm)),
                  _const_spec((LANES, 2 * LANES)),
                  _const_spec((8, 2 * LANES))],
        out_specs=[pl.BlockSpec((None, LANES, seq), lambda i: (i, 0, 0)),
                   pl.BlockSpec((seq, LANES), lambda i: (i, 0))],
        out_shape=[jax.ShapeDtypeStruct((t // seq, LANES, seq), BF16),
                   jax.ShapeDtypeStruct((t, LANES), BF16)],
        compiler_params=_params(1),
        name="cum",
    )(fl, bf_pad, tri, place, consts)


def _local_kernel(bucket_ref, relb_ref, sink_ref,
                  qa_ref, kc_ref, kp_ref, vc_ref, vp_ref, qm_ref, mk_ref, mvt_ref,
                  oa_ref, om_ref, tbl_ref, *, tq):
    b = pl.program_id(0)
    i = pl.program_id(1)
    pairs = SWA_HEADS // 2

    @pl.when(jnp.logical_and(b == 0, i == 0))
    def _():
        bucket = bucket_ref[...]
        band = bucket >= 0
        for h in range(SWA_HEADS):
            t = jnp.zeros(bucket.shape, F32)
            for k in range(REL_BUCKETS):
                t = jnp.where(bucket == k, relb_ref[k, h], t)
            tbl_ref[h // 2, :, (h % 2) * BLOCK:(h % 2 + 1) * BLOCK] = jnp.where(band, t, NEG)

    lane = lax.broadcasted_iota(jnp.int32, (BLOCK, LANES), 1)
    lower = lane < HEAD64
    first_head = lax.broadcasted_iota(jnp.int32, (1, 2 * BLOCK), 1) < BLOCK
    key_row = lax.broadcasted_iota(jnp.int32, (2 * BLOCK, 2 * BLOCK), 0)
    pad_rows = jnp.logical_and(key_row < BLOCK, i == 0)
    ones = jnp.ones((BF16_ROWS, 2 * BLOCK), BF16)

    def windows(j):
        r0 = j * BLOCK
        if j == 0:
            return (jnp.concatenate([kp_ref[...], kc_ref[0:BLOCK, :]], axis=0),
                    jnp.concatenate([vp_ref[...], vc_ref[:, 0:BLOCK]], axis=1))
        return kc_ref[r0 - BLOCK:r0 + BLOCK, :], vc_ref[:, r0 - BLOCK:r0 + BLOCK]

    def swa_scores(j, hb):
        g = (2 * hb) // SWA_GROUP
        qblk = qa_ref[j * BLOCK:(j + 1) * BLOCK, hb * LANES:(hb + 1) * LANES]
        qstack = jnp.concatenate([jnp.where(lower, qblk, 0),
                                  jnp.where(lower, 0, qblk)], axis=0)
        s = _dot_nt(windows(j)[0][:, g * LANES:(g + 1) * LANES], qstack) + tbl_ref[hb]
        return jnp.where(pad_rows, NEG, s) if j == 0 else s

    def swa_finish(j, hb, s):
        g = (2 * hb) // SWA_GROUP
        sink = jnp.where(first_head, sink_ref[2 * hb], sink_ref[2 * hb + 1])
        m = jnp.maximum(jnp.max(s, axis=0, keepdims=True), sink)
        p = jnp.exp2(s - m).astype(BF16)
        vt_ones = jnp.concatenate([windows(j)[1][g * HEAD64:(g + 1) * HEAD64, :], ones], axis=0)
        acc = _dot(vt_ones, p)
        o = acc[:HEAD64, :] / (acc[HEAD64:HEAD64 + 1, :] + jnp.exp2(sink - m))
        o = jnp.concatenate([o[:, :BLOCK], o[:, BLOCK:]], axis=0)
        oa_ref[j * BLOCK:(j + 1) * BLOCK, hb * LANES:(hb + 1) * LANES] = o.T.astype(BF16)

    ones_m = jnp.ones((BF16_ROWS, N_MEM), BF16)

    def mem_scores(h):
        sl = slice(h * MEM_HEAD_DIM, (h + 1) * MEM_HEAD_DIM)
        return _dot_nt(mk_ref[:, sl], qm_ref[:, sl])

    def mem_finish(h, s):
        sl = slice(h * MEM_HEAD_DIM, (h + 1) * MEM_HEAD_DIM)
        m = jnp.max(s, axis=0, keepdims=True)
        p = jnp.exp2(s - m).astype(BF16)
        acc = _dot(jnp.concatenate([mvt_ref[sl, :], ones_m], axis=0), p)
        o = acc[:MEM_HEAD_DIM, :] / acc[MEM_HEAD_DIM:MEM_HEAD_DIM + 1, :]
        om_ref[:, sl] = o.T.astype(BF16)

    units = [(swa_scores, swa_finish, (j, hb)) for j in range(tq // BLOCK) for hb in range(pairs)]
    units += [(mem_scores, mem_finish, (h,)) for h in range(MEM_HEADS)]
    pending = []
    for n in range(len(units) + LOCAL_LOOKAHEAD):
        if n < len(units):
            score_fn, _, args = units[n]
            pending.append(score_fn(*args))
        if n >= LOCAL_LOOKAHEAD:
            _, finish_fn, args = units[n - LOCAL_LOOKAHEAD]
            finish_fn(*args, pending[n - LOCAL_LOOKAHEAD])


def _local(bucket_t, rel_bias, sinks, qa, ka, vat, qm, mk, mvt, batch, seq, tq):
    t = qa.shape[0]
    nq = seq // tq
    sub = tq // BLOCK

    def cur(b, i):
        return (b * nq + i, 0)

    def prev(b, i):
        return (jnp.maximum((b * nq + i) * sub - 1, 0), 0)

    smem = pl.BlockSpec(memory_space=pltpu.SMEM)
    return pl.pallas_call(
        functools.partial(_local_kernel, tq=tq),
        grid=(batch, nq),
        in_specs=[_const_spec((2 * BLOCK, BLOCK)), smem, smem,
                  pl.BlockSpec((tq, SWA_Q), cur),
                  pl.BlockSpec((tq, 2 * SWA_KV), cur),
                  pl.BlockSpec((BLOCK, 2 * SWA_KV), prev),
                  pl.BlockSpec((None, SWA_KV, tq), lambda b, i: (b, 0, i)),
                  pl.BlockSpec((None, SWA_KV, BLOCK),
                               lambda b, i: (b, 0, jnp.maximum(i * sub - 1, 0))),
                  pl.BlockSpec((tq, MEM_W), cur),
                  pl.BlockSpec((N_MEM, MEM_W), lambda b, i: (b, 0)),
                  pl.BlockSpec((None, MEM_W, N_MEM), lambda b, i: (b, 0, 0))],
        out_specs=[pl.BlockSpec((tq, SWA_Q), cur),
                   pl.BlockSpec((tq, MEM_W), cur)],
        out_shape=[jax.ShapeDtypeStruct((t, SWA_Q), BF16),
                   jax.ShapeDtypeStruct((t, MEM_W), BF16)],
        scratch_shapes=[pltpu.VMEM((SWA_HEADS // 2, 2 * BLOCK, 2 * BLOCK), F32)],
        compiler_params=_params(2),
        name="local",
    )(bucket_t, rel_bias, sinks, qa, ka, ka, vat, vat, qm, mk, mvt)


def _fox_kernel(fast_ref, qt_ref, aqt_ref, k_ref, ak_ref, vt_ref, o_ref,
                sa_ref, sb_ref, acc_ref, m_ref, *, tk, n_sub, n_steps):
    pair = pl.program_id(1)
    g = pl.program_id(2)
    row = lax.broadcasted_iota(jnp.int32, (LANES, tk), 0)
    half = [row < HEAD64, row >= HEAD64]
    qcat = []
    for sub in range(n_sub):
        q = qt_ref[:, sub * tk:(sub + 1) * tk]
        aq = aqt_ref[:, sub * tk:(sub + 1) * tk]
        zero = jnp.zeros_like(q)
        per_head = []
        for h in range(2):
            lo = (2 * pair + h) * AUG_PER_HEAD
            mine = jnp.logical_and(row >= lo, row < lo + AUG_PER_HEAD)
            per_head.append(jnp.concatenate([jnp.where(half[h], q, zero),
                                             jnp.where(mine, aq, zero)], axis=0))
        qcat.append(per_head)
    ones = jnp.ones((BF16_ROWS, tk), BF16)
    acc_ref[...] = jnp.zeros_like(acc_ref)
    key = lax.broadcasted_iota(jnp.int32, (tk, tk), 0)
    qry = lax.broadcasted_iota(jnp.int32, (tk, tk), 1)
    causal = key <= qry

    def kcat(j):
        r0 = pl.multiple_of(j * tk, tk)
        return jnp.concatenate([k_ref[pl.ds(r0, tk), :], ak_ref[pl.ds(r0, tk), :]], axis=1)

    def vt_ones(j, h):
        return jnp.concatenate([vt_ref[j, h * HEAD64:(h + 1) * HEAD64, :], ones], axis=0)

    def visible_keys(c, diagonal):
        return min(tk, c + MXU_DIM) if diagonal else tk

    def qk(sub, j, s_ref, diagonal=False):
        kc = kcat(j)
        for h in range(2):
            for c in range(0, tk, MXU_DIM):
                nk = visible_keys(c, diagonal)
                s_ref[h, :nk, c:c + MXU_DIM] = _dot(kc[:nk], qcat[sub][h][:, c:c + MXU_DIM])

    @pl.when(fast_ref[0] == 1)
    def _():
        bufs = (sa_ref, sb_ref)

        def exp_pv(sub, j, s_ref, masked):
            for h in range(2):
                for c in range(0, tk, MXU_DIM):
                    nk = visible_keys(c, masked)
                    s = s_ref[h, :nk, c:c + MXU_DIM]
                    if masked:
                        s = jnp.where(causal[:nk, c:c + MXU_DIM], s, NEG)
                    acc_ref[2 * sub + h, :, c:c + MXU_DIM] += _dot(vt_ones(j, h)[:, :nk],
                                                                   jnp.exp2(s).astype(BF16))

        full = [(sub, jj) for jj in range(n_sub) for sub in range(n_sub)]
        diag = [(sub, jj) for jj in range(n_sub) for sub in range(jj, n_sub)]
        assert len(full) % 2 == 0
        qk(0, 0, bufs[0])

        def body(t, carry):
            for n, (sub, jj) in enumerate(full):
                nsub, njj = full[n + 1] if n + 1 < len(full) else (0, n_sub)
                qk(nsub, n_sub * t + njj, bufs[(n + 1) % 2])
                exp_pv(sub, n_sub * t + jj, bufs[n % 2], False)
            return carry

        if n_steps > 1:
            lax.fori_loop(0, g, body, 0)
        for n, (sub, jj) in enumerate(diag):
            if n + 1 < len(diag):
                nsub, njj = diag[n + 1]
                qk(nsub, n_sub * g + njj, bufs[(n + 1) % 2], diagonal=nsub == njj)
            exp_pv(sub, n_sub * g + jj, bufs[n % 2], sub == jj)

    @pl.when(fast_ref[0] == 0)
    def _():
        m_ref[...] = jnp.full(m_ref.shape, NEG, F32)
        for sub in range(n_sub):
            last = n_sub * g + sub

            def body(j, carry, sub=sub, last=last):
                qk(sub, j, sa_ref)
                for h in range(2):
                    u = 2 * sub + h
                    s = jnp.where(jnp.logical_or(causal, j < last), sa_ref[h], NEG)
                    m_prev = m_ref[u]
                    m_next = jnp.maximum(m_prev, jnp.max(s, axis=0, keepdims=True))
                    alpha = jnp.exp2(m_prev - m_next)
                    p = jnp.exp2(s - m_next).astype(BF16)
                    acc_ref[u] = alpha * acc_ref[u] + _dot(vt_ones(j, h), p)
                    m_ref[u] = m_next
                return carry

            lax.fori_loop(0, last + 1, body, 0)

    for sub in range(n_sub):
        halves = []
        for h in range(2):
            acc = acc_ref[2 * sub + h]
            halves.append(acc[:HEAD64, :] / acc[HEAD64:HEAD64 + 1, :])
        o_ref[sub * tk:(sub + 1) * tk, :] = jnp.concatenate(halves, axis=0).T.astype(BF16)


def _fox(fast, qft, aqt, kf, ak, vt, batch, seq, tk):
    t = kf.shape[0]
    n_sub = min(FOX_SUB, seq // tk)
    tq = n_sub * tk
    assert seq % tq == 0, (seq, tq)
    nq = seq // tq
    pairs = FOX_HEADS // 2
    return pl.pallas_call(
        functools.partial(_fox_kernel, tk=tk, n_sub=n_sub, n_steps=nq),
        grid=(batch, pairs, nq),
        in_specs=[pl.BlockSpec(memory_space=pltpu.SMEM),
                  pl.BlockSpec((None, None, LANES, tq), lambda b, p, i: (b, p, 0, i)),
                  pl.BlockSpec((None, LANES, tq), lambda b, p, i: (b, 0, i)),
                  pl.BlockSpec((seq, LANES), lambda b, p, i: (b, p)),
                  pl.BlockSpec((seq, LANES), lambda b, p, i: (b, 0)),
                  pl.BlockSpec((None, None, seq // tk, LANES, tk),
                               lambda b, p, i: (b, p, 0, 0, 0))],
        out_specs=pl.BlockSpec((tq, LANES), lambda b, p, i: (b * nq + i, p)),
        out_shape=jax.ShapeDtypeStruct((t, FOX_W), BF16),
        scratch_shapes=[pltpu.VMEM((2, tk, tk), F32),
                        pltpu.VMEM((2, tk, tk), F32),
                        pltpu.VMEM((2 * n_sub, HEAD64 + BF16_ROWS, tk), F32),
                        pltpu.VMEM((2 * n_sub, 1, tk), F32)],
        compiler_params=_params(3),
        name="fox",
    )(fast, qft, aqt, kf, ak, vt)


def _post_kernel(x_ref, oa_ref, of_ref, om_ref, gate_ref, wa_ref, wf_ref, wm_ref, wo_ref,
                 g_ref, wu_ref, wd_ref, y_ref, *, chunk):
    ya = _dot(oa_ref[...], wa_ref[...])
    yf = _dot(of_ref[...], wf_ref[...])
    ym = _dot(om_ref[...], wm_ref[...])
    merged = (gate_ref[:, 0:D_MODEL].astype(F32) * ya
              + gate_ref[:, D_MODEL:2 * D_MODEL].astype(F32) * yf
              + gate_ref[:, 2 * D_MODEL:].astype(F32) * ym)
    x1 = x_ref[...] + _dot(merged.astype(BF16), wo_ref[...])
    h = _rms_rows(x1, g_ref[...]).astype(BF16)
    acc = x1
    for c in range(0, D_FF, chunk):
        u = jnp.maximum(_dot(h, wu_ref[:, c:c + chunk]), 0.0)
        acc = acc + _dot((u * u).astype(BF16), wd_ref[c:c + chunk, :])
    y_ref[...] = acc


def _post(x2, oa, of, om, gates, wa, wf, wm, wo, g_mlp, wu, wd, tm, chunk=1024):
    t = x2.shape[0]
    row = lambda w: pl.BlockSpec((tm, w), lambda i: (i, 0))
    resident = lambda shape: pl.BlockSpec(shape, lambda i: (0,) * len(shape),
                                          pipeline_mode=pl.Buffered(1))
    return pl.pallas_call(
        functools.partial(_post_kernel, chunk=chunk),
        grid=(t // tm,),
        in_specs=[row(D_MODEL), row(SWA_Q), row(FOX_W), row(MEM_W), row(GATE_W),
                  resident((SWA_Q, D_MODEL)), resident((FOX_W, D_MODEL)),
                  resident((MEM_W, D_MODEL)), resident((D_MODEL, D_MODEL)),
                  resident((1, D_MODEL)), resident((D_MODEL, D_FF)), resident((D_FF, D_MODEL))],
        out_specs=row(D_MODEL),
        out_shape=jax.ShapeDtypeStruct((t, D_MODEL), F32),
        compiler_params=_params(1),
        name="post",
    )(x2, oa, of, om, gates, wa, wf, wm, wo, g_mlp, wu, wd)


def _t5_bucket_table():
    max_exact = REL_BUCKETS // 2
    t_loc = np.arange(BLOCK)[:, None] + BLOCK
    s_loc = np.arange(2 * BLOCK)[None, :]
    dist = t_loc - s_loc
    d = np.maximum(dist, 0)
    df = np.maximum(d, 1).astype(np.float32)
    scaled = (np.log(df / np.float32(max_exact)) / np.float32(math.log(REL_MAX_DIST / max_exact))
              * np.float32(REL_BUCKETS - max_exact))
    large = np.minimum(max_exact + scaled.astype(np.int32), REL_BUCKETS - 1)
    bucket = np.where(d < max_exact, d, large)
    band = (dist >= 0) & (dist < WINDOW)
    return jnp.asarray(np.where(band, bucket, -1).astype(np.int32).T)


def _aug_placement(shift):
    place = [[0.0] * (2 * LANES) for _ in range(LANES)]
    ones = [0.0] * (2 * LANES)
    shift_lanes = []
    for h in range(FOX_HEADS):
        base = h * AUG_PER_HEAD
        for part in range(3):
            src = part * FOX_HEADS + h
            place[src][base + part] = 1.0
            place[src][LANES + base + 3 + part] = 1.0
            ones[base + 3 + part] = 1.0
            ones[LANES + base + part] = 1.0
        ones[base + 6] = 1.0
        shift_lanes.append(LANES + base + 6)
    consts = jnp.array(ones, F32).at[jnp.array(shift_lanes)].set(-shift)
    return jnp.array(place, BF16), jnp.broadcast_to(consts[None, :], (8, 2 * LANES))


def _tile_gain(g, reps, scale=1.0):
    return (jnp.tile(g.astype(F32), reps) * scale)[None, :]


def _pick_tile(n, target):
    t = min(n, target)
    while n % t:
        t //= 2
    return t


def kernel(x, mem, g_mix, w_in, b_gate, b_forget, qn_swa, kn_swa, sink_swa, rel_bias,
           qn_fox, kn_fox, g_mem, w_mem_kv, qn_mem, kn_mem, w_o_swa, w_o_fox, w_o_mem,
           w_out, g_mlp, w_mlp_up, w_mlp_down):
    batch, seq, _ = x.shape
    n_layers = w_in.shape[0]
    t = batch * seq
    tm = _pick_tile(seq, 512)
    tq = _pick_tile(seq, 1024)

    bucket = _t5_bucket_table()
    tri = (jnp.arange(tm)[:, None] >= jnp.arange(tm)[None, :]).astype(BF16)

    x2 = x.reshape(t, D_MODEL)
    mem2 = mem.reshape(batch * N_MEM, D_MODEL)
    for layer in range(n_layers):
        wt = w_in[layer].T
        o = 0
        parts = {}
        for name, width in (("qa", SWA_Q), ("ka", SWA_KV), ("va", SWA_KV), ("qf", FOX_W),
                            ("kf", FOX_W), ("vf", FOX_W), ("fl", FOX_HEADS), ("qm", MEM_W),
                            ("g", GATE_W)):
            parts[name] = wt[o:o + width]
            o += width

        def dup(m):
            return jnp.concatenate([m[:HEAD64], m[:HEAD64], m[HEAD64:], m[HEAD64:]], axis=0)

        w1 = jnp.concatenate(
            [parts["qa"], dup(parts["ka"]), parts["kf"], parts["qm"], parts["g"],
             jnp.pad(parts["fl"], ((0, LANES - FOX_HEADS), (0, 0)))], axis=0).astype(BF16)

        wmt = w_mem_kv[layer].T.astype(BF16)
        mk, mvt = _memkv(mem2, g_mem[layer][None, :], wmt[:MEM_W], wmt[MEM_W:],
                         _tile_gain(kn_mem[layer], MEM_HEADS))

        bound = (HEAD64 ** 0.5 * LOG2E * 1.02 * jnp.max(jnp.abs(qn_fox[layer]))
                 * jnp.max(jnp.abs(kn_fox[layer]))).astype(BF16).astype(F32)
        fast = bound <= FOX_FAST_MAX_SHIFT
        place, consts = _aug_placement(jnp.where(fast, bound, 0.0))
        bf_pad = jnp.pad(b_forget[layer].astype(F32), (0, LANES - FOX_HEADS))[None, :]

        wt = jnp.concatenate([parts["vf"], parts["va"], parts["qf"]], axis=0).astype(BF16)
        gqft = jnp.broadcast_to(
            _tile_gain(qn_fox[layer], FOX_HEADS, HEAD64 ** -0.5 * LOG2E).T, (FOX_W, LANES))
        qa, ka, vat, qft, kf, vt, qm, gates, fl = _proj(
            x2, g_mix[layer][None, :], w1, wt, b_gate[layer][None, :],
            _tile_gain(qn_swa[layer], SWA_HEADS, HEAD64 ** -0.5 * LOG2E),
            _tile_gain(kn_swa[layer], 2 * SWA_KV_HEADS),
            gqft,
            _tile_gain(kn_fox[layer], FOX_HEADS),
            _tile_gain(qn_mem[layer], MEM_HEADS, MEM_HEAD_DIM ** -0.5 * LOG2E), tm, seq)
        aqt, ak = _cum(fl, bf_pad, tri, place, consts, tm, seq)

        oa, om = _local(bucket, rel_bias.astype(F32) * LOG2E, sink_swa[layer].astype(F32) * LOG2E,
                        qa, ka, vat, qm, mk, mvt, batch, seq, tq)
        of = _fox(fast.astype(jnp.int32)[None], qft, aqt, kf, ak, vt, batch, seq, tm)

        x2 = _post(x2, oa, of, om, gates, w_o_swa[layer].astype(BF16),
                   w_o_fox[layer].astype(BF16), w_o_mem[layer].astype(BF16),
                   w_out[layer].astype(BF16), g_mlp[layer][None, :],
                   w_mlp_up[layer].astype(BF16), w_mlp_down[layer].astype(BF16), tm)
    return x2.reshape(batch, seq, D_MODEL)
```

```python
import functools
import math

import jax
import jax.numpy as jnp
import numpy as np
from jax import lax
from jax.experimental import pallas as pl
from jax.experimental.pallas import tpu as pltpu

F32 = jnp.float32
BF16 = jnp.bfloat16

D_MODEL = 1024
N_MEM = 256
SWA_HEADS = 8
SWA_KV_HEADS = 2
SWA_GROUP = SWA_HEADS // SWA_KV_HEADS
HEAD64 = 64
WINDOW = 128
FOX_HEADS = 8
MEM_HEADS = 4
MEM_HEAD_DIM = 128
D_FF = 4 * D_MODEL
REL_BUCKETS = 32
REL_MAX_DIST = 128
BLOCK = 128
EPS = 1e-6
NEG = -1e30
LOG2E = math.log2(math.e)

LANES = 128
MXU_DIM = 256
BF16_ROWS = 16
VMEM_LIMIT = 56 * 1024 * 1024

SWA_Q = SWA_HEADS * HEAD64
SWA_KV = SWA_KV_HEADS * HEAD64
FOX_W = FOX_HEADS * HEAD64
MEM_W = MEM_HEADS * MEM_HEAD_DIM
GATE_W = 3 * D_MODEL

C_QA = 0
C_KA = C_QA + SWA_Q
C_KF = C_KA + 2 * SWA_KV
C_QM = C_KF + FOX_W
C_G = C_QM + MEM_W
C_FL = C_G + GATE_W
C_END = C_FL + LANES

AUG_PER_HEAD = 7
FOX_FAST_MAX_SHIFT = 60.0
FOX_SUB = 4
LOCAL_LOOKAHEAD = 4


def _dot(a, b):
    return jnp.dot(a, b, preferred_element_type=F32)


def _dot_nt(a, b):
    return lax.dot_general(a, b, (((1,), (1,)), ((), ())), preferred_element_type=F32)


def _rms_rows(x, gain):
    ms = jnp.mean(x * x, axis=-1, keepdims=True)
    return x * lax.rsqrt(ms + EPS) * gain


def _head_rms(y, gain, head_dim):
    y2 = y * y
    lower = lax.broadcasted_iota(jnp.int32, (y.shape[0], LANES), 1) < HEAD64
    cols = []
    for c in range(0, y.shape[1], LANES):
        blk = y2[:, c:c + LANES]
        total = jnp.sum(blk, axis=-1, keepdims=True)
        if head_dim == LANES:
            cols.append(jnp.broadcast_to(total, blk.shape))
        else:
            low = jnp.sum(jnp.where(lower, blk, 0.0), axis=-1, keepdims=True)
            cols.append(jnp.where(lower, low, total - low))
    ss = jnp.concatenate(cols, axis=1)
    return y * lax.rsqrt(ss * (1.0 / head_dim) + EPS) * gain


def _const_spec(shape):
    return pl.BlockSpec(shape, lambda *_: (0,) * len(shape))


def _params(n_axes):
    return pltpu.CompilerParams(dimension_semantics=("arbitrary",) * n_axes,
                                vmem_limit_bytes=VMEM_LIMIT)


def _memkv_kernel(mem_ref, g_ref, wkt_ref, wvt_ref, kn_ref, mk_ref, mvt_ref):
    h = _rms_rows(mem_ref[...], g_ref[...]).astype(BF16)
    mk_ref[...] = _head_rms(_dot_nt(h, wkt_ref[...]), kn_ref[...], MEM_HEAD_DIM).astype(BF16)
    mvt_ref[...] = _dot_nt(wvt_ref[...], h).astype(BF16)


def _memkv(mem2, g_mem, wkt, wvt, kn_t):
    rows = mem2.shape[0]
    nb = rows // N_MEM
    return pl.pallas_call(
        _memkv_kernel,
        grid=(nb,),
        in_specs=[pl.BlockSpec((N_MEM, D_MODEL), lambda i: (i, 0)),
                  _const_spec((1, D_MODEL)),
                  _const_spec((MEM_W, D_MODEL)),
                  _const_spec((MEM_W, D_MODEL)),
                  _const_spec((1, MEM_W))],
        out_specs=[pl.BlockSpec((N_MEM, MEM_W), lambda i: (i, 0)),
                   pl.BlockSpec((None, MEM_W, N_MEM), lambda i: (i, 0, 0))],
        out_shape=[jax.ShapeDtypeStruct((rows, MEM_W), BF16),
                   jax.ShapeDtypeStruct((nb, MEM_W, N_MEM), BF16)],
        compiler_params=_params(1),
        name="memkv",
    )(mem2, g_mem, wkt, wvt, kn_t)


def _proj_kernel(x_ref, g_ref, w_ref, wt_ref, bg_ref,
                 gqa_ref, gka_ref, gqft_ref, gkf_ref, gqm_ref,
                 qa_ref, ka_ref, vat_ref, qft_ref, kf_ref, vt_ref, qm_ref, gate_ref, fl_ref):
    h = _rms_rows(x_ref[...], g_ref[...]).astype(BF16)
    tm = h.shape[0]
    qkv = _dot_nt(h, w_ref[:C_G, :])
    qa_ref[...] = _head_rms(qkv[:, C_QA:C_KA], gqa_ref[...], HEAD64).astype(BF16)
    ka_ref[...] = _head_rms(qkv[:, C_KA:C_KF], gka_ref[...], HEAD64).astype(BF16)
    kf_ref[...] = _head_rms(qkv[:, C_KF:C_QM], gkf_ref[...], HEAD64).astype(BF16)
    qm_ref[...] = _head_rms(qkv[:, C_QM:C_G], gqm_ref[...], MEM_HEAD_DIM).astype(BF16)
    t = _dot_nt(wt_ref[...], h)
    for p in range(FOX_HEADS // 2):
        vt_ref[p] = t[p * LANES:(p + 1) * LANES, :].astype(BF16)
    vat_ref[...] = t[FOX_W:FOX_W + SWA_KV, :].astype(BF16)
    gain_t = jnp.tile(gqft_ref[...], (1, tm // LANES))
    for p in range(FOX_HEADS // 2):
        rows = []
        for e in range(2):
            r0 = FOX_W + SWA_KV + (2 * p + e) * HEAD64
            y = t[r0:r0 + HEAD64, :]
            ms = jnp.sum(y * y, axis=0, keepdims=True) * (1.0 / HEAD64)
            rows.append(y * lax.rsqrt(ms + EPS) * gain_t[r0 - FOX_W - SWA_KV:r0 - FOX_W - SWA_KV
                                                         + HEAD64, :])
        qft_ref[p] = jnp.concatenate(rows, axis=0).astype(BF16)
    zf = _dot_nt(h, w_ref[C_G:C_END, :])
    z = zf[:, :GATE_W] + bg_ref[...]
    gate_ref[...] = (0.5 * jnp.tanh(0.5 * z) + 0.5).astype(BF16)
    fl_ref[...] = zf[:, GATE_W:]


def _proj(x2, g_mix, w1, wt, b_gate, gqa, gka, gqft, gkf, gqm, tm, seq):
    t = x2.shape[0]
    nt = seq // tm
    pairs = FOX_HEADS // 2

    def rows(w, dtype=BF16):
        return jax.ShapeDtypeStruct((t, w), dtype), pl.BlockSpec((tm, w), lambda i: (i, 0))

    outs = [rows(SWA_Q), rows(2 * SWA_KV),
            (jax.ShapeDtypeStruct((t // seq, SWA_KV, seq), BF16),
             pl.BlockSpec((None, SWA_KV, tm), lambda i: (i // nt, 0, i % nt))),
            (jax.ShapeDtypeStruct((t // seq, pairs, LANES, seq), BF16),
             pl.BlockSpec((None, pairs, LANES, tm), lambda i: (i // nt, 0, 0, i % nt))),
            rows(FOX_W),
            (jax.ShapeDtypeStruct((t // seq, pairs, nt, LANES, tm), BF16),
             pl.BlockSpec((None, pairs, None, LANES, tm), lambda i: (i // nt, 0, i % nt, 0, 0))),
            rows(MEM_W), rows(GATE_W), rows(LANES, F32)]
    out_shape = [o[0] for o in outs]
    out_specs = [o[1] for o in outs]
    return pl.pallas_call(
        _proj_kernel,
        grid=(t // tm,),
        in_specs=[pl.BlockSpec((tm, D_MODEL), lambda i: (i, 0)),
                  _const_spec((1, D_MODEL)),
                  _const_spec((C_END, D_MODEL)),
                  _const_spec((2 * FOX_W + SWA_KV, D_MODEL)),
                  _const_spec((1, GATE_W)),
                  _const_spec((1, SWA_Q)),
                  _const_spec((1, 2 * SWA_KV)),
                  _const_spec((FOX_W, LANES)),
                  _const_spec((1, FOX_W)),
                  _const_spec((1, MEM_W))],
        out_specs=out_specs,
        out_shape=out_shape,
        compiler_params=_params(1),
        name="proj",
    )(x2, g_mix, w1, wt, b_gate, gqa, gka, gqft, gkf, gqm)


def _split3(c):
    hi = c.astype(BF16)
    r1 = c - hi.astype(F32)
    mid = r1.astype(BF16)
    lo = (r1 - mid.astype(F32)).astype(BF16)
    return hi, mid, lo


def _pack3(x):
    hi, mid, lo = _split3(x)
    return (hi.astype(F32) + pltpu.roll(mid.astype(F32), FOX_HEADS, 1)
            + pltpu.roll(lo.astype(F32), 2 * FOX_HEADS, 1)).astype(BF16)


def _cum_kernel(fl_ref, bf_ref, tri_ref, place_ref, const_ref, aqt_ref, ak_ref, *, tm):
    n_tiles = fl_ref.shape[0] // tm
    lane = lax.broadcasted_iota(jnp.int32, (tm, LANES), 1)
    heads = lane < FOX_HEADS
    tri = tri_ref[...]
    parts = []
    for k in range(n_tiles):
        z = fl_ref[k * tm:(k + 1) * tm, :] + bf_ref[...]
        logf = jnp.where(heads, jnp.minimum(z, 0.0) - jnp.log(1.0 + jnp.exp(-jnp.abs(z))), 0.0)
        parts.append(_dot(tri, _pack3(logf)))
    carry = jnp.zeros((1, LANES), F32)
    packed = []
    for part in parts:
        c = ((part + pltpu.roll(part, LANES - FOX_HEADS, 1))
             + pltpu.roll(part, LANES - 2 * FOX_HEADS, 1)) + carry
        carry = c[tm - 1:tm, :]
        packed.append(_pack3(jnp.where(heads, c * LOG2E, 0.0)))
    for k in range(n_tiles):
        placed = _dot(packed[k], place_ref[...])
        rows = slice(k * tm, (k + 1) * tm)
        aqt_ref[:, rows] = (placed[:, :LANES] + const_ref[0:1, :LANES]).T.astype(BF16)
        ak_ref[rows, :] = (const_ref[0:1, LANES:] - placed[:, LANES:]).astype(BF16)


def _cum(fl, bf_pad, tri, place, consts, tm, seq):
    t = fl.shape[0]
    return pl.pallas_call(
        functools.partial(_cum_kernel, tm=tm),
        grid=(t // seq,),
        in_specs=[pl.BlockSpec((seq, LANES), lambda i: (i, 0)),
                  _const_spec((1, LANES)),
                  _const_spec((tm, tm)),
                  _const_spec((LANES, 2 * LANES)),
                  _const_spec((8, 2 * LANES))],
        out_specs=[pl.BlockSpec((None, LANES, seq), lambda i: (i, 0, 0)),
                   pl.BlockSpec((seq, LANES), lambda i: (i, 0))],
        out_shape=[jax.ShapeDtypeStruct((t // seq, LANES, seq), BF16),
                   jax.ShapeDtypeStruct((t, LANES), BF16)],
        compiler_params=_params(1),
        name="cum",
    )(fl, bf_pad, tri, place, consts)


def _local_kernel(bucket_ref, relb_ref, sink_ref,
                  qa_ref, kc_ref, kp_ref, vc_ref, vp_ref, qm_ref, mk_ref, mvt_ref,
                  oa_ref, om_ref, tbl_ref, *, tq):
    b = pl.program_id(0)
    i = pl.program_id(1)
    pairs = SWA_HEADS // 2

    @pl.when(jnp.logical_and(b == 0, i == 0))
    def _():
        bucket = bucket_ref[...]
        band = bucket >= 0
        for h in range(SWA_HEADS):
            t = jnp.zeros(bucket.shape, F32)
            for k in range(REL_BUCKETS):
                t = jnp.where(bucket == k, relb_ref[k, h], t)
            tbl_ref[h // 2, :, (h % 2) * BLOCK:(h % 2 + 1) * BLOCK] = jnp.where(band, t, NEG)

    lane = lax.broadcasted_iota(jnp.int32, (BLOCK, LANES), 1)
    lower = lane < HEAD64
    first_head = lax.broadcasted_iota(jnp.int32, (1, 2 * BLOCK), 1) < BLOCK
    key_row = lax.broadcasted_iota(jnp.int32, (2 * BLOCK, 2 * BLOCK), 0)
    pad_rows = jnp.logical_and(key_row < BLOCK, i == 0)
    ones = jnp.ones((BF16_ROWS, 2 * BLOCK), BF16)

    def windows(j):
        r0 = j * BLOCK
        if j == 0:
            return (jnp.concatenate([kp_ref[...], kc_ref[0:BLOCK, :]], axis=0),
                    jnp.concatenate([vp_ref[...], vc_ref[:, 0:BLOCK]], axis=1))
        return kc_ref[r0 - BLOCK:r0 + BLOCK, :], vc_ref[:, r0 - BLOCK:r0 + BLOCK]

    def swa_scores(j, hb):
        g = (2 * hb) // SWA_GROUP
        qblk = qa_ref[j * BLOCK:(j + 1) * BLOCK, hb * LANES:(hb + 1) * LANES]
        qstack = jnp.concatenate([jnp.where(lower, qblk, 0),
                                  jnp.where(lower, 0, qblk)], axis=0)
        s = _dot_nt(windows(j)[0][:, g * LANES:(g + 1) * LANES], qstack) + tbl_ref[hb]
        return jnp.where(pad_rows, NEG, s) if j == 0 else s

    def swa_finish(j, hb, s):
        g = (2 * hb) // SWA_GROUP
        sink = jnp.where(first_head, sink_ref[2 * hb], sink_ref[2 * hb + 1])
        m = jnp.maximum(jnp.max(s, axis=0, keepdims=True), sink)
        p = jnp.exp2(s - m).astype(BF16)
        vt_ones = jnp.concatenate([windows(j)[1][g * HEAD64:(g + 1) * HEAD64, :], ones], axis=0)
        acc = _dot(vt_ones, p)
        o = acc[:HEAD64, :] / (acc[HEAD64:HEAD64 + 1, :] + jnp.exp2(sink - m))
        o = jnp.concatenate([o[:, :BLOCK], o[:, BLOCK:]], axis=0)
        oa_ref[j * BLOCK:(j + 1) * BLOCK, hb * LANES:(hb + 1) * LANES] = o.T.astype(BF16)

    ones_m = jnp.ones((BF16_ROWS, N_MEM), BF16)

    def mem_scores(h):
        sl = slice(h * MEM_HEAD_DIM, (h + 1) * MEM_HEAD_DIM)
        return _dot_nt(mk_ref[:, sl], qm_ref[:, sl])

    def mem_finish(h, s):
        sl = slice(h * MEM_HEAD_DIM, (h + 1) * MEM_HEAD_DIM)
        m = jnp.max(s, axis=0, keepdims=True)
        p = jnp.exp2(s - m).astype(BF16)
        acc = _dot(jnp.concatenate([mvt_ref[sl, :], ones_m], axis=0), p)
        o = acc[:MEM_HEAD_DIM, :] / acc[MEM_HEAD_DIM:MEM_HEAD_DIM + 1, :]
        om_ref[:, sl] = o.T.astype(BF16)

    units = [(swa_scores, swa_finish, (j, hb)) for j in range(tq // BLOCK) for hb in range(pairs)]
    units += [(mem_scores, mem_finish, (h,)) for h in range(MEM_HEADS)]
    pending = []
    for n in range(len(units) + LOCAL_LOOKAHEAD):
        if n < len(units):
            score_fn, _, args = units[n]
            pending.append(score_fn(*args))
        if n >= LOCAL_LOOKAHEAD:
            _, finish_fn, args = units[n - LOCAL_LOOKAHEAD]
            finish_fn(*args, pending[n - LOCAL_LOOKAHEAD])


def _local(bucket_t, rel_bias, sinks, qa, ka, vat, qm, mk, mvt, batch, seq, tq):
    t = qa.shape[0]
    nq = seq // tq
    sub = tq // BLOCK

    def cur(b, i):
        return (b * nq + i, 0)

    def prev(b, i):
        return (jnp.maximum((b * nq + i) * sub - 1, 0), 0)

    smem = pl.BlockSpec(memory_space=pltpu.SMEM)
    return pl.pallas_call(
        functools.partial(_local_kernel, tq=tq),
        grid=(batch, nq),
        in_specs=[_const_spec((2 * BLOCK, BLOCK)), smem, smem,
                  pl.BlockSpec((tq, SWA_Q), cur),
                  pl.BlockSpec((tq, 2 * SWA_KV), cur),
                  pl.BlockSpec((BLOCK, 2 * SWA_KV), prev),
                  pl.BlockSpec((None, SWA_KV, tq), lambda b, i: (b, 0, i)),
                  pl.BlockSpec((None, SWA_KV, BLOCK),
                               lambda b, i: (b, 0, jnp.maximum(i * sub - 1, 0))),
                  pl.BlockSpec((tq, MEM_W), cur),
                  pl.BlockSpec((N_MEM, MEM_W), lambda b, i: (b, 0)),
                  pl.BlockSpec((None, MEM_W, N_MEM), lambda b, i: (b, 0, 0))],
        out_specs=[pl.BlockSpec((tq, SWA_Q), cur),
                   pl.BlockSpec((tq, MEM_W), cur)],
        out_shape=[jax.ShapeDtypeStruct((t, SWA_Q), BF16),
                   jax.ShapeDtypeStruct((t, MEM_W), BF16)],
        scratch_shapes=[pltpu.VMEM((SWA_HEADS // 2, 2 * BLOCK, 2 * BLOCK), F32)],
        compiler_params=_params(2),
        name="local",
    )(bucket_t, rel_bias, sinks, qa, ka, ka, vat, vat, qm, mk, mvt)


def _fox_kernel(fast_ref, qt_ref, aqt_ref, k_ref, ak_ref, vt_ref, o_ref,
                sa_ref, sb_ref, acc_ref, m_ref, *, tk, n_sub, n_steps):
    pair = pl.program_id(1)
    g = pl.program_id(2)
    row = lax.broadcasted_iota(jnp.int32, (LANES, tk), 0)
    half = [row < HEAD64, row >= HEAD64]
    qcat = []
    for sub in range(n_sub):
        q = qt_ref[:, sub * tk:(sub + 1) * tk]
        aq = aqt_ref[:, sub * tk:(sub + 1) * tk]
        zero = jnp.zeros_like(q)
        per_head = []
        for h in range(2):
            lo = (2 * pair + h) * AUG_PER_HEAD
            mine = jnp.logical_and(row >= lo, row < lo + AUG_PER_HEAD)
            per_head.append(jnp.concatenate([jnp.where(half[h], q, zero),
                                             jnp.where(mine, aq, zero)], axis=0))
        qcat.append(per_head)
    ones = jnp.ones((BF16_ROWS, tk), BF16)
    acc_ref[...] = jnp.zeros_like(acc_ref)
    key = lax.broadcasted_iota(jnp.int32, (tk, tk), 0)
    qry = lax.broadcasted_iota(jnp.int32, (tk, tk), 1)
    causal = key <= qry

    def kcat(j):
        r0 = pl.multiple_of(j * tk, tk)
        return jnp.concatenate([k_ref[pl.ds(r0, tk), :], ak_ref[pl.ds(r0, tk), :]], axis=1)

    def vt_ones(j, h):
        return jnp.concatenate([vt_ref[j, h * HEAD64:(h + 1) * HEAD64, :], ones], axis=0)

    def visible_keys(c, diagonal):
        return min(tk, c + MXU_DIM) if diagonal else tk

    def qk(sub, j, s_ref, diagonal=False):
        kc = kcat(j)
        for h in range(2):
            for c in range(0, tk, MXU_DIM):
                nk = visible_keys(c, diagonal)
                s_ref[h, :nk, c:c + MXU_DIM] = _dot(kc[:nk], qcat[sub][h][:, c:c + MXU_DIM])

    @pl.when(fast_ref[0] == 1)
    def _():
        bufs = (sa_ref, sb_ref)

        def exp_pv(sub, j, s_ref, masked):
            for h in range(2):
                for c in range(0, tk, MXU_DIM):
                    nk = visible_keys(c, masked)
                    s = s_ref[h, :nk, c:c + MXU_DIM]
                    if masked:
                        s = jnp.where(causal[:nk, c:c + MXU_DIM], s, NEG)
                    acc_ref[2 * sub + h, :, c:c + MXU_DIM] += _dot(vt_ones(j, h)[:, :nk],
                                                                   jnp.exp2(s).astype(BF16))

        full = [(sub, jj) for jj in range(n_sub) for sub in range(n_sub)]
        diag = [(sub, jj) for jj in range(n_sub) for sub in range(jj, n_sub)]
        assert len(full) % 2 == 0
        qk(0, 0, bufs[0])

        def body(t, carry):
            for n, (sub, jj) in enumerate(full):
                nsub, njj = full[n + 1] if n + 1 < len(full) else (0, n_sub)
                qk(nsub, n_sub * t + njj, bufs[(n + 1) % 2])
                exp_pv(sub, n_sub * t + jj, bufs[n % 2], False)
            return carry

        if n_steps > 1:
            lax.fori_loop(0, g, body, 0)
        for n, (sub, jj) in enumerate(diag):
            if n + 1 < len(diag):
                nsub, njj = diag[n + 1]
                qk(nsub, n_sub * g + njj, bufs[(n + 1) % 2], diagonal=nsub == njj)
            exp_pv(sub, n_sub * g + jj, bufs[n % 2], sub == jj)

    @pl.when(fast_ref[0] == 0)
    def _():
        m_ref[...] = jnp.full(m_ref.shape, NEG, F32)
        for sub in range(n_sub):
            last = n_sub * g + sub

            def body(j, carry, sub=sub, last=last):
                qk(sub, j, sa_ref)
                for h in range(2):
                    u = 2 * sub + h
                    s = jnp.where(jnp.logical_or(causal, j < last), sa_ref[h], NEG)
                    m_prev = m_ref[u]
                    m_next = jnp.maximum(m_prev, jnp.max(s, axis=0, keepdims=True))
                    alpha = jnp.exp2(m_prev - m_next)
                    p = jnp.exp2(s - m_next).astype(BF16)
                    acc_ref[u] = alpha * acc_ref[u] + _dot(vt_ones(j, h), p)
                    m_ref[u] = m_next
                return carry

            lax.fori_loop(0, last + 1, body, 0)

    for sub in range(n_sub):
        halves = []
        for h in range(2):
            acc = acc_ref[2 * sub + h]
            halves.append(acc[:HEAD64, :] / acc[HEAD64:HEAD64 + 1, :])
        o_ref[sub * tk:(sub + 1) * tk, :] = jnp.concatenate(halves, axis=0).T.astype(BF16)


def _fox(fast, qft, aqt, kf, ak, vt, batch, seq, tk):
    t = kf.shape[0]
    n_sub = min(FOX_SUB, seq // tk)
    tq = n_sub * tk
    assert seq % tq == 0, (seq, tq)
    nq = seq // tq
    pairs = FOX_HEADS // 2
    return pl.pallas_call(
        functools.partial(_fox_kernel, tk=tk, n_sub=n_sub, n_steps=nq),
        grid=(batch, pairs, nq),
        in_specs=[pl.BlockSpec(memory_space=pltpu.SMEM),
                  pl.BlockSpec((None, None, LANES, tq), lambda b, p, i: (b, p, 0, i)),
                  pl.BlockSpec((None, LANES, tq), lambda b, p, i: (b, 0, i)),
                  pl.BlockSpec((seq, LANES), lambda b, p, i: (b, p)),
                  pl.BlockSpec((seq, LANES), lambda b, p, i: (b, 0)),
                  pl.BlockSpec((None, None, seq // tk, LANES, tk),
                               lambda b, p, i: (b, p, 0, 0, 0))],
        out_specs=pl.BlockSpec((tq, LANES), lambda b, p, i: (b * nq + i, p)),
        out_shape=jax.ShapeDtypeStruct((t, FOX_W), BF16),
        scratch_shapes=[pltpu.VMEM((2, tk, tk), F32),
                        pltpu.VMEM((2, tk, tk), F32),
                        pltpu.VMEM((2 * n_sub, HEAD64 + BF16_ROWS, tk), F32),
                        pltpu.VMEM((2 * n_sub, 1, tk), F32)],
        compiler_params=_params(3),
        name="fox",
    )(fast, qft, aqt, kf, ak, vt)


def _post_kernel(x_ref, oa_ref, of_ref, om_ref, gate_ref, wa_ref, wf_ref, wm_ref, wo_ref,
                 g_ref, wu_ref, wd_ref, y_ref, *, chunk):
    ya = _dot(oa_ref[...], wa_ref[...])
    yf = _dot(of_ref[...], wf_ref[...])
    ym = _dot(om_ref[...], wm_ref[...])
    merged = (gate_ref[:, 0:D_MODEL].astype(F32) * ya
              + gate_ref[:, D_MODEL:2 * D_MODEL].astype(F32) * yf
              + gate_ref[:, 2 * D_MODEL:].astype(F32) * ym)
    x1 = x_ref[...] + _dot(merged.astype(BF16), wo_ref[...])
    h = _rms_rows(x1, g_ref[...]).astype(BF16)
    acc = x1
    for c in range(0, D_FF, chunk):
        u = jnp.maximum(_dot(h, wu_ref[:, c:c + chunk]), 0.0)
        acc = acc + _dot((u * u).astype(BF16), wd_ref[c:c + chunk, :])
    y_ref[...] = acc


def _post(x2, oa, of, om, gates, wa, wf, wm, wo, g_mlp, wu, wd, tm, chunk=1024):
    t = x2.shape[0]
    row = lambda w: pl.BlockSpec((tm, w), lambda i: (i, 0))
    resident = lambda shape: pl.BlockSpec(shape, lambda i: (0,) * len(shape),
                                          pipeline_mode=pl.Buffered(1))
    return pl.pallas_call(
        functools.partial(_post_kernel, chunk=chunk),
        grid=(t // tm,),
        in_specs=[row(D_MODEL), row(SWA_Q), row(FOX_W), row(MEM_W), row(GATE_W),
                  resident((SWA_Q, D_MODEL)), resident((FOX_W, D_MODEL)),
                  resident((MEM_W, D_MODEL)), resident((D_MODEL, D_MODEL)),
                  resident((1, D_MODEL)), resident((D_MODEL, D_FF)), resident((D_FF, D_MODEL))],
        out_specs=row(D_MODEL),
        out_shape=jax.ShapeDtypeStruct((t, D_MODEL), F32),
        compiler_params=_params(1),
        name="post",
    )(x2, oa, of, om, gates, wa, wf, wm, wo, g_mlp, wu, wd)


def _t5_bucket_table():
    max_exact = REL_BUCKETS // 2
    t_loc = np.arange(BLOCK)[:, None] + BLOCK
    s_loc = np.arange(2 * BLOCK)[None, :]
    dist = t_loc - s_loc
    d = np.maximum(dist, 0)
    df = np.maximum(d, 1).astype(np.float32)
    scaled = (np.log(df / np.float32(max_exact)) / np.float32(math.log(REL_MAX_DIST / max_exact))
              * np.float32(REL_BUCKETS - max_exact))
    large = np.minimum(max_exact + scaled.astype(np.int32), REL_BUCKETS - 1)
    bucket = np.where(d < max_exact, d, large)
    band = (dist >= 0) & (dist < WINDOW)
    return jnp.asarray(np.where(band, bucket, -1).astype(np.int32).T)


def _aug_placement(shift):
    place = [[0.0] * (2 * LANES) for _ in range(LANES)]
    ones = [0.0] * (2 * LANES)
    shift_lanes = []
    for h in range(FOX_HEADS):
        base = h * AUG_PER_HEAD
        for part in range(3):
            src = part * FOX_HEADS + h
            place[src][base + part] = 1.0
            place[src][LANES + base + 3 + part] = 1.0
            ones[base + 3 + part] = 1.0
            ones[LANES + base + part] = 1.0
        ones[base + 6] = 1.0
        shift_lanes.append(LANES + base + 6)
    consts = jnp.array(ones, F32).at[jnp.array(shift_lanes)].set(-shift)
    return jnp.array(place, BF16), jnp.broadcast_to(consts[None, :], (8, 2 * LANES))


def _tile_gain(g, reps, scale=1.0):
    return (jnp.tile(g.astype(F32), reps) * scale)[None, :]


def _pick_tile(n, target):
    t = min(n, target)
    while n % t:
        t //= 2
    return t


def kernel(x, mem, g_mix, w_in, b_gate, b_forget, qn_swa, kn_swa, sink_swa, rel_bias,
           qn_fox, kn_fox, g_mem, w_mem_kv, qn_mem, kn_mem, w_o_swa, w_o_fox, w_o_mem,
           w_out, g_mlp, w_mlp_up, w_mlp_down):
    batch, seq, _ = x.shape
    n_layers = w_in.shape[0]
    t = batch * seq
    tm = _pick_tile(seq, 512)
    tq = _pick_tile(seq, 1024)

    bucket = _t5_bucket_table()
    tri = (jnp.arange(tm)[:, None] >= jnp.arange(tm)[None, :]).astype(BF16)

    x2 = x.reshape(t, D_MODEL)
    mem2 = mem.reshape(batch * N_MEM, D_MODEL)
    for layer in range(n_layers):
        wt = w_in[layer].T
        o = 0
        parts = {}
        for name, width in (("qa", SWA_Q), ("ka", SWA_KV), ("va", SWA_KV), ("qf", FOX_W),
                            ("kf", FOX_W), ("vf", FOX_W), ("fl", FOX_HEADS), ("qm", MEM_W),
                            ("g", GATE_W)):
            parts[name] = wt[o:o + width]
            o += width

        def dup(m):
            return jnp.concatenate([m[:HEAD64], m[:HEAD64], m[HEAD64:], m[HEAD64:]], axis=0)

        w1 = jnp.concatenate(
            [parts["qa"], dup(parts["ka"]), parts["kf"], parts["qm"], parts["g"],
             jnp.pad(parts["fl"], ((0, LANES - FOX_HEADS), (0, 0)))], axis=0).astype(BF16)

        wmt = w_mem_kv[layer].T.astype(BF16)
        mk, mvt = _memkv(mem2, g_mem[layer][None, :], wmt[:MEM_W], wmt[MEM_W:],
                         _tile_gain(kn_mem[layer], MEM_HEADS))

        bound = (HEAD64 ** 0.5 * LOG2E * 1.02 * jnp.max(jnp.abs(qn_fox[layer]))
                 * jnp.max(jnp.abs(kn_fox[layer]))).astype(BF16).astype(F32)
        fast = bound <= FOX_FAST_MAX_SHIFT
        place, consts = _aug_placement(jnp.where(fast, bound, 0.0))
        bf_pad = jnp.pad(b_forget[layer].astype(F32), (0, LANES - FOX_HEADS))[None, :]

        wt = jnp.concatenate([parts["vf"], parts["va"], parts["qf"]], axis=0).astype(BF16)
        gqft = jnp.broadcast_to(
            _tile_gain(qn_fox[layer], FOX_HEADS, HEAD64 ** -0.5 * LOG2E).T, (FOX_W, LANES))
        qa, ka, vat, qft, kf, vt, qm, gates, fl = _proj(
            x2, g_mix[layer][None, :], w1, wt, b_gate[layer][None, :],
            _tile_gain(qn_swa[layer], SWA_HEADS, HEAD64 ** -0.5 * LOG2E),
            _tile_gain(kn_swa[layer], 2 * SWA_KV_HEADS),
            gqft,
            _tile_gain(kn_fox[layer], FOX_HEADS),
            _tile_gain(qn_mem[layer], MEM_HEADS, MEM_HEAD_DIM ** -0.5 * LOG2E), tm, seq)
        aqt, ak = _cum(fl, bf_pad, tri, place, consts, tm, seq)

        oa, om = _local(bucket, rel_bias.astype(F32) * LOG2E, sink_swa[layer].astype(F32) * LOG2E,
                        qa, ka, vat, qm, mk, mvt, batch, seq, tq)
        of = _fox(fast.astype(jnp.int32)[None], qft, aqt, kf, ak, vt, batch, seq, tm)

        x2 = _post(x2, oa, of, om, gates, w_o_swa[layer].astype(BF16),
                   w_o_fox[layer].astype(BF16), w_o_mem[layer].astype(BF16),
                   w_out[layer].astype(BF16), g_mlp[layer][None, :],
                   w_mlp_up[layer].astype(BF16), w_mlp_down[layer].astype(BF16), tm)
    return x2.reshape(batch, seq, D_MODEL)
```

```python
import functools
import math

import jax
import jax.numpy as jnp
import numpy as np
from jax import lax
from jax.experimental import pallas as pl
from jax.experimental.pallas import tpu as pltpu

F32 = jnp.float32
BF16 = jnp.bfloat16

D_MODEL = 1024
N_MEM = 256
SWA_HEADS = 8
SWA_KV_HEADS = 2
SWA_GROUP = SWA_HEADS // SWA_KV_HEADS
HEAD64 = 64
WINDOW = 128
FOX_HEADS = 8
MEM_HEADS = 4
MEM_HEAD_DIM = 128
D_FF = 4 * D_MODEL
REL_BUCKETS = 32
REL_MAX_DIST = 128
BLOCK = 128
EPS = 1e-6
NEG = -1e30
LOG2E = math.log2(math.e)

LANES = 128
MXU_DIM = 256
BF16_ROWS = 16
VMEM_LIMIT = 56 * 1024 * 1024

SWA_Q = SWA_HEADS * HEAD64
SWA_KV = SWA_KV_HEADS * HEAD64
FOX_W = FOX_HEADS * HEAD64
MEM_W = MEM_HEADS * MEM_HEAD_DIM
GATE_W = 3 * D_MODEL

C_KA = 0
C_KF = C_KA + 2 * SWA_KV
C_G = C_KF + FOX_W
C_FL = C_G + GATE_W
C_END = C_FL + LANES
T_VF = 0
T_VA = T_VF + FOX_W
T_QF = T_VA + SWA_KV
T_QA = T_QF + FOX_W
T_QM = T_QA + SWA_Q
T_END = T_QM + MEM_W

AUG_PER_HEAD = 7
FOX_FAST_MAX_SHIFT = 60.0
FOX_SUB = 4
LOCAL_LOOKAHEAD = 4


def _dot(a, b):
    return jnp.dot(a, b, preferred_element_type=F32)


def _dot_nt(a, b):
    return lax.dot_general(a, b, (((1,), (1,)), ((), ())), preferred_element_type=F32)


def _rms_rows(x, gain):
    ms = jnp.mean(x * x, axis=-1, keepdims=True)
    return x * lax.rsqrt(ms + EPS) * gain


def _head_rms(y, gain, head_dim):
    y2 = y * y
    lower = lax.broadcasted_iota(jnp.int32, (y.shape[0], LANES), 1) < HEAD64
    cols = []
    for c in range(0, y.shape[1], LANES):
        blk = y2[:, c:c + LANES]
        total = jnp.sum(blk, axis=-1, keepdims=True)
        if head_dim == LANES:
            cols.append(jnp.broadcast_to(total, blk.shape))
        else:
            low = jnp.sum(jnp.where(lower, blk, 0.0), axis=-1, keepdims=True)
            cols.append(jnp.where(lower, low, total - low))
    ss = jnp.concatenate(cols, axis=1)
    return y * lax.rsqrt(ss * (1.0 / head_dim) + EPS) * gain


def _const_spec(shape):
    return pl.BlockSpec(shape, lambda *_: (0,) * len(shape))


def _params(n_axes):
    return pltpu.CompilerParams(dimension_semantics=("arbitrary",) * n_axes,
                                vmem_limit_bytes=VMEM_LIMIT)


def _memkv_kernel(mem_ref, g_ref, wkt_ref, wvt_ref, kn_ref, mk_ref, mvt_ref):
    h = _rms_rows(mem_ref[...], g_ref[...]).astype(BF16)
    mk_ref[...] = _head_rms(_dot_nt(h, wkt_ref[...]), kn_ref[...], MEM_HEAD_DIM).astype(BF16)
    mvt_ref[...] = _dot_nt(wvt_ref[...], h).astype(BF16)


def _memkv(mem2, g_mem, wkt, wvt, kn_t):
    rows = mem2.shape[0]
    nb = rows // N_MEM
    return pl.pallas_call(
        _memkv_kernel,
        grid=(nb,),
        in_specs=[pl.BlockSpec((N_MEM, D_MODEL), lambda i: (i, 0)),
                  _const_spec((1, D_MODEL)),
                  _const_spec((MEM_W, D_MODEL)),
                  _const_spec((MEM_W, D_MODEL)),
                  _const_spec((1, MEM_W))],
        out_specs=[pl.BlockSpec((N_MEM, MEM_W), lambda i: (i, 0)),
                   pl.BlockSpec((None, MEM_W, N_MEM), lambda i: (i, 0, 0))],
        out_shape=[jax.ShapeDtypeStruct((rows, MEM_W), BF16),
                   jax.ShapeDtypeStruct((nb, MEM_W, N_MEM), BF16)],
        compiler_params=_params(1),
        name="memkv",
    )(mem2, g_mem, wkt, wvt, kn_t)


def _head_rms_t(y, gain, head_dim):
    out = []
    for r in range(0, y.shape[0], head_dim):
        blk = y[r:r + head_dim, :]
        ms = jnp.sum(blk * blk, axis=0, keepdims=True) * (1.0 / head_dim)
        out.append(blk * lax.rsqrt(ms + EPS) * gain[r:r + head_dim, :])
    return jnp.concatenate(out, axis=0)


def _proj_kernel(x_ref, g_ref, w_ref, wt_ref, bg_ref, gka_ref, gkf_ref, gqt_ref,
                 ka_ref, kf_ref, gate_ref, fl_ref, vt_ref, vat_ref, qft_ref, qat_ref, qmt_ref):
    h = _rms_rows(x_ref[...], g_ref[...]).astype(BF16)
    tm = h.shape[0]
    kk = _dot_nt(h, w_ref[:C_G, :])
    ka_ref[...] = _head_rms(kk[:, C_KA:C_KF], gka_ref[...], HEAD64).astype(BF16)
    kf_ref[...] = _head_rms(kk[:, C_KF:C_G], gkf_ref[...], HEAD64).astype(BF16)
    t = _dot_nt(wt_ref[...], h)
    vat_ref[...] = t[T_VA:T_QF, :].astype(BF16)
    gain_t = jnp.tile(gqt_ref[...], (1, tm // LANES))
    qf = _head_rms_t(t[T_QF:T_QA, :], gain_t[:FOX_W, :], HEAD64).astype(BF16)
    qa = _head_rms_t(t[T_QA:T_QM, :], gain_t[FOX_W:FOX_W + SWA_Q, :], HEAD64).astype(BF16)
    qm = _head_rms_t(t[T_QM:T_END, :], gain_t[FOX_W + SWA_Q:, :], MEM_HEAD_DIM).astype(BF16)
    for p in range(FOX_HEADS // 2):
        rows = slice(p * LANES, (p + 1) * LANES)
        vt_ref[p] = t[rows, :].astype(BF16)
        qft_ref[p] = qf[rows, :]
        qat_ref[p] = qa[rows, :]
        qmt_ref[p] = qm[rows, :]
    zf = _dot_nt(h, w_ref[C_G:C_END, :])
    z = zf[:, :GATE_W] + bg_ref[...]
    gate_ref[...] = (0.5 * jnp.tanh(0.5 * z) + 0.5).astype(BF16)
    fl_ref[...] = zf[:, GATE_W:]


def _proj(x2, g_mix, w1, wt, b_gate, gka, gkf, gqt, tm, seq):
    t = x2.shape[0]
    nt = seq // tm
    slabs = FOX_HEADS // 2

    def rows(w, dtype=BF16):
        return jax.ShapeDtypeStruct((t, w), dtype), pl.BlockSpec((tm, w), lambda i: (i, 0))

    def slabs_t():
        return (jax.ShapeDtypeStruct((t // seq, slabs, LANES, seq), BF16),
                pl.BlockSpec((None, slabs, LANES, tm), lambda i: (i // nt, 0, 0, i % nt)))

    outs = [rows(2 * SWA_KV), rows(FOX_W), rows(GATE_W), rows(LANES, F32),
            (jax.ShapeDtypeStruct((t // seq, slabs, nt, LANES, tm), BF16),
             pl.BlockSpec((None, slabs, None, LANES, tm), lambda i: (i // nt, 0, i % nt, 0, 0))),
            (jax.ShapeDtypeStruct((t // seq, SWA_KV, seq), BF16),
             pl.BlockSpec((None, SWA_KV, tm), lambda i: (i // nt, 0, i % nt))),
            slabs_t(), slabs_t(), slabs_t()]
    out_shape = [o[0] for o in outs]
    out_specs = [o[1] for o in outs]
    return pl.pallas_call(
        _proj_kernel,
        grid=(t // tm,),
        in_specs=[pl.BlockSpec((tm, D_MODEL), lambda i: (i, 0)),
                  _const_spec((1, D_MODEL)),
                  _const_spec((C_END, D_MODEL)),
                  _const_spec((T_END, D_MODEL)),
                  _const_spec((1, GATE_W)),
                  _const_spec((1, 2 * SWA_KV)),
                  _const_spec((1, FOX_W)),
                  _const_spec((T_END - T_QF, LANES))],
        out_specs=out_specs,
        out_shape=out_shape,
        compiler_params=_params(1),
        name="proj",
    )(x2, g_mix, w1, wt, b_gate, gka, gkf, gqt)


def _split3(c):
    hi = c.astype(BF16)
    r1 = c - hi.astype(F32)
    mid = r1.astype(BF16)
    lo = (r1 - mid.astype(F32)).astype(BF16)
    return hi, mid, lo


def _pack3(x):
    hi, mid, lo = _split3(x)
    return (hi.astype(F32) + pltpu.roll(mid.astype(F32), FOX_HEADS, 1)
            + pltpu.roll(lo.astype(F32), 2 * FOX_HEADS, 1)).astype(BF16)


def _cum_kernel(fl_ref, bf_ref, tri_ref, place_ref, const_ref, aqt_ref, ak_ref, *, tm):
    n_tiles = fl_ref.shape[0] // tm
    lane = lax.broadcasted_iota(jnp.int32, (tm, LANES), 1)
    heads = lane < FOX_HEADS
    tri = tri_ref[...]
    parts = []
    for k in range(n_tiles):
        z = fl_ref[k * tm:(k + 1) * tm, :] + bf_ref[...]
        logf = jnp.where(heads, jnp.minimum(z, 0.0) - jnp.log(1.0 + jnp.exp(-jnp.abs(z))), 0.0)
        parts.append(_dot(tri, _pack3(logf)))
    carry = jnp.zeros((1, LANES), F32)
    packed = []
    for part in parts:
        c = ((part + pltpu.roll(part, LANES - FOX_HEADS, 1))
             + pltpu.roll(part, LANES - 2 * FOX_HEADS, 1)) + carry
        carry = c[tm - 1:tm, :]
        packed.append(_pack3(jnp.where(heads, c * LOG2E, 0.0)))
    for k in range(n_tiles):
        placed = _dot(packed[k], place_ref[...])
        rows = slice(k * tm, (k + 1) * tm)
        aqt_ref[:, rows] = (placed[:, :LANES] + const_ref[0:1, :LANES]).T.astype(BF16)
        ak_ref[rows, :] = (const_ref[0:1, LANES:] - placed[:, LANES:]).astype(BF16)


def _cum(fl, bf_pad, tri, place, consts, tm, seq):
    t = fl.shape[0]
    return pl.pallas_call(
        functools.partial(_cum_kernel, tm=tm),
        grid=(t // seq,),
        in_specs=[pl.BlockSpec((seq, LANES), lambda i: (i, 0)),
                  _const_spec((1, LANES)),
                  _const_spec((tm, tm)),
                  _const_spec((LANES, 2 * LANES)),
                  _const_spec((8, 2 * LANES))],
        out_specs=[pl.BlockSpec((None, LANES, seq), lambda i: (i, 0, 0)),
                   pl.BlockSpec((seq, LANES), lambda i: (i, 0))],
        out_shape=[jax.ShapeDtypeStruct((t // seq, LANES, seq), BF16),
                   jax.ShapeDtypeStruct((t, LANES), BF16)],
        compiler_params=_params(1),
        name="cum",
    )(fl, bf_pad, tri, place, consts)


def _local_kernel(bucket_ref, relb_ref, sink_ref,
                  qat_ref, kc_ref, kp_ref, vc_ref, vp_ref, qmt_ref, mk_ref, mvt_ref,
                  oa_ref, om_ref, tbl_ref, *, tq):
    b = pl.program_id(0)
    i = pl.program_id(1)
    pairs = SWA_HEADS // 2

    @pl.when(jnp.logical_and(b == 0, i == 0))
    def _():
        bucket = bucket_ref[...]
        band = bucket >= 0
        for h in range(SWA_HEADS):
            t = jnp.zeros(bucket.shape, F32)
            for k in range(REL_BUCKETS):
                t = jnp.where(bucket == k, relb_ref[k, h], t)
            tbl_ref[h // 2, :, (h % 2) * BLOCK:(h % 2 + 1) * BLOCK] = jnp.where(band, t, NEG)

    upper_rows = lax.broadcasted_iota(jnp.int32, (LANES, BLOCK), 0) >= HEAD64
    first_head = lax.broadcasted_iota(jnp.int32, (1, 2 * BLOCK), 1) < BLOCK
    key_row = lax.broadcasted_iota(jnp.int32, (2 * BLOCK, 2 * BLOCK), 0)
    pad_rows = jnp.logical_and(key_row < BLOCK, i == 0)
    ones = jnp.ones((BF16_ROWS, 2 * BLOCK), BF16)

    def windows(j):
        r0 = j * BLOCK
        if j == 0:
            return (jnp.concatenate([kp_ref[...], kc_ref[0:BLOCK, :]], axis=0),
                    jnp.concatenate([vp_ref[...], vc_ref[:, 0:BLOCK]], axis=1))
        return kc_ref[r0 - BLOCK:r0 + BLOCK, :], vc_ref[:, r0 - BLOCK:r0 + BLOCK]

    def swa_scores(j, hb):
        g = (2 * hb) // SWA_GROUP
        qblk = qat_ref[hb, :, j * BLOCK:(j + 1) * BLOCK]
        qstack = jnp.concatenate([jnp.where(upper_rows, 0, qblk),
                                  jnp.where(upper_rows, qblk, 0)], axis=1)
        s = _dot(windows(j)[0][:, g * LANES:(g + 1) * LANES], qstack) + tbl_ref[hb]
        return jnp.where(pad_rows, NEG, s) if j == 0 else s

    def swa_finish(j, hb, s):
        g = (2 * hb) // SWA_GROUP
        sink = jnp.where(first_head, sink_ref[2 * hb], sink_ref[2 * hb + 1])
        m = jnp.maximum(jnp.max(s, axis=0, keepdims=True), sink)
        p = jnp.exp2(s - m).astype(BF16)
        vt_ones = jnp.concatenate([windows(j)[1][g * HEAD64:(g + 1) * HEAD64, :], ones], axis=0)
        acc = _dot(vt_ones, p)
        o = acc[:HEAD64, :] / (acc[HEAD64:HEAD64 + 1, :] + jnp.exp2(sink - m))
        o = jnp.concatenate([o[:, :BLOCK], o[:, BLOCK:]], axis=0)
        oa_ref[j * BLOCK:(j + 1) * BLOCK, hb * LANES:(hb + 1) * LANES] = o.T.astype(BF16)

    ones_m = jnp.ones((BF16_ROWS, N_MEM), BF16)

    def mem_scores(h):
        sl = slice(h * MEM_HEAD_DIM, (h + 1) * MEM_HEAD_DIM)
        return _dot(mk_ref[:, sl], qmt_ref[h])

    def mem_finish(h, s):
        sl = slice(h * MEM_HEAD_DIM, (h + 1) * MEM_HEAD_DIM)
        m = jnp.max(s, axis=0, keepdims=True)
        p = jnp.exp2(s - m).astype(BF16)
        acc = _dot(jnp.concatenate([mvt_ref[sl, :], ones_m], axis=0), p)
        o = acc[:MEM_HEAD_DIM, :] / acc[MEM_HEAD_DIM:MEM_HEAD_DIM + 1, :]
        om_ref[:, sl] = o.T.astype(BF16)

    units = [(swa_scores, swa_finish, (j, hb)) for j in range(tq // BLOCK) for hb in range(pairs)]
    units += [(mem_scores, mem_finish, (h,)) for h in range(MEM_HEADS)]
    pending = []
    for n in range(len(units) + LOCAL_LOOKAHEAD):
        if n < len(units):
            score_fn, _, args = units[n]
            pending.append(score_fn(*args))
        if n >= LOCAL_LOOKAHEAD:
            _, finish_fn, args = units[n - LOCAL_LOOKAHEAD]
            finish_fn(*args, pending[n - LOCAL_LOOKAHEAD])


def _local(bucket_t, rel_bias, sinks, qat, ka, vat, qmt, mk, mvt, batch, seq, tq):
    t = ka.shape[0]
    nq = seq // tq
    sub = tq // BLOCK

    def cur(b, i):
        return (b * nq + i, 0)

    def prev(b, i):
        return (jnp.maximum((b * nq + i) * sub - 1, 0), 0)

    smem = pl.BlockSpec(memory_space=pltpu.SMEM)
    return pl.pallas_call(
        functools.partial(_local_kernel, tq=tq),
        grid=(batch, nq),
        in_specs=[_const_spec((2 * BLOCK, BLOCK)), smem, smem,
                  pl.BlockSpec((None, SWA_HEADS // 2, LANES, tq), lambda b, i: (b, 0, 0, i)),
                  pl.BlockSpec((tq, 2 * SWA_KV), cur),
                  pl.BlockSpec((BLOCK, 2 * SWA_KV), prev),
                  pl.BlockSpec((None, SWA_KV, tq), lambda b, i: (b, 0, i)),
                  pl.BlockSpec((None, SWA_KV, BLOCK),
                               lambda b, i: (b, 0, jnp.maximum(i * sub - 1, 0))),
                  pl.BlockSpec((None, MEM_HEADS, MEM_HEAD_DIM, tq), lambda b, i: (b, 0, 0, i)),
                  pl.BlockSpec((N_MEM, MEM_W), lambda b, i: (b, 0)),
                  pl.BlockSpec((None, MEM_W, N_MEM), lambda b, i: (b, 0, 0))],
        out_specs=[pl.BlockSpec((tq, SWA_Q), cur),
                   pl.BlockSpec((tq, MEM_W), cur)],
        out_shape=[jax.ShapeDtypeStruct((t, SWA_Q), BF16),
                   jax.ShapeDtypeStruct((t, MEM_W), BF16)],
        scratch_shapes=[pltpu.VMEM((SWA_HEADS // 2, 2 * BLOCK, 2 * BLOCK), F32)],
        compiler_params=_params(2),
        name="local",
    )(bucket_t, rel_bias, sinks, qat, ka, ka, vat, vat, qmt, mk, mvt)


def _fox_kernel(fast_ref, qt_ref, aqt_ref, k_ref, ak_ref, vt_ref, o_ref,
                sa_ref, sb_ref, acc_ref, m_ref, *, tk, n_sub, n_steps):
    pair = pl.program_id(1)
    g = pl.program_id(2)
    row = lax.broadcasted_iota(jnp.int32, (LANES, tk), 0)
    half = [row < HEAD64, row >= HEAD64]
    qcat = []
    for sub in range(n_sub):
        q = qt_ref[:, sub * tk:(sub + 1) * tk]
        aq = aqt_ref[:, sub * tk:(sub + 1) * tk]
        zero = jnp.zeros_like(q)
        per_head = []
        for h in range(2):
            lo = (2 * pair + h) * AUG_PER_HEAD
            mine = jnp.logical_and(row >= lo, row < lo + AUG_PER_HEAD)
            per_head.append(jnp.concatenate([jnp.where(half[h], q, zero),
                                             jnp.where(mine, aq, zero)], axis=0))
        qcat.append(per_head)
    ones = jnp.ones((BF16_ROWS, tk), BF16)
    acc_ref[...] = jnp.zeros_like(acc_ref)
    key = lax.broadcasted_iota(jnp.int32, (tk, tk), 0)
    qry = lax.broadcasted_iota(jnp.int32, (tk, tk), 1)
    causal = key <= qry

    def kcat(j):
        r0 = pl.multiple_of(j * tk, tk)
        return jnp.concatenate([k_ref[pl.ds(r0, tk), :], ak_ref[pl.ds(r0, tk), :]], axis=1)

    def vt_ones(j, h):
        return jnp.concatenate([vt_ref[j, h * HEAD64:(h + 1) * HEAD64, :], ones], axis=0)

    def visible_keys(c, diagonal):
        return min(tk, c + MXU_DIM) if diagonal else tk

    def qk(sub, j, s_ref, diagonal=False):
        kc = kcat(j)
        for h in range(2):
            for c in range(0, tk, MXU_DIM):
                nk = visible_keys(c, diagonal)
                s_ref[h, :nk, c:c + MXU_DIM] = _dot(kc[:nk], qcat[sub][h][:, c:c + MXU_DIM])

    @pl.when(fast_ref[0] == 1)
    def _():
        bufs = (sa_ref, sb_ref)

        def exp_pv(sub, j, s_ref, masked):
            for h in range(2):
                for c in range(0, tk, MXU_DIM):
                    nk = visible_keys(c, masked)
                    s = s_ref[h, :nk, c:c + MXU_DIM]
                    if masked:
                        s = jnp.where(causal[:nk, c:c + MXU_DIM], s, NEG)
                    acc_ref[2 * sub + h, :, c:c + MXU_DIM] += _dot(vt_ones(j, h)[:, :nk],
                                                                   jnp.exp2(s).astype(BF16))

        full = [(sub, jj) for jj in range(n_sub) for sub in range(n_sub)]
        diag = [(sub, jj) for jj in range(n_sub) for sub in range(jj, n_sub)]
        assert len(full) % 2 == 0
        qk(0, 0, bufs[0])

        def body(t, carry):
            for n, (sub, jj) in enumerate(full):
                nsub, njj = full[n + 1] if n + 1 < len(full) else (0, n_sub)
                qk(nsub, n_sub * t + njj, bufs[(n + 1) % 2])
                exp_pv(sub, n_sub * t + jj, bufs[n % 2], False)
            return carry

        if n_steps > 1:
            lax.fori_loop(0, g, body, 0)
        for n, (sub, jj) in enumerate(diag):
            if n + 1 < len(diag):
                nsub, njj = diag[n + 1]
                qk(nsub, n_sub * g + njj, bufs[(n + 1) % 2], diagonal=nsub == njj)
            exp_pv(sub, n_sub * g + jj, bufs[n % 2], sub == jj)

    @pl.when(fast_ref[0] == 0)
    def _():
        m_ref[...] = jnp.full(m_ref.shape, NEG, F32)
        for sub in range(n_sub):
            last = n_sub * g + sub

            def body(j, carry, sub=sub, last=last):
                qk(sub, j, sa_ref)
                for h in range(2):
                    u = 2 * sub + h
                    s = jnp.where(jnp.logical_or(causal, j < last), sa_ref[h], NEG)
                    m_prev = m_ref[u]
                    m_next = jnp.maximum(m_prev, jnp.max(s, axis=0, keepdims=True))
                    alpha = jnp.exp2(m_prev - m_next)
                    p = jnp.exp2(s - m_next).astype(BF16)
                    acc_ref[u] = alpha * acc_ref[u] + _dot(vt_ones(j, h), p)
                    m_ref[u] = m_next
                return carry

            lax.fori_loop(0, last + 1, body, 0)

    for sub in range(n_sub):
        halves = []
        for h in range(2):
            acc = acc_ref[2 * sub + h]
            halves.append(acc[:HEAD64, :] / acc[HEAD64:HEAD64 + 1, :])
        o_ref[sub * tk:(sub + 1) * tk, :] = jnp.concatenate(halves, axis=0).T.astype(BF16)


def _fox(fast, qft, aqt, kf, ak, vt, batch, seq, tk):
    t = kf.shape[0]
    n_sub = min(FOX_SUB, seq // tk)
    tq = n_sub * tk
    assert seq % tq == 0, (seq, tq)
    nq = seq // tq
    pairs = FOX_HEADS // 2
    return pl.pallas_call(
        functools.partial(_fox_kernel, tk=tk, n_sub=n_sub, n_steps=nq),
        grid=(batch, pairs, nq),
        in_specs=[pl.BlockSpec(memory_space=pltpu.SMEM),
                  pl.BlockSpec((None, None, LANES, tq), lambda b, p, i: (b, p, 0, i)),
                  pl.BlockSpec((None, LANES, tq), lambda b, p, i: (b, 0, i)),
                  pl.BlockSpec((seq, LANES), lambda b, p, i: (b, p)),
                  pl.BlockSpec((seq, LANES), lambda b, p, i: (b, 0)),
                  pl.BlockSpec((None, None, seq // tk, LANES, tk),
                               lambda b, p, i: (b, p, 0, 0, 0))],
        out_specs=pl.BlockSpec((tq, LANES), lambda b, p, i: (b * nq + i, p)),
        out_shape=jax.ShapeDtypeStruct((t, FOX_W), BF16),
        scratch_shapes=[pltpu.VMEM((2, tk, tk), F32),
                        pltpu.VMEM((2, tk, tk), F32),
                        pltpu.VMEM((2 * n_sub, HEAD64 + BF16_ROWS, tk), F32),
                        pltpu.VMEM((2 * n_sub, 1, tk), F32)],
        compiler_params=_params(3),
        name="fox",
    )(fast, qft, aqt, kf, ak, vt)


def _post_kernel(x_ref, oa_ref, of_ref, om_ref, gate_ref, wa_ref, wf_ref, wm_ref, wo_ref,
                 g_ref, wu_ref, wd_ref, y_ref, *, chunk):
    ya = _dot(oa_ref[...], wa_ref[...])
    yf = _dot(of_ref[...], wf_ref[...])
    ym = _dot(om_ref[...], wm_ref[...])
    merged = (gate_ref[:, 0:D_MODEL].astype(F32) * ya
              + gate_ref[:, D_MODEL:2 * D_MODEL].astype(F32) * yf
              + gate_ref[:, 2 * D_MODEL:].astype(F32) * ym)
    x1 = x_ref[...] + _dot(merged.astype(BF16), wo_ref[...])
    h = _rms_rows(x1, g_ref[...]).astype(BF16)
    acc = x1
    for c in range(0, D_FF, chunk):
        u = jnp.maximum(_dot(h, wu_ref[:, c:c + chunk]), 0.0)
        acc = acc + _dot((u * u).astype(BF16), wd_ref[c:c + chunk, :])
    y_ref[...] = acc


def _post(x2, oa, of, om, gates, wa, wf, wm, wo, g_mlp, wu, wd, tm, chunk=1024):
    t = x2.shape[0]
    row = lambda w: pl.BlockSpec((tm, w), lambda i: (i, 0))
    resident = lambda shape: pl.BlockSpec(shape, lambda i: (0,) * len(shape),
                                          pipeline_mode=pl.Buffered(1))
    return pl.pallas_call(
        functools.partial(_post_kernel, chunk=chunk),
        grid=(t // tm,),
        in_specs=[row(D_MODEL), row(SWA_Q), row(FOX_W), row(MEM_W), row(GATE_W),
                  resident((SWA_Q, D_MODEL)), resident((FOX_W, D_MODEL)),
                  resident((MEM_W, D_MODEL)), resident((D_MODEL, D_MODEL)),
                  resident((1, D_MODEL)), resident((D_MODEL, D_FF)), resident((D_FF, D_MODEL))],
        out_specs=row(D_MODEL),
        out_shape=jax.ShapeDtypeStruct((t, D_MODEL), F32),
        compiler_params=_params(1),
        name="post",
    )(x2, oa, of, om, gates, wa, wf, wm, wo, g_mlp, wu, wd)


def _t5_bucket_table():
    max_exact = REL_BUCKETS // 2
    t_loc = np.arange(BLOCK)[:, None] + BLOCK
    s_loc = np.arange(2 * BLOCK)[None, :]
    dist = t_loc - s_loc
    d = np.maximum(dist, 0)
    df = np.maximum(d, 1).astype(np.float32)
    scaled = (np.log(df / np.float32(max_exact)) / np.float32(math.log(REL_MAX_DIST / max_exact))
              * np.float32(REL_BUCKETS - max_exact))
    large = np.minimum(max_exact + scaled.astype(np.int32), REL_BUCKETS - 1)
    bucket = np.where(d < max_exact, d, large)
    band = (dist >= 0) & (dist < WINDOW)
    return jnp.asarray(np.where(band, bucket, -1).astype(np.int32).T)


def _aug_placement(shift):
    place = [[0.0] * (2 * LANES) for _ in range(LANES)]
    ones = [0.0] * (2 * LANES)
    shift_lanes = []
    for h in range(FOX_HEADS):
        base = h * AUG_PER_HEAD
        for part in range(3):
            src = part * FOX_HEADS + h
            place[src][base + part] = 1.0
            place[src][LANES + base + 3 + part] = 1.0
            ones[base + 3 + part] = 1.0
            ones[LANES + base + part] = 1.0
        ones[base + 6] = 1.0
        shift_lanes.append(LANES + base + 6)
    consts = jnp.array(ones, F32).at[jnp.array(shift_lanes)].set(-shift)
    return jnp.array(place, BF16), jnp.broadcast_to(consts[None, :], (8, 2 * LANES))


def _tile_gain(g, reps, scale=1.0):
    return (jnp.tile(g.astype(F32), reps) * scale)[None, :]


def _pick_tile(n, target):
    t = min(n, target)
    while n % t:
        t //= 2
    return t


def kernel(x, mem, g_mix, w_in, b_gate, b_forget, qn_swa, kn_swa, sink_swa, rel_bias,
           qn_fox, kn_fox, g_mem, w_mem_kv, qn_mem, kn_mem, w_o_swa, w_o_fox, w_o_mem,
           w_out, g_mlp, w_mlp_up, w_mlp_down):
    batch, seq, _ = x.shape
    n_layers = w_in.shape[0]
    t = batch * seq
    tm = _pick_tile(seq, 512)
    tq = _pick_tile(seq, 1024)

    bucket = _t5_bucket_table()
    tri = (jnp.arange(tm)[:, None] >= jnp.arange(tm)[None, :]).astype(BF16)

    x2 = x.reshape(t, D_MODEL)
    mem2 = mem.reshape(batch * N_MEM, D_MODEL)
    for layer in range(n_layers):
        w_rows = w_in[layer].T
        o = 0
        parts = {}
        for name, width in (("qa", SWA_Q), ("ka", SWA_KV), ("va", SWA_KV), ("qf", FOX_W),
                            ("kf", FOX_W), ("vf", FOX_W), ("fl", FOX_HEADS), ("qm", MEM_W),
                            ("g", GATE_W)):
            parts[name] = w_rows[o:o + width]
            o += width

        def dup(m):
            return jnp.concatenate([m[:HEAD64], m[:HEAD64], m[HEAD64:], m[HEAD64:]], axis=0)

        w1 = jnp.concatenate(
            [dup(parts["ka"]), parts["kf"], parts["g"],
             jnp.pad(parts["fl"], ((0, LANES - FOX_HEADS), (0, 0)))], axis=0).astype(BF16)
        wt = jnp.concatenate([parts["vf"], parts["va"], parts["qf"], parts["qa"], parts["qm"]],
                             axis=0).astype(BF16)
        gqt = jnp.broadcast_to(jnp.concatenate([
            _tile_gain(qn_fox[layer], FOX_HEADS, HEAD64 ** -0.5 * LOG2E),
            _tile_gain(qn_swa[layer], SWA_HEADS, HEAD64 ** -0.5 * LOG2E),
            _tile_gain(qn_mem[layer], MEM_HEADS, MEM_HEAD_DIM ** -0.5 * LOG2E)], axis=1).T,
            (T_END - T_QF, LANES))

        wmt = w_mem_kv[layer].T.astype(BF16)
        mk, mvt = _memkv(mem2, g_mem[layer][None, :], wmt[:MEM_W], wmt[MEM_W:],
                         _tile_gain(kn_mem[layer], MEM_HEADS))

        bound = (HEAD64 ** 0.5 * LOG2E * 1.02 * jnp.max(jnp.abs(qn_fox[layer]))
                 * jnp.max(jnp.abs(kn_fox[layer]))).astype(BF16).astype(F32)
        fast = bound <= FOX_FAST_MAX_SHIFT
        place, consts = _aug_placement(jnp.where(fast, bound, 0.0))
        bf_pad = jnp.pad(b_forget[layer].astype(F32), (0, LANES - FOX_HEADS))[None, :]

        ka, kf, gates, fl, vt, vat, qft, qat, qmt = _proj(
            x2, g_mix[layer][None, :], w1, wt, b_gate[layer][None, :],
            _tile_gain(kn_swa[layer], 2 * SWA_KV_HEADS), _tile_gain(kn_fox[layer], FOX_HEADS),
            gqt, tm, seq)
        aqt, ak = _cum(fl, bf_pad, tri, place, consts, tm, seq)

        oa, om = _local(bucket, rel_bias.astype(F32) * LOG2E, sink_swa[layer].astype(F32) * LOG2E,
                        qat, ka, vat, qmt, mk, mvt, batch, seq, tq)
        of = _fox(fast.astype(jnp.int32)[None], qft, aqt, kf, ak, vt, batch, seq, tm)

        x2 = _post(x2, oa, of, om, gates, w_o_swa[layer].astype(BF16),
                   w_o_fox[layer].astype(BF16), w_o_mem[layer].astype(BF16),
                   w_out[layer].astype(BF16), g_mlp[layer][None, :],
                   w_mlp_up[layer].astype(BF16), w_mlp_down[layer].astype(BF16), tm)
    return x2.reshape(batch, seq, D_MODEL)
```

```python
import functools
import math

import jax
import jax.numpy as jnp
import numpy as np
from jax import lax
from jax.experimental import pallas as pl
from jax.experimental.pallas import tpu as pltpu

F32 = jnp.float32
BF16 = jnp.bfloat16

D_MODEL = 1024
N_MEM = 256
SWA_HEADS = 8
SWA_KV_HEADS = 2
SWA_GROUP = SWA_HEADS // SWA_KV_HEADS
HEAD64 = 64
WINDOW = 128
FOX_HEADS = 8
MEM_HEADS = 4
MEM_HEAD_DIM = 128
D_FF = 4 * D_MODEL
REL_BUCKETS = 32
REL_MAX_DIST = 128
BLOCK = 128
EPS = 1e-6
NEG = -1e30
LOG2E = math.log2(math.e)

LANES = 128
MXU_DIM = 256
BF16_ROWS = 16
VMEM_LIMIT = 56 * 1024 * 1024

SWA_Q = SWA_HEADS * HEAD64
SWA_KV = SWA_KV_HEADS * HEAD64
FOX_W = FOX_HEADS * HEAD64
MEM_W = MEM_HEADS * MEM_HEAD_DIM
GATE_W = 3 * D_MODEL

C_KA = 0
C_KF = C_KA + 2 * SWA_KV
C_G = C_KF + FOX_W
C_FL = C_G + GATE_W
C_END = C_FL + LANES
T_VF = 0
T_VA = T_VF + FOX_W
T_QF = T_VA + SWA_KV
T_QA = T_QF + FOX_W
T_QM = T_QA + SWA_Q
T_END = T_QM + MEM_W

AUG_PER_HEAD = 7
FOX_FAST_MAX_SHIFT = 60.0
FOX_SUB = 4
LOCAL_LOOKAHEAD = 4


def _dot(a, b):
    return jnp.dot(a, b, preferred_element_type=F32)


def _dot_nt(a, b):
    return lax.dot_general(a, b, (((1,), (1,)), ((), ())), preferred_element_type=F32)


def _rms_rows(x, gain):
    ms = jnp.mean(x * x, axis=-1, keepdims=True)
    return x * lax.rsqrt(ms + EPS) * gain


def _head_rms(y, gain, head_dim):
    y2 = y * y
    lower = lax.broadcasted_iota(jnp.int32, (y.shape[0], LANES), 1) < HEAD64
    cols = []
    for c in range(0, y.shape[1], LANES):
        blk = y2[:, c:c + LANES]
        total = jnp.sum(blk, axis=-1, keepdims=True)
        if head_dim == LANES:
            cols.append(jnp.broadcast_to(total, blk.shape))
        else:
            low = jnp.sum(jnp.where(lower, blk, 0.0), axis=-1, keepdims=True)
            cols.append(jnp.where(lower, low, total - low))
    ss = jnp.concatenate(cols, axis=1)
    return y * lax.rsqrt(ss * (1.0 / head_dim) + EPS) * gain


def _const_spec(shape):
    return pl.BlockSpec(shape, lambda *_: (0,) * len(shape))


def _params(n_axes):
    return pltpu.CompilerParams(dimension_semantics=("arbitrary",) * n_axes,
                                vmem_limit_bytes=VMEM_LIMIT)


def _memkv_kernel(mem_ref, g_ref, wkt_ref, wvt_ref, kn_ref, mk_ref, mvt_ref):
    h = _rms_rows(mem_ref[...], g_ref[...]).astype(BF16)
    mk_ref[...] = _head_rms(_dot_nt(h, wkt_ref[...]), kn_ref[...], MEM_HEAD_DIM).astype(BF16)
    mvt_ref[...] = _dot_nt(wvt_ref[...], h).astype(BF16)


def _memkv(mem2, g_mem, wkt, wvt, kn_t):
    rows = mem2.shape[0]
    nb = rows // N_MEM
    return pl.pallas_call(
        _memkv_kernel,
        grid=(nb,),
        in_specs=[pl.BlockSpec((N_MEM, D_MODEL), lambda i: (i, 0)),
                  _const_spec((1, D_MODEL)),
                  _const_spec((MEM_W, D_MODEL)),
                  _const_spec((MEM_W, D_MODEL)),
                  _const_spec((1, MEM_W))],
        out_specs=[pl.BlockSpec((N_MEM, MEM_W), lambda i: (i, 0)),
                   pl.BlockSpec((None, MEM_W, N_MEM), lambda i: (i, 0, 0))],
        out_shape=[jax.ShapeDtypeStruct((rows, MEM_W), BF16),
                   jax.ShapeDtypeStruct((nb, MEM_W, N_MEM), BF16)],
        compiler_params=_params(1),
        name="memkv",
    )(mem2, g_mem, wkt, wvt, kn_t)


def _head_rms_t(y, gain, head_dim):
    out = []
    for r in range(0, y.shape[0], head_dim):
        blk = y[r:r + head_dim, :]
        ms = jnp.sum(blk * blk, axis=0, keepdims=True) * (1.0 / head_dim)
        out.append(blk * lax.rsqrt(ms + EPS) * gain[r:r + head_dim, :])
    return jnp.concatenate(out, axis=0)


def _proj_kernel(x_ref, g_ref, w_ref, wt_ref, bg_ref, gka_ref, gkf_ref, gqt_ref,
                 ka_ref, kf_ref, gate_ref, fl_ref, vt_ref, vat_ref, qft_ref, qat_ref, qmt_ref):
    h = _rms_rows(x_ref[...], g_ref[...]).astype(BF16)
    tm = h.shape[0]
    kk = _dot_nt(h, w_ref[:C_G, :])
    ka_ref[...] = _head_rms(kk[:, C_KA:C_KF], gka_ref[...], HEAD64).astype(BF16)
    kf_ref[...] = _head_rms(kk[:, C_KF:C_G], gkf_ref[...], HEAD64).astype(BF16)
    t = _dot_nt(wt_ref[...], h)
    vat_ref[...] = t[T_VA:T_QF, :].astype(BF16)
    gain_t = jnp.tile(gqt_ref[...], (1, tm // LANES))
    qf = _head_rms_t(t[T_QF:T_QA, :], gain_t[:FOX_W, :], HEAD64).astype(BF16)
    qa = _head_rms_t(t[T_QA:T_QM, :], gain_t[FOX_W:FOX_W + SWA_Q, :], HEAD64).astype(BF16)
    qm = _head_rms_t(t[T_QM:T_END, :], gain_t[FOX_W + SWA_Q:, :], MEM_HEAD_DIM).astype(BF16)
    for p in range(FOX_HEADS // 2):
        rows = slice(p * LANES, (p + 1) * LANES)
        vt_ref[p] = t[rows, :].astype(BF16)
        qft_ref[p] = qf[rows, :]
        qat_ref[p] = qa[rows, :]
        qmt_ref[p] = qm[rows, :]
    zf = _dot_nt(h, w_ref[C_G:C_END, :])
    z = zf[:, :GATE_W] + bg_ref[...]
    gate_ref[...] = (0.5 * jnp.tanh(0.5 * z) + 0.5).astype(BF16)
    fl_ref[...] = zf[:, GATE_W:]


def _proj(x2, g_mix, w1, wt, b_gate, gka, gkf, gqt, tm, seq):
    t = x2.shape[0]
    nt = seq // tm
    slabs = FOX_HEADS // 2

    def rows(w, dtype=BF16):
        return jax.ShapeDtypeStruct((t, w), dtype), pl.BlockSpec((tm, w), lambda i: (i, 0))

    def slabs_t():
        return (jax.ShapeDtypeStruct((t // seq, slabs, LANES, seq), BF16),
                pl.BlockSpec((None, slabs, LANES, tm), lambda i: (i // nt, 0, 0, i % nt)))

    outs = [rows(2 * SWA_KV), rows(FOX_W), rows(GATE_W), rows(LANES, F32),
            (jax.ShapeDtypeStruct((t // seq, slabs, nt, LANES, tm), BF16),
             pl.BlockSpec((None, slabs, None, LANES, tm), lambda i: (i // nt, 0, i % nt, 0, 0))),
            (jax.ShapeDtypeStruct((t // seq, SWA_KV, seq), BF16),
             pl.BlockSpec((None, SWA_KV, tm), lambda i: (i // nt, 0, i % nt))),
            slabs_t(), slabs_t(), slabs_t()]
    out_shape = [o[0] for o in outs]
    out_specs = [o[1] for o in outs]
    return pl.pallas_call(
        _proj_kernel,
        grid=(t // tm,),
        in_specs=[pl.BlockSpec((tm, D_MODEL), lambda i: (i, 0)),
                  _const_spec((1, D_MODEL)),
                  _const_spec((C_END, D_MODEL)),
                  _const_spec((T_END, D_MODEL)),
                  _const_spec((1, GATE_W)),
                  _const_spec((1, 2 * SWA_KV)),
                  _const_spec((1, FOX_W)),
                  _const_spec((T_END - T_QF, LANES))],
        out_specs=out_specs,
        out_shape=out_shape,
        compiler_params=_params(1),
        name="proj",
    )(x2, g_mix, w1, wt, b_gate, gka, gkf, gqt)


def _split3(c):
    hi = c.astype(BF16)
    r1 = c - hi.astype(F32)
    mid = r1.astype(BF16)
    lo = (r1 - mid.astype(F32)).astype(BF16)
    return hi, mid, lo


def _pack3(x):
    hi, mid, lo = _split3(x)
    return (hi.astype(F32) + pltpu.roll(mid.astype(F32), FOX_HEADS, 1)
            + pltpu.roll(lo.astype(F32), 2 * FOX_HEADS, 1)).astype(BF16)


def _cum_kernel(fl_ref, bf_ref, tri_ref, place_ref, const_ref, aqt_ref, ak_ref, *, tm):
    n_tiles = fl_ref.shape[0] // tm
    lane = lax.broadcasted_iota(jnp.int32, (tm, LANES), 1)
    heads = lane < FOX_HEADS
    tri = tri_ref[...]
    parts = []
    for k in range(n_tiles):
        z = fl_ref[k * tm:(k + 1) * tm, :] + bf_ref[...]
        logf = jnp.where(heads, jnp.minimum(z, 0.0) - jnp.log(1.0 + jnp.exp(-jnp.abs(z))), 0.0)
        parts.append(_dot(tri, _pack3(logf)))
    carry = jnp.zeros((1, LANES), F32)
    packed = []
    for part in parts:
        c = ((part + pltpu.roll(part, LANES - FOX_HEADS, 1))
             + pltpu.roll(part, LANES - 2 * FOX_HEADS, 1)) + carry
        carry = c[tm - 1:tm, :]
        packed.append(_pack3(jnp.where(heads, c * LOG2E, 0.0)))
    for k in range(n_tiles):
        placed = _dot(packed[k], place_ref[...])
        rows = slice(k * tm, (k + 1) * tm)
        aqt_ref[:, rows] = (placed[:, :LANES] + const_ref[0:1, :LANES]).T.astype(BF16)
        ak_ref[rows, :] = (const_ref[0:1, LANES:] - placed[:, LANES:]).astype(BF16)


def _cum(fl, bf_pad, tri, place, consts, tm, seq):
    t = fl.shape[0]
    return pl.pallas_call(
        functools.partial(_cum_kernel, tm=tm),
        grid=(t // seq,),
        in_specs=[pl.BlockSpec((seq, LANES), lambda i: (i, 0)),
                  _const_spec((1, LANES)),
                  _const_spec((tm, tm)),
                  _const_spec((LANES, 2 * LANES)),
                  _const_spec((8, 2 * LANES))],
        out_specs=[pl.BlockSpec((None, LANES, seq), lambda i: (i, 0, 0)),
                   pl.BlockSpec((seq, LANES), lambda i: (i, 0))],
        out_shape=[jax.ShapeDtypeStruct((t // seq, LANES, seq), BF16),
                   jax.ShapeDtypeStruct((t, LANES), BF16)],
        compiler_params=_params(1),
        name="cum",
    )(fl, bf_pad, tri, place, consts)


def _local_kernel(bucket_ref, relb_ref, sink_ref,
                  qat_ref, kc_ref, kp_ref, vc_ref, vp_ref, qmt_ref, mk_ref, mvt_ref,
                  oa_ref, om_ref, tbl_ref, *, tq):
    b = pl.program_id(0)
    i = pl.program_id(1)
    pairs = SWA_HEADS // 2

    @pl.when(jnp.logical_and(b == 0, i == 0))
    def _():
        bucket = bucket_ref[...]
        band = bucket >= 0
        for h in range(SWA_HEADS):
            t = jnp.zeros(bucket.shape, F32)
            for k in range(REL_BUCKETS):
                t = jnp.where(bucket == k, relb_ref[k, h], t)
            tbl_ref[h // 2, :, (h % 2) * BLOCK:(h % 2 + 1) * BLOCK] = jnp.where(band, t, NEG)

    upper_rows = lax.broadcasted_iota(jnp.int32, (LANES, BLOCK), 0) >= HEAD64
    first_head = lax.broadcasted_iota(jnp.int32, (1, 2 * BLOCK), 1) < BLOCK
    key_row = lax.broadcasted_iota(jnp.int32, (2 * BLOCK, 2 * BLOCK), 0)
    pad_rows = jnp.logical_and(key_row < BLOCK, i == 0)
    ones = jnp.ones((BF16_ROWS, 2 * BLOCK), BF16)

    def windows(j):
        r0 = j * BLOCK
        if j == 0:
            return (jnp.concatenate([kp_ref[...], kc_ref[0:BLOCK, :]], axis=0),
                    jnp.concatenate([vp_ref[...], vc_ref[:, 0:BLOCK]], axis=1))
        return kc_ref[r0 - BLOCK:r0 + BLOCK, :], vc_ref[:, r0 - BLOCK:r0 + BLOCK]

    def swa_scores(j, hb):
        g = (2 * hb) // SWA_GROUP
        qblk = qat_ref[hb, :, j * BLOCK:(j + 1) * BLOCK]
        qstack = jnp.concatenate([jnp.where(upper_rows, 0, qblk),
                                  jnp.where(upper_rows, qblk, 0)], axis=1)
        s = _dot(windows(j)[0][:, g * LANES:(g + 1) * LANES], qstack) + tbl_ref[hb]
        return jnp.where(pad_rows, NEG, s) if j == 0 else s

    def swa_finish(j, hb, s):
        g = (2 * hb) // SWA_GROUP
        sink = jnp.where(first_head, sink_ref[2 * hb], sink_ref[2 * hb + 1])
        m = jnp.maximum(jnp.max(s, axis=0, keepdims=True), sink)
        p = jnp.exp2(s - m).astype(BF16)
        vt_ones = jnp.concatenate([windows(j)[1][g * HEAD64:(g + 1) * HEAD64, :], ones], axis=0)
        acc = _dot(vt_ones, p)
        o = acc[:HEAD64, :] / (acc[HEAD64:HEAD64 + 1, :] + jnp.exp2(sink - m))
        o = jnp.concatenate([o[:, :BLOCK], o[:, BLOCK:]], axis=0)
        oa_ref[j * BLOCK:(j + 1) * BLOCK, hb * LANES:(hb + 1) * LANES] = o.T.astype(BF16)

    ones_m = jnp.ones((BF16_ROWS, N_MEM), BF16)

    def mem_scores(h):
        sl = slice(h * MEM_HEAD_DIM, (h + 1) * MEM_HEAD_DIM)
        return _dot(mk_ref[:, sl], qmt_ref[h])

    def mem_finish(h, s):
        sl = slice(h * MEM_HEAD_DIM, (h + 1) * MEM_HEAD_DIM)
        m = jnp.max(s, axis=0, keepdims=True)
        p = jnp.exp2(s - m).astype(BF16)
        acc = _dot(jnp.concatenate([mvt_ref[sl, :], ones_m], axis=0), p)
        o = acc[:MEM_HEAD_DIM, :] / acc[MEM_HEAD_DIM:MEM_HEAD_DIM + 1, :]
        om_ref[:, sl] = o.T.astype(BF16)

    units = [(swa_scores, swa_finish, (j, hb)) for j in range(tq // BLOCK) for hb in range(pairs)]
    units += [(mem_scores, mem_finish, (h,)) for h in range(MEM_HEADS)]
    pending = []
    for n in range(len(units) + LOCAL_LOOKAHEAD):
        if n < len(units):
            score_fn, _, args = units[n]
            pending.append(score_fn(*args))
        if n >= LOCAL_LOOKAHEAD:
            _, finish_fn, args = units[n - LOCAL_LOOKAHEAD]
            finish_fn(*args, pending[n - LOCAL_LOOKAHEAD])


def _local(bucket_t, rel_bias, sinks, qat, ka, vat, qmt, mk, mvt, batch, seq, tq):
    t = ka.shape[0]
    nq = seq // tq
    sub = tq // BLOCK

    def cur(b, i):
        return (b * nq + i, 0)

    def prev(b, i):
        return (jnp.maximum((b * nq + i) * sub - 1, 0), 0)

    smem = pl.BlockSpec(memory_space=pltpu.SMEM)
    return pl.pallas_call(
        functools.partial(_local_kernel, tq=tq),
        grid=(batch, nq),
        in_specs=[_const_spec((2 * BLOCK, BLOCK)), smem, smem,
                  pl.BlockSpec((None, SWA_HEADS // 2, LANES, tq), lambda b, i: (b, 0, 0, i)),
                  pl.BlockSpec((tq, 2 * SWA_KV), cur),
                  pl.BlockSpec((BLOCK, 2 * SWA_KV), prev),
                  pl.BlockSpec((None, SWA_KV, tq), lambda b, i: (b, 0, i)),
                  pl.BlockSpec((None, SWA_KV, BLOCK),
                               lambda b, i: (b, 0, jnp.maximum(i * sub - 1, 0))),
                  pl.BlockSpec((None, MEM_HEADS, MEM_HEAD_DIM, tq), lambda b, i: (b, 0, 0, i)),
                  pl.BlockSpec((N_MEM, MEM_W), lambda b, i: (b, 0)),
                  pl.BlockSpec((None, MEM_W, N_MEM), lambda b, i: (b, 0, 0))],
        out_specs=[pl.BlockSpec((tq, SWA_Q), cur),
                   pl.BlockSpec((tq, MEM_W), cur)],
        out_shape=[jax.ShapeDtypeStruct((t, SWA_Q), BF16),
                   jax.ShapeDtypeStruct((t, MEM_W), BF16)],
        scratch_shapes=[pltpu.VMEM((SWA_HEADS // 2, 2 * BLOCK, 2 * BLOCK), F32)],
        compiler_params=_params(2),
        name="local",
    )(bucket_t, rel_bias, sinks, qat, ka, ka, vat, vat, qmt, mk, mvt)


def _fox_kernel(fast_ref, qt_ref, aqt_ref, k_ref, ak_ref, vt_ref, o_ref,
                sa_ref, sb_ref, acc_ref, m_ref, *, tk, n_sub, n_steps):
    pair = pl.program_id(1)
    g = pl.program_id(2)
    row = lax.broadcasted_iota(jnp.int32, (LANES, tk), 0)
    half = [row < HEAD64, row >= HEAD64]
    qcat = []
    for sub in range(n_sub):
        q = qt_ref[:, sub * tk:(sub + 1) * tk]
        aq = aqt_ref[:, sub * tk:(sub + 1) * tk]
        zero = jnp.zeros_like(q)
        per_head = []
        for h in range(2):
            lo = (2 * pair + h) * AUG_PER_HEAD
            mine = jnp.logical_and(row >= lo, row < lo + AUG_PER_HEAD)
            per_head.append(jnp.concatenate([jnp.where(half[h], q, zero),
                                             jnp.where(mine, aq, zero)], axis=0))
        qcat.append(per_head)
    ones = jnp.ones((BF16_ROWS, tk), BF16)
    acc_ref[...] = jnp.zeros_like(acc_ref)
    key = lax.broadcasted_iota(jnp.int32, (tk, tk), 0)
    qry = lax.broadcasted_iota(jnp.int32, (tk, tk), 1)
    causal = key <= qry

    def kcat(j):
        r0 = pl.multiple_of(j * tk, tk)
        return jnp.concatenate([k_ref[pl.ds(r0, tk), :], ak_ref[pl.ds(r0, tk), :]], axis=1)

    def vt_ones(j, h):
        return jnp.concatenate([vt_ref[j, h * HEAD64:(h + 1) * HEAD64, :], ones], axis=0)

    def visible_keys(c, diagonal):
        return min(tk, c + MXU_DIM) if diagonal else tk

    def qk(sub, j, s_ref, diagonal=False):
        kc = kcat(j)
        for h in range(2):
            for c in range(0, tk, MXU_DIM):
                nk = visible_keys(c, diagonal)
                s_ref[h, :nk, c:c + MXU_DIM] = _dot(kc[:nk], qcat[sub][h][:, c:c + MXU_DIM])

    @pl.when(fast_ref[0] == 1)
    def _():
        bufs = (sa_ref, sb_ref)

        def exp_pv(sub, j, s_ref, masked):
            for h in range(2):
                for c in range(0, tk, MXU_DIM):
                    nk = visible_keys(c, masked)
                    s = s_ref[h, :nk, c:c + MXU_DIM]
                    if masked:
                        s = jnp.where(causal[:nk, c:c + MXU_DIM], s, NEG)
                    acc_ref[2 * sub + h, :, c:c + MXU_DIM] += _dot(vt_ones(j, h)[:, :nk],
                                                                   jnp.exp2(s).astype(BF16))

        diag = [(sub, jj) for jj in range(n_sub) for sub in range(jj, n_sub)]
        qk(0, 0, bufs[0])

        def body(j, carry):
            for sub in range(n_sub):
                nsub, nj = (sub + 1, j) if sub + 1 < n_sub else (0, j + 1)
                qk(nsub, nj, bufs[(sub + 1) % 2])
                exp_pv(sub, j, bufs[sub % 2], False)
            return carry

        if n_steps > 1:
            assert n_sub % 2 == 0
            lax.fori_loop(0, n_sub * g, body, 0)
        for n, (sub, jj) in enumerate(diag):
            if n + 1 < len(diag):
                nsub, njj = diag[n + 1]
                qk(nsub, n_sub * g + njj, bufs[(n + 1) % 2], diagonal=nsub == njj)
            exp_pv(sub, n_sub * g + jj, bufs[n % 2], sub == jj)

    @pl.when(fast_ref[0] == 0)
    def _():
        m_ref[...] = jnp.full(m_ref.shape, NEG, F32)
        for sub in range(n_sub):
            last = n_sub * g + sub

            def body(j, carry, sub=sub, last=last):
                qk(sub, j, sa_ref)
                for h in range(2):
                    u = 2 * sub + h
                    s = jnp.where(jnp.logical_or(causal, j < last), sa_ref[h], NEG)
                    m_prev = m_ref[u]
                    m_next = jnp.maximum(m_prev, jnp.max(s, axis=0, keepdims=True))
                    alpha = jnp.exp2(m_prev - m_next)
                    p = jnp.exp2(s - m_next).astype(BF16)
                    acc_ref[u] = alpha * acc_ref[u] + _dot(vt_ones(j, h), p)
                    m_ref[u] = m_next
                return carry

            lax.fori_loop(0, last + 1, body, 0)

    for sub in range(n_sub):
        halves = []
        for h in range(2):
            acc = acc_ref[2 * sub + h]
            halves.append(acc[:HEAD64, :] / acc[HEAD64:HEAD64 + 1, :])
        o_ref[sub * tk:(sub + 1) * tk, :] = jnp.concatenate(halves, axis=0).T.astype(BF16)


def _fox(fast, qft, aqt, kf, ak, vt, batch, seq, tk):
    t = kf.shape[0]
    n_sub = min(FOX_SUB, seq // tk)
    tq = n_sub * tk
    assert seq % tq == 0, (seq, tq)
    nq = seq // tq
    pairs = FOX_HEADS // 2
    return pl.pallas_call(
        functools.partial(_fox_kernel, tk=tk, n_sub=n_sub, n_steps=nq),
        grid=(batch, pairs, nq),
        in_specs=[pl.BlockSpec(memory_space=pltpu.SMEM),
                  pl.BlockSpec((None, None, LANES, tq), lambda b, p, i: (b, p, 0, i)),
                  pl.BlockSpec((None, LANES, tq), lambda b, p, i: (b, 0, i)),
                  pl.BlockSpec((seq, LANES), lambda b, p, i: (b, p)),
                  pl.BlockSpec((seq, LANES), lambda b, p, i: (b, 0)),
                  pl.BlockSpec((None, None, seq // tk, LANES, tk),
                               lambda b, p, i: (b, p, 0, 0, 0))],
        out_specs=pl.BlockSpec((tq, LANES), lambda b, p, i: (b * nq + i, p)),
        out_shape=jax.ShapeDtypeStruct((t, FOX_W), BF16),
        scratch_shapes=[pltpu.VMEM((2, tk, tk), F32),
                        pltpu.VMEM((2, tk, tk), F32),
                        pltpu.VMEM((2 * n_sub, HEAD64 + BF16_ROWS, tk), F32),
                        pltpu.VMEM((2 * n_sub, 1, tk), F32)],
        compiler_params=_params(3),
        name="fox",
    )(fast, qft, aqt, kf, ak, vt)


def _post_kernel(x_ref, oa_ref, of_ref, om_ref, gate_ref, wa_ref, wf_ref, wm_ref, wo_ref,
                 g_ref, wu_ref, wd_ref, y_ref, *, chunk):
    ya = _dot(oa_ref[...], wa_ref[...])
    yf = _dot(of_ref[...], wf_ref[...])
    ym = _dot(om_ref[...], wm_ref[...])
    merged = (gate_ref[:, 0:D_MODEL].astype(F32) * ya
              + gate_ref[:, D_MODEL:2 * D_MODEL].astype(F32) * yf
              + gate_ref[:, 2 * D_MODEL:].astype(F32) * ym)
    x1 = x_ref[...] + _dot(merged.astype(BF16), wo_ref[...])
    h = _rms_rows(x1, g_ref[...]).astype(BF16)
    acc = x1
    for c in range(0, D_FF, chunk):
        u = jnp.maximum(_dot(h, wu_ref[:, c:c + chunk]), 0.0)
        acc = acc + _dot((u * u).astype(BF16), wd_ref[c:c + chunk, :])
    y_ref[...] = acc


def _post(x2, oa, of, om, gates, wa, wf, wm, wo, g_mlp, wu, wd, tm, chunk=1024):
    t = x2.shape[0]
    row = lambda w: pl.BlockSpec((tm, w), lambda i: (i, 0))
    resident = lambda shape: pl.BlockSpec(shape, lambda i: (0,) * len(shape),
                                          pipeline_mode=pl.Buffered(1))
    return pl.pallas_call(
        functools.partial(_post_kernel, chunk=chunk),
        grid=(t // tm,),
        in_specs=[row(D_MODEL), row(SWA_Q), row(FOX_W), row(MEM_W), row(GATE_W),
                  resident((SWA_Q, D_MODEL)), resident((FOX_W, D_MODEL)),
                  resident((MEM_W, D_MODEL)), resident((D_MODEL, D_MODEL)),
                  resident((1, D_MODEL)), resident((D_MODEL, D_FF)), resident((D_FF, D_MODEL))],
        out_specs=row(D_MODEL),
        out_shape=jax.ShapeDtypeStruct((t, D_MODEL), F32),
        compiler_params=_params(1),
        name="post",
    )(x2, oa, of, om, gates, wa, wf, wm, wo, g_mlp, wu, wd)


def _t5_bucket_table():
    max_exact = REL_BUCKETS // 2
    t_loc = np.arange(BLOCK)[:, None] + BLOCK
    s_loc = np.arange(2 * BLOCK)[None, :]
    dist = t_loc - s_loc
    d = np.maximum(dist, 0)
    df = np.maximum(d, 1).astype(np.float32)
    scaled = (np.log(df / np.float32(max_exact)) / np.float32(math.log(REL_MAX_DIST / max_exact))
              * np.float32(REL_BUCKETS - max_exact))
    large = np.minimum(max_exact + scaled.astype(np.int32), REL_BUCKETS - 1)
    bucket = np.where(d < max_exact, d, large)
    band = (dist >= 0) & (dist < WINDOW)
    return jnp.asarray(np.where(band, bucket, -1).astype(np.int32).T)


def _aug_placement(shift):
    place = [[0.0] * (2 * LANES) for _ in range(LANES)]
    ones = [0.0] * (2 * LANES)
    shift_lanes = []
    for h in range(FOX_HEADS):
        base = h * AUG_PER_HEAD
        for part in range(3):
            src = part * FOX_HEADS + h
            place[src][base + part] = 1.0
            place[src][LANES + base + 3 + part] = 1.0
            ones[base + 3 + part] = 1.0
            ones[LANES + base + part] = 1.0
        ones[base + 6] = 1.0
        shift_lanes.append(LANES + base + 6)
    consts = jnp.array(ones, F32).at[jnp.array(shift_lanes)].set(-shift)
    return jnp.array(place, BF16), jnp.broadcast_to(consts[None, :], (8, 2 * LANES))


def _tile_gain(g, reps, scale=1.0):
    return (jnp.tile(g.astype(F32), reps) * scale)[None, :]


def _pick_tile(n, target):
    t = min(n, target)
    while n % t:
        t //= 2
    return t


def kernel(x, mem, g_mix, w_in, b_gate, b_forget, qn_swa, kn_swa, sink_swa, rel_bias,
           qn_fox, kn_fox, g_mem, w_mem_kv, qn_mem, kn_mem, w_o_swa, w_o_fox, w_o_mem,
           w_out, g_mlp, w_mlp_up, w_mlp_down):
    batch, seq, _ = x.shape
    n_layers = w_in.shape[0]
    t = batch * seq
    tm = _pick_tile(seq, 512)
    tq = _pick_tile(seq, 1024)

    bucket = _t5_bucket_table()
    tri = (jnp.arange(tm)[:, None] >= jnp.arange(tm)[None, :]).astype(BF16)

    x2 = x.reshape(t, D_MODEL)
    mem2 = mem.reshape(batch * N_MEM, D_MODEL)
    for layer in range(n_layers):
        w_rows = w_in[layer].T
        o = 0
        parts = {}
        for name, width in (("qa", SWA_Q), ("ka", SWA_KV), ("va", SWA_KV), ("qf", FOX_W),
                            ("kf", FOX_W), ("vf", FOX_W), ("fl", FOX_HEADS), ("qm", MEM_W),
                            ("g", GATE_W)):
            parts[name] = w_rows[o:o + width]
            o += width

        def dup(m):
            return jnp.concatenate([m[:HEAD64], m[:HEAD64], m[HEAD64:], m[HEAD64:]], axis=0)

        w1 = jnp.concatenate(
            [dup(parts["ka"]), parts["kf"], parts["g"],
             jnp.pad(parts["fl"], ((0, LANES - FOX_HEADS), (0, 0)))], axis=0).astype(BF16)
        wt = jnp.concatenate([parts["vf"], parts["va"], parts["qf"], parts["qa"], parts["qm"]],
                             axis=0).astype(BF16)
        gqt = jnp.broadcast_to(jnp.concatenate([
            _tile_gain(qn_fox[layer], FOX_HEADS, HEAD64 ** -0.5 * LOG2E),
            _tile_gain(qn_swa[layer], SWA_HEADS, HEAD64 ** -0.5 * LOG2E),
            _tile_gain(qn_mem[layer], MEM_HEADS, MEM_HEAD_DIM ** -0.5 * LOG2E)], axis=1).T,
            (T_END - T_QF, LANES))

        wmt = w_mem_kv[layer].T.astype(BF16)
        mk, mvt = _memkv(mem2, g_mem[layer][None, :], wmt[:MEM_W], wmt[MEM_W:],
                         _tile_gain(kn_mem[layer], MEM_HEADS))

        bound = (HEAD64 ** 0.5 * LOG2E * 1.02 * jnp.max(jnp.abs(qn_fox[layer]))
                 * jnp.max(jnp.abs(kn_fox[layer]))).astype(BF16).astype(F32)
        fast = bound <= FOX_FAST_MAX_SHIFT
        place, consts = _aug_placement(jnp.where(fast, bound, 0.0))
        bf_pad = jnp.pad(b_forget[layer].astype(F32), (0, LANES - FOX_HEADS))[None, :]

        ka, kf, gates, fl, vt, vat, qft, qat, qmt = _proj(
            x2, g_mix[layer][None, :], w1, wt, b_gate[layer][None, :],
            _tile_gain(kn_swa[layer], 2 * SWA_KV_HEADS), _tile_gain(kn_fox[layer], FOX_HEADS),
            gqt, tm, seq)
        aqt, ak = _cum(fl, bf_pad, tri, place, consts, tm, seq)

        oa, om = _local(bucket, rel_bias.astype(F32) * LOG2E, sink_swa[layer].astype(F32) * LOG2E,
                        qat, ka, vat, qmt, mk, mvt, batch, seq, tq)
        of = _fox(fast.astype(jnp.int32)[None], qft, aqt, kf, ak, vt, batch, seq, tm)

        x2 = _post(x2, oa, of, om, gates, w_o_swa[layer].astype(BF16),
                   w_o_fox[layer].astype(BF16), w_o_mem[layer].astype(BF16),
                   w_out[layer].astype(BF16), g_mlp[layer][None, :],
                   w_mlp_up[layer].astype(BF16), w_mlp_down[layer].astype(BF16), tm)
    return x2.reshape(batch, seq, D_MODEL)
```

```python
import functools
import math

import jax
import jax.numpy as jnp
import numpy as np
from jax import lax
from jax.experimental import pallas as pl
from jax.experimental.pallas import tpu as pltpu

F32 = jnp.float32
BF16 = jnp.bfloat16

D_MODEL = 1024
N_MEM = 256
SWA_HEADS = 8
SWA_KV_HEADS = 2
SWA_GROUP = SWA_HEADS // SWA_KV_HEADS
HEAD64 = 64
WINDOW = 128
FOX_HEADS = 8
MEM_HEADS = 4
MEM_HEAD_DIM = 128
D_FF = 4 * D_MODEL
REL_BUCKETS = 32
REL_MAX_DIST = 128
BLOCK = 128
EPS = 1e-6
NEG = -1e30
LOG2E = math.log2(math.e)

LANES = 128
MXU_DIM = 256
BF16_ROWS = 16
VMEM_LIMIT = 56 * 1024 * 1024

SWA_Q = SWA_HEADS * HEAD64
SWA_KV = SWA_KV_HEADS * HEAD64
FOX_W = FOX_HEADS * HEAD64
MEM_W = MEM_HEADS * MEM_HEAD_DIM
GATE_W = 3 * D_MODEL

C_QA = 0
C_KA = C_QA + SWA_Q
C_KF = C_KA + 2 * SWA_KV
C_QM = C_KF + FOX_W
C_G = C_QM + MEM_W
C_FL = C_G + GATE_W
C_END = C_FL + LANES

AUG_PER_HEAD = 7
FOX_FAST_MAX_SHIFT = 60.0
FOX_SUB = 4
LOCAL_LOOKAHEAD = 4


def _dot(a, b):
    return jnp.dot(a, b, preferred_element_type=F32)


def _dot_nt(a, b):
    return lax.dot_general(a, b, (((1,), (1,)), ((), ())), preferred_element_type=F32)


def _rms_rows(x, gain):
    ms = jnp.mean(x * x, axis=-1, keepdims=True)
    return x * lax.rsqrt(ms + EPS) * gain


def _head_rms(y, gain, head_dim):
    y2 = y * y
    lower = lax.broadcasted_iota(jnp.int32, (y.shape[0], LANES), 1) < HEAD64
    cols = []
    for c in range(0, y.shape[1], LANES):
        blk = y2[:, c:c + LANES]
        total = jnp.sum(blk, axis=-1, keepdims=True)
        if head_dim == LANES:
            cols.append(jnp.broadcast_to(total, blk.shape))
        else:
            low = jnp.sum(jnp.where(lower, blk, 0.0), axis=-1, keepdims=True)
            cols.append(jnp.where(lower, low, total - low))
    ss = jnp.concatenate(cols, axis=1)
    return y * lax.rsqrt(ss * (1.0 / head_dim) + EPS) * gain


def _const_spec(shape):
    return pl.BlockSpec(shape, lambda *_: (0,) * len(shape))


def _params(n_axes):
    return pltpu.CompilerParams(dimension_semantics=("arbitrary",) * n_axes,
                                vmem_limit_bytes=VMEM_LIMIT)


def _memkv_kernel(mem_ref, g_ref, wkt_ref, wvt_ref, kn_ref, mk_ref, mvt_ref):
    h = _rms_rows(mem_ref[...], g_ref[...]).astype(BF16)
    mk_ref[...] = _head_rms(_dot_nt(h, wkt_ref[...]), kn_ref[...], MEM_HEAD_DIM).astype(BF16)
    mvt_ref[...] = _dot_nt(wvt_ref[...], h).astype(BF16)


def _memkv(mem2, g_mem, wkt, wvt, kn_t):
    rows = mem2.shape[0]
    nb = rows // N_MEM
    return pl.pallas_call(
        _memkv_kernel,
        grid=(nb,),
        in_specs=[pl.BlockSpec((N_MEM, D_MODEL), lambda i: (i, 0)),
                  _const_spec((1, D_MODEL)),
                  _const_spec((MEM_W, D_MODEL)),
                  _const_spec((MEM_W, D_MODEL)),
                  _const_spec((1, MEM_W))],
        out_specs=[pl.BlockSpec((N_MEM, MEM_W), lambda i: (i, 0)),
                   pl.BlockSpec((None, MEM_W, N_MEM), lambda i: (i, 0, 0))],
        out_shape=[jax.ShapeDtypeStruct((rows, MEM_W), BF16),
                   jax.ShapeDtypeStruct((nb, MEM_W, N_MEM), BF16)],
        compiler_params=_params(1),
        name="memkv",
    )(mem2, g_mem, wkt, wvt, kn_t)


def _proj_kernel(x_ref, g_ref, w_ref, wt_ref, bg_ref,
                 gqa_ref, gka_ref, gqft_ref, gkf_ref, gqm_ref,
                 qa_ref, ka_ref, vat_ref, qft_ref, kf_ref, vt_ref, qm_ref, gate_ref, fl_ref):
    h = _rms_rows(x_ref[...], g_ref[...]).astype(BF16)
    tm = h.shape[0]
    qkv = _dot_nt(h, w_ref[:C_G, :])
    qa_ref[...] = _head_rms(qkv[:, C_QA:C_KA], gqa_ref[...], HEAD64).astype(BF16)
    ka_ref[...] = _head_rms(qkv[:, C_KA:C_KF], gka_ref[...], HEAD64).astype(BF16)
    kf_ref[...] = _head_rms(qkv[:, C_KF:C_QM], gkf_ref[...], HEAD64).astype(BF16)
    qm_ref[...] = _head_rms(qkv[:, C_QM:C_G], gqm_ref[...], MEM_HEAD_DIM).astype(BF16)
    t = _dot_nt(wt_ref[...], h)
    for p in range(FOX_HEADS // 2):
        vt_ref[p] = t[p * LANES:(p + 1) * LANES, :].astype(BF16)
    vat_ref[...] = t[FOX_W:FOX_W + SWA_KV, :].astype(BF16)
    gain_t = jnp.tile(gqft_ref[...], (1, tm // LANES))
    for p in range(FOX_HEADS // 2):
        rows = []
        for e in range(2):
            r0 = FOX_W + SWA_KV + (2 * p + e) * HEAD64
            y = t[r0:r0 + HEAD64, :]
            ms = jnp.sum(y * y, axis=0, keepdims=True) * (1.0 / HEAD64)
            rows.append(y * lax.rsqrt(ms + EPS) * gain_t[r0 - FOX_W - SWA_KV:r0 - FOX_W - SWA_KV
                                                         + HEAD64, :])
        qft_ref[p] = jnp.concatenate(rows, axis=0).astype(BF16)
    zf = _dot_nt(h, w_ref[C_G:C_END, :])
    z = zf[:, :GATE_W] + bg_ref[...]
    gate_ref[...] = (0.5 * jnp.tanh(0.5 * z) + 0.5).astype(BF16)
    fl_ref[...] = zf[:, GATE_W:]


def _proj(x2, g_mix, w1, wt, b_gate, gqa, gka, gqft, gkf, gqm, tm, seq):
    t = x2.shape[0]
    nt = seq // tm
    pairs = FOX_HEADS // 2

    def rows(w, dtype=BF16):
        return jax.ShapeDtypeStruct((t, w), dtype), pl.BlockSpec((tm, w), lambda i: (i, 0))

    outs = [rows(SWA_Q), rows(2 * SWA_KV),
            (jax.ShapeDtypeStruct((t // seq, SWA_KV, seq), BF16),
             pl.BlockSpec((None, SWA_KV, tm), lambda i: (i // nt, 0, i % nt))),
            (jax.ShapeDtypeStruct((t // seq, pairs, LANES, seq), BF16),
             pl.BlockSpec((None, pairs, LANES, tm), lambda i: (i // nt, 0, 0, i % nt))),
            rows(FOX_W),
            (jax.ShapeDtypeStruct((t // seq, pairs, nt, LANES, tm), BF16),
             pl.BlockSpec((None, pairs, None, LANES, tm), lambda i: (i // nt, 0, i % nt, 0, 0))),
            rows(MEM_W), rows(GATE_W), rows(LANES, F32)]
    out_shape = [o[0] for o in outs]
    out_specs = [o[1] for o in outs]
    return pl.pallas_call(
        _proj_kernel,
        grid=(t // tm,),
        in_specs=[pl.BlockSpec((tm, D_MODEL), lambda i: (i, 0)),
                  _const_spec((1, D_MODEL)),
                  _const_spec((C_END, D_MODEL)),
                  _const_spec((2 * FOX_W + SWA_KV, D_MODEL)),
                  _const_spec((1, GATE_W)),
                  _const_spec((1, SWA_Q)),
                  _const_spec((1, 2 * SWA_KV)),
                  _const_spec((FOX_W, LANES)),
                  _const_spec((1, FOX_W)),
                  _const_spec((1, MEM_W))],
        out_specs=out_specs,
        out_shape=out_shape,
        compiler_params=_params(1),
        name="proj",
    )(x2, g_mix, w1, wt, b_gate, gqa, gka, gqft, gkf, gqm)


def _split3(c):
    hi = c.astype(BF16)
    r1 = c - hi.astype(F32)
    mid = r1.astype(BF16)
    lo = (r1 - mid.astype(F32)).astype(BF16)
    return hi, mid, lo


def _pack3(x):
    hi, mid, lo = _split3(x)
    return (hi.astype(F32) + pltpu.roll(mid.astype(F32), FOX_HEADS, 1)
            + pltpu.roll(lo.astype(F32), 2 * FOX_HEADS, 1)).astype(BF16)


def _cum_kernel(fl_ref, bf_ref, tri_ref, place_ref, const_ref, aqt_ref, ak_ref, *, tm):
    n_tiles = fl_ref.shape[0] // tm
    lane = lax.broadcasted_iota(jnp.int32, (tm, LANES), 1)
    heads = lane < FOX_HEADS
    tri = tri_ref[...]
    parts = []
    for k in range(n_tiles):
        z = fl_ref[k * tm:(k + 1) * tm, :] + bf_ref[...]
        logf = jnp.where(heads, jnp.minimum(z, 0.0) - jnp.log(1.0 + jnp.exp(-jnp.abs(z))), 0.0)
        parts.append(_dot(tri, _pack3(logf)))
    carry = jnp.zeros((1, LANES), F32)
    packed = []
    for part in parts:
        c = ((part + pltpu.roll(part, LANES - FOX_HEADS, 1))
             + pltpu.roll(part, LANES - 2 * FOX_HEADS, 1)) + carry
        carry = c[tm - 1:tm, :]
        packed.append(_pack3(jnp.where(heads, c * LOG2E, 0.0)))
    for k in range(n_tiles):
        placed = _dot(packed[k], place_ref[...])
        rows = slice(k * tm, (k + 1) * tm)
        aqt_ref[:, rows] = (placed[:, :LANES] + const_ref[0:1, :LANES]).T.astype(BF16)
        ak_ref[rows, :] = (const_ref[0:1, LANES:] - placed[:, LANES:]).astype(BF16)


def _cum(fl, bf_pad, tri, place, consts, tm, seq):
    t = fl.shape[0]
    return pl.pallas_call(
        functools.partial(_cum_kernel, tm=tm),
        grid=(t // seq,),
        in_specs=[pl.BlockSpec((seq, LANES), lambda i: (i, 0)),
                  _const_spec((1, LANES)),
                  _const_spec((tm, tm)),
                  _const_spec((LANES, 2 * LANES)),
                  _const_spec((8, 2 * LANES))],
        out_specs=[pl.BlockSpec((None, LANES, seq), lambda i: (i, 0, 0)),
                   pl.BlockSpec((seq, LANES), lambda i: (i, 0))],
        out_shape=[jax.ShapeDtypeStruct((t // seq, LANES, seq), BF16),
                   jax.ShapeDtypeStruct((t, LANES), BF16)],
        compiler_params=_params(1),
        name="cum",
    )(fl, bf_pad, tri, place, consts)


def _local_kernel(bucket_ref, relb_ref, sink_ref,
                  qa_ref, kc_ref, kp_ref, vc_ref, vp_ref, qm_ref, mk_ref, mvt_ref,
                  oa_ref, om_ref, tbl_ref, *, tq):
    b = pl.program_id(0)
    i = pl.program_id(1)
    pairs = SWA_HEADS // 2

    @pl.when(jnp.logical_and(b == 0, i == 0))
    def _():
        bucket = bucket_ref[...]
        band = bucket >= 0
        for h in range(SWA_HEADS):
            t = jnp.zeros(bucket.shape, F32)
            for k in range(REL_BUCKETS):
                t = jnp.where(bucket == k, relb_ref[k, h], t)
            tbl_ref[h // 2, :, (h % 2) * BLOCK:(h % 2 + 1) * BLOCK] = jnp.where(band, t, NEG)

    lane = lax.broadcasted_iota(jnp.int32, (BLOCK, LANES), 1)
    lower = lane < HEAD64
    first_head = lax.broadcasted_iota(jnp.int32, (1, 2 * BLOCK), 1) < BLOCK
    key_row = lax.broadcasted_iota(jnp.int32, (2 * BLOCK, 2 * BLOCK), 0)
    pad_rows = jnp.logical_and(key_row < BLOCK, i == 0)
    ones = jnp.ones((BF16_ROWS, 2 * BLOCK), BF16)

    def windows(j):
        r0 = j * BLOCK
        if j == 0:
            return (jnp.concatenate([kp_ref[...], kc_ref[0:BLOCK, :]], axis=0),
                    jnp.concatenate([vp_ref[...], vc_ref[:, 0:BLOCK]], axis=1))
        return kc_ref[r0 - BLOCK:r0 + BLOCK, :], vc_ref[:, r0 - BLOCK:r0 + BLOCK]

    def swa_scores(j, hb):
        g = (2 * hb) // SWA_GROUP
        qblk = qa_ref[j * BLOCK:(j + 1) * BLOCK, hb * LANES:(hb + 1) * LANES]
        qstack = jnp.concatenate([jnp.where(lower, qblk, 0),
                                  jnp.where(lower, 0, qblk)], axis=0)
        s = _dot_nt(windows(j)[0][:, g * LANES:(g + 1) * LANES], qstack) + tbl_ref[hb]
        return jnp.where(pad_rows, NEG, s) if j == 0 else s

    def swa_finish(j, hb, s):
        g = (2 * hb) // SWA_GROUP
        sink = jnp.where(first_head, sink_ref[2 * hb], sink_ref[2 * hb + 1])
        m = jnp.maximum(jnp.max(s, axis=0, keepdims=True), sink)
        p = jnp.exp2(s - m).astype(BF16)
        vt_ones = jnp.concatenate([windows(j)[1][g * HEAD64:(g + 1) * HEAD64, :], ones], axis=0)
        acc = _dot(vt_ones, p)
        o = acc[:HEAD64, :] / (acc[HEAD64:HEAD64 + 1, :] + jnp.exp2(sink - m))
        o = jnp.concatenate([o[:, :BLOCK], o[:, BLOCK:]], axis=0)
        oa_ref[j * BLOCK:(j + 1) * BLOCK, hb * LANES:(hb + 1) * LANES] = o.T.astype(BF16)

    ones_m = jnp.ones((BF16_ROWS, N_MEM), BF16)

    def mem_scores(h):
        sl = slice(h * MEM_HEAD_DIM, (h + 1) * MEM_HEAD_DIM)
        return _dot_nt(mk_ref[:, sl], qm_ref[:, sl])

    def mem_finish(h, s):
        sl = slice(h * MEM_HEAD_DIM, (h + 1) * MEM_HEAD_DIM)
        m = jnp.max(s, axis=0, keepdims=True)
        p = jnp.exp2(s - m).astype(BF16)
        acc = _dot(jnp.concatenate([mvt_ref[sl, :], ones_m], axis=0), p)
        o = acc[:MEM_HEAD_DIM, :] / acc[MEM_HEAD_DIM:MEM_HEAD_DIM + 1, :]
        om_ref[:, sl] = o.T.astype(BF16)

    units = [(swa_scores, swa_finish, (j, hb)) for j in range(tq // BLOCK) for hb in range(pairs)]
    units += [(mem_scores, mem_finish, (h,)) for h in range(MEM_HEADS)]
    pending = []
    for n in range(len(units) + LOCAL_LOOKAHEAD):
        if n < len(units):
            score_fn, _, args = units[n]
            pending.append(score_fn(*args))
        if n >= LOCAL_LOOKAHEAD:
            _, finish_fn, args = units[n - LOCAL_LOOKAHEAD]
            finish_fn(*args, pending[n - LOCAL_LOOKAHEAD])


def _local(bucket_t, rel_bias, sinks, qa, ka, vat, qm, mk, mvt, batch, seq, tq):
    t = qa.shape[0]
    nq = seq // tq
    sub = tq // BLOCK

    def cur(b, i):
        return (b * nq + i, 0)

    def prev(b, i):
        return (jnp.maximum((b * nq + i) * sub - 1, 0), 0)

    smem = pl.BlockSpec(memory_space=pltpu.SMEM)
    return pl.pallas_call(
        functools.partial(_local_kernel, tq=tq),
        grid=(batch, nq),
        in_specs=[_const_spec((2 * BLOCK, BLOCK)), smem, smem,
                  pl.BlockSpec((tq, SWA_Q), cur),
                  pl.BlockSpec((tq, 2 * SWA_KV), cur),
                  pl.BlockSpec((BLOCK, 2 * SWA_KV), prev),
                  pl.BlockSpec((None, SWA_KV, tq), lambda b, i: (b, 0, i)),
                  pl.BlockSpec((None, SWA_KV, BLOCK),
                               lambda b, i: (b, 0, jnp.maximum(i * sub - 1, 0))),
                  pl.BlockSpec((tq, MEM_W), cur),
                  pl.BlockSpec((N_MEM, MEM_W), lambda b, i: (b, 0)),
                  pl.BlockSpec((None, MEM_W, N_MEM), lambda b, i: (b, 0, 0))],
        out_specs=[pl.BlockSpec((tq, SWA_Q), cur),
                   pl.BlockSpec((tq, MEM_W), cur)],
        out_shape=[jax.ShapeDtypeStruct((t, SWA_Q), BF16),
                   jax.ShapeDtypeStruct((t, MEM_W), BF16)],
        scratch_shapes=[pltpu.VMEM((SWA_HEADS // 2, 2 * BLOCK, 2 * BLOCK), F32)],
        compiler_params=_params(2),
        name="local",
    )(bucket_t, rel_bias, sinks, qa, ka, ka, vat, vat, qm, mk, mvt)


def _fox_kernel(fast_ref, qt_ref, aqt_ref, k_ref, ak_ref, vt_ref, o_ref,
                sa_ref, sb_ref, acc_ref, m_ref, *, tk, n_sub, n_steps):
    pair = pl.program_id(1)
    g = pl.program_id(2)
    row = lax.broadcasted_iota(jnp.int32, (LANES, tk), 0)
    half = [row < HEAD64, row >= HEAD64]
    qcat = []
    for sub in range(n_sub):
        q = qt_ref[:, sub * tk:(sub + 1) * tk]
        aq = aqt_ref[:, sub * tk:(sub + 1) * tk]
        zero = jnp.zeros_like(q)
        per_head = []
        for h in range(2):
            lo = (2 * pair + h) * AUG_PER_HEAD
            mine = jnp.logical_and(row >= lo, row < lo + AUG_PER_HEAD)
            per_head.append(jnp.concatenate([jnp.where(half[h], q, zero),
                                             jnp.where(mine, aq, zero)], axis=0))
        qcat.append(per_head)
    ones = jnp.ones((BF16_ROWS, tk), BF16)
    acc_ref[...] = jnp.zeros_like(acc_ref)
    key = lax.broadcasted_iota(jnp.int32, (tk, tk), 0)
    qry = lax.broadcasted_iota(jnp.int32, (tk, tk), 1)
    causal = key <= qry

    def kcat(j):
        r0 = pl.multiple_of(j * tk, tk)
        return jnp.concatenate([k_ref[pl.ds(r0, tk), :], ak_ref[pl.ds(r0, tk), :]], axis=1)

    def vt_ones(j, h):
        return jnp.concatenate([vt_ref[j, h * HEAD64:(h + 1) * HEAD64, :], ones], axis=0)

    def visible_keys(c, diagonal):
        return min(tk, c + MXU_DIM) if diagonal else tk

    def qk(sub, j, s_ref, diagonal=False):
        kc = kcat(j)
        for h in range(2):
            for c in range(0, tk, MXU_DIM):
                nk = visible_keys(c, diagonal)
                s_ref[h, :nk, c:c + MXU_DIM] = _dot(kc[:nk], qcat[sub][h][:, c:c + MXU_DIM])

    @pl.when(fast_ref[0] == 1)
    def _():
        bufs = (sa_ref, sb_ref)

        def exp_pv(sub, j, s_ref, masked):
            for h in range(2):
                for c in range(0, tk, MXU_DIM):
                    nk = visible_keys(c, masked)
                    s = s_ref[h, :nk, c:c + MXU_DIM]
                    if masked:
                        s = jnp.where(causal[:nk, c:c + MXU_DIM], s, NEG)
                    acc_ref[2 * sub + h, :, c:c + MXU_DIM] += _dot(vt_ones(j, h)[:, :nk],
                                                                   jnp.exp2(s).astype(BF16))

        full = [(sub, jj) for jj in range(n_sub) for sub in range(n_sub)]
        diag = [(sub, jj) for jj in range(n_sub) for sub in range(jj, n_sub)]
        assert len(full) % 2 == 0
        qk(0, 0, bufs[0])

        def body(t, carry):
            for n, (sub, jj) in enumerate(full):
                nsub, njj = full[n + 1] if n + 1 < len(full) else (0, n_sub)
                qk(nsub, n_sub * t + njj, bufs[(n + 1) % 2])
                exp_pv(sub, n_sub * t + jj, bufs[n % 2], False)
            return carry

        if n_steps > 1:
            lax.fori_loop(0, g, body, 0)
        for n, (sub, jj) in enumerate(diag):
            if n + 1 < len(diag):
                nsub, njj = diag[n + 1]
                qk(nsub, n_sub * g + njj, bufs[(n + 1) % 2], diagonal=nsub == njj)
            exp_pv(sub, n_sub * g + jj, bufs[n % 2], sub == jj)

    @pl.when(fast_ref[0] == 0)
    def _():
        m_ref[...] = jnp.full(m_ref.shape, NEG, F32)
        for sub in range(n_sub):
            last = n_sub * g + sub

            def body(j, carry, sub=sub, last=last):
                qk(sub, j, sa_ref)
                for h in range(2):
                    u = 2 * sub + h
                    s = jnp.where(jnp.logical_or(causal, j < last), sa_ref[h], NEG)
                    m_prev = m_ref[u]
                    m_next = jnp.maximum(m_prev, jnp.max(s, axis=0, keepdims=True))
                    alpha = jnp.exp2(m_prev - m_next)
                    p = jnp.exp2(s - m_next).astype(BF16)
                    acc_ref[u] = alpha * acc_ref[u] + _dot(vt_ones(j, h), p)
                    m_ref[u] = m_next
                return carry

            lax.fori_loop(0, last + 1, body, 0)

    for sub in range(n_sub):
        halves = []
        for h in range(2):
            acc = acc_ref[2 * sub + h]
            halves.append(acc[:HEAD64, :] / acc[HEAD64:HEAD64 + 1, :])
        o_ref[sub * tk:(sub + 1) * tk, :] = jnp.concatenate(halves, axis=0).T.astype(BF16)


def _fox(fast, qft, aqt, kf, ak, vt, batch, seq, tk):
    t = kf.shape[0]
    n_sub = min(FOX_SUB, seq // tk)
    tq = n_sub * tk
    assert seq % tq == 0, (seq, tq)
    nq = seq // tq
    pairs = FOX_HEADS // 2
    return pl.pallas_call(
        functools.partial(_fox_kernel, tk=tk, n_sub=n_sub, n_steps=nq),
        grid=(batch, pairs, nq),
        in_specs=[pl.BlockSpec(memory_space=pltpu.SMEM),
                  pl.BlockSpec((None, None, LANES, tq), lambda b, p, i: (b, p, 0, i)),
                  pl.BlockSpec((None, LANES, tq), lambda b, p, i: (b, 0, i)),
                  pl.BlockSpec((seq, LANES), lambda b, p, i: (b, p)),
                  pl.BlockSpec((seq, LANES), lambda b, p, i: (b, 0)),
                  pl.BlockSpec((None, None, seq // tk, LANES, tk),
                               lambda b, p, i: (b, p, 0, 0, 0))],
        out_specs=pl.BlockSpec((tq, LANES), lambda b, p, i: (b * nq + i, p)),
        out_shape=jax.ShapeDtypeStruct((t, FOX_W), BF16),
        scratch_shapes=[pltpu.VMEM((2, tk, tk), F32),
                        pltpu.VMEM((2, tk, tk), F32),
                        pltpu.VMEM((2 * n_sub, HEAD64 + BF16_ROWS, tk), F32),
                        pltpu.VMEM((2 * n_sub, 1, tk), F32)],
        compiler_params=_params(3),
        name="fox",
    )(fast, qft, aqt, kf, ak, vt)


def _post_kernel(x_ref, oa_ref, of_ref, om_ref, gate_ref, wa_ref, wf_ref, wm_ref, wo_ref,
                 g_ref, wu_ref, wd_ref, y_ref, *, chunk):
    ya = _dot(oa_ref[...], wa_ref[...])
    yf = _dot(of_ref[...], wf_ref[...])
    ym = _dot(om_ref[...], wm_ref[...])
    merged = (gate_ref[:, 0:D_MODEL].astype(F32) * ya
              + gate_ref[:, D_MODEL:2 * D_MODEL].astype(F32) * yf
              + gate_ref[:, 2 * D_MODEL:].astype(F32) * ym)
    x1 = x_ref[...] + _dot(merged.astype(BF16), wo_ref[...])
    h = _rms_rows(x1, g_ref[...]).astype(BF16)
    acc = x1
    for c in range(0, D_FF, chunk):
        u = jnp.maximum(_dot(h, wu_ref[:, c:c + chunk]), 0.0)
        acc = acc + _dot((u * u).astype(BF16), wd_ref[c:c + chunk, :])
    y_ref[...] = acc


def _post(x2, oa, of, om, gates, wa, wf, wm, wo, g_mlp, wu, wd, tm, chunk=1024):
    t = x2.shape[0]
    row = lambda w: pl.BlockSpec((tm, w), lambda i: (i, 0))
    resident = lambda shape: pl.BlockSpec(shape, lambda i: (0,) * len(shape),
                                          pipeline_mode=pl.Buffered(1))
    return pl.pallas_call(
        functools.partial(_post_kernel, chunk=chunk),
        grid=(t // tm,),
        in_specs=[row(D_MODEL), row(SWA_Q), row(FOX_W), row(MEM_W), row(GATE_W),
                  resident((SWA_Q, D_MODEL)), resident((FOX_W, D_MODEL)),
                  resident((MEM_W, D_MODEL)), resident((D_MODEL, D_MODEL)),
                  resident((1, D_MODEL)), resident((D_MODEL, D_FF)), resident((D_FF, D_MODEL))],
        out_specs=row(D_MODEL),
        out_shape=jax.ShapeDtypeStruct((t, D_MODEL), F32),
        compiler_params=_params(1),
        name="post",
    )(x2, oa, of, om, gates, wa, wf, wm, wo, g_mlp, wu, wd)


def _t5_bucket_table():
    max_exact = REL_BUCKETS // 2
    t_loc = np.arange(BLOCK)[:, None] + BLOCK
    s_loc = np.arange(2 * BLOCK)[None, :]
    dist = t_loc - s_loc
    d = np.maximum(dist, 0)
    df = np.maximum(d, 1).astype(np.float32)
    scaled = (np.log(df / np.float32(max_exact)) / np.float32(math.log(REL_MAX_DIST / max_exact))
              * np.float32(REL_BUCKETS - max_exact))
    large = np.minimum(max_exact + scaled.astype(np.int32), REL_BUCKETS - 1)
    bucket = np.where(d < max_exact, d, large)
    band = (dist >= 0) & (dist < WINDOW)
    return jnp.asarray(np.where(band, bucket, -1).astype(np.int32).T)


def _aug_placement(shift):
    place = [[0.0] * (2 * LANES) for _ in range(LANES)]
    ones = [0.0] * (2 * LANES)
    shift_lanes = []
    for h in range(FOX_HEADS):
        base = h * AUG_PER_HEAD
        for part in range(3):
            src = part * FOX_HEADS + h
            place[src][base + part] = 1.0
            place[src][LANES + base + 3 + part] = 1.0
            ones[base + 3 + part] = 1.0
            ones[LANES + base + part] = 1.0
        ones[base + 6] = 1.0
        shift_lanes.append(LANES + base + 6)
    consts = jnp.array(ones, F32).at[jnp.array(shift_lanes)].set(-shift)
    return jnp.array(place, BF16), jnp.broadcast_to(consts[None, :], (8, 2 * LANES))


def _tile_gain(g, reps, scale=1.0):
    return (jnp.tile(g.astype(F32), reps) * scale)[None, :]


def _pick_tile(n, target):
    t = min(n, target)
    while n % t:
        t //= 2
    return t


def kernel(x, mem, g_mix, w_in, b_gate, b_forget, qn_swa, kn_swa, sink_swa, rel_bias,
           qn_fox, kn_fox, g_mem, w_mem_kv, qn_mem, kn_mem, w_o_swa, w_o_fox, w_o_mem,
           w_out, g_mlp, w_mlp_up, w_mlp_down):
    batch, seq, _ = x.shape
    n_layers = w_in.shape[0]
    t = batch * seq
    tm = _pick_tile(seq, 512)
    tq = _pick_tile(seq, 2048)

    bucket = _t5_bucket_table()
    tri = (jnp.arange(tm)[:, None] >= jnp.arange(tm)[None, :]).astype(BF16)

    x2 = x.reshape(t, D_MODEL)
    mem2 = mem.reshape(batch * N_MEM, D_MODEL)
    for layer in range(n_layers):
        wt = w_in[layer].T
        o = 0
        parts = {}
        for name, width in (("qa", SWA_Q), ("ka", SWA_KV), ("va", SWA_KV), ("qf", FOX_W),
                            ("kf", FOX_W), ("vf", FOX_W), ("fl", FOX_HEADS), ("qm", MEM_W),
                            ("g", GATE_W)):
            parts[name] = wt[o:o + width]
            o += width

        def dup(m):
            return jnp.concatenate([m[:HEAD64], m[:HEAD64], m[HEAD64:], m[HEAD64:]], axis=0)

        w1 = jnp.concatenate(
            [parts["qa"], dup(parts["ka"]), parts["kf"], parts["qm"], parts["g"],
             jnp.pad(parts["fl"], ((0, LANES - FOX_HEADS), (0, 0)))], axis=0).astype(BF16)

        wmt = w_mem_kv[layer].T.astype(BF16)
        mk, mvt = _memkv(mem2, g_mem[layer][None, :], wmt[:MEM_W], wmt[MEM_W:],
                         _tile_gain(kn_mem[layer], MEM_HEADS))

        bound = (HEAD64 ** 0.5 * LOG2E * 1.02 * jnp.max(jnp.abs(qn_fox[layer]))
                 * jnp.max(jnp.abs(kn_fox[layer]))).astype(BF16).astype(F32)
        fast = bound <= FOX_FAST_MAX_SHIFT
        place, consts = _aug_placement(jnp.where(fast, bound, 0.0))
        bf_pad = jnp.pad(b_forget[layer].astype(F32), (0, LANES - FOX_HEADS))[None, :]

        wt = jnp.concatenate([parts["vf"], parts["va"], parts["qf"]], axis=0).astype(BF16)
        gqft = jnp.broadcast_to(
            _tile_gain(qn_fox[layer], FOX_HEADS, HEAD64 ** -0.5 * LOG2E).T, (FOX_W, LANES))
        qa, ka, vat, qft, kf, vt, qm, gates, fl = _proj(
            x2, g_mix[layer][None, :], w1, wt, b_gate[layer][None, :],
            _tile_gain(qn_swa[layer], SWA_HEADS, HEAD64 ** -0.5 * LOG2E),
            _tile_gain(kn_swa[layer], 2 * SWA_KV_HEADS),
            gqft,
            _tile_gain(kn_fox[layer], FOX_HEADS),
            _tile_gain(qn_mem[layer], MEM_HEADS, MEM_HEAD_DIM ** -0.5 * LOG2E), tm, seq)
        aqt, ak = _cum(fl, bf_pad, tri, place, consts, tm, seq)

        oa, om = _local(bucket, rel_bias.astype(F32) * LOG2E, sink_swa[layer].astype(F32) * LOG2E,
                        qa, ka, vat, qm, mk, mvt, batch, seq, tq)
        of = _fox(fast.astype(jnp.int32)[None], qft, aqt, kf, ak, vt, batch, seq, tm)

        x2 = _post(x2, oa, of, om, gates, w_o_swa[layer].astype(BF16),
                   w_o_fox[layer].astype(BF16), w_o_mem[layer].astype(BF16),
                   w_out[layer].astype(BF16), g_mlp[layer][None, :],
                   w_mlp_up[layer].astype(BF16), w_mlp_down[layer].astype(BF16), tm)
    return x2.reshape(batch, seq, D_MODEL)
```

```python
import functools
import math

import jax
import jax.numpy as jnp
import numpy as np
from jax import lax
from jax.experimental import pallas as pl
from jax.experimental.pallas import tpu as pltpu

F32 = jnp.float32
BF16 = jnp.bfloat16

D_MODEL = 1024
N_MEM = 256
SWA_HEADS = 8
SWA_KV_HEADS = 2
SWA_GROUP = SWA_HEADS // SWA_KV_HEADS
HEAD64 = 64
WINDOW = 128
FOX_HEADS = 8
MEM_HEADS = 4
MEM_HEAD_DIM = 128
D_FF = 4 * D_MODEL
REL_BUCKETS = 32
REL_MAX_DIST = 128
BLOCK = 128
EPS = 1e-6
NEG = -1e30
LOG2E = math.log2(math.e)

LANES = 128
MXU_DIM = 256
BF16_ROWS = 16
VMEM_LIMIT = 56 * 1024 * 1024

SWA_Q = SWA_HEADS * HEAD64
SWA_KV = SWA_KV_HEADS * HEAD64
FOX_W = FOX_HEADS * HEAD64
MEM_W = MEM_HEADS * MEM_HEAD_DIM
GATE_W = 3 * D_MODEL

C_QA = 0
C_KA = C_QA + SWA_Q
C_KF = C_KA + 2 * SWA_KV
C_QM = C_KF + FOX_W
C_G = C_QM + MEM_W
C_FL = C_G + GATE_W
C_END = C_FL + LANES

AUG_PER_HEAD = 7
FOX_FAST_MAX_SHIFT = 60.0
FOX_SUB = 4
LOCAL_LOOKAHEAD = 4


def _dot(a, b):
    return jnp.dot(a, b, preferred_element_type=F32)


def _dot_nt(a, b):
    return lax.dot_general(a, b, (((1,), (1,)), ((), ())), preferred_element_type=F32)


def _rms_rows(x, gain):
    ms = jnp.mean(x * x, axis=-1, keepdims=True)
    return x * lax.rsqrt(ms + EPS) * gain


def _head_rms(y, gain, head_dim):
    y2 = y * y
    lower = lax.broadcasted_iota(jnp.int32, (y.shape[0], LANES), 1) < HEAD64
    cols = []
    for c in range(0, y.shape[1], LANES):
        blk = y2[:, c:c + LANES]
        total = jnp.sum(blk, axis=-1, keepdims=True)
        if head_dim == LANES:
            cols.append(jnp.broadcast_to(total, blk.shape))
        else:
            low = jnp.sum(jnp.where(lower, blk, 0.0), axis=-1, keepdims=True)
            cols.append(jnp.where(lower, low, total - low))
    ss = jnp.concatenate(cols, axis=1)
    return y * lax.rsqrt(ss * (1.0 / head_dim) + EPS) * gain


def _const_spec(shape):
    return pl.BlockSpec(shape, lambda *_: (0,) * len(shape))


def _params(n_axes):
    return pltpu.CompilerParams(dimension_semantics=("arbitrary",) * n_axes,
                                vmem_limit_bytes=VMEM_LIMIT)


def _proj_kernel(x_ref, g_ref, w_ref, wt_ref, bg_ref,
                 gqa_ref, gka_ref, gqft_ref, gkf_ref, gqm_ref,
                 qa_ref, ka_ref, vat_ref, qft_ref, kf_ref, vt_ref, qm_ref, gate_ref, fl_ref):
    h = _rms_rows(x_ref[...], g_ref[...]).astype(BF16)
    tm = h.shape[0]
    qkv = _dot_nt(h, w_ref[:C_G, :])
    qa_ref[...] = _head_rms(qkv[:, C_QA:C_KA], gqa_ref[...], HEAD64).astype(BF16)
    ka_ref[...] = _head_rms(qkv[:, C_KA:C_KF], gka_ref[...], HEAD64).astype(BF16)
    kf_ref[...] = _head_rms(qkv[:, C_KF:C_QM], gkf_ref[...], HEAD64).astype(BF16)
    qm_ref[...] = _head_rms(qkv[:, C_QM:C_G], gqm_ref[...], MEM_HEAD_DIM).astype(BF16)
    t = _dot_nt(wt_ref[...], h)
    for p in range(FOX_HEADS // 2):
        vt_ref[p] = t[p * LANES:(p + 1) * LANES, :].astype(BF16)
    vat_ref[...] = t[FOX_W:FOX_W + SWA_KV, :].astype(BF16)
    gain_t = jnp.tile(gqft_ref[...], (1, tm // LANES))
    for p in range(FOX_HEADS // 2):
        rows = []
        for e in range(2):
            r0 = FOX_W + SWA_KV + (2 * p + e) * HEAD64
            y = t[r0:r0 + HEAD64, :]
            ms = jnp.sum(y * y, axis=0, keepdims=True) * (1.0 / HEAD64)
            rows.append(y * lax.rsqrt(ms + EPS) * gain_t[r0 - FOX_W - SWA_KV:r0 - FOX_W - SWA_KV
                                                         + HEAD64, :])
        qft_ref[p] = jnp.concatenate(rows, axis=0).astype(BF16)
    zf = _dot_nt(h, w_ref[C_G:C_END, :])
    z = zf[:, :GATE_W] + bg_ref[...]
    gate_ref[...] = (0.5 * jnp.tanh(0.5 * z) + 0.5).astype(BF16)
    fl_ref[...] = zf[:, GATE_W:]


def _proj(x2, g_mix, w1, wt, b_gate, gqa, gka, gqft, gkf, gqm, tm, seq):
    t = x2.shape[0]
    nt = seq // tm
    pairs = FOX_HEADS // 2

    def rows(w, dtype=BF16):
        return jax.ShapeDtypeStruct((t, w), dtype), pl.BlockSpec((tm, w), lambda i: (i, 0))

    outs = [rows(SWA_Q), rows(2 * SWA_KV),
            (jax.ShapeDtypeStruct((t // seq, SWA_KV, seq), BF16),
             pl.BlockSpec((None, SWA_KV, tm), lambda i: (i // nt, 0, i % nt))),
            (jax.ShapeDtypeStruct((t // seq, pairs, LANES, seq), BF16),
             pl.BlockSpec((None, pairs, LANES, tm), lambda i: (i // nt, 0, 0, i % nt))),
            rows(FOX_W),
            (jax.ShapeDtypeStruct((t // seq, pairs, nt, LANES, tm), BF16),
             pl.BlockSpec((None, pairs, None, LANES, tm), lambda i: (i // nt, 0, i % nt, 0, 0))),
            rows(MEM_W), rows(GATE_W), rows(LANES, F32)]
    out_shape = [o[0] for o in outs]
    out_specs = [o[1] for o in outs]
    return pl.pallas_call(
        _proj_kernel,
        grid=(t // tm,),
        in_specs=[pl.BlockSpec((tm, D_MODEL), lambda i: (i, 0)),
                  _const_spec((1, D_MODEL)),
                  _const_spec((C_END, D_MODEL)),
                  _const_spec((2 * FOX_W + SWA_KV, D_MODEL)),
                  _const_spec((1, GATE_W)),
                  _const_spec((1, SWA_Q)),
                  _const_spec((1, 2 * SWA_KV)),
                  _const_spec((FOX_W, LANES)),
                  _const_spec((1, FOX_W)),
                  _const_spec((1, MEM_W))],
        out_specs=out_specs,
        out_shape=out_shape,
        compiler_params=_params(1),
        name="proj",
    )(x2, g_mix, w1, wt, b_gate, gqa, gka, gqft, gkf, gqm)


def _split3(c):
    hi = c.astype(BF16)
    r1 = c - hi.astype(F32)
    mid = r1.astype(BF16)
    lo = (r1 - mid.astype(F32)).astype(BF16)
    return hi, mid, lo


def _pack3(x):
    hi, mid, lo = _split3(x)
    return (hi.astype(F32) + pltpu.roll(mid.astype(F32), FOX_HEADS, 1)
            + pltpu.roll(lo.astype(F32), 2 * FOX_HEADS, 1)).astype(BF16)


def _cum_kernel(fl_ref, bf_ref, tri_ref, place_ref, const_ref, mem_ref, gm_ref, wkt_ref, wvt_ref,
                kn_ref, aqt_ref, ak_ref, mk_ref, mvt_ref, *, tm):
    hm = _rms_rows(mem_ref[...], gm_ref[...]).astype(BF16)
    mk_ref[...] = _head_rms(_dot_nt(hm, wkt_ref[...]), kn_ref[...], MEM_HEAD_DIM).astype(BF16)
    mvt_ref[...] = _dot_nt(wvt_ref[...], hm).astype(BF16)

    n_tiles = fl_ref.shape[0] // tm
    lane = lax.broadcasted_iota(jnp.int32, (tm, LANES), 1)
    heads = lane < FOX_HEADS
    tri = tri_ref[...]
    parts = []
    for k in range(n_tiles):
        z = fl_ref[k * tm:(k + 1) * tm, :] + bf_ref[...]
        logf = jnp.where(heads, jnp.minimum(z, 0.0) - jnp.log(1.0 + jnp.exp(-jnp.abs(z))), 0.0)
        parts.append(_dot(tri, _pack3(logf)))
    carry = jnp.zeros((1, LANES), F32)
    packed = []
    for part in parts:
        c = ((part + pltpu.roll(part, LANES - FOX_HEADS, 1))
             + pltpu.roll(part, LANES - 2 * FOX_HEADS, 1)) + carry
        carry = c[tm - 1:tm, :]
        packed.append(_pack3(jnp.where(heads, c * LOG2E, 0.0)))
    for k in range(n_tiles):
        placed = _dot(packed[k], place_ref[...])
        rows = slice(k * tm, (k + 1) * tm)
        aqt_ref[:, rows] = (placed[:, :LANES] + const_ref[0:1, :LANES]).T.astype(BF16)
        ak_ref[rows, :] = (const_ref[0:1, LANES:] - placed[:, LANES:]).astype(BF16)


def _cum(fl, bf_pad, tri, place, consts, mem2, g_mem, wkt, wvt, kn_t, tm, seq):
    t = fl.shape[0]
    nb = t // seq
    return pl.pallas_call(
        functools.partial(_cum_kernel, tm=tm),
        grid=(nb,),
        in_specs=[pl.BlockSpec((seq, LANES), lambda i: (i, 0)),
                  _const_spec((1, LANES)),
                  _const_spec((tm, tm)),
                  _const_spec((LANES, 2 * LANES)),
                  _const_spec((8, 2 * LANES)),
                  pl.BlockSpec((N_MEM, D_MODEL), lambda i: (i, 0)),
                  _const_spec((1, D_MODEL)),
                  _const_spec((MEM_W, D_MODEL)),
                  _const_spec((MEM_W, D_MODEL)),
                  _const_spec((1, MEM_W))],
        out_specs=[pl.BlockSpec((None, LANES, seq), lambda i: (i, 0, 0)),
                   pl.BlockSpec((seq, LANES), lambda i: (i, 0)),
                   pl.BlockSpec((N_MEM, MEM_W), lambda i: (i, 0)),
                   pl.BlockSpec((None, MEM_W, N_MEM), lambda i: (i, 0, 0))],
        out_shape=[jax.ShapeDtypeStruct((nb, LANES, seq), BF16),
                   jax.ShapeDtypeStruct((t, LANES), BF16),
                   jax.ShapeDtypeStruct((nb * N_MEM, MEM_W), BF16),
                   jax.ShapeDtypeStruct((nb, MEM_W, N_MEM), BF16)],
        compiler_params=_params(1),
        name="cum",
    )(fl, bf_pad, tri, place, consts, mem2, g_mem, wkt, wvt, kn_t)


def _local_kernel(bucket_ref, relb_ref, sink_ref,
                  qa_ref, kc_ref, kp_ref, vc_ref, vp_ref, qm_ref, mk_ref, mvt_ref,
                  oa_ref, om_ref, tbl_ref, *, tq):
    b = pl.program_id(0)
    i = pl.program_id(1)
    pairs = SWA_HEADS // 2

    @pl.when(jnp.logical_and(b == 0, i == 0))
    def _():
        bucket = bucket_ref[...]
        band = bucket >= 0
        for h in range(SWA_HEADS):
            t = jnp.zeros(bucket.shape, F32)
            for k in range(REL_BUCKETS):
                t = jnp.where(bucket == k, relb_ref[k, h], t)
            tbl_ref[h // 2, :, (h % 2) * BLOCK:(h % 2 + 1) * BLOCK] = jnp.where(band, t, NEG)

    lane = lax.broadcasted_iota(jnp.int32, (BLOCK, LANES), 1)
    lower = lane < HEAD64
    first_head = lax.broadcasted_iota(jnp.int32, (1, 2 * BLOCK), 1) < BLOCK
    key_row = lax.broadcasted_iota(jnp.int32, (2 * BLOCK, 2 * BLOCK), 0)
    pad_rows = jnp.logical_and(key_row < BLOCK, i == 0)
    ones = jnp.ones((BF16_ROWS, 2 * BLOCK), BF16)

    def windows(j):
        r0 = j * BLOCK
        if j == 0:
            return (jnp.concatenate([kp_ref[...], kc_ref[0:BLOCK, :]], axis=0),
                    jnp.concatenate([vp_ref[...], vc_ref[:, 0:BLOCK]], axis=1))
        return kc_ref[r0 - BLOCK:r0 + BLOCK, :], vc_ref[:, r0 - BLOCK:r0 + BLOCK]

    def swa_scores(j, hb):
        g = (2 * hb) // SWA_GROUP
        qblk = qa_ref[j * BLOCK:(j + 1) * BLOCK, hb * LANES:(hb + 1) * LANES]
        qstack = jnp.concatenate([jnp.where(lower, qblk, 0),
                                  jnp.where(lower, 0, qblk)], axis=0)
        s = _dot_nt(windows(j)[0][:, g * LANES:(g + 1) * LANES], qstack) + tbl_ref[hb]
        return jnp.where(pad_rows, NEG, s) if j == 0 else s

    def swa_finish(j, hb, s):
        g = (2 * hb) // SWA_GROUP
        sink = jnp.where(first_head, sink_ref[2 * hb], sink_ref[2 * hb + 1])
        m = jnp.maximum(jnp.max(s, axis=0, keepdims=True), sink)
        p = jnp.exp2(s - m).astype(BF16)
        vt_ones = jnp.concatenate([windows(j)[1][g * HEAD64:(g + 1) * HEAD64, :], ones], axis=0)
        acc = _dot(vt_ones, p)
        o = acc[:HEAD64, :] / (acc[HEAD64:HEAD64 + 1, :] + jnp.exp2(sink - m))
        o = jnp.concatenate([o[:, :BLOCK], o[:, BLOCK:]], axis=0)
        oa_ref[j * BLOCK:(j + 1) * BLOCK, hb * LANES:(hb + 1) * LANES] = o.T.astype(BF16)

    ones_m = jnp.ones((BF16_ROWS, N_MEM), BF16)

    def mem_scores(h):
        sl = slice(h * MEM_HEAD_DIM, (h + 1) * MEM_HEAD_DIM)
        return _dot_nt(mk_ref[:, sl], qm_ref[:, sl])

    def mem_finish(h, s):
        sl = slice(h * MEM_HEAD_DIM, (h + 1) * MEM_HEAD_DIM)
        m = jnp.max(s, axis=0, keepdims=True)
        p = jnp.exp2(s - m).astype(BF16)
        acc = _dot(jnp.concatenate([mvt_ref[sl, :], ones_m], axis=0), p)
        o = acc[:MEM_HEAD_DIM, :] / acc[MEM_HEAD_DIM:MEM_HEAD_DIM + 1, :]
        om_ref[:, sl] = o.T.astype(BF16)

    units = [(swa_scores, swa_finish, (j, hb)) for j in range(tq // BLOCK) for hb in range(pairs)]
    units += [(mem_scores, mem_finish, (h,)) for h in range(MEM_HEADS)]
    pending = []
    for n in range(len(units) + LOCAL_LOOKAHEAD):
        if n < len(units):
            score_fn, _, args = units[n]
            pending.append(score_fn(*args))
        if n >= LOCAL_LOOKAHEAD:
            _, finish_fn, args = units[n - LOCAL_LOOKAHEAD]
            finish_fn(*args, pending[n - LOCAL_LOOKAHEAD])


def _local(bucket_t, rel_bias, sinks, qa, ka, vat, qm, mk, mvt, batch, seq, tq):
    t = qa.shape[0]
    nq = seq // tq
    sub = tq // BLOCK

    def cur(b, i):
        return (b * nq + i, 0)

    def prev(b, i):
        return (jnp.maximum((b * nq + i) * sub - 1, 0), 0)

    smem = pl.BlockSpec(memory_space=pltpu.SMEM)
    return pl.pallas_call(
        functools.partial(_local_kernel, tq=tq),
        grid=(batch, nq),
        in_specs=[_const_spec((2 * BLOCK, BLOCK)), smem, smem,
                  pl.BlockSpec((tq, SWA_Q), cur),
                  pl.BlockSpec((tq, 2 * SWA_KV), cur),
                  pl.BlockSpec((BLOCK, 2 * SWA_KV), prev),
                  pl.BlockSpec((None, SWA_KV, tq), lambda b, i: (b, 0, i)),
                  pl.BlockSpec((None, SWA_KV, BLOCK),
                               lambda b, i: (b, 0, jnp.maximum(i * sub - 1, 0))),
                  pl.BlockSpec((tq, MEM_W), cur),
                  pl.BlockSpec((N_MEM, MEM_W), lambda b, i: (b, 0)),
                  pl.BlockSpec((None, MEM_W, N_MEM), lambda b, i: (b, 0, 0))],
        out_specs=[pl.BlockSpec((tq, SWA_Q), cur),
                   pl.BlockSpec((tq, MEM_W), cur)],
        out_shape=[jax.ShapeDtypeStruct((t, SWA_Q), BF16),
                   jax.ShapeDtypeStruct((t, MEM_W), BF16)],
        scratch_shapes=[pltpu.VMEM((SWA_HEADS // 2, 2 * BLOCK, 2 * BLOCK), F32)],
        compiler_params=_params(2),
        name="local",
    )(bucket_t, rel_bias, sinks, qa, ka, ka, vat, vat, qm, mk, mvt)


def _fox_kernel(fast_ref, qt_ref, aqt_ref, k_ref, ak_ref, vt_ref, o_ref,
                sa_ref, sb_ref, acc_ref, m_ref, *, tk, n_sub, n_steps):
    pair = pl.program_id(1)
    g = pl.program_id(2)
    row = lax.broadcasted_iota(jnp.int32, (LANES, tk), 0)
    half = [row < HEAD64, row >= HEAD64]
    qcat = []
    for sub in range(n_sub):
        q = qt_ref[:, sub * tk:(sub + 1) * tk]
        aq = aqt_ref[:, sub * tk:(sub + 1) * tk]
        zero = jnp.zeros_like(q)
        per_head = []
        for h in range(2):
            lo = (2 * pair + h) * AUG_PER_HEAD
            mine = jnp.logical_and(row >= lo, row < lo + AUG_PER_HEAD)
            per_head.append(jnp.concatenate([jnp.where(half[h], q, zero),
                                             jnp.where(mine, aq, zero)], axis=0))
        qcat.append(per_head)
    ones = jnp.ones((BF16_ROWS, tk), BF16)
    acc_ref[...] = jnp.zeros_like(acc_ref)
    key = lax.broadcasted_iota(jnp.int32, (tk, tk), 0)
    qry = lax.broadcasted_iota(jnp.int32, (tk, tk), 1)
    causal = key <= qry

    def kcat(j):
        r0 = pl.multiple_of(j * tk, tk)
        return jnp.concatenate([k_ref[pl.ds(r0, tk), :], ak_ref[pl.ds(r0, tk), :]], axis=1)

    def vt_ones(j, h):
        return jnp.concatenate([vt_ref[j, h * HEAD64:(h + 1) * HEAD64, :], ones], axis=0)

    def visible_keys(c, diagonal):
        return min(tk, c + MXU_DIM) if diagonal else tk

    def qk(sub, j, s_ref, diagonal=False):
        kc = kcat(j)
        for h in range(2):
            for c in range(0, tk, MXU_DIM):
                nk = visible_keys(c, diagonal)
                s_ref[h, :nk, c:c + MXU_DIM] = _dot(kc[:nk], qcat[sub][h][:, c:c + MXU_DIM])

    @pl.when(fast_ref[0] == 1)
    def _():
        bufs = (sa_ref, sb_ref)

        def exp_pv(sub, j, s_ref, masked):
            for h in range(2):
                for c in range(0, tk, MXU_DIM):
                    nk = visible_keys(c, masked)
                    s = s_ref[h, :nk, c:c + MXU_DIM]
                    if masked:
                        s = jnp.where(causal[:nk, c:c + MXU_DIM], s, NEG)
                    acc_ref[2 * sub + h, :, c:c + MXU_DIM] += _dot(vt_ones(j, h)[:, :nk],
                                                                   jnp.exp2(s).astype(BF16))

        full = [(sub, jj) for jj in range(n_sub) for sub in range(n_sub)]
        diag = [(sub, jj) for jj in range(n_sub) for sub in range(jj, n_sub)]
        assert len(full) % 2 == 0
        qk(0, 0, bufs[0])

        def body(t, carry):
            for n, (sub, jj) in enumerate(full):
                nsub, njj = full[n + 1] if n + 1 < len(full) else (0, n_sub)
                qk(nsub, n_sub * t + njj, bufs[(n + 1) % 2])
                exp_pv(sub, n_sub * t + jj, bufs[n % 2], False)
            return carry

        if n_steps > 1:
            lax.fori_loop(0, g, body, 0)
        for n, (sub, jj) in enumerate(diag):
            if n + 1 < len(diag):
                nsub, njj = diag[n + 1]
                qk(nsub, n_sub * g + njj, bufs[(n + 1) % 2], diagonal=nsub == njj)
            exp_pv(sub, n_sub * g + jj, bufs[n % 2], sub == jj)

    @pl.when(fast_ref[0] == 0)
    def _():
        m_ref[...] = jnp.full(m_ref.shape, NEG, F32)
        for sub in range(n_sub):
            last = n_sub * g + sub

            def body(j, carry, sub=sub, last=last):
                qk(sub, j, sa_ref)
                for h in range(2):
                    u = 2 * sub + h
                    s = jnp.where(jnp.logical_or(causal, j < last), sa_ref[h], NEG)
                    m_prev = m_ref[u]
                    m_next = jnp.maximum(m_prev, jnp.max(s, axis=0, keepdims=True))
                    alpha = jnp.exp2(m_prev - m_next)
                    p = jnp.exp2(s - m_next).astype(BF16)
                    acc_ref[u] = alpha * acc_ref[u] + _dot(vt_ones(j, h), p)
                    m_ref[u] = m_next
                return carry

            lax.fori_loop(0, last + 1, body, 0)

    for sub in range(n_sub):
        halves = []
        for h in range(2):
            acc = acc_ref[2 * sub + h]
            halves.append(acc[:HEAD64, :] / acc[HEAD64:HEAD64 + 1, :])
        o_ref[sub * tk:(sub + 1) * tk, :] = jnp.concatenate(halves, axis=0).T.astype(BF16)


def _fox(fast, qft, aqt, kf, ak, vt, batch, seq, tk):
    t = kf.shape[0]
    n_sub = min(FOX_SUB, seq // tk)
    tq = n_sub * tk
    assert seq % tq == 0, (seq, tq)
    nq = seq // tq
    pairs = FOX_HEADS // 2
    return pl.pallas_call(
        functools.partial(_fox_kernel, tk=tk, n_sub=n_sub, n_steps=nq),
        grid=(batch, pairs, nq),
        in_specs=[pl.BlockSpec(memory_space=pltpu.SMEM),
                  pl.BlockSpec((None, None, LANES, tq), lambda b, p, i: (b, p, 0, i)),
                  pl.BlockSpec((None, LANES, tq), lambda b, p, i: (b, 0, i)),
                  pl.BlockSpec((seq, LANES), lambda b, p, i: (b, p)),
                  pl.BlockSpec((seq, LANES), lambda b, p, i: (b, 0)),
                  pl.BlockSpec((None, None, seq // tk, LANES, tk),
                               lambda b, p, i: (b, p, 0, 0, 0))],
        out_specs=pl.BlockSpec((tq, LANES), lambda b, p, i: (b * nq + i, p)),
        out_shape=jax.ShapeDtypeStruct((t, FOX_W), BF16),
        scratch_shapes=[pltpu.VMEM((2, tk, tk), F32),
                        pltpu.VMEM((2, tk, tk), F32),
                        pltpu.VMEM((2 * n_sub, HEAD64 + BF16_ROWS, tk), F32),
                        pltpu.VMEM((2 * n_sub, 1, tk), F32)],
        compiler_params=_params(3),
        name="fox",
    )(fast, qft, aqt, kf, ak, vt)


def _post_kernel(x_ref, oa_ref, of_ref, om_ref, gate_ref, wa_ref, wf_ref, wm_ref, wo_ref,
                 g_ref, wu_ref, wd_ref, y_ref, *, chunk):
    ya = _dot(oa_ref[...], wa_ref[...])
    yf = _dot(of_ref[...], wf_ref[...])
    ym = _dot(om_ref[...], wm_ref[...])
    merged = (gate_ref[:, 0:D_MODEL].astype(F32) * ya
              + gate_ref[:, D_MODEL:2 * D_MODEL].astype(F32) * yf
              + gate_ref[:, 2 * D_MODEL:].astype(F32) * ym)
    x1 = x_ref[...] + _dot(merged.astype(BF16), wo_ref[...])
    h = _rms_rows(x1, g_ref[...]).astype(BF16)
    acc = x1
    for c in range(0, D_FF, chunk):
        u = jnp.maximum(_dot(h, wu_ref[:, c:c + chunk]), 0.0)
        acc = acc + _dot((u * u).astype(BF16), wd_ref[c:c + chunk, :])
    y_ref[...] = acc


def _post(x2, oa, of, om, gates, wa, wf, wm, wo, g_mlp, wu, wd, tm, chunk=1024):
    t = x2.shape[0]
    row = lambda w: pl.BlockSpec((tm, w), lambda i: (i, 0))
    resident = lambda shape: pl.BlockSpec(shape, lambda i: (0,) * len(shape),
                                          pipeline_mode=pl.Buffered(1))
    return pl.pallas_call(
        functools.partial(_post_kernel, chunk=chunk),
        grid=(t // tm,),
        in_specs=[row(D_MODEL), row(SWA_Q), row(FOX_W), row(MEM_W), row(GATE_W),
                  resident((SWA_Q, D_MODEL)), resident((FOX_W, D_MODEL)),
                  resident((MEM_W, D_MODEL)), resident((D_MODEL, D_MODEL)),
                  resident((1, D_MODEL)), resident((D_MODEL, D_FF)), resident((D_FF, D_MODEL))],
        out_specs=row(D_MODEL),
        out_shape=jax.ShapeDtypeStruct((t, D_MODEL), F32),
        compiler_params=_params(1),
        name="post",
    )(x2, oa, of, om, gates, wa, wf, wm, wo, g_mlp, wu, wd)


def _t5_bucket_table():
    max_exact = REL_BUCKETS // 2
    t_loc = np.arange(BLOCK)[:, None] + BLOCK
    s_loc = np.arange(2 * BLOCK)[None, :]
    dist = t_loc - s_loc
    d = np.maximum(dist, 0)
    df = np.maximum(d, 1).astype(np.float32)
    scaled = (np.log(df / np.float32(max_exact)) / np.float32(math.log(REL_MAX_DIST / max_exact))
              * np.float32(REL_BUCKETS - max_exact))
    large = np.minimum(max_exact + scaled.astype(np.int32), REL_BUCKETS - 1)
    bucket = np.where(d < max_exact, d, large)
    band = (dist >= 0) & (dist < WINDOW)
    return jnp.asarray(np.where(band, bucket, -1).astype(np.int32).T)


def _aug_placement(shift):
    place = [[0.0] * (2 * LANES) for _ in range(LANES)]
    ones = [0.0] * (2 * LANES)
    shift_lanes = []
    for h in range(FOX_HEADS):
        base = h * AUG_PER_HEAD
        for part in range(3):
            src = part * FOX_HEADS + h
            place[src][base + part] = 1.0
            place[src][LANES + base + 3 + part] = 1.0
            ones[base + 3 + part] = 1.0
            ones[LANES + base + part] = 1.0
        ones[base + 6] = 1.0
        shift_lanes.append(LANES + base + 6)
    consts = jnp.array(ones, F32).at[jnp.array(shift_lanes)].set(-shift)
    return jnp.array(place, BF16), jnp.broadcast_to(consts[None, :], (8, 2 * LANES))


def _tile_gain(g, reps, scale=1.0):
    return (jnp.tile(g.astype(F32), reps) * scale)[None, :]


def _pick_tile(n, target):
    t = min(n, target)
    while n % t:
        t //= 2
    return t


def kernel(x, mem, g_mix, w_in, b_gate, b_forget, qn_swa, kn_swa, sink_swa, rel_bias,
           qn_fox, kn_fox, g_mem, w_mem_kv, qn_mem, kn_mem, w_o_swa, w_o_fox, w_o_mem,
           w_out, g_mlp, w_mlp_up, w_mlp_down):
    batch, seq, _ = x.shape
    n_layers = w_in.shape[0]
    t = batch * seq
    tm = _pick_tile(seq, 512)
    tq = _pick_tile(seq, 1024)

    bucket = _t5_bucket_table()
    tri = (jnp.arange(tm)[:, None] >= jnp.arange(tm)[None, :]).astype(BF16)

    x2 = x.reshape(t, D_MODEL)
    mem2 = mem.reshape(batch * N_MEM, D_MODEL)
    for layer in range(n_layers):
        wt = w_in[layer].T
        o = 0
        parts = {}
        for name, width in (("qa", SWA_Q), ("ka", SWA_KV), ("va", SWA_KV), ("qf", FOX_W),
                            ("kf", FOX_W), ("vf", FOX_W), ("fl", FOX_HEADS), ("qm", MEM_W),
                            ("g", GATE_W)):
            parts[name] = wt[o:o + width]
            o += width

        def dup(m):
            return jnp.concatenate([m[:HEAD64], m[:HEAD64], m[HEAD64:], m[HEAD64:]], axis=0)

        w1 = jnp.concatenate(
            [parts["qa"], dup(parts["ka"]), parts["kf"], parts["qm"], parts["g"],
             jnp.pad(parts["fl"], ((0, LANES - FOX_HEADS), (0, 0)))], axis=0).astype(BF16)

        wmt = w_mem_kv[layer].T.astype(BF16)

        bound = (HEAD64 ** 0.5 * LOG2E * 1.02 * jnp.max(jnp.abs(qn_fox[layer]))
                 * jnp.max(jnp.abs(kn_fox[layer]))).astype(BF16).astype(F32)
        fast = bound <= FOX_FAST_MAX_SHIFT
        place, consts = _aug_placement(jnp.where(fast, bound, 0.0))
        bf_pad = jnp.pad(b_forget[layer].astype(F32), (0, LANES - FOX_HEADS))[None, :]

        wt = jnp.concatenate([parts["vf"], parts["va"], parts["qf"]], axis=0).astype(BF16)
        gqft = jnp.broadcast_to(
            _tile_gain(qn_fox[layer], FOX_HEADS, HEAD64 ** -0.5 * LOG2E).T, (FOX_W, LANES))
        qa, ka, vat, qft, kf, vt, qm, gates, fl = _proj(
            x2, g_mix[layer][None, :], w1, wt, b_gate[layer][None, :],
            _tile_gain(qn_swa[layer], SWA_HEADS, HEAD64 ** -0.5 * LOG2E),
            _tile_gain(kn_swa[layer], 2 * SWA_KV_HEADS),
            gqft,
            _tile_gain(kn_fox[layer], FOX_HEADS),
            _tile_gain(qn_mem[layer], MEM_HEADS, MEM_HEAD_DIM ** -0.5 * LOG2E), tm, seq)
        aqt, ak, mk, mvt = _cum(fl, bf_pad, tri, place, consts, mem2, g_mem[layer][None, :],
                                wmt[:MEM_W], wmt[MEM_W:], _tile_gain(kn_mem[layer], MEM_HEADS),
                                tm, seq)

        oa, om = _local(bucket, rel_bias.astype(F32) * LOG2E, sink_swa[layer].astype(F32) * LOG2E,
                        qa, ka, vat, qm, mk, mvt, batch, seq, tq)
        of = _fox(fast.astype(jnp.int32)[None], qft, aqt, kf, ak, vt, batch, seq, tm)

        x2 = _post(x2, oa, of, om, gates, w_o_swa[layer].astype(BF16),
                   w_o_fox[layer].astype(BF16), w_o_mem[layer].astype(BF16),
                   w_out[layer].astype(BF16), g_mlp[layer][None, :],
                   w_mlp_up[layer].astype(BF16), w_mlp_down[layer].astype(BF16), tm)
    return x2.reshape(batch, seq, D_MODEL)
```

```python
import functools
import math

import jax
import jax.numpy as jnp
import numpy as np
from jax import lax
from jax.experimental import pallas as pl
from jax.experimental.pallas import tpu as pltpu

F32 = jnp.float32
BF16 = jnp.bfloat16

D_MODEL = 1024
N_MEM = 256
SWA_HEADS = 8
SWA_KV_HEADS = 2
SWA_GROUP = SWA_HEADS // SWA_KV_HEADS
HEAD64 = 64
WINDOW = 128
FOX_HEADS = 8
MEM_HEADS = 4
MEM_HEAD_DIM = 128
D_FF = 4 * D_MODEL
REL_BUCKETS = 32
REL_MAX_DIST = 128
BLOCK = 128
EPS = 1e-6
NEG = -1e30
LOG2E = math.log2(math.e)

LANES = 128
MXU_DIM = 256
BF16_ROWS = 16
VMEM_LIMIT = 56 * 1024 * 1024

SWA_Q = SWA_HEADS * HEAD64
SWA_KV = SWA_KV_HEADS * HEAD64
FOX_W = FOX_HEADS * HEAD64
MEM_W = MEM_HEADS * MEM_HEAD_DIM
GATE_W = 3 * D_MODEL

C_QA = 0
C_KA = C_QA + SWA_Q
C_KF = C_KA + 2 * SWA_KV
C_QM = C_KF + FOX_W
C_G = C_QM + MEM_W
C_FL = C_G + GATE_W
C_END = C_FL + LANES

AUG_PER_HEAD = 7
FOX_FAST_MAX_SHIFT = 60.0
FOX_SUB = 4
LOCAL_LOOKAHEAD = 4


def _dot(a, b):
    return jnp.dot(a, b, preferred_element_type=F32)


def _dot_nt(a, b):
    return lax.dot_general(a, b, (((1,), (1,)), ((), ())), preferred_element_type=F32)


def _rms_rows(x, gain):
    ms = jnp.mean(x * x, axis=-1, keepdims=True)
    return x * lax.rsqrt(ms + EPS) * gain


def _head_rms(y, gain, head_dim):
    y2 = y * y
    lower = lax.broadcasted_iota(jnp.int32, (y.shape[0], LANES), 1) < HEAD64
    cols = []
    for c in range(0, y.shape[1], LANES):
        blk = y2[:, c:c + LANES]
        total = jnp.sum(blk, axis=-1, keepdims=True)
        if head_dim == LANES:
            cols.append(jnp.broadcast_to(total, blk.shape))
        else:
            low = jnp.sum(jnp.where(lower, blk, 0.0), axis=-1, keepdims=True)
            cols.append(jnp.where(lower, low, total - low))
    ss = jnp.concatenate(cols, axis=1)
    return y * lax.rsqrt(ss * (1.0 / head_dim) + EPS) * gain


def _const_spec(shape):
    return pl.BlockSpec(shape, lambda *_: (0,) * len(shape))


def _params(n_axes):
    return pltpu.CompilerParams(dimension_semantics=("arbitrary",) * n_axes,
                                vmem_limit_bytes=VMEM_LIMIT)


def _memkv_kernel(mem_ref, g_ref, wkt_ref, wvt_ref, kn_ref, mk_ref, mvt_ref):
    h = _rms_rows(mem_ref[...], g_ref[...]).astype(BF16)
    mk_ref[...] = _head_rms(_dot_nt(h, wkt_ref[...]), kn_ref[...], MEM_HEAD_DIM).astype(BF16)
    mvt_ref[...] = _dot_nt(wvt_ref[...], h).astype(BF16)


def _memkv(mem2, g_mem, wkt, wvt, kn_t):
    rows = mem2.shape[0]
    nb = rows // N_MEM
    return pl.pallas_call(
        _memkv_kernel,
        grid=(nb,),
        in_specs=[pl.BlockSpec((N_MEM, D_MODEL), lambda i: (i, 0)),
                  _const_spec((1, D_MODEL)),
                  _const_spec((MEM_W, D_MODEL)),
                  _const_spec((MEM_W, D_MODEL)),
                  _const_spec((1, MEM_W))],
        out_specs=[pl.BlockSpec((N_MEM, MEM_W), lambda i: (i, 0)),
                   pl.BlockSpec((None, MEM_W, N_MEM), lambda i: (i, 0, 0))],
        out_shape=[jax.ShapeDtypeStruct((rows, MEM_W), BF16),
                   jax.ShapeDtypeStruct((nb, MEM_W, N_MEM), BF16)],
        compiler_params=_params(1),
        name="memkv",
    )(mem2, g_mem, wkt, wvt, kn_t)


def _proj_kernel(x_ref, g_ref, w_ref, wt_ref, bg_ref,
                 gqa_ref, gka_ref, gqft_ref, gkf_ref, gqm_ref,
                 qa_ref, ka_ref, vat_ref, qft_ref, kf_ref, vt_ref, qm_ref, gate_ref, fl_ref):
    h = _rms_rows(x_ref[...], g_ref[...]).astype(BF16)
    tm = h.shape[0]
    qkv = _dot_nt(h, w_ref[:C_G, :])
    qa_ref[...] = _head_rms(qkv[:, C_QA:C_KA], gqa_ref[...], HEAD64).astype(BF16)
    ka_ref[...] = _head_rms(qkv[:, C_KA:C_KF], gka_ref[...], HEAD64).astype(BF16)
    kf_ref[...] = _head_rms(qkv[:, C_KF:C_QM], gkf_ref[...], HEAD64).astype(BF16)
    qm_ref[...] = _head_rms(qkv[:, C_QM:C_G], gqm_ref[...], MEM_HEAD_DIM).astype(BF16)
    t = _dot_nt(wt_ref[...], h)
    for p in range(FOX_HEADS // 2):
        vt_ref[p] = t[p * LANES:(p + 1) * LANES, :].astype(BF16)
    vat_ref[...] = t[FOX_W:FOX_W + SWA_KV, :].astype(BF16)
    gain_t = jnp.tile(gqft_ref[...], (1, tm // LANES))
    for p in range(FOX_HEADS // 2):
        rows = []
        for e in range(2):
            r0 = FOX_W + SWA_KV + (2 * p + e) * HEAD64
            y = t[r0:r0 + HEAD64, :]
            ms = jnp.sum(y * y, axis=0, keepdims=True) * (1.0 / HEAD64)
            rows.append(y * lax.rsqrt(ms + EPS) * gain_t[r0 - FOX_W - SWA_KV:r0 - FOX_W - SWA_KV
                                                         + HEAD64, :])
        qft_ref[p] = jnp.concatenate(rows, axis=0).astype(BF16)
    zf = _dot_nt(h, w_ref[C_G:C_END, :])
    z = zf[:, :GATE_W] + bg_ref[...]
    gate_ref[...] = (0.5 * jnp.tanh(0.5 * z) + 0.5).astype(BF16)
    fl_ref[...] = zf[:, GATE_W:]


def _proj(x2, g_mix, w1, wt, b_gate, gqa, gka, gqft, gkf, gqm, tm, seq):
    t = x2.shape[0]
    nt = seq // tm
    pairs = FOX_HEADS // 2

    def rows(w, dtype=BF16):
        return jax.ShapeDtypeStruct((t, w), dtype), pl.BlockSpec((tm, w), lambda i: (i, 0))

    outs = [rows(SWA_Q), rows(2 * SWA_KV),
            (jax.ShapeDtypeStruct((t // seq, SWA_KV, seq), BF16),
             pl.BlockSpec((None, SWA_KV, tm), lambda i: (i // nt, 0, i % nt))),
            (jax.ShapeDtypeStruct((t // seq, pairs, LANES, seq), BF16),
             pl.BlockSpec((None, pairs, LANES, tm), lambda i: (i // nt, 0, 0, i % nt))),
            rows(FOX_W),
            (jax.ShapeDtypeStruct((t // seq, pairs, nt, LANES, tm), BF16),
             pl.BlockSpec((None, pairs, None, LANES, tm), lambda i: (i // nt, 0, i % nt, 0, 0))),
            rows(MEM_W), rows(GATE_W), rows(LANES, F32)]
    out_shape = [o[0] for o in outs]
    out_specs = [o[1] for o in outs]
    return pl.pallas_call(
        _proj_kernel,
        grid=(t // tm,),
        in_specs=[pl.BlockSpec((tm, D_MODEL), lambda i: (i, 0)),
                  _const_spec((1, D_MODEL)),
                  _const_spec((C_END, D_MODEL)),
                  _const_spec((2 * FOX_W + SWA_KV, D_MODEL)),
                  _const_spec((1, GATE_W)),
                  _const_spec((1, SWA_Q)),
                  _const_spec((1, 2 * SWA_KV)),
                  _const_spec((FOX_W, LANES)),
                  _const_spec((1, FOX_W)),
                  _const_spec((1, MEM_W))],
        out_specs=out_specs,
        out_shape=out_shape,
        compiler_params=_params(1),
        name="proj",
    )(x2, g_mix, w1, wt, b_gate, gqa, gka, gqft, gkf, gqm)


def _split3(c):
    hi = c.astype(BF16)
    r1 = c - hi.astype(F32)
    mid = r1.astype(BF16)
    lo = (r1 - mid.astype(F32)).astype(BF16)
    return hi, mid, lo


def _pack3(x):
    hi, mid, lo = _split3(x)
    return (hi.astype(F32) + pltpu.roll(mid.astype(F32), FOX_HEADS, 1)
            + pltpu.roll(lo.astype(F32), 2 * FOX_HEADS, 1)).astype(BF16)


def _cum_kernel(fl_ref, bf_ref, tri_ref, place_ref, const_ref, aqt_ref, ak_ref, *, tm):
    n_tiles = fl_ref.shape[0] // tm
    lane = lax.broadcasted_iota(jnp.int32, (tm, LANES), 1)
    heads = lane < FOX_HEADS
    tri = tri_ref[...]
    parts = []
    for k in range(n_tiles):
        z = fl_ref[k * tm:(k + 1) * tm, :] + bf_ref[...]
        logf = jnp.where(heads, jnp.minimum(z, 0.0) - jnp.log(1.0 + jnp.exp(-jnp.abs(z))), 0.0)
        parts.append(_dot(tri, _pack3(logf)))
    carry = jnp.zeros((1, LANES), F32)
    packed = []
    for part in parts:
        c = ((part + pltpu.roll(part, LANES - FOX_HEADS, 1))
             + pltpu.roll(part, LANES - 2 * FOX_HEADS, 1)) + carry
        carry = c[tm - 1:tm, :]
        packed.append(_pack3(jnp.where(heads, c * LOG2E, 0.0)))
    for k in range(n_tiles):
        placed = _dot(packed[k], place_ref[...])
        rows = slice(k * tm, (k + 1) * tm)
        aqt_ref[:, rows] = (placed[:, :LANES] + const_ref[0:1, :LANES]).T.astype(BF16)
        ak_ref[rows, :] = (const_ref[0:1, LANES:] - placed[:, LANES:]).astype(BF16)


def _cum(fl, bf_pad, tri, place, consts, tm, seq):
    t = fl.shape[0]
    return pl.pallas_call(
        functools.partial(_cum_kernel, tm=tm),
        grid=(t // seq,),
        in_specs=[pl.BlockSpec((seq, LANES), lambda i: (i, 0)),
                  _const_spec((1, LANES)),
                  _const_spec((tm, tm)),
                  _const_spec((LANES, 2 * LANES)),
                  _const_spec((8, 2 * LANES))],
        out_specs=[pl.BlockSpec((None, LANES, seq), lambda i: (i, 0, 0)),
                   pl.BlockSpec((seq, LANES), lambda i: (i, 0))],
        out_shape=[jax.ShapeDtypeStruct((t // seq, LANES, seq), BF16),
                   jax.ShapeDtypeStruct((t, LANES), BF16)],
        compiler_params=_params(1),
        name="cum",
    )(fl, bf_pad, tri, place, consts)


def _local_kernel(bucket_ref, relb_ref, sink_ref,
                  qa_ref, kc_ref, kp_ref, vc_ref, vp_ref, qm_ref, mk_ref, mvt_ref,
                  oa_ref, om_ref, tbl_ref, *, tq):
    b = pl.program_id(0)
    i = pl.program_id(1)
    pairs = SWA_HEADS // 2

    @pl.when(jnp.logical_and(b == 0, i == 0))
    def _():
        bucket = bucket_ref[...]
        band = bucket >= 0
        for h in range(SWA_HEADS):
            t = jnp.zeros(bucket.shape, F32)
            for k in range(REL_BUCKETS):
                t = jnp.where(bucket == k, relb_ref[k, h], t)
            tbl_ref[h // 2, :, (h % 2) * BLOCK:(h % 2 + 1) * BLOCK] = jnp.where(band, t, NEG)

    lane = lax.broadcasted_iota(jnp.int32, (BLOCK, LANES), 1)
    lower = lane < HEAD64
    first_head = lax.broadcasted_iota(jnp.int32, (1, 2 * BLOCK), 1) < BLOCK
    key_row = lax.broadcasted_iota(jnp.int32, (2 * BLOCK, 2 * BLOCK), 0)
    pad_rows = jnp.logical_and(key_row < BLOCK, i == 0)
    ones = jnp.ones((BF16_ROWS, 2 * BLOCK), BF16)

    def windows(j):
        r0 = j * BLOCK
        if j == 0:
            return (jnp.concatenate([kp_ref[...], kc_ref[0:BLOCK, :]], axis=0),
                    jnp.concatenate([vp_ref[...], vc_ref[:, 0:BLOCK]], axis=1))
        return kc_ref[r0 - BLOCK:r0 + BLOCK, :], vc_ref[:, r0 - BLOCK:r0 + BLOCK]

    def swa_scores(j, hb):
        g = (2 * hb) // SWA_GROUP
        qblk = qa_ref[j * BLOCK:(j + 1) * BLOCK, hb * LANES:(hb + 1) * LANES]
        qstack = jnp.concatenate([jnp.where(lower, qblk, 0),
                                  jnp.where(lower, 0, qblk)], axis=0)
        s = _dot_nt(windows(j)[0][:, g * LANES:(g + 1) * LANES], qstack) + tbl_ref[hb]
        return jnp.where(pad_rows, NEG, s) if j == 0 else s

    def swa_finish(j, hb, s):
        g = (2 * hb) // SWA_GROUP
        sink = jnp.where(first_head, sink_ref[2 * hb], sink_ref[2 * hb + 1])
        m = jnp.maximum(jnp.max(s, axis=0, keepdims=True), sink)
        p = jnp.exp2(s - m).astype(BF16)
        vt_ones = jnp.concatenate([windows(j)[1][g * HEAD64:(g + 1) * HEAD64, :], ones], axis=0)
        acc = _dot(vt_ones, p)
        o = acc[:HEAD64, :] / (acc[HEAD64:HEAD64 + 1, :] + jnp.exp2(sink - m))
        o = jnp.concatenate([o[:, :BLOCK], o[:, BLOCK:]], axis=0)
        oa_ref[j * BLOCK:(j + 1) * BLOCK, hb * LANES:(hb + 1) * LANES] = o.T.astype(BF16)

    ones_m = jnp.ones((BF16_ROWS, N_MEM), BF16)

    def mem_scores(h):
        sl = slice(h * MEM_HEAD_DIM, (h + 1) * MEM_HEAD_DIM)
        return _dot_nt(mk_ref[:, sl], qm_ref[:, sl])

    def mem_finish(h, s):
        sl = slice(h * MEM_HEAD_DIM, (h + 1) * MEM_HEAD_DIM)
        m = jnp.max(s, axis=0, keepdims=True)
        p = jnp.exp2(s - m).astype(BF16)
        acc = _dot(jnp.concatenate([mvt_ref[sl, :], ones_m], axis=0), p)
        o = acc[:MEM_HEAD_DIM, :] / acc[MEM_HEAD_DIM:MEM_HEAD_DIM + 1, :]
        om_ref[:, sl] = o.T.astype(BF16)

    units = [(swa_scores, swa_finish, (j, hb)) for j in range(tq // BLOCK) for hb in range(pairs)]
    units += [(mem_scores, mem_finish, (h,)) for h in range(MEM_HEADS)]
    pending = []
    for n in range(len(units) + LOCAL_LOOKAHEAD):
        if n < len(units):
            score_fn, _, args = units[n]
            pending.append(score_fn(*args))
        if n >= LOCAL_LOOKAHEAD:
            _, finish_fn, args = units[n - LOCAL_LOOKAHEAD]
            finish_fn(*args, pending[n - LOCAL_LOOKAHEAD])


def _local(bucket_t, rel_bias, sinks, qa, ka, vat, qm, mk, mvt, batch, seq, tq):
    t = qa.shape[0]
    nq = seq // tq
    sub = tq // BLOCK

    def cur(b, i):
        return (b * nq + i, 0)

    def prev(b, i):
        return (jnp.maximum((b * nq + i) * sub - 1, 0), 0)

    smem = pl.BlockSpec(memory_space=pltpu.SMEM)
    return pl.pallas_call(
        functools.partial(_local_kernel, tq=tq),
        grid=(batch, nq),
        in_specs=[_const_spec((2 * BLOCK, BLOCK)), smem, smem,
                  pl.BlockSpec((tq, SWA_Q), cur),
                  pl.BlockSpec((tq, 2 * SWA_KV), cur),
                  pl.BlockSpec((BLOCK, 2 * SWA_KV), prev),
                  pl.BlockSpec((None, SWA_KV, tq), lambda b, i: (b, 0, i)),
                  pl.BlockSpec((None, SWA_KV, BLOCK),
                               lambda b, i: (b, 0, jnp.maximum(i * sub - 1, 0))),
                  pl.BlockSpec((tq, MEM_W), cur),
                  pl.BlockSpec((N_MEM, MEM_W), lambda b, i: (b, 0)),
                  pl.BlockSpec((None, MEM_W, N_MEM), lambda b, i: (b, 0, 0))],
        out_specs=[pl.BlockSpec((tq, SWA_Q), cur),
                   pl.BlockSpec((tq, MEM_W), cur)],
        out_shape=[jax.ShapeDtypeStruct((t, SWA_Q), BF16),
                   jax.ShapeDtypeStruct((t, MEM_W), BF16)],
        scratch_shapes=[pltpu.VMEM((SWA_HEADS // 2, 2 * BLOCK, 2 * BLOCK), F32)],
        compiler_params=_params(2),
        name="local",
    )(bucket_t, rel_bias, sinks, qa, ka, ka, vat, vat, qm, mk, mvt)


def _fox_kernel(fast_ref, qt_ref, aqt_ref, k_ref, ak_ref, vt_ref, o_ref,
                sa_ref, sb_ref, acc_ref, m_ref, *, tk, n_sub, n_steps):
    pair = pl.program_id(1)
    g = pl.program_id(2)
    row = lax.broadcasted_iota(jnp.int32, (LANES, tk), 0)
    half = [row < HEAD64, row >= HEAD64]
    qcat = []
    for sub in range(n_sub):
        q = qt_ref[:, sub * tk:(sub + 1) * tk]
        aq = aqt_ref[:, sub * tk:(sub + 1) * tk]
        zero = jnp.zeros_like(q)
        per_head = []
        for h in range(2):
            lo = (2 * pair + h) * AUG_PER_HEAD
            mine = jnp.logical_and(row >= lo, row < lo + AUG_PER_HEAD)
            per_head.append(jnp.concatenate([jnp.where(half[h], q, zero),
                                             jnp.where(mine, aq, zero)], axis=0))
        qcat.append(per_head)
    ones = jnp.ones((BF16_ROWS, tk), BF16)
    acc_ref[...] = jnp.zeros_like(acc_ref)
    key = lax.broadcasted_iota(jnp.int32, (tk, tk), 0)
    qry = lax.broadcasted_iota(jnp.int32, (tk, tk), 1)
    causal = key <= qry

    def kcat(j):
        r0 = pl.multiple_of(j * tk, tk)
        return jnp.concatenate([k_ref[pl.ds(r0, tk), :], ak_ref[pl.ds(r0, tk), :]], axis=1)

    def vt_ones(j, h):
        return jnp.concatenate([vt_ref[j, h * HEAD64:(h + 1) * HEAD64, :], ones], axis=0)

    def visible_keys(c, diagonal):
        return min(tk, c + MXU_DIM) if diagonal else tk

    def qk(sub, j, s_ref, diagonal=False):
        kc = kcat(j)
        for h in range(2):
            for c in range(0, tk, MXU_DIM):
                nk = visible_keys(c, diagonal)
                s_ref[h, :nk, c:c + MXU_DIM] = _dot(kc[:nk], qcat[sub][h][:, c:c + MXU_DIM])

    @pl.when(fast_ref[0] == 1)
    def _():
        bufs = (sa_ref, sb_ref)

        def exp_pv(sub, j, s_ref, masked):
            for h in range(2):
                for c in range(0, tk, MXU_DIM):
                    nk = visible_keys(c, masked)
                    s = s_ref[h, :nk, c:c + MXU_DIM]
                    if masked:
                        s = jnp.where(causal[:nk, c:c + MXU_DIM], s, NEG)
                    acc_ref[2 * sub + h, :, c:c + MXU_DIM] += _dot(vt_ones(j, h)[:, :nk],
                                                                   jnp.exp2(s).astype(BF16))

        full = [(sub, jj) for jj in range(n_sub) for sub in range(n_sub)]
        diag = [(sub, jj) for jj in range(n_sub) for sub in range(jj, n_sub)]
        assert len(full) % 2 == 0
        qk(0, 0, bufs[0])

        def body(t, carry):
            for n, (sub, jj) in enumerate(full):
                nsub, njj = full[n + 1] if n + 1 < len(full) else (0, n_sub)
                qk(nsub, n_sub * t + njj, bufs[(n + 1) % 2])
                exp_pv(sub, n_sub * t + jj, bufs[n % 2], False)
            return carry

        if n_steps > 1:
            lax.fori_loop(0, g, body, 0)
        for n, (sub, jj) in enumerate(diag):
            if n + 1 < len(diag):
                nsub, njj = diag[n + 1]
                qk(nsub, n_sub * g + njj, bufs[(n + 1) % 2], diagonal=nsub == njj)
            exp_pv(sub, n_sub * g + jj, bufs[n % 2], sub == jj)

    @pl.when(fast_ref[0] == 0)
    def _():
        m_ref[...] = jnp.full(m_ref.shape, NEG, F32)
        for sub in range(n_sub):
            last = n_sub * g + sub

            def body(j, carry, sub=sub, last=last):
                qk(sub, j, sa_ref)
                for h in range(2):
                    u = 2 * sub + h
                    s = jnp.where(jnp.logical_or(causal, j < last), sa_ref[h], NEG)
                    m_prev = m_ref[u]
                    m_next = jnp.maximum(m_prev, jnp.max(s, axis=0, keepdims=True))
                    alpha = jnp.exp2(m_prev - m_next)
                    p = jnp.exp2(s - m_next).astype(BF16)
                    acc_ref[u] = alpha * acc_ref[u] + _dot(vt_ones(j, h), p)
                    m_ref[u] = m_next
                return carry

            lax.fori_loop(0, last + 1, body, 0)

    for sub in range(n_sub):
        halves = []
        for h in range(2):
            acc = acc_ref[2 * sub + h]
            halves.append(acc[:HEAD64, :] / acc[HEAD64:HEAD64 + 1, :])
        o_ref[sub * tk:(sub + 1) * tk, :] = jnp.concatenate(halves, axis=0).T.astype(BF16)


def _fox(fast, qft, aqt, kf, ak, vt, batch, seq, tk):
    t = kf.shape[0]
    n_sub = min(FOX_SUB, seq // tk)
    tq = n_sub * tk
    assert seq % tq == 0, (seq, tq)
    nq = seq // tq
    pairs = FOX_HEADS // 2
    return pl.pallas_call(
        functools.partial(_fox_kernel, tk=tk, n_sub=n_sub, n_steps=nq),
        grid=(batch, pairs, nq),
        in_specs=[pl.BlockSpec(memory_space=pltpu.SMEM),
                  pl.BlockSpec((None, None, LANES, tq), lambda b, p, i: (b, p, 0, i)),
                  pl.BlockSpec((None, LANES, tq), lambda b, p, i: (b, 0, i)),
                  pl.BlockSpec((seq, LANES), lambda b, p, i: (b, p)),
                  pl.BlockSpec((seq, LANES), lambda b, p, i: (b, 0)),
                  pl.BlockSpec((None, None, seq // tk, LANES, tk),
                               lambda b, p, i: (b, p, 0, 0, 0))],
        out_specs=pl.BlockSpec((tq, LANES), lambda b, p, i: (b * nq + i, p)),
        out_shape=jax.ShapeDtypeStruct((t, FOX_W), BF16),
        scratch_shapes=[pltpu.VMEM((2, tk, tk), F32),
                        pltpu.VMEM((2, tk, tk), F32),
                        pltpu.VMEM((2 * n_sub, HEAD64 + BF16_ROWS, tk), F32),
                        pltpu.VMEM((2 * n_sub, 1, tk), F32)],
        compiler_params=_params(3),
        name="fox",
    )(fast, qft, aqt, kf, ak, vt)


def _post_kernel(x_ref, oa_ref, of_ref, om_ref, gate_ref, wa_ref, wf_ref, wm_ref, wo_ref,
                 g_ref, wu_ref, wd_ref, y_ref, *, chunk):
    ya = _dot(oa_ref[...], wa_ref[...])
    yf = _dot(of_ref[...], wf_ref[...])
    ym = _dot(om_ref[...], wm_ref[...])
    merged = (gate_ref[:, 0:D_MODEL].astype(F32) * ya
              + gate_ref[:, D_MODEL:2 * D_MODEL].astype(F32) * yf
              + gate_ref[:, 2 * D_MODEL:].astype(F32) * ym)
    x1 = x_ref[...] + _dot(merged.astype(BF16), wo_ref[...])
    h = _rms_rows(x1, g_ref[...]).astype(BF16)
    acc = x1
    for c in range(0, D_FF, chunk):
        u = jnp.maximum(_dot(h, wu_ref[:, c:c + chunk]), 0.0)
        acc = acc + _dot((u * u).astype(BF16), wd_ref[c:c + chunk, :])
    y_ref[...] = acc


def _post(x2, oa, of, om, gates, wa, wf, wm, wo, g_mlp, wu, wd, tm, chunk=1024):
    t = x2.shape[0]
    row = lambda w: pl.BlockSpec((tm, w), lambda i: (i, 0))
    resident = lambda shape: pl.BlockSpec(shape, lambda i: (0,) * len(shape),
                                          pipeline_mode=pl.Buffered(1))
    return pl.pallas_call(
        functools.partial(_post_kernel, chunk=chunk),
        grid=(t // tm,),
        in_specs=[row(D_MODEL), row(SWA_Q), row(FOX_W), row(MEM_W), row(GATE_W),
                  resident((SWA_Q, D_MODEL)), resident((FOX_W, D_MODEL)),
                  resident((MEM_W, D_MODEL)), resident((D_MODEL, D_MODEL)),
                  resident((1, D_MODEL)), resident((D_MODEL, D_FF)), resident((D_FF, D_MODEL))],
        out_specs=row(D_MODEL),
        out_shape=jax.ShapeDtypeStruct((t, D_MODEL), F32),
        compiler_params=_params(1),
        name="post",
    )(x2, oa, of, om, gates, wa, wf, wm, wo, g_mlp, wu, wd)


def _t5_bucket_table():
    max_exact = REL_BUCKETS // 2
    t_loc = np.arange(BLOCK)[:, None] + BLOCK
    s_loc = np.arange(2 * BLOCK)[None, :]
    dist = t_loc - s_loc
    d = np.maximum(dist, 0)
    df = np.maximum(d, 1).astype(np.float32)
    scaled = (np.log(df / np.float32(max_exact)) / np.float32(math.log(REL_MAX_DIST / max_exact))
              * np.float32(REL_BUCKETS - max_exact))
    large = np.minimum(max_exact + scaled.astype(np.int32), REL_BUCKETS - 1)
    bucket = np.where(d < max_exact, d, large)
    band = (dist >= 0) & (dist < WINDOW)
    return jnp.asarray(np.where(band, bucket, -1).astype(np.int32).T)


def _aug_placement(shift):
    place = [[0.0] * (2 * LANES) for _ in range(LANES)]
    ones = [0.0] * (2 * LANES)
    shift_lanes = []
    for h in range(FOX_HEADS):
        base = h * AUG_PER_HEAD
        for part in range(3):
            src = part * FOX_HEADS + h
            place[src][base + part] = 1.0
            place[src][LANES + base + 3 + part] = 1.0
            ones[base + 3 + part] = 1.0
            ones[LANES + base + part] = 1.0
        ones[base + 6] = 1.0
        shift_lanes.append(LANES + base + 6)
    consts = jnp.array(ones, F32).at[jnp.array(shift_lanes)].set(-shift)
    return jnp.array(place, BF16), jnp.broadcast_to(consts[None, :], (8, 2 * LANES))


def _tile_gain(g, reps, scale=1.0):
    return (jnp.tile(g.astype(F32), reps) * scale)[None, :]


def _pick_tile(n, target):
    t = min(n, target)
    while n % t:
        t //= 2
    return t


def kernel(x, mem, g_mix, w_in, b_gate, b_forget, qn_swa, kn_swa, sink_swa, rel_bias,
           qn_fox, kn_fox, g_mem, w_mem_kv, qn_mem, kn_mem, w_o_swa, w_o_fox, w_o_mem,
           w_out, g_mlp, w_mlp_up, w_mlp_down):
    batch, seq, _ = x.shape
    n_layers = w_in.shape[0]
    t = batch * seq
    tm = _pick_tile(seq, 512)
    tq = _pick_tile(seq, 1024)

    bucket = _t5_bucket_table()
    tri = (jnp.arange(tm)[:, None] >= jnp.arange(tm)[None, :]).astype(BF16)

    x2 = x.reshape(t, D_MODEL)
    mem2 = mem.reshape(batch * N_MEM, D_MODEL)
    for layer in range(n_layers):
        wt = w_in[layer].T
        o = 0
        parts = {}
        for name, width in (("qa", SWA_Q), ("ka", SWA_KV), ("va", SWA_KV), ("qf", FOX_W),
                            ("kf", FOX_W), ("vf", FOX_W), ("fl", FOX_HEADS), ("qm", MEM_W),
                            ("g", GATE_W)):
            parts[name] = wt[o:o + width]
            o += width

        def dup(m):
            return jnp.concatenate([m[:HEAD64], m[:HEAD64], m[HEAD64:], m[HEAD64:]], axis=0)

        w1 = jnp.concatenate(
            [parts["qa"], dup(parts["ka"]), parts["kf"], parts["qm"], parts["g"],
             jnp.pad(parts["fl"], ((0, LANES - FOX_HEADS), (0, 0)))], axis=0).astype(BF16)

        wmt = w_mem_kv[layer].T.astype(BF16)
        mk, mvt = _memkv(mem2, g_mem[layer][None, :], wmt[:MEM_W], wmt[MEM_W:],
                         _tile_gain(kn_mem[layer], MEM_HEADS))

        bound = (HEAD64 ** 0.5 * LOG2E * 1.02 * jnp.max(jnp.abs(qn_fox[layer]))
                 * jnp.max(jnp.abs(kn_fox[layer]))).astype(BF16).astype(F32)
        fast = bound <= FOX_FAST_MAX_SHIFT
        place, consts = _aug_placement(jnp.where(fast, bound, 0.0))
        bf_pad = jnp.pad(b_forget[layer].astype(F32), (0, LANES - FOX_HEADS))[None, :]

        wt = jnp.concatenate([parts["vf"], parts["va"], parts["qf"]], axis=0).astype(BF16)
        gqft = jnp.broadcast_to(
            _tile_gain(qn_fox[layer], FOX_HEADS, HEAD64 ** -0.5 * LOG2E).T, (FOX_W, LANES))
        qa, ka, vat, qft, kf, vt, qm, gates, fl = _proj(
            x2, g_mix[layer][None, :], w1, wt, b_gate[layer][None, :],
            _tile_gain(qn_swa[layer], SWA_HEADS, HEAD64 ** -0.5 * LOG2E),
            _tile_gain(kn_swa[layer], 2 * SWA_KV_HEADS),
            gqft,
            _tile_gain(kn_fox[layer], FOX_HEADS),
            _tile_gain(qn_mem[layer], MEM_HEADS, MEM_HEAD_DIM ** -0.5 * LOG2E), tm, seq)
        aqt, ak = _cum(fl, bf_pad, tri, place, consts, tm, seq)

        oa, om = _local(bucket, rel_bias.astype(F32) * LOG2E, sink_swa[layer].astype(F32) * LOG2E,
                        qa, ka, vat, qm, mk, mvt, batch, seq, tq)
        of = _fox(fast.astype(jnp.int32)[None], qft, aqt, kf, ak, vt, batch, seq, tm)

        x2 = _post(x2, oa, of, om, gates, w_o_swa[layer].astype(BF16),
                   w_o_fox[layer].astype(BF16), w_o_mem[layer].astype(BF16),
                   w_out[layer].astype(BF16), g_mlp[layer][None, :],
                   w_mlp_up[layer].astype(BF16), w_mlp_down[layer].astype(BF16), tm)
    return x2.reshape(batch, seq, D_MODEL)
```

```python
import functools
import math

import jax
import jax.numpy as jnp
import numpy as np
from jax import lax
from jax.experimental import pallas as pl
from jax.experimental.pallas import tpu as pltpu

F32 = jnp.float32
BF16 = jnp.bfloat16

D_MODEL = 1024
N_MEM = 256
SWA_HEADS = 8
SWA_KV_HEADS = 2
SWA_GROUP = SWA_HEADS // SWA_KV_HEADS
HEAD64 = 64
WINDOW = 128
FOX_HEADS = 8
MEM_HEADS = 4
MEM_HEAD_DIM = 128
D_FF = 4 * D_MODEL
REL_BUCKETS = 32
REL_MAX_DIST = 128
BLOCK = 128
EPS = 1e-6
NEG = -1e30
LOG2E = math.log2(math.e)

LANES = 128
MXU_DIM = 256
BF16_ROWS = 16
VMEM_LIMIT = 56 * 1024 * 1024

SWA_Q = SWA_HEADS * HEAD64
SWA_KV = SWA_KV_HEADS * HEAD64
FOX_W = FOX_HEADS * HEAD64
MEM_W = MEM_HEADS * MEM_HEAD_DIM
GATE_W = 3 * D_MODEL

C_QA = 0
C_KA = C_QA + SWA_Q
C_KF = C_KA + 2 * SWA_KV
C_QM = C_KF + FOX_W
C_G = C_QM + MEM_W
C_FL = C_G + GATE_W
C_END = C_FL + LANES

AUG_PER_HEAD = 7
FOX_FAST_MAX_SHIFT = 60.0
FOX_SUB = 4
LOCAL_LOOKAHEAD = 4


def _dot(a, b):
    return jnp.dot(a, b, preferred_element_type=F32)


def _dot_nt(a, b):
    return lax.dot_general(a, b, (((1,), (1,)), ((), ())), preferred_element_type=F32)


def _rms_rows(x, gain):
    ms = jnp.mean(x * x, axis=-1, keepdims=True)
    return x * lax.rsqrt(ms + EPS) * gain


def _head_rms(y, gain, head_dim):
    y2 = y * y
    lower = lax.broadcasted_iota(jnp.int32, (y.shape[0], LANES), 1) < HEAD64
    cols = []
    for c in range(0, y.shape[1], LANES):
        blk = y2[:, c:c + LANES]
        total = jnp.sum(blk, axis=-1, keepdims=True)
        if head_dim == LANES:
            cols.append(jnp.broadcast_to(total, blk.shape))
        else:
            low = jnp.sum(jnp.where(lower, blk, 0.0), axis=-1, keepdims=True)
            cols.append(jnp.where(lower, low, total - low))
    ss = jnp.concatenate(cols, axis=1)
    return y * lax.rsqrt(ss * (1.0 / head_dim) + EPS) * gain


def _const_spec(shape):
    return pl.BlockSpec(shape, lambda *_: (0,) * len(shape))


def _params(n_axes):
    return pltpu.CompilerParams(dimension_semantics=("arbitrary",) * n_axes,
                                vmem_limit_bytes=VMEM_LIMIT)


def _proj_kernel(x_ref, g_ref, w_ref, wt_ref, bg_ref,
                 gqa_ref, gka_ref, gqft_ref, gkf_ref, gqm_ref,
                 qa_ref, ka_ref, vat_ref, qft_ref, kf_ref, vt_ref, qm_ref, gate_ref, fl_ref):
    h = _rms_rows(x_ref[...], g_ref[...]).astype(BF16)
    tm = h.shape[0]
    qkv = _dot_nt(h, w_ref[:C_G, :])
    qa_ref[...] = _head_rms(qkv[:, C_QA:C_KA], gqa_ref[...], HEAD64).astype(BF16)
    ka_ref[...] = _head_rms(qkv[:, C_KA:C_KF], gka_ref[...], HEAD64).astype(BF16)
    kf_ref[...] = _head_rms(qkv[:, C_KF:C_QM], gkf_ref[...], HEAD64).astype(BF16)
    qm_ref[...] = _head_rms(qkv[:, C_QM:C_G], gqm_ref[...], MEM_HEAD_DIM).astype(BF16)
    t = _dot_nt(wt_ref[...], h)
    for p in range(FOX_HEADS // 2):
        vt_ref[p] = t[p * LANES:(p + 1) * LANES, :].astype(BF16)
    vat_ref[...] = t[FOX_W:FOX_W + SWA_KV, :].astype(BF16)
    gain_t = jnp.tile(gqft_ref[...], (1, tm // LANES))
    for p in range(FOX_HEADS // 2):
        rows = []
        for e in range(2):
            r0 = FOX_W + SWA_KV + (2 * p + e) * HEAD64
            y = t[r0:r0 + HEAD64, :]
            ms = jnp.sum(y * y, axis=0, keepdims=True) * (1.0 / HEAD64)
            rows.append(y * lax.rsqrt(ms + EPS) * gain_t[r0 - FOX_W - SWA_KV:r0 - FOX_W - SWA_KV
                                                         + HEAD64, :])
        qft_ref[p] = jnp.concatenate(rows, axis=0).astype(BF16)
    zf = _dot_nt(h, w_ref[C_G:C_END, :])
    z = zf[:, :GATE_W] + bg_ref[...]
    gate_ref[...] = (0.5 * jnp.tanh(0.5 * z) + 0.5).astype(BF16)
    fl_ref[...] = zf[:, GATE_W:]


def _proj(x2, g_mix, w1, wt, b_gate, gqa, gka, gqft, gkf, gqm, tm, seq):
    t = x2.shape[0]
    nt = seq // tm
    pairs = FOX_HEADS // 2

    def rows(w, dtype=BF16):
        return jax.ShapeDtypeStruct((t, w), dtype), pl.BlockSpec((tm, w), lambda i: (i, 0))

    outs = [rows(SWA_Q), rows(2 * SWA_KV),
            (jax.ShapeDtypeStruct((t // seq, SWA_KV, seq), BF16),
             pl.BlockSpec((None, SWA_KV, tm), lambda i: (i // nt, 0, i % nt))),
            (jax.ShapeDtypeStruct((t // seq, pairs, LANES, seq), BF16),
             pl.BlockSpec((None, pairs, LANES, tm), lambda i: (i // nt, 0, 0, i % nt))),
            rows(FOX_W),
            (jax.ShapeDtypeStruct((t // seq, pairs, nt, LANES, tm), BF16),
             pl.BlockSpec((None, pairs, None, LANES, tm), lambda i: (i // nt, 0, i % nt, 0, 0))),
            rows(MEM_W), rows(GATE_W), rows(LANES, F32)]
    out_shape = [o[0] for o in outs]
    out_specs = [o[1] for o in outs]
    return pl.pallas_call(
        _proj_kernel,
        grid=(t // tm,),
        in_specs=[pl.BlockSpec((tm, D_MODEL), lambda i: (i, 0)),
                  _const_spec((1, D_MODEL)),
                  _const_spec((C_END, D_MODEL)),
                  _const_spec((2 * FOX_W + SWA_KV, D_MODEL)),
                  _const_spec((1, GATE_W)),
                  _const_spec((1, SWA_Q)),
                  _const_spec((1, 2 * SWA_KV)),
                  _const_spec((FOX_W, LANES)),
                  _const_spec((1, FOX_W)),
                  _const_spec((1, MEM_W))],
        out_specs=out_specs,
        out_shape=out_shape,
        compiler_params=_params(1),
        name="proj",
    )(x2, g_mix, w1, wt, b_gate, gqa, gka, gqft, gkf, gqm)


def _split3(c):
    hi = c.astype(BF16)
    r1 = c - hi.astype(F32)
    mid = r1.astype(BF16)
    lo = (r1 - mid.astype(F32)).astype(BF16)
    return hi, mid, lo


def _pack3(x):
    hi, mid, lo = _split3(x)
    return (hi.astype(F32) + pltpu.roll(mid.astype(F32), FOX_HEADS, 1)
            + pltpu.roll(lo.astype(F32), 2 * FOX_HEADS, 1)).astype(BF16)


def _cum_kernel(fl_ref, bf_ref, tri_ref, place_ref, const_ref, mem_ref, gm_ref, wkt_ref, wvt_ref,
                kn_ref, aqt_ref, ak_ref, mk_ref, mvt_ref, *, tm):
    hm = _rms_rows(mem_ref[...], gm_ref[...]).astype(BF16)
    mk_ref[...] = _head_rms(_dot_nt(hm, wkt_ref[...]), kn_ref[...], MEM_HEAD_DIM).astype(BF16)
    mvt_ref[...] = _dot_nt(wvt_ref[...], hm).astype(BF16)

    n_tiles = fl_ref.shape[0] // tm
    lane = lax.broadcasted_iota(jnp.int32, (tm, LANES), 1)
    heads = lane < FOX_HEADS
    tri = tri_ref[...]
    parts = []
    for k in range(n_tiles):
        z = fl_ref[k * tm:(k + 1) * tm, :] + bf_ref[...]
        logf = jnp.where(heads, jnp.minimum(z, 0.0) - jnp.log(1.0 + jnp.exp(-jnp.abs(z))), 0.0)
        parts.append(_dot(tri, _pack3(logf)))
    carry = jnp.zeros((1, LANES), F32)
    packed = []
    for part in parts:
        c = ((part + pltpu.roll(part, LANES - FOX_HEADS, 1))
             + pltpu.roll(part, LANES - 2 * FOX_HEADS, 1)) + carry
        carry = c[tm - 1:tm, :]
        packed.append(_pack3(jnp.where(heads, c * LOG2E, 0.0)))
    for k in range(n_tiles):
        placed = _dot(packed[k], place_ref[...])
        rows = slice(k * tm, (k + 1) * tm)
        aqt_ref[:, rows] = (placed[:, :LANES] + const_ref[0:1, :LANES]).T.astype(BF16)
        ak_ref[rows, :] = (const_ref[0:1, LANES:] - placed[:, LANES:]).astype(BF16)


def _cum(fl, bf_pad, tri, place, consts, mem2, g_mem, wkt, wvt, kn_t, tm, seq):
    t = fl.shape[0]
    nb = t // seq
    return pl.pallas_call(
        functools.partial(_cum_kernel, tm=tm),
        grid=(nb,),
        in_specs=[pl.BlockSpec((seq, LANES), lambda i: (i, 0)),
                  _const_spec((1, LANES)),
                  _const_spec((tm, tm)),
                  _const_spec((LANES, 2 * LANES)),
                  _const_spec((8, 2 * LANES)),
                  pl.BlockSpec((N_MEM, D_MODEL), lambda i: (i, 0)),
                  _const_spec((1, D_MODEL)),
                  _const_spec((MEM_W, D_MODEL)),
                  _const_spec((MEM_W, D_MODEL)),
                  _const_spec((1, MEM_W))],
        out_specs=[pl.BlockSpec((None, LANES, seq), lambda i: (i, 0, 0)),
                   pl.BlockSpec((seq, LANES), lambda i: (i, 0)),
                   pl.BlockSpec((N_MEM, MEM_W), lambda i: (i, 0)),
                   pl.BlockSpec((None, MEM_W, N_MEM), lambda i: (i, 0, 0))],
        out_shape=[jax.ShapeDtypeStruct((nb, LANES, seq), BF16),
                   jax.ShapeDtypeStruct((t, LANES), BF16),
                   jax.ShapeDtypeStruct((nb * N_MEM, MEM_W), BF16),
                   jax.ShapeDtypeStruct((nb, MEM_W, N_MEM), BF16)],
        compiler_params=_params(1),
        name="cum",
    )(fl, bf_pad, tri, place, consts, mem2, g_mem, wkt, wvt, kn_t)


def _local_kernel(bucket_ref, relb_ref, sink_ref,
                  qa_ref, kc_ref, kp_ref, vc_ref, vp_ref, qm_ref, mk_ref, mvt_ref,
                  oa_ref, om_ref, tbl_ref, *, tq):
    b = pl.program_id(0)
    i = pl.program_id(1)
    pairs = SWA_HEADS // 2

    @pl.when(jnp.logical_and(b == 0, i == 0))
    def _():
        bucket = bucket_ref[...]
        band = bucket >= 0
        for h in range(SWA_HEADS):
            t = jnp.zeros(bucket.shape, F32)
            for k in range(REL_BUCKETS):
                t = jnp.where(bucket == k, relb_ref[k, h], t)
            tbl_ref[h // 2, :, (h % 2) * BLOCK:(h % 2 + 1) * BLOCK] = jnp.where(band, t, NEG)

    lane = lax.broadcasted_iota(jnp.int32, (BLOCK, LANES), 1)
    lower = lane < HEAD64
    first_head = lax.broadcasted_iota(jnp.int32, (1, 2 * BLOCK), 1) < BLOCK
    key_row = lax.broadcasted_iota(jnp.int32, (2 * BLOCK, 2 * BLOCK), 0)
    pad_rows = jnp.logical_and(key_row < BLOCK, i == 0)
    ones = jnp.ones((BF16_ROWS, 2 * BLOCK), BF16)

    def windows(j):
        r0 = j * BLOCK
        if j == 0:
            return (jnp.concatenate([kp_ref[...], kc_ref[0:BLOCK, :]], axis=0),
                    jnp.concatenate([vp_ref[...], vc_ref[:, 0:BLOCK]], axis=1))
        return kc_ref[r0 - BLOCK:r0 + BLOCK, :], vc_ref[:, r0 - BLOCK:r0 + BLOCK]

    def swa_scores(j, hb):
        g = (2 * hb) // SWA_GROUP
        qblk = qa_ref[j * BLOCK:(j + 1) * BLOCK, hb * LANES:(hb + 1) * LANES]
        qstack = jnp.concatenate([jnp.where(lower, qblk, 0),
                                  jnp.where(lower, 0, qblk)], axis=0)
        s = _dot_nt(windows(j)[0][:, g * LANES:(g + 1) * LANES], qstack) + tbl_ref[hb]
        return jnp.where(pad_rows, NEG, s) if j == 0 else s

    def swa_finish(j, hb, s):
        g = (2 * hb) // SWA_GROUP
        sink = jnp.where(first_head, sink_ref[2 * hb], sink_ref[2 * hb + 1])
        m = jnp.maximum(jnp.max(s, axis=0, keepdims=True), sink)
        p = jnp.exp2(s - m).astype(BF16)
        vt_ones = jnp.concatenate([windows(j)[1][g * HEAD64:(g + 1) * HEAD64, :], ones], axis=0)
        acc = _dot(vt_ones, p)
        o = acc[:HEAD64, :] / (acc[HEAD64:HEAD64 + 1, :] + jnp.exp2(sink - m))
        o = jnp.concatenate([o[:, :BLOCK], o[:, BLOCK:]], axis=0)
        oa_ref[j * BLOCK:(j + 1) * BLOCK, hb * LANES:(hb + 1) * LANES] = o.T.astype(BF16)

    ones_m = jnp.ones((BF16_ROWS, N_MEM), BF16)

    def mem_scores(h):
        sl = slice(h * MEM_HEAD_DIM, (h + 1) * MEM_HEAD_DIM)
        return _dot_nt(mk_ref[:, sl], qm_ref[:, sl])

    def mem_finish(h, s):
        sl = slice(h * MEM_HEAD_DIM, (h + 1) * MEM_HEAD_DIM)
        m = jnp.max(s, axis=0, keepdims=True)
        p = jnp.exp2(s - m).astype(BF16)
        acc = _dot(jnp.concatenate([mvt_ref[sl, :], ones_m], axis=0), p)
        o = acc[:MEM_HEAD_DIM, :] / acc[MEM_HEAD_DIM:MEM_HEAD_DIM + 1, :]
        om_ref[:, sl] = o.T.astype(BF16)

    units = [(swa_scores, swa_finish, (j, hb)) for j in range(tq // BLOCK) for hb in range(pairs)]
    units += [(mem_scores, mem_finish, (h,)) for h in range(MEM_HEADS)]
    pending = []
    for n in range(len(units) + LOCAL_LOOKAHEAD):
        if n < len(units):
            score_fn, _, args = units[n]
            pending.append(score_fn(*args))
        if n >= LOCAL_LOOKAHEAD:
            _, finish_fn, args = units[n - LOCAL_LOOKAHEAD]
            finish_fn(*args, pending[n - LOCAL_LOOKAHEAD])


def _local(bucket_t, rel_bias, sinks, qa, ka, vat, qm, mk, mvt, batch, seq, tq):
    t = qa.shape[0]
    nq = seq // tq
    sub = tq // BLOCK

    def cur(b, i):
        return (b * nq + i, 0)

    def prev(b, i):
        return (jnp.maximum((b * nq + i) * sub - 1, 0), 0)

    smem = pl.BlockSpec(memory_space=pltpu.SMEM)
    return pl.pallas_call(
        functools.partial(_local_kernel, tq=tq),
        grid=(batch, nq),
        in_specs=[_const_spec((2 * BLOCK, BLOCK)), smem, smem,
                  pl.BlockSpec((tq, SWA_Q), cur),
                  pl.BlockSpec((tq, 2 * SWA_KV), cur),
                  pl.BlockSpec((BLOCK, 2 * SWA_KV), prev),
                  pl.BlockSpec((None, SWA_KV, tq), lambda b, i: (b, 0, i)),
                  pl.BlockSpec((None, SWA_KV, BLOCK),
                               lambda b, i: (b, 0, jnp.maximum(i * sub - 1, 0))),
                  pl.BlockSpec((tq, MEM_W), cur),
                  pl.BlockSpec((N_MEM, MEM_W), lambda b, i: (b, 0)),
                  pl.BlockSpec((None, MEM_W, N_MEM), lambda b, i: (b, 0, 0))],
        out_specs=[pl.BlockSpec((tq, SWA_Q), cur),
                   pl.BlockSpec((tq, MEM_W), cur)],
        out_shape=[jax.ShapeDtypeStruct((t, SWA_Q), BF16),
                   jax.ShapeDtypeStruct((t, MEM_W), BF16)],
        scratch_shapes=[pltpu.VMEM((SWA_HEADS // 2, 2 * BLOCK, 2 * BLOCK), F32)],
        compiler_params=_params(2),
        name="local",
    )(bucket_t, rel_bias, sinks, qa, ka, ka, vat, vat, qm, mk, mvt)


def _fox_kernel(fast_ref, qt_ref, aqt_ref, k_ref, ak_ref, vt_ref, o_ref,
                sa_ref, sb_ref, acc_ref, m_ref, *, tk, n_sub, n_steps):
    pair = pl.program_id(1)
    g = pl.program_id(2)
    row = lax.broadcasted_iota(jnp.int32, (LANES, tk), 0)
    half = [row < HEAD64, row >= HEAD64]
    qcat = []
    for sub in range(n_sub):
        q = qt_ref[:, sub * tk:(sub + 1) * tk]
        aq = aqt_ref[:, sub * tk:(sub + 1) * tk]
        zero = jnp.zeros_like(q)
        per_head = []
        for h in range(2):
            lo = (2 * pair + h) * AUG_PER_HEAD
            mine = jnp.logical_and(row >= lo, row < lo + AUG_PER_HEAD)
            per_head.append(jnp.concatenate([jnp.where(half[h], q, zero),
                                             jnp.where(mine, aq, zero)], axis=0))
        qcat.append(per_head)
    ones = jnp.ones((BF16_ROWS, tk), BF16)
    acc_ref[...] = jnp.zeros_like(acc_ref)
    key = lax.broadcasted_iota(jnp.int32, (tk, tk), 0)
    qry = lax.broadcasted_iota(jnp.int32, (tk, tk), 1)
    causal = key <= qry

    def kcat(j):
        r0 = pl.multiple_of(j * tk, tk)
        return jnp.concatenate([k_ref[pl.ds(r0, tk), :], ak_ref[pl.ds(r0, tk), :]], axis=1)

    def vt_ones(j, h):
        return jnp.concatenate([vt_ref[j, h * HEAD64:(h + 1) * HEAD64, :], ones], axis=0)

    def visible_keys(c, diagonal):
        return min(tk, c + MXU_DIM) if diagonal else tk

    def qk(sub, j, s_ref, diagonal=False):
        kc = kcat(j)
        for h in range(2):
            for c in range(0, tk, MXU_DIM):
                nk = visible_keys(c, diagonal)
                s_ref[h, :nk, c:c + MXU_DIM] = _dot(kc[:nk], qcat[sub][h][:, c:c + MXU_DIM])

    @pl.when(fast_ref[0] == 1)
    def _():
        bufs = (sa_ref, sb_ref)

        def exp_pv(sub, j, s_ref, masked):
            for h in range(2):
                for c in range(0, tk, MXU_DIM):
                    nk = visible_keys(c, masked)
                    s = s_ref[h, :nk, c:c + MXU_DIM]
                    if masked:
                        s = jnp.where(causal[:nk, c:c + MXU_DIM], s, NEG)
                    acc_ref[2 * sub + h, :, c:c + MXU_DIM] += _dot(vt_ones(j, h)[:, :nk],
                                                                   jnp.exp2(s).astype(BF16))

        full = [(sub, jj) for jj in range(n_sub) for sub in range(n_sub)]
        diag = [(sub, jj) for jj in range(n_sub) for sub in range(jj, n_sub)]
        assert len(full) % 2 == 0
        qk(0, 0, bufs[0])

        def body(t, carry):
            for n, (sub, jj) in enumerate(full):
                nsub, njj = full[n + 1] if n + 1 < len(full) else (0, n_sub)
                qk(nsub, n_sub * t + njj, bufs[(n + 1) % 2])
                exp_pv(sub, n_sub * t + jj, bufs[n % 2], False)
            return carry

        if n_steps > 1:
            lax.fori_loop(0, g, body, 0)
        for n, (sub, jj) in enumerate(diag):
            if n + 1 < len(diag):
                nsub, njj = diag[n + 1]
                qk(nsub, n_sub * g + njj, bufs[(n + 1) % 2], diagonal=nsub == njj)
            exp_pv(sub, n_sub * g + jj, bufs[n % 2], sub == jj)

    @pl.when(fast_ref[0] == 0)
    def _():
        m_ref[...] = jnp.full(m_ref.shape, NEG, F32)
        for sub in range(n_sub):
            last = n_sub * g + sub

            def body(j, carry, sub=sub, last=last):
                qk(sub, j, sa_ref)
                for h in range(2):
                    u = 2 * sub + h
                    s = jnp.where(jnp.logical_or(causal, j < last), sa_ref[h], NEG)
                    m_prev = m_ref[u]
                    m_next = jnp.maximum(m_prev, jnp.max(s, axis=0, keepdims=True))
                    alpha = jnp.exp2(m_prev - m_next)
                    p = jnp.exp2(s - m_next).astype(BF16)
                    acc_ref[u] = alpha * acc_ref[u] + _dot(vt_ones(j, h), p)
                    m_ref[u] = m_next
                return carry

            lax.fori_loop(0, last + 1, body, 0)

    for sub in range(n_sub):
        halves = []
        for h in range(2):
            acc = acc_ref[2 * sub + h]
            halves.append(acc[:HEAD64, :] / acc[HEAD64:HEAD64 + 1, :])
        o_ref[sub * tk:(sub + 1) * tk, :] = jnp.concatenate(halves, axis=0).T.astype(BF16)


def _fox(fast, qft, aqt, kf, ak, vt, batch, seq, tk):
    t = kf.shape[0]
    n_sub = min(FOX_SUB, seq // tk)
    tq = n_sub * tk
    assert seq % tq == 0, (seq, tq)
    nq = seq // tq
    pairs = FOX_HEADS // 2
    return pl.pallas_call(
        functools.partial(_fox_kernel, tk=tk, n_sub=n_sub, n_steps=nq),
        grid=(batch, pairs, nq),
        in_specs=[pl.BlockSpec(memory_space=pltpu.SMEM),
                  pl.BlockSpec((None, None, LANES, tq), lambda b, p, i: (b, p, 0, i)),
                  pl.BlockSpec((None, LANES, tq), lambda b, p, i: (b, 0, i)),
                  pl.BlockSpec((seq, LANES), lambda b, p, i: (b, p)),
                  pl.BlockSpec((seq, LANES), lambda b, p, i: (b, 0)),
                  pl.BlockSpec((None, None, seq // tk, LANES, tk),
                               lambda b, p, i: (b, p, 0, 0, 0))],
        out_specs=pl.BlockSpec((tq, LANES), lambda b, p, i: (b * nq + i, p)),
        out_shape=jax.ShapeDtypeStruct((t, FOX_W), BF16),
        scratch_shapes=[pltpu.VMEM((2, tk, tk), F32),
                        pltpu.VMEM((2, tk, tk), F32),
                        pltpu.VMEM((2 * n_sub, HEAD64 + BF16_ROWS, tk), F32),
                        pltpu.VMEM((2 * n_sub, 1, tk), F32)],
        compiler_params=_params(3),
        name="fox",
    )(fast, qft, aqt, kf, ak, vt)


def _post_kernel(x_ref, oa_ref, of_ref, om_ref, gate_ref, wa_ref, wf_ref, wm_ref, wo_ref,
                 g_ref, wu_ref, wd_ref, y_ref, *, chunk):
    ya = _dot(oa_ref[...], wa_ref[...])
    yf = _dot(of_ref[...], wf_ref[...])
    ym = _dot(om_ref[...], wm_ref[...])
    merged = (gate_ref[:, 0:D_MODEL].astype(F32) * ya
              + gate_ref[:, D_MODEL:2 * D_MODEL].astype(F32) * yf
              + gate_ref[:, 2 * D_MODEL:].astype(F32) * ym)
    x1 = x_ref[...] + _dot(merged.astype(BF16), wo_ref[...])
    h = _rms_rows(x1, g_ref[...]).astype(BF16)
    acc = x1
    for c in range(0, D_FF, chunk):
        u = jnp.maximum(_dot(h, wu_ref[:, c:c + chunk]), 0.0)
        acc = acc + _dot((u * u).astype(BF16), wd_ref[c:c + chunk, :])
    y_ref[...] = acc


def _post(x2, oa, of, om, gates, wa, wf, wm, wo, g_mlp, wu, wd, tm, chunk=1024):
    t = x2.shape[0]
    row = lambda w: pl.BlockSpec((tm, w), lambda i: (i, 0))
    resident = lambda shape: pl.BlockSpec(shape, lambda i: (0,) * len(shape),
                                          pipeline_mode=pl.Buffered(1))
    return pl.pallas_call(
        functools.partial(_post_kernel, chunk=chunk),
        grid=(t // tm,),
        in_specs=[row(D_MODEL), row(SWA_Q), row(FOX_W), row(MEM_W), row(GATE_W),
                  resident((SWA_Q, D_MODEL)), resident((FOX_W, D_MODEL)),
                  resident((MEM_W, D_MODEL)), resident((D_MODEL, D_MODEL)),
                  resident((1, D_MODEL)), resident((D_MODEL, D_FF)), resident((D_FF, D_MODEL))],
        out_specs=row(D_MODEL),
        out_shape=jax.ShapeDtypeStruct((t, D_MODEL), F32),
        compiler_params=_params(1),
        name="post",
    )(x2, oa, of, om, gates, wa, wf, wm, wo, g_mlp, wu, wd)


def _t5_bucket_table():
    max_exact = REL_BUCKETS // 2
    t_loc = np.arange(BLOCK)[:, None] + BLOCK
    s_loc = np.arange(2 * BLOCK)[None, :]
    dist = t_loc - s_loc
    d = np.maximum(dist, 0)
    df = np.maximum(d, 1).astype(np.float32)
    scaled = (np.log(df / np.float32(max_exact)) / np.float32(math.log(REL_MAX_DIST / max_exact))
              * np.float32(REL_BUCKETS - max_exact))
    large = np.minimum(max_exact + scaled.astype(np.int32), REL_BUCKETS - 1)
    bucket = np.where(d < max_exact, d, large)
    band = (dist >= 0) & (dist < WINDOW)
    return jnp.asarray(np.where(band, bucket, -1).astype(np.int32).T)


def _aug_placement(shift):
    place = [[0.0] * (2 * LANES) for _ in range(LANES)]
    ones = [0.0] * (2 * LANES)
    shift_lanes = []
    for h in range(FOX_HEADS):
        base = h * AUG_PER_HEAD
        for part in range(3):
            src = part * FOX_HEADS + h
            place[src][base + part] = 1.0
            place[src][LANES + base + 3 + part] = 1.0
            ones[base + 3 + part] = 1.0
            ones[LANES + base + part] = 1.0
        ones[base + 6] = 1.0
        shift_lanes.append(LANES + base + 6)
    consts = jnp.array(ones, F32).at[jnp.array(shift_lanes)].set(-shift)
    return jnp.array(place, BF16), jnp.broadcast_to(consts[None, :], (8, 2 * LANES))


def _tile_gain(g, reps, scale=1.0):
    return (jnp.tile(g.astype(F32), reps) * scale)[None, :]


def _pick_tile(n, target):
    t = min(n, target)
    while n % t:
        t //= 2
    return t


def kernel(x, mem, g_mix, w_in, b_gate, b_forget, qn_swa, kn_swa, sink_swa, rel_bias,
           qn_fox, kn_fox, g_mem, w_mem_kv, qn_mem, kn_mem, w_o_swa, w_o_fox, w_o_mem,
           w_out, g_mlp, w_mlp_up, w_mlp_down):
    batch, seq, _ = x.shape
    n_layers = w_in.shape[0]
    t = batch * seq
    tm = _pick_tile(seq, 512)
    tq = _pick_tile(seq, 2048)

    bucket = _t5_bucket_table()
    tri = (jnp.arange(tm)[:, None] >= jnp.arange(tm)[None, :]).astype(BF16)

    x2 = x.reshape(t, D_MODEL)
    mem2 = mem.reshape(batch * N_MEM, D_MODEL)
    for layer in range(n_layers):
        wt = w_in[layer].T
        o = 0
        parts = {}
        for name, width in (("qa", SWA_Q), ("ka", SWA_KV), ("va", SWA_KV), ("qf", FOX_W),
                            ("kf", FOX_W), ("vf", FOX_W), ("fl", FOX_HEADS), ("qm", MEM_W),
                            ("g", GATE_W)):
            parts[name] = wt[o:o + width]
            o += width

        def dup(m):
            return jnp.concatenate([m[:HEAD64], m[:HEAD64], m[HEAD64:], m[HEAD64:]], axis=0)

        w1 = jnp.concatenate(
            [parts["qa"], dup(parts["ka"]), parts["kf"], parts["qm"], parts["g"],
             jnp.pad(parts["fl"], ((0, LANES - FOX_HEADS), (0, 0)))], axis=0).astype(BF16)

        wmt = w_mem_kv[layer].T.astype(BF16)

        bound = (HEAD64 ** 0.5 * LOG2E * 1.02 * jnp.max(jnp.abs(qn_fox[layer]))
                 * jnp.max(jnp.abs(kn_fox[layer]))).astype(BF16).astype(F32)
        fast = bound <= FOX_FAST_MAX_SHIFT
        place, consts = _aug_placement(jnp.where(fast, bound, 0.0))
        bf_pad = jnp.pad(b_forget[layer].astype(F32), (0, LANES - FOX_HEADS))[None, :]

        wt = jnp.concatenate([parts["vf"], parts["va"], parts["qf"]], axis=0).astype(BF16)
        gqft = jnp.broadcast_to(
            _tile_gain(qn_fox[layer], FOX_HEADS, HEAD64 ** -0.5 * LOG2E).T, (FOX_W, LANES))
        qa, ka, vat, qft, kf, vt, qm, gates, fl = _proj(
            x2, g_mix[layer][None, :], w1, wt, b_gate[layer][None, :],
            _tile_gain(qn_swa[layer], SWA_HEADS, HEAD64 ** -0.5 * LOG2E),
            _tile_gain(kn_swa[layer], 2 * SWA_KV_HEADS),
            gqft,
            _tile_gain(kn_fox[layer], FOX_HEADS),
            _tile_gain(qn_mem[layer], MEM_HEADS, MEM_HEAD_DIM ** -0.5 * LOG2E), tm, seq)
        aqt, ak, mk, mvt = _cum(fl, bf_pad, tri, place, consts, mem2, g_mem[layer][None, :],
                                wmt[:MEM_W], wmt[MEM_W:], _tile_gain(kn_mem[layer], MEM_HEADS),
                                tm, seq)

        oa, om = _local(bucket, rel_bias.astype(F32) * LOG2E, sink_swa[layer].astype(F32) * LOG2E,
                        qa, ka, vat, qm, mk, mvt, batch, seq, tq)
        of = _fox(fast.astype(jnp.int32)[None], qft, aqt, kf, ak, vt, batch, seq, tm)

        x2 = _post(x2, oa, of, om, gates, w_o_swa[layer].astype(BF16),
                   w_o_fox[layer].astype(BF16), w_o_mem[layer].astype(BF16),
                   w_out[layer].astype(BF16), g_mlp[layer][None, :],
                   w_mlp_up[layer].astype(BF16), w_mlp_down[layer].astype(BF16), tm)
    return x2.reshape(batch, seq, D_MODEL)
```
